```python
import jax, jax.numpy as jnp
from jax import lax
import numpy as np

D_MODEL = 1024
BATCH = 8
SEQ = 2048
DEPTH = 1
DEC_BATCH = 128
DEC_SEQ = 8
PAST_LEN = 16384
PAGE_SIZE = 128

N_META = 16
D_SCONV = D_MODEL // 2
SCONV_GROUPS = 8
SCONV_WIDTH = 3
SSM_HEAD_DIM = 64
SSM_HEADS = D_MODEL // SSM_HEAD_DIM
D_SSM = SSM_HEADS * SSM_HEAD_DIM
SSM_GROUPS = 2
HEADS_PER_GROUP = SSM_HEADS // SSM_GROUPS
SSM_STATE = 128
SSM_CONV_WIDTH = 4
SSD_CHUNK = 128
D_XBC = D_SSM + 2 * SSM_GROUPS * SSM_STATE
D_MIX = D_SCONV + D_SSM
D_IN = 3 * D_SCONV + D_SSM + D_XBC + SSM_HEADS
D_FF = -(-8 * D_MODEL // (3 * 256)) * 256
EPS = 1e-6

kernel_name = "hymba_sconv_ssd_decoder_step"


def rmsnorm(x, w):
    xf = x.astype(jnp.float32)
    y = xf * lax.rsqrt(jnp.mean(xf * xf, axis=-1, keepdims=True) + EPS) * w.astype(jnp.float32)
    return y.astype(x.dtype)


def group_rmsnorm(x, w, groups, out_dtype):
    shp = x.shape
    xf = x.astype(jnp.float32).reshape(shp[:-1] + (groups, shp[-1] // groups))
    xf = xf * lax.rsqrt(jnp.mean(xf * xf, axis=-1, keepdims=True) + EPS)
    return (xf.reshape(shp) * w.astype(jnp.float32)).astype(out_dtype)


def causal_dwconv(u, buf, w):
    full = jnp.concatenate([buf.astype(u.dtype), u], axis=1)
    L = u.shape[1]
    K = w.shape[0]
    out = full[:, 0:L] * w[0]
    for i in range(1, K):
        out = out + full[:, i:i + L] * w[i]
    return out, full[:, full.shape[1] - (K - 1):]


def ssd_chunked(x, dt, A, B, C, s0, chunk):
    b, L = x.shape[:2]
    nc = L // chunk
    x = x.reshape((b, nc, chunk) + x.shape[2:])
    dt = dt.reshape((b, nc, chunk) + dt.shape[2:])
    B = B.reshape((b, nc, chunk) + B.shape[2:])
    C = C.reshape((b, nc, chunk) + C.shape[2:])
    a_cum = jnp.cumsum(dt * A, axis=2)
    xdt = x * dt[..., None]
    seg = a_cum[:, :, :, None] - a_cum[:, :, None, :]
    causal = (jnp.arange(chunk)[:, None] >= jnp.arange(chunk)[None, :])[None, None, :, :, None, None]
    decay = jnp.exp(jnp.where(causal, seg, -jnp.inf))
    cb = jnp.einsum('bclgn,bcsgn->bclsg', C, B)
    y_diag = jnp.einsum('bclsg,bclsge,bcsgep->bclgep', cb, decay, xdt)
    decay_end = jnp.exp(a_cum[:, :, -1:] - a_cum)
    chunk_states = jnp.einsum('bclgn,bclge,bclgep->bcgepn', B, decay_end, xdt)
    chunk_decay = jnp.exp(a_cum[:, :, -1])

    def step(s, inp):
        dec, st = inp
        return dec[..., None, None] * s + st, s

    s_final, s_prev = lax.scan(step, s0, (jnp.moveaxis(chunk_decay, 1, 0),
                                          jnp.moveaxis(chunk_states.astype(jnp.float32), 1, 0)))
    s_prev = jnp.moveaxis(s_prev, 0, 1)
    y_off = jnp.einsum('bclgn,bcgepn,bclge->bclgep', C, s_prev, jnp.exp(a_cum))
    y = (y_diag + y_off).reshape((b, L) + x.shape[3:])
    return y, s_final


def hybrid_layer(h, buf_sc, buf_xbc, s0, segments,
                 norm_mix_pre, norm_mix_post, w_in, sconv_w, sconv_norm,
                 ssm_conv_w, ssm_conv_b, dt_bias, A_log, D_skip, ssm_norm, w_out,
                 norm_ffn_pre, norm_ffn_post, w_gate, w_up, w_down):
    b, L = h.shape[:2]
    hn = rmsnorm(h, norm_mix_pre)
    proj = hn @ w_in
    o1 = D_SCONV; o2 = 2 * D_SCONV; o3 = 3 * D_SCONV
    o4 = o3 + D_SSM; o5 = o4 + D_XBC
    gate_b = proj[..., :o1]
    gate_c = proj[..., o1:o2]
    hv = proj[..., o2:o3]
    z = proj[..., o3:o4]
    xbc = proj[..., o4:o5]
    dt_raw = proj[..., o5:]
    u = gate_c * hv
    conv_u, new_buf_sc = causal_dwconv(u, buf_sc, sconv_w)
    ya = group_rmsnorm(gate_b * conv_u, sconv_norm, SCONV_GROUPS, h.dtype)
    xbc_c, new_buf_xbc = causal_dwconv(xbc, buf_xbc, ssm_conv_w)
    xbc_c = jax.nn.silu(xbc_c + ssm_conv_b)
    xs = xbc_c[..., :D_SSM].reshape(b, L, SSM_GROUPS, HEADS_PER_GROUP, SSM_HEAD_DIM)
    Bm = xbc_c[..., D_SSM:D_SSM + SSM_GROUPS * SSM_STATE].reshape(b, L, SSM_GROUPS, SSM_STATE)
    Cm = xbc_c[..., D_SSM + SSM_GROUPS * SSM_STATE:].reshape(b, L, SSM_GROUPS, SSM_STATE)
    dt = jax.nn.softplus(dt_raw.astype(jnp.float32) + dt_bias.astype(jnp.float32))
    dt = dt.reshape(b, L, SSM_GROUPS, HEADS_PER_GROUP)
    A = -jnp.exp(A_log.astype(jnp.float32)).reshape(SSM_GROUPS, HEADS_PER_GROUP)
    s = s0.astype(jnp.float32)
    ys = []
    start = 0
    for length, chunk in segments:
        y_seg, s = ssd_chunked(xs[:, start:start + length], dt[:, start:start + length], A,
                               Bm[:, start:start + length], Cm[:, start:start + length], s, chunk)
        ys.append(y_seg)
        start += length
    y = jnp.concatenate(ys, axis=1) if len(ys) > 1 else ys[0]
    y = y + D_skip.astype(jnp.float32).reshape(SSM_GROUPS, HEADS_PER_GROUP)[:, :, None] * xs
    y = y.reshape(b, L, D_SSM) * jax.nn.silu(z.astype(jnp.float32))
    yb = group_rmsnorm(y, ssm_norm, SSM_GROUPS, h.dtype)
    mix = jnp.concatenate([ya, yb], axis=-1) @ w_out
    h = h + rmsnorm(mix, norm_mix_post)
    fn = rmsnorm(h, norm_ffn_pre)
    f = (jax.nn.silu(fn @ w_gate) * (fn @ w_up)) @ w_down
    h = h + rmsnorm(f, norm_ffn_post)
    return h, new_buf_sc, new_buf_xbc, s


def setup_inputs(seed: int = 0) -> dict:
    key = jax.random.key(seed)
    ks = jax.random.split(key, 24)
    f32 = jnp.float32
    n = lambda k, shp, s: jax.random.normal(k, shp, f32) * s
    gain = lambda k, shp: 1.0 + 0.05 * jax.random.normal(k, shp, f32)
    dt0 = jnp.exp(jax.random.uniform(ks[12], (DEPTH, SSM_HEADS), f32, np.log(1e-3), np.log(1e-1)))
    return {
        "x_prompt": n(ks[0], (BATCH, SEQ, D_MODEL), 1.0),
        "x_sample": n(ks[1], (DEC_BATCH, DEC_SEQ, D_MODEL), 1.0),
        "state_sconv": n(ks[2], (DEPTH, DEC_BATCH, SCONV_WIDTH - 1, D_SCONV), 1.0),
        "state_ssm_conv": n(ks[3], (DEPTH, DEC_BATCH, SSM_CONV_WIDTH - 1, D_XBC), 1.0),
        "state_ssm": n(ks[4], (DEPTH, DEC_BATCH, SSM_GROUPS, HEADS_PER_GROUP, SSM_HEAD_DIM, SSM_STATE), 0.5),
        "meta_tokens": n(ks[5], (N_META, D_MODEL), 1.0),
        "norm_mix_pre": gain(ks[6], (DEPTH, D_MODEL)),
        "norm_mix_post": gain(ks[7], (DEPTH, D_MODEL)),
        "w_in": n(ks[8], (DEPTH, D_MODEL, D_IN), D_MODEL ** -0.5),
        "sconv_w": n(ks[9], (DEPTH, SCONV_WIDTH, D_SCONV), SCONV_WIDTH ** -0.5),
        "sconv_norm": gain(ks[10], (DEPTH, D_SCONV)),
        "ssm_conv_w": n(ks[11], (DEPTH, SSM_CONV_WIDTH, D_XBC), SSM_CONV_WIDTH ** -0.5),
        "ssm_conv_b": n(ks[13], (DEPTH, D_XBC), 0.01),
        "dt_bias": dt0 + jnp.log(-jnp.expm1(-dt0)),
        "A_log": jnp.log(jax.random.uniform(ks[14], (DEPTH, SSM_HEADS), f32, 1.0, 16.0)),
        "D_skip": gain(ks[15], (DEPTH, SSM_HEADS)),
        "ssm_norm": gain(ks[16], (DEPTH, D_SSM)),
        "w_out": n(ks[17], (DEPTH, D_MIX, D_MODEL), D_MIX ** -0.5),
        "norm_ffn_pre": gain(ks[18], (DEPTH, D_MODEL)),
        "norm_ffn_post": gain(ks[19], (DEPTH, D_MODEL)),
        "w_gate": n(ks[20], (DEPTH, D_MODEL, D_FF), D_MODEL ** -0.5),
        "w_up": n(ks[21], (DEPTH, D_MODEL, D_FF), D_MODEL ** -0.5),
        "w_down": n(ks[22], (DEPTH, D_FF, D_MODEL), D_FF ** -0.5),
    }


def reference(x_prompt, x_sample, state_sconv, state_ssm_conv, state_ssm, meta_tokens,
              norm_mix_pre, norm_mix_post, w_in, sconv_w, sconv_norm, ssm_conv_w, ssm_conv_b,
              dt_bias, A_log, D_skip, ssm_norm, w_out, norm_ffn_pre, norm_ffn_post,
              w_gate, w_up, w_down):
    bp, seq = x_prompt.shape[:2]
    bs, dseq = x_sample.shape[:2]
    meta = jnp.broadcast_to(meta_tokens[None].astype(x_prompt.dtype), (bp, N_META, D_MODEL))
    hp = jnp.concatenate([meta, x_prompt], axis=1)
    hs = x_sample
    seg_p = ((N_META, N_META), (seq, SSD_CHUNK))
    seg_s = ((dseq, dseq),)
    p_sc, p_xbc, p_ssm, s_sc, s_xbc, s_ssm = [], [], [], [], [], []
    for l in range(DEPTH):
        lw = (norm_mix_pre[l], norm_mix_post[l], w_in[l], sconv_w[l], sconv_norm[l],
              ssm_conv_w[l], ssm_conv_b[l], dt_bias[l], A_log[l], D_skip[l], ssm_norm[l], w_out[l],
              norm_ffn_pre[l], norm_ffn_post[l], w_gate[l], w_up[l], w_down[l])
        hp, b1, b2, st = hybrid_layer(
            hp, jnp.zeros((bp, SCONV_WIDTH - 1, D_SCONV), hp.dtype),
            jnp.zeros((bp, SSM_CONV_WIDTH - 1, D_XBC), hp.dtype),
            jnp.zeros((bp, SSM_GROUPS, HEADS_PER_GROUP, SSM_HEAD_DIM, SSM_STATE), jnp.float32),
            seg_p, *lw)
        p_sc.append(b1); p_xbc.append(b2); p_ssm.append(st)
        hs, b1, b2, st = hybrid_layer(hs, state_sconv[l], state_ssm_conv[l], state_ssm[l], seg_s, *lw)
        s_sc.append(b1); s_xbc.append(b2); s_ssm.append(st)
    y_prompt = hp[:, N_META:]
    y_sample = hs
    return (y_prompt, y_sample, jnp.stack(p_sc), jnp.stack(p_xbc), jnp.stack(p_ssm),
            jnp.stack(s_sc), jnp.stack(s_xbc), jnp.stack(s_ssm))
```

```python
import functools

import jax
import jax.numpy as jnp
from jax import lax
from jax.experimental import pallas as pl
from jax.experimental.pallas import tpu as pltpu

D_MODEL = 1024
N_META = 16
D_SCONV = 512
SCONV_GROUPS = 8
SCONV_WIDTH = 3
SSM_HEAD_DIM = 64
SSM_HEADS = 16
D_SSM = 1024
SSM_GROUPS = 2
HEADS_PER_GROUP = 8
D_GROUP = HEADS_PER_GROUP * SSM_HEAD_DIM
SSM_STATE = 128
SSM_CONV_WIDTH = 4
D_BC = 2 * SSM_GROUPS * SSM_STATE
D_XBC = D_SSM + D_BC
D_MIX = D_SCONV + D_SSM
D_IN = 3 * D_SCONV + D_SSM + D_XBC + SSM_HEADS
D_FF = 2816
EPS = 1e-6

LANES = 128
SUBLANES = 8
D_IN_PAD = D_IN + (LANES - SSM_HEADS)

O_GATE_B = 0
O_GATE_C = D_SCONV
O_HV = 2 * D_SCONV
O_Z = 3 * D_SCONV
O_XS = O_Z + D_SSM
O_BC = O_XS + D_SSM
O_DT = O_BC + D_BC

CHUNK = 128
SEQ_TILE = 16
FFN_ROWS = 512
VMEM_LIMIT = 56 * 1024 * 1024

BF16 = jnp.bfloat16
F32 = jnp.float32


def _dot(a, b):
    return jnp.dot(a, b, preferred_element_type=F32)


def _dot_nt(a, b):
    return lax.dot_general(a, b, (((1,), (1,)), ((), ())), preferred_element_type=F32)


def _dot_tn(a, b):
    return lax.dot_general(a, b, (((0,), (0,)), ((), ())), preferred_element_type=F32)


def _split3(v):
    hi = v.astype(BF16)
    r1 = v - hi.astype(F32)
    mid = r1.astype(BF16)
    lo = (r1 - mid.astype(F32)).astype(BF16)
    return hi, mid, lo


def _dot3(v, m):
    hi, mid, lo = _split3(v)
    return _dot(hi, m) + _dot(mid, m) + _dot(lo, m)


def _dot3_left(m, v):
    hi, mid, lo = _split3(v)
    return _dot(m, hi) + _dot(m, mid) + _dot(m, lo)


def _rms(x, w):
    return x * lax.rsqrt(jnp.mean(x * x, axis=-1, keepdims=True) + EPS) * w


def _silu(x):
    return x * (1.0 / (1.0 + jnp.exp(-x)))


def _softplus(x):
    return jnp.maximum(x, 0.0) + jnp.log1p(jnp.exp(-jnp.abs(x)))


def _shifted(u, prev, j, rows_per_seq):
    t = u.shape[0]
    s = pltpu.roll(u, j, 0)
    if rows_per_seq is None:
        row = lax.broadcasted_iota(jnp.int32, (SUBLANES, u.shape[1]), 0)
        first = jnp.where(row < j, pltpu.roll(prev, j, 0), s[0:SUBLANES])
        if t == SUBLANES:
            return first
        return jnp.concatenate([first, s[SUBLANES:]], axis=0)
    row = lax.broadcasted_iota(jnp.int32, u.shape, 0) % SUBLANES
    return jnp.where(row < j, pltpu.roll(prev, t - SUBLANES + j, 0), s)


def _causal_conv(u, prev, w, rows_per_seq):
    k = w.shape[0]
    out = _shifted(u, prev, k - 1, rows_per_seq) * w[0:1]
    for i in range(1, k - 1):
        out = out + _shifted(u, prev, k - 1 - i, rows_per_seq) * w[i:i + 1]
    return out + u * w[k - 1:k]


def _sconv_mixer(gate_b, gate_c, hv, prev, sconv_w, sconv_norm, gmat, rows_per_seq):
    u = gate_c * hv
    conv_u = _causal_conv(u, prev, sconv_w, rows_per_seq)
    v = gate_b * conv_u
    sq = v * v
    hi = sq.astype(BF16)
    lo = (sq - hi.astype(F32)).astype(BF16)
    group_sum = _dot(hi, gmat) + _dot(lo, gmat)
    ya = v * lax.rsqrt(group_sum * (SCONV_GROUPS / D_SCONV) + EPS) * sconv_norm
    return ya, u


def _ssd_prepare(dt_raw, dt_bias, a_log, tri, t):
    lane = lax.broadcasted_iota(jnp.int32, (t, LANES), 1)
    valid = lane < SSM_HEADS
    dt = jnp.where(valid, _softplus(dt_raw + dt_bias), 0.0)
    a = dt * (-jnp.exp(a_log))
    a_cum = _dot3_left(tri, a)
    return dt, a, a_cum


def _state_decay(a_total_row, expand_t):
    q = jnp.broadcast_to(jnp.exp(a_total_row), (LANES, LANES))
    hi, mid, lo = _split3(q)
    return _dot_nt(expand_t, hi) + _dot_nt(expand_t, mid) + _dot_nt(expand_t, lo)


def _group_rms_ssm(y, w):
    parts = []
    for g in range(SSM_GROUPS):
        yg = y[:, g * D_GROUP:(g + 1) * D_GROUP]
        parts.append(yg * lax.rsqrt(jnp.mean(yg * yg, axis=-1, keepdims=True) + EPS))
    return jnp.concatenate(parts, axis=1) * w


def _ssd_intra(xdt_b, bm_b, cm_b, a_cum, mask, t):
    a_cum_t = a_cum.T
    lane = lax.broadcasted_iota(jnp.int32, (t, LANES), 1)
    first_half = lane < SSM_HEAD_DIM
    pairs = []
    for g in range(SSM_GROUPS):
        cb = _dot_nt(cm_b[:, g * SSM_STATE:(g + 1) * SSM_STATE], bm_b[:, g * SSM_STATE:(g + 1) * SSM_STATE])
        for q in range(HEADS_PER_GROUP // 2):
            res = []
            for e in (g * HEADS_PER_GROUP + 2 * q, g * HEADS_PER_GROUP + 2 * q + 1):
                seg = a_cum[:, e:e + 1] - a_cum_t[e:e + 1, :]
                decay = jnp.exp(jnp.where(mask, seg, -jnp.inf))
                m_e = (cb * decay).astype(BF16)
                c0 = (e // 2) * LANES
                res.append(_dot(m_e, xdt_b[:, c0:c0 + LANES]))
            pairs.append(jnp.where(first_half, res[0], res[1]))
    return jnp.concatenate(pairs, axis=1)


def _ssm_tail(y, xs, z, d_full, ssm_norm):
    y = y + d_full * xs
    y = y * _silu(z)
    return _group_rms_ssm(y, ssm_norm)


def _inproj_kernel(x_ref, npre_ref, w_in_ref, o_ref):
    hn = _rms(x_ref[...], npre_ref[...]).astype(BF16)
    o_ref[...] = _dot(hn, w_in_ref[...])


def _inproj(x, npre, w_in_b, rows):
    m = x.shape[0]
    return pl.pallas_call(
        _inproj_kernel,
        grid=(m // rows,),
        in_specs=[
            pl.BlockSpec((rows, D_MODEL), lambda i: (i, 0)),
            pl.BlockSpec((1, D_MODEL), lambda i: (0, 0)),
            pl.BlockSpec((D_MODEL, D_IN_PAD), lambda i: (0, 0)),
        ],
        out_specs=pl.BlockSpec((rows, D_IN_PAD), lambda i: (i, 0)),
        out_shape=jax.ShapeDtypeStruct((m, D_IN_PAD), F32),
        compiler_params=pltpu.CompilerParams(dimension_semantics=("arbitrary",), vmem_limit_bytes=VMEM_LIMIT),
        name="inproj",
    )(x, npre, w_in_b)


def _meta_kernel(proj_ref, xconv_w_ref, xconv_b_ref, dt_bias_ref, a_log_ref, tri_ref, expand_ref,
                 sc_tail_ref, xbc_tail_ref, state_ref):
    t = N_META
    proj = proj_ref[...]
    u = proj[:, O_GATE_C:O_GATE_C + D_SCONV] * proj[:, O_HV:O_HV + D_SCONV]
    sc_tail_ref[...] = u[t - SUBLANES:t]
    xbc = proj[:, O_XS:O_XS + D_XBC]
    xbc_tail_ref[...] = xbc[t - SUBLANES:t]
    zeros_prev = jnp.zeros((SUBLANES, D_XBC), F32)
    xbc_c = _silu(_causal_conv(xbc, zeros_prev, xconv_w_ref[...], None) + xconv_b_ref[...])
    xs = xbc_c[:, 0:D_SSM]
    dt, _, a_cum = _ssd_prepare(proj[:, O_DT:O_DT + LANES], dt_bias_ref[...], a_log_ref[...],
                                tri_ref[0:t, 0:t], t)
    w = jnp.exp(a_cum[t - 1:t, :] - a_cum) * dt
    xw_b = (xs * _dot3(w, expand_ref[...])).astype(BF16)
    for g in range(SSM_GROUPS):
        bm_b = xbc_c[:, D_SSM + g * SSM_STATE:D_SSM + (g + 1) * SSM_STATE].astype(BF16)
        state_ref[g * D_GROUP:(g + 1) * D_GROUP, :] = _dot_tn(xw_b[:, g * D_GROUP:(g + 1) * D_GROUP], bm_b)


def _meta_state(proj_meta, xconv_w, xconv_b, dt_bias, a_log, tri, expand):
    full = lambda shape: pl.BlockSpec(shape, lambda: tuple(0 for _ in shape))
    return pl.pallas_call(
        _meta_kernel,
        in_specs=[full(proj_meta.shape), full(xconv_w.shape), full(xconv_b.shape), full(dt_bias.shape),
                  full(a_log.shape), full(tri.shape), full(expand.shape)],
        out_specs=[full((SUBLANES, D_SCONV)), full((SUBLANES, D_XBC)), full((D_SSM, SSM_STATE))],
        out_shape=[jax.ShapeDtypeStruct((SUBLANES, D_SCONV), F32),
                   jax.ShapeDtypeStruct((SUBLANES, D_XBC), F32),
                   jax.ShapeDtypeStruct((D_SSM, SSM_STATE), F32)],
        name="meta_state",
    )(proj_meta, xconv_w, xconv_b, dt_bias, a_log, tri, expand)


def _pmix_kernel(x_ref, npre_ref, w_in_ref, sconv_w_ref, sconv_norm_ref, gmat_ref,
                 xconv_w_ref, xconv_b_ref, dt_bias_ref, a_log_ref, d_full_ref, ssm_norm_ref,
                 tri_ref, expand_ref, expand_t_ref, w_out_ref, npost_ref,
                 sc0_ref, xbc0_ref, state0_ref,
                 h1_ref, sc_tail_ref, xbc_tail_ref, state_ref):
    t = CHUNK

    @pl.when(pl.program_id(1) == 0)
    def _():
        sc_tail_ref[0] = sc0_ref[...]
        xbc_tail_ref[0] = xbc0_ref[...]
        state_ref[0] = state0_ref[...]

    h = x_ref[0]
    hn = _rms(h, npre_ref[...]).astype(BF16)

    def proj(off, width):
        return _dot(hn, w_in_ref[:, off:off + width])

    ya, u = _sconv_mixer(proj(O_GATE_B, D_SCONV), proj(O_GATE_C, D_SCONV), proj(O_HV, D_SCONV),
                         sc_tail_ref[0], sconv_w_ref[...], sconv_norm_ref[...], gmat_ref[...], None)
    sc_tail_ref[0] = u[t - SUBLANES:t]

    xbc = proj(O_XS, D_XBC)
    xbc_c = _silu(_causal_conv(xbc, xbc_tail_ref[0], xconv_w_ref[...], None) + xconv_b_ref[...])
    xbc_tail_ref[0] = xbc[t - SUBLANES:t]
    xs = xbc_c[:, 0:D_SSM]
    bm_b = xbc_c[:, D_SSM:D_SSM + SSM_GROUPS * SSM_STATE].astype(BF16)
    cm_b = xbc_c[:, D_SSM + SSM_GROUPS * SSM_STATE:D_XBC].astype(BF16)
    tri = tri_ref[...]
    dt, _, a_cum = _ssd_prepare(proj(O_DT, LANES), dt_bias_ref[...], a_log_ref[...], tri, t)
    expand = expand_ref[...]
    a_last = a_cum[t - 1:t, :]
    xdt_b = (xs * _dot3(dt, expand)).astype(BF16)
    xw_b = (xs * _dot3(jnp.exp(a_last - a_cum) * dt, expand)).astype(BF16)
    ea_full = _dot3(jnp.exp(a_cum), expand)
    state_decay = _state_decay(a_last, expand_t_ref[...])

    causal = lax.broadcasted_iota(jnp.int32, (t, t), 0) >= lax.broadcasted_iota(jnp.int32, (t, t), 1)
    y = _ssd_intra(xdt_b, bm_b, cm_b, a_cum, causal, t)
    y_off = []
    for g in range(SSM_GROUPS):
        rows = slice(g * D_GROUP, (g + 1) * D_GROUP)
        s_prev = state_ref[0, rows, :]
        cm_g = cm_b[:, g * SSM_STATE:(g + 1) * SSM_STATE]
        bm_g = bm_b[:, g * SSM_STATE:(g + 1) * SSM_STATE]
        y_off.append(_dot_nt(cm_g, s_prev.astype(BF16)))
        state_ref[0, rows, :] = state_decay[rows, :] * s_prev + _dot_tn(xw_b[:, rows], bm_g)
    y = y + jnp.concatenate(y_off, axis=1) * ea_full
    yb = _ssm_tail(y, xs, proj(O_Z, D_SSM), d_full_ref[...], ssm_norm_ref[...])

    mix = _dot(ya.astype(BF16), w_out_ref[0:D_SCONV, :]) + _dot(yb.astype(BF16), w_out_ref[D_SCONV:D_MIX, :])
    h1_ref[0] = h + _rms(mix, npost_ref[...])


def _pmix(x_prompt, consts, sc0, xbc0, state0):
    batch, seq, _ = x_prompt.shape
    const_spec = lambda a: pl.BlockSpec(a.shape, lambda b, c: (0, 0))
    ins = list(consts) + [sc0, xbc0, state0]
    return pl.pallas_call(
        _pmix_kernel,
        grid=(batch, seq // CHUNK),
        in_specs=[pl.BlockSpec((1, CHUNK, D_MODEL), lambda b, c: (b, c, 0))] + [const_spec(a) for a in ins],
        out_specs=[
            pl.BlockSpec((1, CHUNK, D_MODEL), lambda b, c: (b, c, 0)),
            pl.BlockSpec((1, SUBLANES, D_SCONV), lambda b, c: (b, 0, 0)),
            pl.BlockSpec((1, SUBLANES, D_XBC), lambda b, c: (b, 0, 0)),
            pl.BlockSpec((1, D_SSM, SSM_STATE), lambda b, c: (b, 0, 0)),
        ],
        out_shape=[
            jax.ShapeDtypeStruct((batch, seq, D_MODEL), F32),
            jax.ShapeDtypeStruct((batch, SUBLANES, D_SCONV), F32),
            jax.ShapeDtypeStruct((batch, SUBLANES, D_XBC), F32),
            jax.ShapeDtypeStruct((batch, D_SSM, SSM_STATE), F32),
        ],
        compiler_params=pltpu.CompilerParams(dimension_semantics=("arbitrary", "arbitrary"),
                                             vmem_limit_bytes=VMEM_LIMIT),
        name="prompt_mixer",
    )(x_prompt, *ins)


def _smix_kernel(proj_ref, sc_prev_ref, xbc_prev_ref, state_in_ref,
                 sconv_w_ref, sconv_norm_ref, gmat_ref, xconv_w_ref, xconv_b_ref,
                 dt_bias_ref, a_log_ref, d_full_ref, ssm_norm_ref, tri_ref, expand_ref, expand_t_ref,
                 mix_ref, u_ref, xbc_ref, state_ref, *, dec_seq):
    t = SEQ_TILE * dec_seq
    proj = proj_ref[...]
    ya, u = _sconv_mixer(proj[:, O_GATE_B:O_GATE_B + D_SCONV], proj[:, O_GATE_C:O_GATE_C + D_SCONV],
                         proj[:, O_HV:O_HV + D_SCONV], sc_prev_ref[...], sconv_w_ref[...],
                         sconv_norm_ref[...], gmat_ref[...], dec_seq)
    u_ref[...] = u
    xbc = proj[:, O_XS:O_XS + D_XBC]
    xbc_ref[...] = xbc
    xbc_c = _silu(_causal_conv(xbc, xbc_prev_ref[...], xconv_w_ref[...], dec_seq) + xconv_b_ref[...])
    xs = xbc_c[:, 0:D_SSM]
    bm = xbc_c[:, D_SSM:D_SSM + SSM_GROUPS * SSM_STATE]
    bm_b = bm.astype(BF16)
    cm_b = xbc_c[:, D_SSM + SSM_GROUPS * SSM_STATE:D_XBC].astype(BF16)

    row_seq = lax.broadcasted_iota(jnp.int32, (t, t), 0) // dec_seq
    col_seq = lax.broadcasted_iota(jnp.int32, (t, t), 1) // dec_seq
    causal = lax.broadcasted_iota(jnp.int32, (t, t), 0) >= lax.broadcasted_iota(jnp.int32, (t, t), 1)
    same_seq = row_seq == col_seq
    mask = jnp.logical_and(causal, same_seq)
    tri = jnp.where(mask, 1.0, 0.0).astype(BF16)

    dt, a, a_cum = _ssd_prepare(proj[:, O_DT:O_DT + LANES], dt_bias_ref[...], a_log_ref[...], tri, t)
    expand = expand_ref[...]
    expand_t = expand_t_ref[...]
    seq_ones = jnp.where(same_seq, 1.0, 0.0).astype(BF16)
    a_total = _dot3_left(seq_ones, a)
    xdt_b = (xs * _dot3(dt, expand)).astype(BF16)
    xw_b = (xs * _dot3(jnp.exp(a_total - a_cum) * dt, expand)).astype(BF16)
    ea_full = _dot3(jnp.exp(a_cum), expand)

    y = _ssd_intra(xdt_b, bm_b, cm_b, a_cum, mask, t)
    row_id = lax.broadcasted_iota(jnp.int32, (t, SSM_STATE), 0) // dec_seq
    y_off_rows = []
    for s in range(SEQ_TILE):
        r0 = s * dec_seq
        state_decay = _state_decay(a_total[r0:r0 + 1, :], expand_t)
        y_off_g = []
        for g in range(SSM_GROUPS):
            rows = slice(g * D_GROUP, (g + 1) * D_GROUP)
            s_prev = state_in_ref[s, rows, :]
            cm_g = cm_b[:, g * SSM_STATE:(g + 1) * SSM_STATE]
            y_off_g.append(_dot_nt(cm_g, s_prev.astype(BF16))[r0:r0 + dec_seq, :])
            bm_g = jnp.where(row_id == s, bm[:, g * SSM_STATE:(g + 1) * SSM_STATE], 0.0).astype(BF16)
            state_ref[s, rows, :] = state_decay[rows, :] * s_prev + _dot_tn(xw_b[:, rows], bm_g)
        y_off_rows.append(jnp.concatenate(y_off_g, axis=1))
    y = y + jnp.concatenate(y_off_rows, axis=0) * ea_full
    yb = _ssm_tail(y, xs, proj[:, O_Z:O_Z + D_SSM], d_full_ref[...], ssm_norm_ref[...])
    mix_ref[:, 0:D_SCONV] = ya
    mix_ref[:, D_SCONV:D_MIX] = yb


def _smix(proj, sc_prev, xbc_prev, state, consts, dec_seq):
    n_seq = state.shape[0]
    t = SEQ_TILE * dec_seq
    const_spec = lambda a: pl.BlockSpec(a.shape, lambda i: (0, 0))
    return pl.pallas_call(
        functools.partial(_smix_kernel, dec_seq=dec_seq),
        grid=(n_seq // SEQ_TILE,),
        in_specs=[
            pl.BlockSpec((t, D_IN_PAD), lambda i: (i, 0)),
            pl.BlockSpec((t, D_SCONV), lambda i: (i, 0)),
            pl.BlockSpec((t, D_XBC), lambda i: (i, 0)),
            pl.BlockSpec((SEQ_TILE, D_SSM, SSM_STATE), lambda i: (i, 0, 0)),
        ] + [const_spec(a) for a in consts],
        out_specs=[
            pl.BlockSpec((t, D_MIX), lambda i: (i, 0)),
            pl.BlockSpec((t, D_SCONV), lambda i: (i, 0)),
            pl.BlockSpec((t, D_XBC), lambda i: (i, 0)),
            pl.BlockSpec((SEQ_TILE, D_SSM, SSM_STATE), lambda i: (i, 0, 0)),
        ],
        out_shape=[
            jax.ShapeDtypeStruct((n_seq * dec_seq, D_MIX), F32),
            jax.ShapeDtypeStruct((n_seq * dec_seq, D_SCONV), F32),
            jax.ShapeDtypeStruct((n_seq * dec_seq, D_XBC), F32),
            jax.ShapeDtypeStruct(state.shape, F32),
        ],
        compiler_params=pltpu.CompilerParams(dimension_semantics=("arbitrary",), vmem_limit_bytes=VMEM_LIMIT),
        name="sample_mixer",
    )(proj, sc_prev, xbc_prev, state, *consts)


def _ffn_body(h1, npre_ref, w_gate_ref, w_up_ref, w_down_ref, npost_ref, o_ref):
    fn = _rms(h1, npre_ref[...]).astype(BF16)
    act = (_silu(_dot(fn, w_gate_ref[...])) * _dot(fn, w_up_ref[...])).astype(BF16)
    o_ref[...] = h1 + _rms(_dot(act, w_down_ref[...]), npost_ref[...])


def _ffn_kernel(h_ref, npre_ref, w_gate_ref, w_up_ref, w_down_ref, npost_ref, o_ref):
    _ffn_body(h_ref[...], npre_ref, w_gate_ref, w_up_ref, w_down_ref, npost_ref, o_ref)


def _outproj_ffn_kernel(h_ref, mix_ref, w_out_ref, nmix_ref, npre_ref, w_gate_ref, w_up_ref, w_down_ref,
                        npost_ref, o_ref):
    mix = _dot(mix_ref[...].astype(BF16), w_out_ref[...])
    h1 = h_ref[...] + _rms(mix, nmix_ref[...])
    _ffn_body(h1, npre_ref, w_gate_ref, w_up_ref, w_down_ref, npost_ref, o_ref)


def _ffn(h, ffn_consts, mix=None, mix_consts=()):
    m = h.shape[0]
    rows = min(FFN_ROWS, m)
    row_spec = lambda width: pl.BlockSpec((rows, width), lambda i: (i, 0))
    const_spec = lambda a: pl.BlockSpec(a.shape, lambda i: (0, 0), pipeline_mode=pl.Buffered(1))
    if mix is None:
        kern, ins = _ffn_kernel, [h] + list(ffn_consts)
        specs = [row_spec(D_MODEL)] + [const_spec(a) for a in ffn_consts]
    else:
        kern, ins = _outproj_ffn_kernel, [h, mix] + list(mix_consts) + list(ffn_consts)
        specs = [row_spec(D_MODEL), row_spec(D_MIX)] + [const_spec(a) for a in list(mix_consts) + list(ffn_consts)]
    return pl.pallas_call(
        kern,
        grid=(m // rows,),
        in_specs=specs,
        out_specs=row_spec(D_MODEL),
        out_shape=jax.ShapeDtypeStruct((m, D_MODEL), F32),
        compiler_params=pltpu.CompilerParams(dimension_semantics=("arbitrary",), vmem_limit_bytes=VMEM_LIMIT),
        name="ffn" if mix is None else "outproj_ffn",
    )(*ins)


def _row(v, width=None):
    v = v.reshape(1, -1).astype(F32)
    if width is not None and v.shape[1] < width:
        v = jnp.pad(v, ((0, 0), (0, width - v.shape[1])))
    return v


def _layer(hp, hs, sc_state, xbc_state, ssm_state, meta_tokens,
           norm_mix_pre, norm_mix_post, w_in, sconv_w, sconv_norm, ssm_conv_w, ssm_conv_b,
           dt_bias, a_log, d_skip, ssm_norm, w_out, norm_ffn_pre, norm_ffn_post, w_gate, w_up, w_down):
    batch, seq, _ = hp.shape
    n_seq, dec_seq, _ = hs.shape

    w_in_b = jnp.pad(w_in, ((0, 0), (0, D_IN_PAD - D_IN))).astype(BF16)
    w_out_b = w_out.astype(BF16)
    w_gate_b, w_up_b, w_down_b = w_gate.astype(BF16), w_up.astype(BF16), w_down.astype(BF16)
    npre, npost = _row(norm_mix_pre), _row(norm_mix_post)
    fpre, fpost = _row(norm_ffn_pre), _row(norm_ffn_post)
    sconv_w = sconv_w.astype(F32)
    xconv_w = ssm_conv_w.astype(F32)
    xconv_b = _row(ssm_conv_b)
    dt_bias_p, a_log_p = _row(dt_bias, LANES), _row(a_log, LANES)
    d_full = jnp.repeat(d_skip.astype(F32), SSM_HEAD_DIM).reshape(1, D_SSM)
    sconv_norm_r, ssm_norm_r = _row(sconv_norm), _row(ssm_norm)

    r = jnp.arange(CHUNK)
    tri = (r[:, None] >= r[None, :]).astype(BF16)
    expand = (jnp.arange(LANES)[:, None] == (jnp.arange(D_SSM)[None, :] // SSM_HEAD_DIM)).astype(BF16)
    gsz = D_SCONV // SCONV_GROUPS
    gmat = ((jnp.arange(D_SCONV)[:, None] // gsz) == (jnp.arange(D_SCONV)[None, :] // gsz)).astype(BF16)

    expand_t = expand.T
    mixer_consts = (sconv_w, sconv_norm_r, gmat, xconv_w, xconv_b, dt_bias_p, a_log_p, d_full, ssm_norm_r,
                    tri, expand, expand_t)
    ffn_consts = (fpre, w_gate_b, w_up_b, w_down_b, fpost)

    proj_meta = _inproj(meta_tokens.astype(F32), npre, w_in_b, N_META)
    sc0, xbc0, state0 = _meta_state(proj_meta, xconv_w, xconv_b, dt_bias_p, a_log_p, tri, expand)

    pm_consts = (npre, w_in_b) + mixer_consts + (w_out_b, npost)
    h1, sc_tail, xbc_tail, p_state = _pmix(hp, pm_consts, sc0, xbc0, state0)
    y_prompt = _ffn(h1.reshape(batch * seq, D_MODEL), ffn_consts).reshape(batch, seq, D_MODEL)
    new_sc_p = sc_tail[:, SUBLANES - (SCONV_WIDTH - 1):, :]
    new_xbc_p = xbc_tail[:, SUBLANES - (SSM_CONV_WIDTH - 1):, :]
    new_ssm_p = p_state.reshape(batch, SSM_GROUPS, HEADS_PER_GROUP, SSM_HEAD_DIM, SSM_STATE)

    hs_rows = hs.reshape(n_seq * dec_seq, D_MODEL)
    proj_s = _inproj(hs_rows, npre, w_in_b, 256)
    sc_prev = jnp.pad(sc_state, ((0, 0), (dec_seq - (SCONV_WIDTH - 1), 0), (0, 0))).reshape(n_seq * dec_seq, D_SCONV)
    xbc_prev = jnp.pad(xbc_state, ((0, 0), (dec_seq - (SSM_CONV_WIDTH - 1), 0), (0, 0))).reshape(n_seq * dec_seq, D_XBC)
    mix_s, u_s, xbc_s, s_state = _smix(proj_s, sc_prev, xbc_prev, ssm_state.reshape(n_seq, D_SSM, SSM_STATE),
                                       mixer_consts, dec_seq)
    y_sample = _ffn(hs_rows, ffn_consts, mix_s, (w_out_b, npost)).reshape(n_seq, dec_seq, D_MODEL)
    new_sc_s = u_s.reshape(n_seq, dec_seq, D_SCONV)[:, dec_seq - (SCONV_WIDTH - 1):, :]
    new_xbc_s = xbc_s.reshape(n_seq, dec_seq, D_XBC)[:, dec_seq - (SSM_CONV_WIDTH - 1):, :]
    new_ssm_s = s_state.reshape(n_seq, SSM_GROUPS, HEADS_PER_GROUP, SSM_HEAD_DIM, SSM_STATE)
    return y_prompt, y_sample, new_sc_p, new_xbc_p, new_ssm_p, new_sc_s, new_xbc_s, new_ssm_s


def kernel(x_prompt, x_sample, state_sconv, state_ssm_conv, state_ssm, meta_tokens, norm_mix_pre, norm_mix_post, w_in, sconv_w, sconv_norm, ssm_conv_w, ssm_conv_b, dt_bias, A_log, D_skip, ssm_norm, w_out, norm_ffn_pre, norm_ffn_post, w_gate, w_up, w_down):
    depth = w_in.shape[0]
    assert depth == 1 and x_sample.shape[1] == SUBLANES
    outs = _layer(x_prompt, x_sample, state_sconv[0], state_ssm_conv[0], state_ssm[0], meta_tokens,
                  norm_mix_pre[0], norm_mix_post[0], w_in[0], sconv_w[0], sconv_norm[0], ssm_conv_w[0],
                  ssm_conv_b[0], dt_bias[0], A_log[0], D_skip[0], ssm_norm[0], w_out[0],
                  norm_ffn_pre[0], norm_ffn_post[0], w_gate[0], w_up[0], w_down[0])
    y_prompt, y_sample = outs[0], outs[1]
    return (y_prompt, y_sample) + tuple(o[None] for o in outs[2:])
```

```python
import functools

import jax
import jax.numpy as jnp
from jax import lax
from jax.experimental import pallas as pl
from jax.experimental.pallas import tpu as pltpu

D_MODEL = 1024
N_META = 16
D_SCONV = 512
SCONV_GROUPS = 8
SCONV_WIDTH = 3
SSM_HEAD_DIM = 64
SSM_HEADS = 16
D_SSM = 1024
SSM_GROUPS = 2
HEADS_PER_GROUP = 8
D_GROUP = HEADS_PER_GROUP * SSM_HEAD_DIM
SSM_STATE = 128
SSM_CONV_WIDTH = 4
D_BC = 2 * SSM_GROUPS * SSM_STATE
D_XBC = D_SSM + D_BC
D_MIX = D_SCONV + D_SSM
D_IN = 3 * D_SCONV + D_SSM + D_XBC + SSM_HEADS
D_FF = 2816
EPS = 1e-6

LANES = 128
SUBLANES = 8
D_IN_PAD = D_IN + (LANES - SSM_HEADS)
HEAD_PAIRS = D_SSM // LANES

O_GATE_B = 0
O_GATE_C = D_SCONV
O_HV = 2 * D_SCONV
O_Z = 3 * D_SCONV
O_XS = O_Z + D_SSM
O_BC = O_XS + D_SSM
O_DT = O_BC + D_BC

CHUNK = 128
SEQ_TILE = 16
FFN_ROWS = 512
VMEM_LIMIT = 56 * 1024 * 1024

BF16 = jnp.bfloat16
F32 = jnp.float32


def _dot(a, b):
    return jnp.dot(a, b, preferred_element_type=F32)


def _dot_nt(a, b):
    return lax.dot_general(a, b, (((1,), (1,)), ((), ())), preferred_element_type=F32)


def _dot_tn(a, b):
    return lax.dot_general(a, b, (((0,), (0,)), ((), ())), preferred_element_type=F32)


def _split3(v):
    hi = v.astype(BF16)
    r1 = v - hi.astype(F32)
    mid = r1.astype(BF16)
    lo = (r1 - mid.astype(F32)).astype(BF16)
    return hi, mid, lo


def _dot3(v, m):
    hi, mid, lo = _split3(v)
    return _dot(hi, m) + _dot(mid, m) + _dot(lo, m)


def _dot3_left(m, v):
    hi, mid, lo = _split3(v)
    return _dot(m, hi) + _dot(m, mid) + _dot(m, lo)


def _rms(x, w):
    return x * lax.rsqrt(jnp.mean(x * x, axis=-1, keepdims=True) + EPS) * w


def _silu(x):
    return x * (1.0 / (1.0 + jnp.exp(-x)))


def _softplus(x):
    return jnp.maximum(x, 0.0) + jnp.log1p(jnp.exp(-jnp.abs(x)))


def _shifted(u, prev, j, rows_per_seq):
    t = u.shape[0]
    s = pltpu.roll(u, j, 0)
    if rows_per_seq is None:
        row = lax.broadcasted_iota(jnp.int32, (SUBLANES, u.shape[1]), 0)
        first = jnp.where(row < j, pltpu.roll(prev, j, 0), s[0:SUBLANES])
        if t == SUBLANES:
            return first
        return jnp.concatenate([first, s[SUBLANES:]], axis=0)
    row = lax.broadcasted_iota(jnp.int32, u.shape, 0) % SUBLANES
    return jnp.where(row < j, pltpu.roll(prev, t - SUBLANES + j, 0), s)


def _causal_conv(u, prev, w, rows_per_seq):
    k = w.shape[0]
    out = _shifted(u, prev, k - 1, rows_per_seq) * w[0:1]
    for i in range(1, k - 1):
        out = out + _shifted(u, prev, k - 1 - i, rows_per_seq) * w[i:i + 1]
    return out + u * w[k - 1:k]


def _sconv_mixer(gate_b, gate_c, hv, prev, sconv_w, sconv_norm, gmat, rows_per_seq):
    u = gate_c * hv
    conv_u = _causal_conv(u, prev, sconv_w, rows_per_seq)
    v = gate_b * conv_u
    sq = v * v
    hi = sq.astype(BF16)
    lo = (sq - hi.astype(F32)).astype(BF16)
    group_sum = _dot(hi, gmat) + _dot(lo, gmat)
    ya = v * lax.rsqrt(group_sum * (SCONV_GROUPS / D_SCONV) + EPS) * sconv_norm
    return ya, u


def _ssd_prepare(dt_raw, dt_bias, a_log, tri, t):
    lane = lax.broadcasted_iota(jnp.int32, (t, LANES), 1)
    valid = lane < SSM_HEADS
    dt = jnp.where(valid, _softplus(dt_raw + dt_bias), 0.0)
    a = dt * (-jnp.exp(a_log))
    a_cum = _dot3_left(tri, a)
    return dt, a_cum


def _group_rms_ssm(y, w):
    parts = []
    for g in range(SSM_GROUPS):
        yg = y[:, g * D_GROUP:(g + 1) * D_GROUP]
        parts.append(yg * lax.rsqrt(jnp.mean(yg * yg, axis=-1, keepdims=True) + EPS))
    return jnp.concatenate(parts, axis=1) * w


def _ssd_heads(xs, bm_b, cm, a_cum, dt, w, mask, state_t=None):
    t = xs.shape[0]
    assert t == LANES
    a_cum_t = a_cum.T
    dt_t = dt.T
    first_half = lax.broadcasted_iota(jnp.int32, (t, LANES), 1) < SSM_HEAD_DIM
    xs_b = xs.astype(BF16)
    y_pairs, xw_pairs, ea_pairs = [], [], []
    for g in range(SSM_GROUPS):
        cm_g = cm[:, g * SSM_STATE:(g + 1) * SSM_STATE]
        cb = _dot_nt(cm_g.astype(BF16), bm_b[:, g * SSM_STATE:(g + 1) * SSM_STATE])
        for q in range(HEADS_PER_GROUP // 2):
            pair = g * (HEADS_PER_GROUP // 2) + q
            cols = slice(pair * LANES, (pair + 1) * LANES)
            rhs = xs_b[:, cols]
            if state_t is not None:
                rhs = jnp.concatenate([rhs, state_t[:, cols].astype(BF16)], axis=0)
            res, eas, ws = [], [], []
            for e in (2 * pair, 2 * pair + 1):
                a_col = jnp.broadcast_to(a_cum[:, e:e + 1], (t, LANES))
                decay = jnp.exp(jnp.where(mask, a_col - a_cum_t[e:e + 1, :], -jnp.inf))
                lhs = (cb * decay * dt_t[e:e + 1, :]).astype(BF16)
                ea = jnp.exp(a_col)
                if state_t is not None:
                    lhs = jnp.concatenate([lhs, (cm_g * ea).astype(BF16)], axis=1)
                res.append(_dot(lhs, rhs))
                eas.append(ea)
                ws.append(jnp.broadcast_to(w[:, e:e + 1], (t, LANES)))
            y_pairs.append(jnp.where(first_half, res[0], res[1]))
            xw_pairs.append((xs[:, cols] * jnp.where(first_half, ws[0], ws[1])).astype(BF16))
            ea_pairs.append(jnp.where(first_half, eas[0], eas[1]))
    return (jnp.concatenate(y_pairs, axis=1), jnp.concatenate(xw_pairs, axis=1),
            jnp.concatenate(ea_pairs, axis=1))


def _ssm_tail(y, xs, z, d_full, ssm_norm):
    y = y + d_full * xs
    y = y * _silu(z)
    return _group_rms_ssm(y, ssm_norm)


def _inproj_kernel(x_ref, npre_ref, w_in_ref, o_ref):
    hn = _rms(x_ref[...], npre_ref[...]).astype(BF16)
    o_ref[...] = _dot(hn, w_in_ref[...])


def _inproj(x, npre, w_in_b, rows):
    m = x.shape[0]
    return pl.pallas_call(
        _inproj_kernel,
        grid=(m // rows,),
        in_specs=[
            pl.BlockSpec((rows, D_MODEL), lambda i: (i, 0)),
            pl.BlockSpec((1, D_MODEL), lambda i: (0, 0)),
            pl.BlockSpec((D_MODEL, D_IN_PAD), lambda i: (0, 0)),
        ],
        out_specs=pl.BlockSpec((rows, D_IN_PAD), lambda i: (i, 0)),
        out_shape=jax.ShapeDtypeStruct((m, D_IN_PAD), F32),
        compiler_params=pltpu.CompilerParams(dimension_semantics=("arbitrary",), vmem_limit_bytes=VMEM_LIMIT),
        name="inproj",
    )(x, npre, w_in_b)


def _meta_kernel(proj_ref, xconv_w_ref, xconv_b_ref, dt_bias_ref, a_log_ref, tri_ref, expand_ref,
                 sc_tail_ref, xbc_tail_ref, state_t_ref):
    t = N_META
    proj = proj_ref[...]
    u = proj[:, O_GATE_C:O_GATE_C + D_SCONV] * proj[:, O_HV:O_HV + D_SCONV]
    sc_tail_ref[...] = u[t - SUBLANES:t]
    xbc = proj[:, O_XS:O_XS + D_XBC]
    xbc_tail_ref[...] = xbc[t - SUBLANES:t]
    zeros_prev = jnp.zeros((SUBLANES, D_XBC), F32)
    xbc_c = _silu(_causal_conv(xbc, zeros_prev, xconv_w_ref[...], None) + xconv_b_ref[...])
    xs = xbc_c[:, 0:D_SSM]
    dt, a_cum = _ssd_prepare(proj[:, O_DT:O_DT + LANES], dt_bias_ref[...], a_log_ref[...],
                             tri_ref[0:t, 0:t], t)
    w = jnp.exp(a_cum[t - 1:t, :] - a_cum) * dt
    xw_b = (xs * _dot3(w, expand_ref[...])).astype(BF16)
    for g in range(SSM_GROUPS):
        bm_b = xbc_c[:, D_SSM + g * SSM_STATE:D_SSM + (g + 1) * SSM_STATE].astype(BF16)
        state_t_ref[:, g * D_GROUP:(g + 1) * D_GROUP] = _dot_tn(bm_b, xw_b[:, g * D_GROUP:(g + 1) * D_GROUP])


def _meta_state(proj_meta, xconv_w, xconv_b, dt_bias, a_log, tri, expand):
    full = lambda shape: pl.BlockSpec(shape, lambda: tuple(0 for _ in shape))
    return pl.pallas_call(
        _meta_kernel,
        in_specs=[full(proj_meta.shape), full(xconv_w.shape), full(xconv_b.shape), full(dt_bias.shape),
                  full(a_log.shape), full(tri.shape), full(expand.shape)],
        out_specs=[full((SUBLANES, D_SCONV)), full((SUBLANES, D_XBC)), full((SSM_STATE, D_SSM))],
        out_shape=[jax.ShapeDtypeStruct((SUBLANES, D_SCONV), F32),
                   jax.ShapeDtypeStruct((SUBLANES, D_XBC), F32),
                   jax.ShapeDtypeStruct((SSM_STATE, D_SSM), F32)],
        name="meta_state",
    )(proj_meta, xconv_w, xconv_b, dt_bias, a_log, tri, expand)


def _pmix_kernel(x_ref, npre_ref, w_in_ref, sconv_w_ref, sconv_norm_ref, gmat_ref,
                 xconv_w_ref, xconv_b_ref, dt_bias_ref, a_log_ref, d_full_ref, ssm_norm_ref,
                 tri_ref, w_out_ref, npost_ref, sc0_ref, xbc0_ref, state0_t_ref,
                 h1_ref, sc_out_ref, xbc_out_ref, state_ref,
                 proj_s0, proj_s1, h_s0, h_s1, hn_s, sc_tail, xbc_tail, state_t_ref, *, chunks_per_seq):
    t = CHUNK
    k = pl.program_id(0)
    chunk = lax.rem(jnp.maximum(k - 1, 0), chunks_per_seq)

    @pl.when(k == 0)
    def _():
        proj_s1[...] = jnp.zeros((t, D_IN_PAD), F32)
        h_s1[...] = jnp.zeros((t, D_MODEL), F32)

    @pl.when(chunk == 0)
    def _():
        sc_tail[...] = sc0_ref[...]
        xbc_tail[...] = xbc0_ref[...]
        state_t_ref[...] = state0_t_ref[...]

    for parity in (0, 1):
        pl.when(lax.rem(k, 2) == parity)(functools.partial(
            _pmix_step, parity, x_ref, npre_ref, w_in_ref, sconv_w_ref, sconv_norm_ref, gmat_ref,
            xconv_w_ref, xconv_b_ref, dt_bias_ref, a_log_ref, d_full_ref, ssm_norm_ref, tri_ref, w_out_ref,
            npost_ref, h1_ref, (proj_s0, proj_s1), (h_s0, h_s1), hn_s, sc_tail, xbc_tail, state_t_ref))

    @pl.when(chunk == chunks_per_seq - 1)
    def _():
        sc_out_ref[0] = sc_tail[...]
        xbc_out_ref[0] = xbc_tail[...]
        for p in range(HEAD_PAIRS):
            state_ref[0, p * LANES:(p + 1) * LANES, :] = state_t_ref[:, p * LANES:(p + 1) * LANES].T


def _pmix_step(slot_a, x_ref, npre_ref, w_in_ref, sconv_w_ref, sconv_norm_ref, gmat_ref,
               xconv_w_ref, xconv_b_ref, dt_bias_ref, a_log_ref, d_full_ref, ssm_norm_ref, tri_ref, w_out_ref,
               npost_ref, h1_ref, proj_s, h_s, hn_s, sc_tail, xbc_tail, state_t_ref):
    t = CHUNK
    slot_b = 1 - slot_a

    h_a = x_ref[0]
    h_s[slot_a][...] = h_a
    hn_s[...] = _rms(h_a, npre_ref[...]).astype(BF16)

    def stage_a(first, last):
        for off in range(first * D_SCONV, min(last * D_SCONV, D_IN_PAD), D_SCONV):
            width = min(D_SCONV, D_IN_PAD - off)
            proj_s[slot_a][:, off:off + width] = _dot(hn_s[...], w_in_ref[:, off:off + width])

    def proj(off, width):
        return proj_s[slot_b][:, off:off + width]

    h = h_s[slot_b][...]
    ya, u = _sconv_mixer(proj(O_GATE_B, D_SCONV), proj(O_GATE_C, D_SCONV), proj(O_HV, D_SCONV),
                         sc_tail[...], sconv_w_ref[...], sconv_norm_ref[...], gmat_ref[...], None)
    sc_tail[...] = u[t - SUBLANES:t]
    stage_a(0, 2)

    xbc = proj(O_XS, D_XBC)
    xbc_c = _silu(_causal_conv(xbc, xbc_tail[...], xconv_w_ref[...], None) + xconv_b_ref[...])
    xbc_tail[...] = xbc[t - SUBLANES:t]
    stage_a(2, 5)
    xs = xbc_c[:, 0:D_SSM]
    bm_b = xbc_c[:, D_SSM:D_SSM + SSM_GROUPS * SSM_STATE].astype(BF16)
    cm = xbc_c[:, D_SSM + SSM_GROUPS * SSM_STATE:D_XBC]
    dt, a_cum = _ssd_prepare(proj(O_DT, LANES), dt_bias_ref[...], a_log_ref[...], tri_ref[...], t)
    w = jnp.exp(a_cum[t - 1:t, :] - a_cum) * dt
    causal = lax.broadcasted_iota(jnp.int32, (t, t), 0) >= lax.broadcasted_iota(jnp.int32, (t, t), 1)
    state_t = state_t_ref[...]
    y, xw_b, ea_full = _ssd_heads(xs, bm_b, cm, a_cum, dt, w, causal, state_t)
    stage_a(5, 7)
    chunk_decay = ea_full[t - 1:t, :]
    for g in range(SSM_GROUPS):
        cols = slice(g * D_GROUP, (g + 1) * D_GROUP)
        state_t_ref[:, cols] = chunk_decay[:, cols] * state_t[:, cols] + _dot_tn(
            bm_b[:, g * SSM_STATE:(g + 1) * SSM_STATE], xw_b[:, cols])
    yb = _ssm_tail(y, xs, proj(O_Z, D_SSM), d_full_ref[...], ssm_norm_ref[...])
    stage_a(7, 9)

    mix = _dot(ya.astype(BF16), w_out_ref[0:D_SCONV, :]) + _dot(yb.astype(BF16), w_out_ref[D_SCONV:D_MIX, :])
    h1_ref[0] = h + _rms(mix, npost_ref[...])


def _pmix(x_prompt, consts, sc0, xbc0, state0_t):
    batch, seq, _ = x_prompt.shape
    cps = seq // CHUNK
    n_tiles = batch * cps
    const_spec = lambda a: pl.BlockSpec(a.shape, lambda k: (0, 0))
    ins = list(consts) + [sc0, xbc0, state0_t]

    def tile_a(k):
        tile = jnp.minimum(k, n_tiles - 1)
        return (tile // cps, tile % cps, 0)

    def tile_b(k):
        tile = jnp.maximum(k - 1, 0)
        return (tile // cps, tile % cps, 0)

    def seq_b(k):
        return (jnp.maximum(k - 1, 0) // cps, 0, 0)

    return pl.pallas_call(
        functools.partial(_pmix_kernel, chunks_per_seq=cps),
        grid=(n_tiles + 1,),
        in_specs=[pl.BlockSpec((1, CHUNK, D_MODEL), tile_a)] + [const_spec(a) for a in ins],
        out_specs=[
            pl.BlockSpec((1, CHUNK, D_MODEL), tile_b),
            pl.BlockSpec((1, SUBLANES, D_SCONV), seq_b),
            pl.BlockSpec((1, SUBLANES, D_XBC), seq_b),
            pl.BlockSpec((1, D_SSM, SSM_STATE), seq_b),
        ],
        out_shape=[
            jax.ShapeDtypeStruct((batch, seq, D_MODEL), F32),
            jax.ShapeDtypeStruct((batch, SUBLANES, D_SCONV), F32),
            jax.ShapeDtypeStruct((batch, SUBLANES, D_XBC), F32),
            jax.ShapeDtypeStruct((batch, D_SSM, SSM_STATE), F32),
        ],
        scratch_shapes=[
            pltpu.VMEM((CHUNK, D_IN_PAD), F32),
            pltpu.VMEM((CHUNK, D_IN_PAD), F32),
            pltpu.VMEM((CHUNK, D_MODEL), F32),
            pltpu.VMEM((CHUNK, D_MODEL), F32),
            pltpu.VMEM((CHUNK, D_MODEL), BF16),
            pltpu.VMEM((SUBLANES, D_SCONV), F32),
            pltpu.VMEM((SUBLANES, D_XBC), F32),
            pltpu.VMEM((SSM_STATE, D_SSM), F32),
        ],
        compiler_params=pltpu.CompilerParams(dimension_semantics=("arbitrary",), vmem_limit_bytes=VMEM_LIMIT),
        name="prompt_mixer",
    )(x_prompt, *ins)


def _smix_kernel(proj_ref, sc_prev_ref, xbc_prev_ref, state_in_ref,
                 sconv_w_ref, sconv_norm_ref, gmat_ref, xconv_w_ref, xconv_b_ref,
                 dt_bias_ref, a_log_ref, d_full_ref, ssm_norm_ref,
                 mix_ref, u_ref, xbc_ref, state_ref, *, dec_seq):
    t = SEQ_TILE * dec_seq
    proj = proj_ref[...]
    ya, u = _sconv_mixer(proj[:, O_GATE_B:O_GATE_B + D_SCONV], proj[:, O_GATE_C:O_GATE_C + D_SCONV],
                         proj[:, O_HV:O_HV + D_SCONV], sc_prev_ref[...], sconv_w_ref[...],
                         sconv_norm_ref[...], gmat_ref[...], dec_seq)
    u_ref[...] = u
    xbc = proj[:, O_XS:O_XS + D_XBC]
    xbc_ref[...] = xbc
    xbc_c = _silu(_causal_conv(xbc, xbc_prev_ref[...], xconv_w_ref[...], dec_seq) + xconv_b_ref[...])
    xs = xbc_c[:, 0:D_SSM]
    bm = xbc_c[:, D_SSM:D_SSM + SSM_GROUPS * SSM_STATE]
    bm_b = bm.astype(BF16)
    cm = xbc_c[:, D_SSM + SSM_GROUPS * SSM_STATE:D_XBC]

    row = lax.broadcasted_iota(jnp.int32, (t, t), 0)
    col = lax.broadcasted_iota(jnp.int32, (t, t), 1)
    mask = jnp.logical_and(row >= col, row // dec_seq == col // dec_seq)
    tri = jnp.where(mask, 1.0, 0.0).astype(BF16)
    dt, a_cum = _ssd_prepare(proj[:, O_DT:O_DT + LANES], dt_bias_ref[...], a_log_ref[...], tri, t)
    a_total = jnp.concatenate(
        [jnp.broadcast_to(a_cum[(s + 1) * dec_seq - 1:(s + 1) * dec_seq, :], (dec_seq, LANES))
         for s in range(SEQ_TILE)], axis=0)
    w = jnp.exp(a_total - a_cum) * dt
    y, xw_b, ea_full = _ssd_heads(xs, bm_b, cm, a_cum, dt, w, mask)

    row_id = lax.broadcasted_iota(jnp.int32, (t, SSM_STATE), 0) // dec_seq
    y_off_rows = []
    for s in range(SEQ_TILE):
        r0 = s * dec_seq
        q_t = jnp.broadcast_to(jnp.exp(a_total[r0:r0 + 1, :]), (LANES, LANES)).T
        y_off_g = []
        for g in range(SSM_GROUPS):
            rows = slice(g * D_GROUP, (g + 1) * D_GROUP)
            s_prev = state_in_ref[s, rows, :]
            cm_s = cm[r0:r0 + dec_seq, g * SSM_STATE:(g + 1) * SSM_STATE].astype(BF16)
            y_off_g.append(_dot_nt(cm_s, s_prev.astype(BF16)))
            bm_g = jnp.where(row_id == s, bm[:, g * SSM_STATE:(g + 1) * SSM_STATE], 0.0).astype(BF16)
            decay = jnp.concatenate(
                [jnp.broadcast_to(q_t[e:e + 1, :], (SSM_HEAD_DIM, SSM_STATE))
                 for e in range(g * HEADS_PER_GROUP, (g + 1) * HEADS_PER_GROUP)], axis=0)
            state_ref[s, rows, :] = decay * s_prev + _dot_tn(xw_b[:, rows], bm_g)
        y_off_rows.append(jnp.concatenate(y_off_g, axis=1))
    y = y + jnp.concatenate(y_off_rows, axis=0) * ea_full
    yb = _ssm_tail(y, xs, proj[:, O_Z:O_Z + D_SSM], d_full_ref[...], ssm_norm_ref[...])
    mix_ref[:, 0:D_SCONV] = ya
    mix_ref[:, D_SCONV:D_MIX] = yb


def _smix(proj, sc_prev, xbc_prev, state, consts, dec_seq):
    n_seq = state.shape[0]
    t = SEQ_TILE * dec_seq
    const_spec = lambda a: pl.BlockSpec(a.shape, lambda i: (0, 0))
    return pl.pallas_call(
        functools.partial(_smix_kernel, dec_seq=dec_seq),
        grid=(n_seq // SEQ_TILE,),
        in_specs=[
            pl.BlockSpec((t, D_IN_PAD), lambda i: (i, 0)),
            pl.BlockSpec((t, D_SCONV), lambda i: (i, 0)),
            pl.BlockSpec((t, D_XBC), lambda i: (i, 0)),
            pl.BlockSpec((SEQ_TILE, D_SSM, SSM_STATE), lambda i: (i, 0, 0)),
        ] + [const_spec(a) for a in consts],
        out_specs=[
            pl.BlockSpec((t, D_MIX), lambda i: (i, 0)),
            pl.BlockSpec((t, D_SCONV), lambda i: (i, 0)),
            pl.BlockSpec((t, D_XBC), lambda i: (i, 0)),
            pl.BlockSpec((SEQ_TILE, D_SSM, SSM_STATE), lambda i: (i, 0, 0)),
        ],
        out_shape=[
            jax.ShapeDtypeStruct((n_seq * dec_seq, D_MIX), F32),
            jax.ShapeDtypeStruct((n_seq * dec_seq, D_SCONV), F32),
            jax.ShapeDtypeStruct((n_seq * dec_seq, D_XBC), F32),
            jax.ShapeDtypeStruct(state.shape, F32),
        ],
        compiler_params=pltpu.CompilerParams(dimension_semantics=("arbitrary",), vmem_limit_bytes=VMEM_LIMIT),
        name="sample_mixer",
    )(proj, sc_prev, xbc_prev, state, *consts)


def _ffn_body(h1, npre_ref, w_gate_ref, w_up_ref, w_down_ref, npost_ref, o_ref):
    fn = _rms(h1, npre_ref[...]).astype(BF16)
    act = (_silu(_dot(fn, w_gate_ref[...])) * _dot(fn, w_up_ref[...])).astype(BF16)
    o_ref[...] = h1 + _rms(_dot(act, w_down_ref[...]), npost_ref[...])


def _ffn_kernel(h_ref, npre_ref, w_gate_ref, w_up_ref, w_down_ref, npost_ref, o_ref):
    _ffn_body(h_ref[...], npre_ref, w_gate_ref, w_up_ref, w_down_ref, npost_ref, o_ref)


def _outproj_ffn_kernel(h_ref, mix_ref, w_out_ref, nmix_ref, npre_ref, w_gate_ref, w_up_ref, w_down_ref,
                        npost_ref, o_ref):
    mix = _dot(mix_ref[...].astype(BF16), w_out_ref[...])
    h1 = h_ref[...] + _rms(mix, nmix_ref[...])
    _ffn_body(h1, npre_ref, w_gate_ref, w_up_ref, w_down_ref, npost_ref, o_ref)


def _ffn(h, ffn_consts, mix=None, mix_consts=()):
    m = h.shape[0]
    rows = min(FFN_ROWS, m)
    row_spec = lambda width: pl.BlockSpec((rows, width), lambda i: (i, 0))
    const_spec = lambda a: pl.BlockSpec(a.shape, lambda i: (0, 0), pipeline_mode=pl.Buffered(1))
    if mix is None:
        kern, ins = _ffn_kernel, [h] + list(ffn_consts)
        specs = [row_spec(D_MODEL)] + [const_spec(a) for a in ffn_consts]
    else:
        kern, ins = _outproj_ffn_kernel, [h, mix] + list(mix_consts) + list(ffn_consts)
        specs = [row_spec(D_MODEL), row_spec(D_MIX)] + [const_spec(a) for a in list(mix_consts) + list(ffn_consts)]
    return pl.pallas_call(
        kern,
        grid=(m // rows,),
        in_specs=specs,
        out_specs=row_spec(D_MODEL),
        out_shape=jax.ShapeDtypeStruct((m, D_MODEL), F32),
        compiler_params=pltpu.CompilerParams(dimension_semantics=("arbitrary",), vmem_limit_bytes=VMEM_LIMIT),
        name="ffn" if mix is None else "outproj_ffn",
    )(*ins)


def _row(v, width=None):
    v = v.reshape(1, -1).astype(F32)
    if width is not None and v.shape[1] < width:
        v = jnp.pad(v, ((0, 0), (0, width - v.shape[1])))
    return v


def _layer(hp, hs, sc_state, xbc_state, ssm_state, meta_tokens,
           norm_mix_pre, norm_mix_post, w_in, sconv_w, sconv_norm, ssm_conv_w, ssm_conv_b,
           dt_bias, a_log, d_skip, ssm_norm, w_out, norm_ffn_pre, norm_ffn_post, w_gate, w_up, w_down):
    batch, seq, _ = hp.shape
    n_seq, dec_seq, _ = hs.shape

    w_in_b = jnp.pad(w_in, ((0, 0), (0, D_IN_PAD - D_IN))).astype(BF16)
    w_out_b = w_out.astype(BF16)
    w_gate_b, w_up_b, w_down_b = w_gate.astype(BF16), w_up.astype(BF16), w_down.astype(BF16)
    npre, npost = _row(norm_mix_pre), _row(norm_mix_post)
    fpre, fpost = _row(norm_ffn_pre), _row(norm_ffn_post)
    sconv_w = sconv_w.astype(F32)
    xconv_w = ssm_conv_w.astype(F32)
    xconv_b = _row(ssm_conv_b)
    dt_bias_p, a_log_p = _row(dt_bias, LANES), _row(a_log, LANES)
    d_full = jnp.repeat(d_skip.astype(F32), SSM_HEAD_DIM).reshape(1, D_SSM)
    sconv_norm_r, ssm_norm_r = _row(sconv_norm), _row(ssm_norm)

    r = jnp.arange(CHUNK)
    tri = (r[:, None] >= r[None, :]).astype(BF16)
    expand = (jnp.arange(LANES)[:, None] == (jnp.arange(D_SSM)[None, :] // SSM_HEAD_DIM)).astype(BF16)
    gsz = D_SCONV // SCONV_GROUPS
    gmat = ((jnp.arange(D_SCONV)[:, None] // gsz) == (jnp.arange(D_SCONV)[None, :] // gsz)).astype(BF16)

    mixer_consts = (sconv_w, sconv_norm_r, gmat, xconv_w, xconv_b, dt_bias_p, a_log_p, d_full, ssm_norm_r)
    ffn_consts = (fpre, w_gate_b, w_up_b, w_down_b, fpost)

    proj_meta = _inproj(meta_tokens.astype(F32), npre, w_in_b, N_META)
    sc0, xbc0, state0_t = _meta_state(proj_meta, xconv_w, xconv_b, dt_bias_p, a_log_p, tri, expand)

    pm_consts = (npre, w_in_b) + mixer_consts + (tri, w_out_b, npost)
    h1, sc_tail, xbc_tail, p_state = _pmix(hp, pm_consts, sc0, xbc0, state0_t)
    y_prompt = _ffn(h1.reshape(batch * seq, D_MODEL), ffn_consts).reshape(batch, seq, D_MODEL)
    new_sc_p = sc_tail[:, SUBLANES - (SCONV_WIDTH - 1):, :]
    new_xbc_p = xbc_tail[:, SUBLANES - (SSM_CONV_WIDTH - 1):, :]
    new_ssm_p = p_state.reshape(batch, SSM_GROUPS, HEADS_PER_GROUP, SSM_HEAD_DIM, SSM_STATE)

    hs_rows = hs.reshape(n_seq * dec_seq, D_MODEL)
    proj_s = _inproj(hs_rows, npre, w_in_b, 256)
    sc_prev = jnp.pad(sc_state, ((0, 0), (dec_seq - (SCONV_WIDTH - 1), 0), (0, 0))).reshape(n_seq * dec_seq, D_SCONV)
    xbc_prev = jnp.pad(xbc_state, ((0, 0), (dec_seq - (SSM_CONV_WIDTH - 1), 0), (0, 0))).reshape(n_seq * dec_seq, D_XBC)
    mix_s, u_s, xbc_s, s_state = _smix(proj_s, sc_prev, xbc_prev, ssm_state.reshape(n_seq, D_SSM, SSM_STATE),
                                       mixer_consts, dec_seq)
    y_sample = _ffn(hs_rows, ffn_consts, mix_s, (w_out_b, npost)).reshape(n_seq, dec_seq, D_MODEL)
    new_sc_s = u_s.reshape(n_seq, dec_seq, D_SCONV)[:, dec_seq - (SCONV_WIDTH - 1):, :]
    new_xbc_s = xbc_s.reshape(n_seq, dec_seq, D_XBC)[:, dec_seq - (SSM_CONV_WIDTH - 1):, :]
    new_ssm_s = s_state.reshape(n_seq, SSM_GROUPS, HEADS_PER_GROUP, SSM_HEAD_DIM, SSM_STATE)
    return y_prompt, y_sample, new_sc_p, new_xbc_p, new_ssm_p, new_sc_s, new_xbc_s, new_ssm_s


def kernel(x_prompt, x_sample, state_sconv, state_ssm_conv, state_ssm, meta_tokens, norm_mix_pre, norm_mix_post, w_in, sconv_w, sconv_norm, ssm_conv_w, ssm_conv_b, dt_bias, A_log, D_skip, ssm_norm, w_out, norm_ffn_pre, norm_ffn_post, w_gate, w_up, w_down):
    depth = w_in.shape[0]
    assert depth == 1 and x_sample.shape[1] == SUBLANES
    outs = _layer(x_prompt, x_sample, state_sconv[0], state_ssm_conv[0], state_ssm[0], meta_tokens,
                  norm_mix_pre[0], norm_mix_post[0], w_in[0], sconv_w[0], sconv_norm[0], ssm_conv_w[0],
                  ssm_conv_b[0], dt_bias[0], A_log[0], D_skip[0], ssm_norm[0], w_out[0],
                  norm_ffn_pre[0], norm_ffn_post[0], w_gate[0], w_up[0], w_down[0])
    y_prompt, y_sample = outs[0], outs[1]
    return (y_prompt, y_sample) + tuple(o[None] for o in outs[2:])
```

```python
import functools

import jax
import jax.numpy as jnp
from jax import lax
from jax.experimental import pallas as pl
from jax.experimental.pallas import tpu as pltpu

D_MODEL = 1024
N_META = 16
D_SCONV = 512
SCONV_GROUPS = 8
SCONV_WIDTH = 3
SSM_HEAD_DIM = 64
SSM_HEADS = 16
D_SSM = 1024
SSM_GROUPS = 2
HEADS_PER_GROUP = 8
D_GROUP = HEADS_PER_GROUP * SSM_HEAD_DIM
SSM_STATE = 128
SSM_CONV_WIDTH = 4
D_BC = 2 * SSM_GROUPS * SSM_STATE
D_XBC = D_SSM + D_BC
D_MIX = D_SCONV + D_SSM
D_IN = 3 * D_SCONV + D_SSM + D_XBC + SSM_HEADS
D_FF = 2816
EPS = 1e-6

LANES = 128
SUBLANES = 8
D_IN_PAD = D_IN + (LANES - SSM_HEADS)
HEAD_PAIRS = D_SSM // LANES

O_GATE_B = 0
O_GATE_C = D_SCONV
O_HV = 2 * D_SCONV
O_Z = 3 * D_SCONV
O_XS = O_Z + D_SSM
O_BC = O_XS + D_SSM
O_DT = O_BC + D_BC

CHUNK = 128
SEQ_TILE = 16
FFN_ROWS = 512
VMEM_LIMIT = 56 * 1024 * 1024

BF16 = jnp.bfloat16
F32 = jnp.float32


def _dot(a, b):
    return jnp.dot(a, b, preferred_element_type=F32)


def _dot_nt(a, b):
    return lax.dot_general(a, b, (((1,), (1,)), ((), ())), preferred_element_type=F32)


def _dot_tn(a, b):
    return lax.dot_general(a, b, (((0,), (0,)), ((), ())), preferred_element_type=F32)


def _split3(v):
    hi = v.astype(BF16)
    r1 = v - hi.astype(F32)
    mid = r1.astype(BF16)
    lo = (r1 - mid.astype(F32)).astype(BF16)
    return hi, mid, lo


def _dot3(v, m):
    hi, mid, lo = _split3(v)
    return _dot(hi, m) + _dot(mid, m) + _dot(lo, m)


def _dot3_left(m, v):
    hi, mid, lo = _split3(v)
    return _dot(m, hi) + _dot(m, mid) + _dot(m, lo)


def _rms(x, w):
    return x * lax.rsqrt(jnp.mean(x * x, axis=-1, keepdims=True) + EPS) * w


def _silu(x):
    return x * (1.0 / (1.0 + jnp.exp(-x)))


def _softplus(x):
    return jnp.maximum(x, 0.0) + jnp.log1p(jnp.exp(-jnp.abs(x)))


def _shifted(u, prev, j, rows_per_seq):
    t = u.shape[0]
    s = pltpu.roll(u, j, 0)
    if rows_per_seq is None:
        row = lax.broadcasted_iota(jnp.int32, (SUBLANES, u.shape[1]), 0)
        first = jnp.where(row < j, pltpu.roll(prev, j, 0), s[0:SUBLANES])
        if t == SUBLANES:
            return first
        return jnp.concatenate([first, s[SUBLANES:]], axis=0)
    row = lax.broadcasted_iota(jnp.int32, u.shape, 0) % SUBLANES
    return jnp.where(row < j, pltpu.roll(prev, t - SUBLANES + j, 0), s)


def _causal_conv(u, prev, w, rows_per_seq):
    k = w.shape[0]
    out = _shifted(u, prev, k - 1, rows_per_seq) * w[0:1]
    for i in range(1, k - 1):
        out = out + _shifted(u, prev, k - 1 - i, rows_per_seq) * w[i:i + 1]
    return out + u * w[k - 1:k]


def _sconv_mixer(gate_b, gate_c, hv, prev, sconv_w, sconv_norm, gmat, rows_per_seq):
    u = gate_c * hv
    conv_u = _causal_conv(u, prev, sconv_w, rows_per_seq)
    return _sconv_norm(gate_b, conv_u, sconv_norm, gmat), u


def _sconv_norm(gate_b, conv_u, sconv_norm, gmat):
    v = gate_b * conv_u
    sq = v * v
    hi = sq.astype(BF16)
    lo = (sq - hi.astype(F32)).astype(BF16)
    group_sum = _dot(hi, gmat) + _dot(lo, gmat)
    return v * lax.rsqrt(group_sum * (SCONV_GROUPS / D_SCONV) + EPS) * sconv_norm


def _ssd_prepare(dt_raw, dt_bias, a_log, tri, t):
    lane = lax.broadcasted_iota(jnp.int32, (t, LANES), 1)
    valid = lane < SSM_HEADS
    dt = jnp.where(valid, _softplus(dt_raw + dt_bias), 0.0)
    a = dt * (-jnp.exp(a_log))
    a_cum = _dot3_left(tri, a)
    return dt, a_cum


def _group_rms_ssm(y, w):
    parts = []
    for g in range(SSM_GROUPS):
        yg = y[:, g * D_GROUP:(g + 1) * D_GROUP]
        parts.append(yg * lax.rsqrt(jnp.mean(yg * yg, axis=-1, keepdims=True) + EPS))
    return jnp.concatenate(parts, axis=1) * w


def _ssd_heads(xs_of, bm_b, cm, a_cum, dt, w, mask, state_t=None, gate=None, after_pair=None):
    t = a_cum.shape[0]
    assert t == LANES
    a_cum_t = a_cum.T
    dt_t = dt.T
    first_half = lax.broadcasted_iota(jnp.int32, (t, LANES), 1) < SSM_HEAD_DIM
    y_pairs, xw_pairs, ea_pairs, xs_pairs = [], [], [], []
    for g in range(SSM_GROUPS):
        cm_g = cm[:, g * SSM_STATE:(g + 1) * SSM_STATE]
        cb = _dot_nt(cm_g.astype(BF16), bm_b[:, g * SSM_STATE:(g + 1) * SSM_STATE])
        for q in range(HEADS_PER_GROUP // 2):
            pair = g * (HEADS_PER_GROUP // 2) + q
            cols = slice(pair * LANES, (pair + 1) * LANES)
            xs_p = xs_of(pair)
            rhs = xs_p.astype(BF16)
            if state_t is not None:
                rhs = jnp.concatenate([rhs, state_t[:, cols].astype(BF16)], axis=0)
            res, eas, ws = [], [], []
            for e in (2 * pair, 2 * pair + 1):
                a_col = jnp.broadcast_to(a_cum[:, e:e + 1], (t, LANES))
                decay = jnp.exp(jnp.where(mask, a_col - a_cum_t[e:e + 1, :], -jnp.inf))
                lhs = (cb * decay * dt_t[e:e + 1, :]).astype(BF16)
                ea = jnp.exp(a_col)
                if state_t is not None:
                    lhs = jnp.concatenate([lhs, (cm_g * ea).astype(BF16)], axis=1)
                res.append(_dot(lhs, rhs))
                eas.append(ea)
                ws.append(jnp.broadcast_to(w[:, e:e + 1], (t, LANES)))
            y_p = jnp.where(first_half, res[0], res[1])
            y_pairs.append(y_p if gate is None else gate(pair, y_p, xs_p))
            xw_pairs.append((xs_p * jnp.where(first_half, ws[0], ws[1])).astype(BF16))
            ea_pairs.append(jnp.where(first_half, eas[0], eas[1]))
            xs_pairs.append(xs_p)
            if after_pair is not None:
                after_pair(pair)
    return (jnp.concatenate(y_pairs, axis=1), jnp.concatenate(xw_pairs, axis=1),
            jnp.concatenate(ea_pairs, axis=1), jnp.concatenate(xs_pairs, axis=1))


def _ssm_tail(y, xs, z, d_full, ssm_norm):
    y = y + d_full * xs
    y = y * _silu(z)
    return _group_rms_ssm(y, ssm_norm)


def _inproj_kernel(x_ref, npre_ref, w_in_ref, o_ref):
    hn = _rms(x_ref[...], npre_ref[...]).astype(BF16)
    o_ref[...] = _dot(hn, w_in_ref[...])


def _inproj(x, npre, w_in_b, rows):
    m = x.shape[0]
    return pl.pallas_call(
        _inproj_kernel,
        grid=(m // rows,),
        in_specs=[
            pl.BlockSpec((rows, D_MODEL), lambda i: (i, 0)),
            pl.BlockSpec((1, D_MODEL), lambda i: (0, 0)),
            pl.BlockSpec((D_MODEL, D_IN_PAD), lambda i: (0, 0)),
        ],
        out_specs=pl.BlockSpec((rows, D_IN_PAD), lambda i: (i, 0)),
        out_shape=jax.ShapeDtypeStruct((m, D_IN_PAD), F32),
        compiler_params=pltpu.CompilerParams(dimension_semantics=("arbitrary",), vmem_limit_bytes=VMEM_LIMIT),
        name="inproj",
    )(x, npre, w_in_b)


def _meta_kernel(proj_ref, xconv_w_ref, xconv_b_ref, dt_bias_ref, a_log_ref, tri_ref, expand_ref,
                 sc_tail_ref, xbc_tail_ref, state_t_ref):
    t = N_META
    proj = proj_ref[...]
    u = proj[:, O_GATE_C:O_GATE_C + D_SCONV] * proj[:, O_HV:O_HV + D_SCONV]
    sc_tail_ref[...] = u[t - SUBLANES:t]
    xbc = proj[:, O_XS:O_XS + D_XBC]
    xbc_tail_ref[...] = xbc[t - SUBLANES:t]
    zeros_prev = jnp.zeros((SUBLANES, D_XBC), F32)
    xbc_c = _silu(_causal_conv(xbc, zeros_prev, xconv_w_ref[...], None) + xconv_b_ref[...])
    xs = xbc_c[:, 0:D_SSM]
    dt, a_cum = _ssd_prepare(proj[:, O_DT:O_DT + LANES], dt_bias_ref[...], a_log_ref[...],
                             tri_ref[0:t, 0:t], t)
    w = jnp.exp(a_cum[t - 1:t, :] - a_cum) * dt
    xw_b = (xs * _dot3(w, expand_ref[...])).astype(BF16)
    for g in range(SSM_GROUPS):
        bm_b = xbc_c[:, D_SSM + g * SSM_STATE:D_SSM + (g + 1) * SSM_STATE].astype(BF16)
        state_t_ref[:, g * D_GROUP:(g + 1) * D_GROUP] = _dot_tn(bm_b, xw_b[:, g * D_GROUP:(g + 1) * D_GROUP])


def _meta_state(proj_meta, xconv_w, xconv_b, dt_bias, a_log, tri, expand):
    full = lambda shape: pl.BlockSpec(shape, lambda: tuple(0 for _ in shape))
    return pl.pallas_call(
        _meta_kernel,
        in_specs=[full(proj_meta.shape), full(xconv_w.shape), full(xconv_b.shape), full(dt_bias.shape),
                  full(a_log.shape), full(tri.shape), full(expand.shape)],
        out_specs=[full((SUBLANES, D_SCONV)), full((SUBLANES, D_XBC)), full((SSM_STATE, D_SSM))],
        out_shape=[jax.ShapeDtypeStruct((SUBLANES, D_SCONV), F32),
                   jax.ShapeDtypeStruct((SUBLANES, D_XBC), F32),
                   jax.ShapeDtypeStruct((SSM_STATE, D_SSM), F32)],
        name="meta_state",
    )(proj_meta, xconv_w, xconv_b, dt_bias, a_log, tri, expand)


def _pmix_kernel(x_ref, npre_ref, w_in_ref, sconv_w_ref, sconv_norm_ref, gmat_ref,
                 xconv_w_ref, xconv_b_ref, dt_bias_ref, a_log_ref, d_full_ref, ssm_norm_ref,
                 tri_ref, w_out_ref, npost_ref, sc0_ref, xbc0_ref, state0_t_ref,
                 h1_ref, sc_out_ref, xbc_out_ref, state_ref,
                 proj_s0, proj_s1, h_s0, h_s1, hn_s, sc_tail, xbc_tail, state_t_ref, *, chunks_per_seq):
    t = CHUNK
    k = pl.program_id(0)
    chunk = lax.rem(jnp.maximum(k - 1, 0), chunks_per_seq)

    @pl.when(k == 0)
    def _():
        proj_s1[...] = jnp.zeros((t, D_IN_PAD), F32)
        h_s1[...] = jnp.zeros((t, D_MODEL), F32)

    @pl.when(chunk == 0)
    def _():
        sc_tail[...] = sc0_ref[...]
        xbc_tail[...] = xbc0_ref[...]
        state_t_ref[...] = state0_t_ref[...]

    for parity in (0, 1):
        pl.when(lax.rem(k, 2) == parity)(functools.partial(
            _pmix_step, parity, x_ref, npre_ref, w_in_ref, sconv_w_ref, sconv_norm_ref, gmat_ref,
            xconv_w_ref, xconv_b_ref, dt_bias_ref, a_log_ref, d_full_ref, ssm_norm_ref, tri_ref, w_out_ref,
            npost_ref, h1_ref, (proj_s0, proj_s1), (h_s0, h_s1), hn_s, sc_tail, xbc_tail, state_t_ref))

    @pl.when(chunk == chunks_per_seq - 1)
    def _():
        sc_out_ref[0] = sc_tail[...]
        xbc_out_ref[0] = xbc_tail[...]
        for p in range(HEAD_PAIRS):
            state_ref[0, p * LANES:(p + 1) * LANES, :] = state_t_ref[:, p * LANES:(p + 1) * LANES].T


def _pmix_step(slot_a, x_ref, npre_ref, w_in_ref, sconv_w_ref, sconv_norm_ref, gmat_ref,
               xconv_w_ref, xconv_b_ref, dt_bias_ref, a_log_ref, d_full_ref, ssm_norm_ref, tri_ref, w_out_ref,
               npost_ref, h1_ref, proj_s, h_s, hn_s, sc_tail, xbc_tail, state_t_ref):
    t = CHUNK
    slot_b = 1 - slot_a

    h_a = x_ref[0]
    h_s[slot_a][...] = h_a
    hn_s[...] = _rms(h_a, npre_ref[...]).astype(BF16)

    def stage_a(first, last_seg):
        for off in range(first * D_SCONV, min(last_seg * D_SCONV, D_IN_PAD), D_SCONV):
            width = min(D_SCONV, D_IN_PAD - off)
            proj_s[slot_a][:, off:off + width] = _dot(hn_s[...], w_in_ref[:, off:off + width])

    def proj(off, width):
        return proj_s[slot_b][:, off:off + width]

    h = h_s[slot_b][...]
    ya, u = _sconv_mixer(proj(O_GATE_B, D_SCONV), proj(O_GATE_C, D_SCONV), proj(O_HV, D_SCONV),
                         sc_tail[...], sconv_w_ref[...], sconv_norm_ref[...], gmat_ref[...], None)
    sc_tail[...] = u[t - SUBLANES:t]
    stage_a(0, 2)
    mix_a = _dot(ya.astype(BF16), w_out_ref[0:D_SCONV, :])

    def xconv(col, width):
        cols = slice(col, col + width)
        raw = proj(O_XS + col, width)
        out = _silu(_causal_conv(raw, xbc_tail[:, cols], xconv_w_ref[:, cols], None) + xconv_b_ref[:, cols])
        xbc_tail[:, cols] = raw[t - SUBLANES:t]
        return out

    bc = xconv(D_SSM, D_BC)
    bm_b = bc[:, 0:SSM_GROUPS * SSM_STATE].astype(BF16)
    cm = bc[:, SSM_GROUPS * SSM_STATE:D_BC]
    dt, a_cum = _ssd_prepare(proj(O_DT, LANES), dt_bias_ref[...], a_log_ref[...], tri_ref[...], t)
    w = jnp.exp(a_cum[t - 1:t, :] - a_cum) * dt
    causal = lax.broadcasted_iota(jnp.int32, (t, t), 0) >= lax.broadcasted_iota(jnp.int32, (t, t), 1)
    state_t = state_t_ref[...]
    stage_a(2, 4)

    def gate(p, y_p, xs_p):
        cols = slice(p * LANES, (p + 1) * LANES)
        return (y_p + d_full_ref[:, cols] * xs_p) * _silu(proj(O_Z + p * LANES, LANES))

    def after_pair(p):
        if p % 2 == 1 and p < 6:
            stage_a(4 + p // 2, 5 + p // 2)

    y, xw_b, ea_full, _ = _ssd_heads(lambda p: xconv(p * LANES, LANES), bm_b, cm, a_cum, dt, w, causal,
                                     state_t, gate, after_pair)
    chunk_decay = ea_full[t - 1:t, :]
    for g in range(SSM_GROUPS):
        cols = slice(g * D_GROUP, (g + 1) * D_GROUP)
        state_t_ref[:, cols] = chunk_decay[:, cols] * state_t[:, cols] + _dot_tn(
            bm_b[:, g * SSM_STATE:(g + 1) * SSM_STATE], xw_b[:, cols])
    yb = _group_rms_ssm(y, ssm_norm_ref[...])

    mix = mix_a + _dot(yb.astype(BF16), w_out_ref[D_SCONV:D_MIX, :])
    stage_a(7, 9)
    h1_ref[0] = h + _rms(mix, npost_ref[...])


def _pmix(x_prompt, consts, sc0, xbc0, state0_t):
    batch, seq, _ = x_prompt.shape
    cps = seq // CHUNK
    n_tiles = batch * cps
    const_spec = lambda a: pl.BlockSpec(a.shape, lambda k: (0, 0))
    ins = list(consts) + [sc0, xbc0, state0_t]

    def tile_a(k):
        tile = jnp.minimum(k, n_tiles - 1)
        return (tile // cps, tile % cps, 0)

    def tile_b(k):
        tile = jnp.maximum(k - 1, 0)
        return (tile // cps, tile % cps, 0)

    def seq_b(k):
        return (jnp.maximum(k - 1, 0) // cps, 0, 0)

    return pl.pallas_call(
        functools.partial(_pmix_kernel, chunks_per_seq=cps),
        grid=(n_tiles + 1,),
        in_specs=[pl.BlockSpec((1, CHUNK, D_MODEL), tile_a)] + [const_spec(a) for a in ins],
        out_specs=[
            pl.BlockSpec((1, CHUNK, D_MODEL), tile_b),
            pl.BlockSpec((1, SUBLANES, D_SCONV), seq_b),
            pl.BlockSpec((1, SUBLANES, D_XBC), seq_b),
            pl.BlockSpec((1, D_SSM, SSM_STATE), seq_b),
        ],
        out_shape=[
            jax.ShapeDtypeStruct((batch, seq, D_MODEL), F32),
            jax.ShapeDtypeStruct((batch, SUBLANES, D_SCONV), F32),
            jax.ShapeDtypeStruct((batch, SUBLANES, D_XBC), F32),
            jax.ShapeDtypeStruct((batch, D_SSM, SSM_STATE), F32),
        ],
        scratch_shapes=[
            pltpu.VMEM((CHUNK, D_IN_PAD), F32),
            pltpu.VMEM((CHUNK, D_IN_PAD), F32),
            pltpu.VMEM((CHUNK, D_MODEL), F32),
            pltpu.VMEM((CHUNK, D_MODEL), F32),
            pltpu.VMEM((CHUNK, D_MODEL), BF16),
            pltpu.VMEM((SUBLANES, D_SCONV), F32),
            pltpu.VMEM((SUBLANES, D_XBC), F32),
            pltpu.VMEM((SSM_STATE, D_SSM), F32),
        ],
        compiler_params=pltpu.CompilerParams(dimension_semantics=("arbitrary",), vmem_limit_bytes=VMEM_LIMIT),
        name="prompt_mixer",
    )(x_prompt, *ins)


def _smix_kernel(proj_ref, sc_prev_ref, xbc_prev_ref, state_in_ref,
                 sconv_w_ref, sconv_norm_ref, gmat_ref, xconv_w_ref, xconv_b_ref,
                 dt_bias_ref, a_log_ref, d_full_ref, ssm_norm_ref,
                 mix_ref, u_ref, xbc_ref, state_ref, *, dec_seq):
    t = SEQ_TILE * dec_seq
    proj = proj_ref[...]
    ya, u = _sconv_mixer(proj[:, O_GATE_B:O_GATE_B + D_SCONV], proj[:, O_GATE_C:O_GATE_C + D_SCONV],
                         proj[:, O_HV:O_HV + D_SCONV], sc_prev_ref[...], sconv_w_ref[...],
                         sconv_norm_ref[...], gmat_ref[...], dec_seq)
    u_ref[...] = u
    xbc = proj[:, O_XS:O_XS + D_XBC]
    xbc_ref[...] = xbc
    xbc_c = _silu(_causal_conv(xbc, xbc_prev_ref[...], xconv_w_ref[...], dec_seq) + xconv_b_ref[...])
    xs = xbc_c[:, 0:D_SSM]
    bm = xbc_c[:, D_SSM:D_SSM + SSM_GROUPS * SSM_STATE]
    bm_b = bm.astype(BF16)
    cm = xbc_c[:, D_SSM + SSM_GROUPS * SSM_STATE:D_XBC]

    row = lax.broadcasted_iota(jnp.int32, (t, t), 0)
    col = lax.broadcasted_iota(jnp.int32, (t, t), 1)
    mask = jnp.logical_and(row >= col, row // dec_seq == col // dec_seq)
    tri = jnp.where(mask, 1.0, 0.0).astype(BF16)
    dt, a_cum = _ssd_prepare(proj[:, O_DT:O_DT + LANES], dt_bias_ref[...], a_log_ref[...], tri, t)
    a_total = jnp.concatenate(
        [jnp.broadcast_to(a_cum[(s + 1) * dec_seq - 1:(s + 1) * dec_seq, :], (dec_seq, LANES))
         for s in range(SEQ_TILE)], axis=0)
    w = jnp.exp(a_total - a_cum) * dt
    y, xw_b, ea_full, _ = _ssd_heads(lambda p: xs[:, p * LANES:(p + 1) * LANES], bm_b, cm, a_cum, dt, w, mask)

    row_id = lax.broadcasted_iota(jnp.int32, (t, SSM_STATE), 0) // dec_seq
    y_off_rows = []
    for s in range(SEQ_TILE):
        r0 = s * dec_seq
        q_t = jnp.broadcast_to(jnp.exp(a_total[r0:r0 + 1, :]), (LANES, LANES)).T
        y_off_g = []
        for g in range(SSM_GROUPS):
            rows = slice(g * D_GROUP, (g + 1) * D_GROUP)
            s_prev = state_in_ref[s, rows, :]
            cm_s = cm[r0:r0 + dec_seq, g * SSM_STATE:(g + 1) * SSM_STATE].astype(BF16)
            y_off_g.append(_dot_nt(cm_s, s_prev.astype(BF16)))
            bm_g = jnp.where(row_id == s, bm[:, g * SSM_STATE:(g + 1) * SSM_STATE], 0.0).astype(BF16)
            decay = jnp.concatenate(
                [jnp.broadcast_to(q_t[e:e + 1, :], (SSM_HEAD_DIM, SSM_STATE))
                 for e in range(g * HEADS_PER_GROUP, (g + 1) * HEADS_PER_GROUP)], axis=0)
            state_ref[s, rows, :] = decay * s_prev + _dot_tn(xw_b[:, rows], bm_g)
        y_off_rows.append(jnp.concatenate(y_off_g, axis=1))
    y = y + jnp.concatenate(y_off_rows, axis=0) * ea_full
    yb = _ssm_tail(y, xs, proj[:, O_Z:O_Z + D_SSM], d_full_ref[...], ssm_norm_ref[...])
    mix_ref[:, 0:D_SCONV] = ya
    mix_ref[:, D_SCONV:D_MIX] = yb


def _smix(proj, sc_prev, xbc_prev, state, consts, dec_seq):
    n_seq = state.shape[0]
    t = SEQ_TILE * dec_seq
    const_spec = lambda a: pl.BlockSpec(a.shape, lambda i: (0, 0))
    return pl.pallas_call(
        functools.partial(_smix_kernel, dec_seq=dec_seq),
        grid=(n_seq // SEQ_TILE,),
        in_specs=[
            pl.BlockSpec((t, D_IN_PAD), lambda i: (i, 0)),
            pl.BlockSpec((t, D_SCONV), lambda i: (i, 0)),
            pl.BlockSpec((t, D_XBC), lambda i: (i, 0)),
            pl.BlockSpec((SEQ_TILE, D_SSM, SSM_STATE), lambda i: (i, 0, 0)),
        ] + [const_spec(a) for a in consts],
        out_specs=[
            pl.BlockSpec((t, D_MIX), lambda i: (i, 0)),
            pl.BlockSpec((t, D_SCONV), lambda i: (i, 0)),
            pl.BlockSpec((t, D_XBC), lambda i: (i, 0)),
            pl.BlockSpec((SEQ_TILE, D_SSM, SSM_STATE), lambda i: (i, 0, 0)),
        ],
        out_shape=[
            jax.ShapeDtypeStruct((n_seq * dec_seq, D_MIX), F32),
            jax.ShapeDtypeStruct((n_seq * dec_seq, D_SCONV), F32),
            jax.ShapeDtypeStruct((n_seq * dec_seq, D_XBC), F32),
            jax.ShapeDtypeStruct(state.shape, F32),
        ],
        compiler_params=pltpu.CompilerParams(dimension_semantics=("arbitrary",), vmem_limit_bytes=VMEM_LIMIT),
        name="sample_mixer",
    )(proj, sc_prev, xbc_prev, state, *consts)


def _ffn_body(h1, npre_ref, w_gate_ref, w_up_ref, w_down_ref, npost_ref, o_ref):
    fn = _rms(h1, npre_ref[...]).astype(BF16)
    act = (_silu(_dot(fn, w_gate_ref[...])) * _dot(fn, w_up_ref[...])).astype(BF16)
    o_ref[...] = h1 + _rms(_dot(act, w_down_ref[...]), npost_ref[...])


def _ffn_kernel(h_ref, npre_ref, w_gate_ref, w_up_ref, w_down_ref, npost_ref, o_ref):
    _ffn_body(h_ref[...], npre_ref, w_gate_ref, w_up_ref, w_down_ref, npost_ref, o_ref)


def _outproj_ffn_kernel(h_ref, mix_ref, w_out_ref, nmix_ref, npre_ref, w_gate_ref, w_up_ref, w_down_ref,
                        npost_ref, o_ref):
    mix = _dot(mix_ref[...].astype(BF16), w_out_ref[...])
    h1 = h_ref[...] + _rms(mix, nmix_ref[...])
    _ffn_body(h1, npre_ref, w_gate_ref, w_up_ref, w_down_ref, npost_ref, o_ref)


def _ffn(h, ffn_consts, mix=None, mix_consts=()):
    m = h.shape[0]
    rows = min(FFN_ROWS, m)
    row_spec = lambda width: pl.BlockSpec((rows, width), lambda i: (i, 0))
    const_spec = lambda a: pl.BlockSpec(a.shape, lambda i: (0, 0), pipeline_mode=pl.Buffered(1))
    if mix is None:
        kern, ins = _ffn_kernel, [h] + list(ffn_consts)
        specs = [row_spec(D_MODEL)] + [const_spec(a) for a in ffn_consts]
    else:
        kern, ins = _outproj_ffn_kernel, [h, mix] + list(mix_consts) + list(ffn_consts)
        specs = [row_spec(D_MODEL), row_spec(D_MIX)] + [const_spec(a) for a in list(mix_consts) + list(ffn_consts)]
    return pl.pallas_call(
        kern,
        grid=(m // rows,),
        in_specs=specs,
        out_specs=row_spec(D_MODEL),
        out_shape=jax.ShapeDtypeStruct((m, D_MODEL), F32),
        compiler_params=pltpu.CompilerParams(dimension_semantics=("arbitrary",), vmem_limit_bytes=VMEM_LIMIT),
        name="ffn" if mix is None else "outproj_ffn",
    )(*ins)


def _row(v, width=None):
    v = v.reshape(1, -1).astype(F32)
    if width is not None and v.shape[1] < width:
        v = jnp.pad(v, ((0, 0), (0, width - v.shape[1])))
    return v


def _layer(hp, hs, sc_state, xbc_state, ssm_state, meta_tokens,
           norm_mix_pre, norm_mix_post, w_in, sconv_w, sconv_norm, ssm_conv_w, ssm_conv_b,
           dt_bias, a_log, d_skip, ssm_norm, w_out, norm_ffn_pre, norm_ffn_post, w_gate, w_up, w_down):
    batch, seq, _ = hp.shape
    n_seq, dec_seq, _ = hs.shape

    w_in_b = jnp.pad(w_in, ((0, 0), (0, D_IN_PAD - D_IN))).astype(BF16)
    w_out_b = w_out.astype(BF16)
    w_gate_b, w_up_b, w_down_b = w_gate.astype(BF16), w_up.astype(BF16), w_down.astype(BF16)
    npre, npost = _row(norm_mix_pre), _row(norm_mix_post)
    fpre, fpost = _row(norm_ffn_pre), _row(norm_ffn_post)
    sconv_w = sconv_w.astype(F32)
    xconv_w = ssm_conv_w.astype(F32)
    xconv_b = _row(ssm_conv_b)
    dt_bias_p, a_log_p = _row(dt_bias, LANES), _row(a_log, LANES)
    d_full = jnp.repeat(d_skip.astype(F32), SSM_HEAD_DIM).reshape(1, D_SSM)
    sconv_norm_r, ssm_norm_r = _row(sconv_norm), _row(ssm_norm)

    r = jnp.arange(CHUNK)
    tri = (r[:, None] >= r[None, :]).astype(BF16)
    expand = (jnp.arange(LANES)[:, None] == (jnp.arange(D_SSM)[None, :] // SSM_HEAD_DIM)).astype(BF16)
    gsz = D_SCONV // SCONV_GROUPS
    gmat = ((jnp.arange(D_SCONV)[:, None] // gsz) == (jnp.arange(D_SCONV)[None, :] // gsz)).astype(BF16)

    mixer_consts = (sconv_w, sconv_norm_r, gmat, xconv_w, xconv_b, dt_bias_p, a_log_p, d_full, ssm_norm_r)
    ffn_consts = (fpre, w_gate_b, w_up_b, w_down_b, fpost)

    proj_meta = _inproj(meta_tokens.astype(F32), npre, w_in_b, N_META)
    sc0, xbc0, state0_t = _meta_state(proj_meta, xconv_w, xconv_b, dt_bias_p, a_log_p, tri, expand)

    pm_consts = (npre, w_in_b) + mixer_consts + (tri, w_out_b, npost)
    h1, sc_tail, xbc_tail, p_state = _pmix(hp, pm_consts, sc0, xbc0, state0_t)
    y_prompt = _ffn(h1.reshape(batch * seq, D_MODEL), ffn_consts).reshape(batch, seq, D_MODEL)
    new_sc_p = sc_tail[:, SUBLANES - (SCONV_WIDTH - 1):, :]
    new_xbc_p = xbc_tail[:, SUBLANES - (SSM_CONV_WIDTH - 1):, :]
    new_ssm_p = p_state.reshape(batch, SSM_GROUPS, HEADS_PER_GROUP, SSM_HEAD_DIM, SSM_STATE)

    hs_rows = hs.reshape(n_seq * dec_seq, D_MODEL)
    proj_s = _inproj(hs_rows, npre, w_in_b, 256)
    sc_prev = jnp.pad(sc_state, ((0, 0), (dec_seq - (SCONV_WIDTH - 1), 0), (0, 0))).reshape(n_seq * dec_seq, D_SCONV)
    xbc_prev = jnp.pad(xbc_state, ((0, 0), (dec_seq - (SSM_CONV_WIDTH - 1), 0), (0, 0))).reshape(n_seq * dec_seq, D_XBC)
    mix_s, u_s, xbc_s, s_state = _smix(proj_s, sc_prev, xbc_prev, ssm_state.reshape(n_seq, D_SSM, SSM_STATE),
                                       mixer_consts, dec_seq)
    y_sample = _ffn(hs_rows, ffn_consts, mix_s, (w_out_b, npost)).reshape(n_seq, dec_seq, D_MODEL)
    new_sc_s = u_s.reshape(n_seq, dec_seq, D_SCONV)[:, dec_seq - (SCONV_WIDTH - 1):, :]
    new_xbc_s = xbc_s.reshape(n_seq, dec_seq, D_XBC)[:, dec_seq - (SSM_CONV_WIDTH - 1):, :]
    new_ssm_s = s_state.reshape(n_seq, SSM_GROUPS, HEADS_PER_GROUP, SSM_HEAD_DIM, SSM_STATE)
    return y_prompt, y_sample, new_sc_p, new_xbc_p, new_ssm_p, new_sc_s, new_xbc_s, new_ssm_s


def kernel(x_prompt, x_sample, state_sconv, state_ssm_conv, state_ssm, meta_tokens, norm_mix_pre, norm_mix_post, w_in, sconv_w, sconv_norm, ssm_conv_w, ssm_conv_b, dt_bias, A_log, D_skip, ssm_norm, w_out, norm_ffn_pre, norm_ffn_post, w_gate, w_up, w_down):
    depth = w_in.shape[0]
    assert depth == 1 and x_sample.shape[1] == SUBLANES
    outs = _layer(x_prompt, x_sample, state_sconv[0], state_ssm_conv[0], state_ssm[0], meta_tokens,
                  norm_mix_pre[0], norm_mix_post[0], w_in[0], sconv_w[0], sconv_norm[0], ssm_conv_w[0],
                  ssm_conv_b[0], dt_bias[0], A_log[0], D_skip[0], ssm_norm[0], w_out[0],
                  norm_ffn_pre[0], norm_ffn_post[0], w_gate[0], w_up[0], w_down[0])
    y_prompt, y_sample = outs[0], outs[1]
    return (y_prompt, y_sample) + tuple(o[None] for o in outs[2:])
```

```python
import functools

import jax
import jax.numpy as jnp
from jax import lax
from jax.experimental import pallas as pl
from jax.experimental.pallas import tpu as pltpu

D_MODEL = 1024
N_META = 16
D_SCONV = 512
SCONV_GROUPS = 8
SCONV_WIDTH = 3
SSM_HEAD_DIM = 64
SSM_HEADS = 16
D_SSM = 1024
SSM_GROUPS = 2
HEADS_PER_GROUP = 8
D_GROUP = HEADS_PER_GROUP * SSM_HEAD_DIM
SSM_STATE = 128
SSM_CONV_WIDTH = 4
D_BC = 2 * SSM_GROUPS * SSM_STATE
D_XBC = D_SSM + D_BC
D_MIX = D_SCONV + D_SSM
D_IN = 3 * D_SCONV + D_SSM + D_XBC + SSM_HEADS
D_FF = 2816
EPS = 1e-6

LANES = 128
SUBLANES = 8
D_IN_PAD = D_IN + (LANES - SSM_HEADS)
HEAD_PAIRS = D_SSM // LANES

O_GATE_B = 0
O_GATE_C = D_SCONV
O_HV = 2 * D_SCONV
O_Z = 3 * D_SCONV
O_XS = O_Z + D_SSM
O_BC = O_XS + D_SSM
O_DT = O_BC + D_BC

CHUNK = 128
STAGE_A_COLS = tuple(range(0, O_DT + 1, D_SCONV))
STAGE_A_PLAN = ("sconv", "sconv", "prepare", "prepare", 1, 3, 5, "end", "end")
MIX_YA_AT = "sconv"
SEQ_TILE = 16
FFN_ROWS = 512
VMEM_LIMIT = 60000 * 1024

BF16 = jnp.bfloat16
F32 = jnp.float32


def _dot(a, b):
    return jnp.dot(a, b, preferred_element_type=F32)


def _dot_nt(a, b):
    return lax.dot_general(a, b, (((1,), (1,)), ((), ())), preferred_element_type=F32)


def _dot_tn(a, b):
    return lax.dot_general(a, b, (((0,), (0,)), ((), ())), preferred_element_type=F32)


def _split3(v):
    hi = v.astype(BF16)
    r1 = v - hi.astype(F32)
    mid = r1.astype(BF16)
    lo = (r1 - mid.astype(F32)).astype(BF16)
    return hi, mid, lo


def _dot3(v, m):
    hi, mid, lo = _split3(v)
    return _dot(hi, m) + _dot(mid, m) + _dot(lo, m)


def _dot3_left(m, v):
    hi, mid, lo = _split3(v)
    return _dot(m, hi) + _dot(m, mid) + _dot(m, lo)


def _rms(x, w):
    return x * lax.rsqrt(jnp.mean(x * x, axis=-1, keepdims=True) + EPS) * w


def _silu(x):
    return x * (1.0 / (1.0 + jnp.exp(-x)))


def _softplus(x):
    return jnp.maximum(x, 0.0) + jnp.log1p(jnp.exp(-jnp.abs(x)))


def _shifted(u, prev, j, rows_per_seq):
    t = u.shape[0]
    s = pltpu.roll(u, j, 0)
    if rows_per_seq is None:
        row = lax.broadcasted_iota(jnp.int32, (SUBLANES, u.shape[1]), 0)
        first = jnp.where(row < j, pltpu.roll(prev, j, 0), s[0:SUBLANES])
        if t == SUBLANES:
            return first
        return jnp.concatenate([first, s[SUBLANES:]], axis=0)
    row = lax.broadcasted_iota(jnp.int32, u.shape, 0) % SUBLANES
    return jnp.where(row < j, pltpu.roll(prev, t - SUBLANES + j, 0), s)


def _causal_conv(u, prev, w, rows_per_seq):
    k = w.shape[0]
    out = _shifted(u, prev, k - 1, rows_per_seq) * w[0:1]
    for i in range(1, k - 1):
        out = out + _shifted(u, prev, k - 1 - i, rows_per_seq) * w[i:i + 1]
    return out + u * w[k - 1:k]


def _sconv_mixer(gate_b, gate_c, hv, prev, sconv_w, sconv_norm, rows_per_seq):
    u = gate_c * hv
    conv_u = _causal_conv(u, prev, sconv_w, rows_per_seq)
    return _sconv_norm(gate_b, conv_u, sconv_norm), u


def _sconv_norm(gate_b, conv_u, sconv_norm):
    v = gate_b * conv_u
    first_half = lax.broadcasted_iota(jnp.int32, (v.shape[0], LANES), 1) < D_SCONV // SCONV_GROUPS
    parts = []
    for c in range(0, D_SCONV, LANES):
        vc = v[:, c:c + LANES]
        sq = vc * vc
        s_first = jnp.sum(jnp.where(first_half, sq, 0.0), axis=-1, keepdims=True)
        s_second = jnp.sum(jnp.where(first_half, 0.0, sq), axis=-1, keepdims=True)
        mean_sq = jnp.where(first_half, s_first, s_second) * (SCONV_GROUPS / D_SCONV)
        parts.append(vc * lax.rsqrt(mean_sq + EPS))
    return jnp.concatenate(parts, axis=1) * sconv_norm


def _ssd_prepare(dt_raw, dt_bias, a_log, tri, t):
    lane = lax.broadcasted_iota(jnp.int32, (t, LANES), 1)
    valid = lane < SSM_HEADS
    dt = jnp.where(valid, _softplus(dt_raw + dt_bias), 0.0)
    a = dt * (-jnp.exp(a_log))
    a_cum = _dot3_left(tri, a)
    return dt, a_cum


def _ssd_prepare_rows(dt_raw_t, dt_bias_t, a_log_t, tri, seq_ones=None):
    t = dt_raw_t.shape[1]
    dt_t = _softplus(dt_raw_t + dt_bias_t)
    a_t = dt_t * (-jnp.exp(a_log_t))
    hi, mid, lo = _split3(a_t)
    a_cum_t = _dot_nt(hi, tri) + _dot_nt(mid, tri) + _dot_nt(lo, tri)
    if seq_ones is None:
        a_total_t = jnp.broadcast_to(a_cum_t[:, t - 1:t], (SSM_HEADS, t))
    else:
        a_total_t = _dot_nt(hi, seq_ones) + _dot_nt(mid, seq_ones) + _dot_nt(lo, seq_ones)
    w_t = jnp.exp(a_total_t - a_cum_t) * dt_t
    pad = jnp.zeros((LANES - SSM_HEADS, t), F32)
    a_cum = jnp.concatenate([a_cum_t, pad], axis=0).T
    w = jnp.concatenate([w_t, pad], axis=0).T
    return a_cum, w, a_cum_t, dt_t, a_total_t


def _group_rms_ssm(y, w):
    parts = []
    for g in range(SSM_GROUPS):
        yg = y[:, g * D_GROUP:(g + 1) * D_GROUP]
        parts.append(yg * lax.rsqrt(jnp.mean(yg * yg, axis=-1, keepdims=True) + EPS))
    return jnp.concatenate(parts, axis=1) * w


def _ssd_heads(xs_of, bm_b, cm, a_cum, w, a_cum_t, dt_t, mask, state_t=None, gate=None, after_pair=None):
    t = a_cum.shape[0]
    assert t == LANES
    first_half = lax.broadcasted_iota(jnp.int32, (t, LANES), 1) < SSM_HEAD_DIM
    y_pairs, xw_pairs, ea_pairs, xs_pairs = [], [], [], []
    for g in range(SSM_GROUPS):
        cm_g = cm[:, g * SSM_STATE:(g + 1) * SSM_STATE]
        cb = _dot_nt(cm_g.astype(BF16), bm_b[:, g * SSM_STATE:(g + 1) * SSM_STATE])
        for q in range(HEADS_PER_GROUP // 2):
            pair = g * (HEADS_PER_GROUP // 2) + q
            cols = slice(pair * LANES, (pair + 1) * LANES)
            xs_p = xs_of(pair)
            rhs = xs_p.astype(BF16)
            if state_t is not None:
                rhs = jnp.concatenate([rhs, state_t[:, cols].astype(BF16)], axis=0)
            res, eas, ws = [], [], []
            for e in (2 * pair, 2 * pair + 1):
                a_col = jnp.broadcast_to(a_cum[:, e:e + 1], (t, LANES))
                decay = jnp.exp(jnp.where(mask, a_col - a_cum_t[e:e + 1, :], -jnp.inf))
                lhs = (cb * decay * dt_t[e:e + 1, :]).astype(BF16)
                ea = jnp.exp(a_col)
                if state_t is not None:
                    lhs = jnp.concatenate([lhs, (cm_g * ea).astype(BF16)], axis=1)
                res.append(_dot(lhs, rhs))
                eas.append(ea)
                ws.append(jnp.broadcast_to(w[:, e:e + 1], (t, LANES)))
            y_p = jnp.where(first_half, res[0], res[1])
            y_pairs.append(y_p if gate is None else gate(pair, y_p, xs_p))
            xw_pairs.append((xs_p * jnp.where(first_half, ws[0], ws[1])).astype(BF16))
            ea_pairs.append(jnp.where(first_half, eas[0], eas[1]))
            xs_pairs.append(xs_p)
            if after_pair is not None:
                after_pair(pair, y_pairs, xw_pairs, ea_pairs)
    return (jnp.concatenate(y_pairs, axis=1), jnp.concatenate(xw_pairs, axis=1),
            jnp.concatenate(ea_pairs, axis=1), jnp.concatenate(xs_pairs, axis=1))


def _ssm_tail(y, xs, z, d_full, ssm_norm):
    y = y + d_full * xs
    y = y * _silu(z)
    return _group_rms_ssm(y, ssm_norm)


def _inproj_kernel(x_ref, npre_ref, w_main_ref, w_dt_ref, o_ref):
    hn = _rms(x_ref[...], npre_ref[...]).astype(BF16)
    o_ref[:, 0:O_DT] = _dot(hn, w_main_ref[...])
    o_ref[:, O_DT:D_IN_PAD] = _dot(hn, w_dt_ref[...])


def _inproj(x, npre, w_main_b, w_dt_b, rows):
    m = x.shape[0]
    return pl.pallas_call(
        _inproj_kernel,
        grid=(m // rows,),
        in_specs=[
            pl.BlockSpec((rows, D_MODEL), lambda i: (i, 0)),
            pl.BlockSpec((1, D_MODEL), lambda i: (0, 0)),
            pl.BlockSpec((D_MODEL, O_DT), lambda i: (0, 0)),
            pl.BlockSpec((D_MODEL, LANES), lambda i: (0, 0)),
        ],
        out_specs=pl.BlockSpec((rows, D_IN_PAD), lambda i: (i, 0)),
        out_shape=jax.ShapeDtypeStruct((m, D_IN_PAD), F32),
        compiler_params=pltpu.CompilerParams(dimension_semantics=("arbitrary",), vmem_limit_bytes=VMEM_LIMIT),
        name="inproj",
    )(x, npre, w_main_b, w_dt_b)


def _meta_kernel(proj_ref, xconv_w_ref, xconv_b_ref, dt_bias_ref, a_log_ref, tri_ref, expand_ref,
                 sc_tail_ref, xbc_tail_ref, state_t_ref):
    t = N_META
    proj = proj_ref[...]
    u = proj[:, O_GATE_C:O_GATE_C + D_SCONV] * proj[:, O_HV:O_HV + D_SCONV]
    sc_tail_ref[...] = u[t - SUBLANES:t]
    xbc = proj[:, O_XS:O_XS + D_XBC]
    xbc_tail_ref[...] = xbc[t - SUBLANES:t]
    zeros_prev = jnp.zeros((SUBLANES, D_XBC), F32)
    xbc_c = _silu(_causal_conv(xbc, zeros_prev, xconv_w_ref[...], None) + xconv_b_ref[...])
    xs = xbc_c[:, 0:D_SSM]
    dt, a_cum = _ssd_prepare(proj[:, O_DT:O_DT + LANES], dt_bias_ref[...], a_log_ref[...],
                             tri_ref[0:t, 0:t], t)
    w = jnp.exp(a_cum[t - 1:t, :] - a_cum) * dt
    xw_b = (xs * _dot3(w, expand_ref[...])).astype(BF16)
    for g in range(SSM_GROUPS):
        bm_b = xbc_c[:, D_SSM + g * SSM_STATE:D_SSM + (g + 1) * SSM_STATE].astype(BF16)
        state_t_ref[:, g * D_GROUP:(g + 1) * D_GROUP] = _dot_tn(bm_b, xw_b[:, g * D_GROUP:(g + 1) * D_GROUP])


def _meta_state(proj_meta, xconv_w, xconv_b, dt_bias, a_log, tri, expand):
    full = lambda shape: pl.BlockSpec(shape, lambda: tuple(0 for _ in shape))
    return pl.pallas_call(
        _meta_kernel,
        in_specs=[full(proj_meta.shape), full(xconv_w.shape), full(xconv_b.shape), full(dt_bias.shape),
                  full(a_log.shape), full(tri.shape), full(expand.shape)],
        out_specs=[full((SUBLANES, D_SCONV)), full((SUBLANES, D_XBC)), full((SSM_STATE, D_SSM))],
        out_shape=[jax.ShapeDtypeStruct((SUBLANES, D_SCONV), F32),
                   jax.ShapeDtypeStruct((SUBLANES, D_XBC), F32),
                   jax.ShapeDtypeStruct((SSM_STATE, D_SSM), F32)],
        name="meta_state",
    )(proj_meta, xconv_w, xconv_b, dt_bias, a_log, tri, expand)


def _pmix_kernel(x_ref, npre_ref, w_main_ref, w_dt_ref, sconv_w_ref, sconv_norm_ref,
                 xconv_w_ref, xconv_b_ref, dt_bias_ref, a_log_ref, d_full_ref, ssm_norm_ref,
                 tri_ref, w_out_ref, npost_ref, sc0_ref, xbc0_ref, state0_t_ref,
                 h1_ref, sc_out_ref, xbc_out_ref, state_ref,
                 proj_s0, proj_s1, dt_s0, dt_s1, h_s0, h_s1, hn_s, sc_tail, xbc_tail, state_t_ref, *,
                 chunks_per_seq):
    t = CHUNK
    k = pl.program_id(0)
    chunk = lax.rem(jnp.maximum(k - 1, 0), chunks_per_seq)

    @pl.when(k == 0)
    def _():
        proj_s1[...] = jnp.zeros((t, O_DT), F32)
        dt_s1[...] = jnp.zeros((SSM_HEADS, t), F32)
        h_s1[...] = jnp.zeros((t, D_MODEL), F32)

    @pl.when(chunk == 0)
    def _():
        sc_tail[...] = sc0_ref[...]
        xbc_tail[...] = xbc0_ref[...]
        state_t_ref[...] = state0_t_ref[...]

    for parity in (0, 1):
        pl.when(lax.rem(k, 2) == parity)(functools.partial(
            _pmix_step, parity, x_ref, npre_ref, w_main_ref, w_dt_ref, sconv_w_ref, sconv_norm_ref,
            xconv_w_ref, xconv_b_ref, dt_bias_ref, a_log_ref, d_full_ref, ssm_norm_ref, tri_ref, w_out_ref,
            npost_ref, h1_ref, (proj_s0, proj_s1), (dt_s0, dt_s1), (h_s0, h_s1), hn_s, sc_tail, xbc_tail,
            state_t_ref))

    @pl.when(chunk == chunks_per_seq - 1)
    def _():
        sc_out_ref[0] = sc_tail[...]
        xbc_out_ref[0] = xbc_tail[...]
        for p in range(HEAD_PAIRS):
            state_ref[0, p * LANES:(p + 1) * LANES, :] = state_t_ref[:, p * LANES:(p + 1) * LANES].T


def _pmix_step(slot_a, x_ref, npre_ref, w_main_ref, w_dt_ref, sconv_w_ref, sconv_norm_ref,
               xconv_w_ref, xconv_b_ref, dt_bias_ref, a_log_ref, d_full_ref, ssm_norm_ref, tri_ref, w_out_ref,
               npost_ref, h1_ref, proj_s, dt_s, h_s, hn_s, sc_tail, xbc_tail, state_t_ref):
    t = CHUNK
    slot_b = 1 - slot_a

    h_a = x_ref[0]
    h_s[slot_a][...] = h_a
    hn_s[...] = _rms(h_a, npre_ref[...]).astype(BF16)

    def stage_a(slot_name):
        for piece, planned in enumerate(STAGE_A_PLAN):
            if planned == slot_name:
                if piece + 1 < len(STAGE_A_COLS):
                    off, end = STAGE_A_COLS[piece], STAGE_A_COLS[piece + 1]
                    proj_s[slot_a][:, off:end] = _dot(hn_s[...], w_main_ref[:, off:end])
                else:
                    dt_s[slot_a][...] = _dot_nt(w_dt_ref[...], hn_s[...])

    def proj(off, width):
        return proj_s[slot_b][:, off:off + width]

    h = h_s[slot_b][...]
    ya, u = _sconv_mixer(proj(O_GATE_B, D_SCONV), proj(O_GATE_C, D_SCONV), proj(O_HV, D_SCONV),
                         sc_tail[...], sconv_w_ref[...], sconv_norm_ref[...], None)
    sc_tail[...] = u[t - SUBLANES:t]
    stage_a("sconv")
    if MIX_YA_AT == "sconv":
        mix_ya = _dot(ya.astype(BF16), w_out_ref[0:D_SCONV, :])

    def xconv(col, width):
        cols = slice(col, col + width)
        raw = proj(O_XS + col, width)
        out = _silu(_causal_conv(raw, xbc_tail[:, cols], xconv_w_ref[:, cols], None) + xconv_b_ref[:, cols])
        xbc_tail[:, cols] = raw[t - SUBLANES:t]
        return out

    bc = xconv(D_SSM, D_BC)
    bm_b = bc[:, 0:SSM_GROUPS * SSM_STATE].astype(BF16)
    cm = bc[:, SSM_GROUPS * SSM_STATE:D_BC]
    stage_a("bc")
    a_cum, w, a_cum_t, dt_t, _ = _ssd_prepare_rows(dt_s[slot_b][...], dt_bias_ref[...], a_log_ref[...],
                                                   tri_ref[...])
    causal = lax.broadcasted_iota(jnp.int32, (t, t), 0) >= lax.broadcasted_iota(jnp.int32, (t, t), 1)
    state_t = state_t_ref[...]
    stage_a("prepare")
    if MIX_YA_AT == "prepare":
        mix_ya = _dot(ya.astype(BF16), w_out_ref[0:D_SCONV, :])

    def gate(p, y_p, xs_p):
        cols = slice(p * LANES, (p + 1) * LANES)
        return (y_p + d_full_ref[:, cols] * xs_p) * _silu(proj(O_Z + p * LANES, LANES))

    y, xw_b, ea_full, _ = _ssd_heads(lambda p: xconv(p * LANES, LANES), bm_b, cm, a_cum, w, a_cum_t, dt_t,
                                     causal, state_t, gate, lambda p, *partial: stage_a(p))
    chunk_decay = ea_full[t - 1:t, :]
    for g in range(SSM_GROUPS):
        cols = slice(g * D_GROUP, (g + 1) * D_GROUP)
        state_t_ref[:, cols] = chunk_decay[:, cols] * state_t[:, cols] + _dot_tn(
            bm_b[:, g * SSM_STATE:(g + 1) * SSM_STATE], xw_b[:, cols])
    stage_a("state")
    yb = _group_rms_ssm(y, ssm_norm_ref[...])
    if MIX_YA_AT == "end":
        mix_ya = _dot(ya.astype(BF16), w_out_ref[0:D_SCONV, :])
    mix = mix_ya + _dot(yb.astype(BF16), w_out_ref[D_SCONV:D_MIX, :])
    stage_a("end")
    h1_ref[0] = h + _rms(mix, npost_ref[...])


def _pmix(x_prompt, consts, sc0, xbc0, state0_t):
    batch, seq, _ = x_prompt.shape
    cps = seq // CHUNK
    n_tiles = batch * cps
    const_spec = lambda a: pl.BlockSpec(a.shape, lambda k: (0, 0))
    ins = list(consts) + [sc0, xbc0, state0_t]

    def tile_a(k):
        tile = jnp.minimum(k, n_tiles - 1)
        return (tile // cps, tile % cps, 0)

    def tile_b(k):
        tile = jnp.maximum(k - 1, 0)
        return (tile // cps, tile % cps, 0)

    def seq_b(k):
        return (jnp.maximum(k - 1, 0) // cps, 0, 0)

    return pl.pallas_call(
        functools.partial(_pmix_kernel, chunks_per_seq=cps),
        grid=(n_tiles + 1,),
        in_specs=[pl.BlockSpec((1, CHUNK, D_MODEL), tile_a)] + [const_spec(a) for a in ins],
        out_specs=[
            pl.BlockSpec((1, CHUNK, D_MODEL), tile_b),
            pl.BlockSpec((1, SUBLANES, D_SCONV), seq_b),
            pl.BlockSpec((1, SUBLANES, D_XBC), seq_b),
            pl.BlockSpec((1, D_SSM, SSM_STATE), seq_b),
        ],
        out_shape=[
            jax.ShapeDtypeStruct((batch, seq, D_MODEL), F32),
            jax.ShapeDtypeStruct((batch, SUBLANES, D_SCONV), F32),
            jax.ShapeDtypeStruct((batch, SUBLANES, D_XBC), F32),
            jax.ShapeDtypeStruct((batch, D_SSM, SSM_STATE), F32),
        ],
        scratch_shapes=[
            pltpu.VMEM((CHUNK, O_DT), F32),
            pltpu.VMEM((CHUNK, O_DT), F32),
            pltpu.VMEM((SSM_HEADS, CHUNK), F32),
            pltpu.VMEM((SSM_HEADS, CHUNK), F32),
            pltpu.VMEM((CHUNK, D_MODEL), F32),
            pltpu.VMEM((CHUNK, D_MODEL), F32),
            pltpu.VMEM((CHUNK, D_MODEL), BF16),
            pltpu.VMEM((SUBLANES, D_SCONV), F32),
            pltpu.VMEM((SUBLANES, D_XBC), F32),
            pltpu.VMEM((SSM_STATE, D_SSM), F32),
        ],
        compiler_params=pltpu.CompilerParams(dimension_semantics=("arbitrary",), vmem_limit_bytes=VMEM_LIMIT),
        name="prompt_mixer",
    )(x_prompt, *ins)


def _smix_kernel(x_ref, sc_prev_ref, xbc_prev_ref, state_in_ref, npre_ref, w_main_ref, w_dt_ref,
                 sconv_w_ref, sconv_norm_ref, xconv_w_ref, xconv_b_ref,
                 dt_bias_ref, a_log_ref, d_full_ref, ssm_norm_ref, w_out_ref, npost_ref,
                 h1_ref, u_ref, xbc_ref, state_ref, *, dec_seq):
    t = SEQ_TILE * dec_seq
    h = x_ref[...]
    hn = _rms(h, npre_ref[...]).astype(BF16)

    def proj(off, width):
        return _dot(hn, w_main_ref[:, off:off + width])

    ya, u = _sconv_mixer(proj(O_GATE_B, D_SCONV), proj(O_GATE_C, D_SCONV), proj(O_HV, D_SCONV),
                         sc_prev_ref[...], sconv_w_ref[...], sconv_norm_ref[...], dec_seq)
    u_ref[...] = u
    xbc = proj(O_XS, D_XBC)
    xbc_ref[...] = xbc
    xbc_c = _silu(_causal_conv(xbc, xbc_prev_ref[...], xconv_w_ref[...], dec_seq) + xconv_b_ref[...])
    xs = xbc_c[:, 0:D_SSM]
    bm = xbc_c[:, D_SSM:D_SSM + SSM_GROUPS * SSM_STATE]
    bm_b = bm.astype(BF16)
    cm = xbc_c[:, D_SSM + SSM_GROUPS * SSM_STATE:D_XBC]

    row = lax.broadcasted_iota(jnp.int32, (t, t), 0)
    col = lax.broadcasted_iota(jnp.int32, (t, t), 1)
    same_seq = row // dec_seq == col // dec_seq
    mask = jnp.logical_and(row >= col, same_seq)
    tri = jnp.where(mask, 1.0, 0.0).astype(BF16)
    seq_ones = jnp.where(same_seq, 1.0, 0.0).astype(BF16)
    a_cum, w, a_cum_t, dt_t, a_total_t = _ssd_prepare_rows(_dot_nt(w_dt_ref[...], hn), dt_bias_ref[...],
                                                           a_log_ref[...], tri, seq_ones)
    y, xw_b, ea_full, _ = _ssd_heads(lambda p: xs[:, p * LANES:(p + 1) * LANES], bm_b, cm, a_cum, w, a_cum_t,
                                     dt_t, mask)

    row_id = lax.broadcasted_iota(jnp.int32, (t, SSM_STATE), 0) // dec_seq
    seq_decay_t = jnp.exp(a_total_t)
    y_off_rows = []
    for s in range(SEQ_TILE):
        r0 = s * dec_seq
        q_t = jnp.broadcast_to(seq_decay_t[:, r0:r0 + 1], (SSM_HEADS, SSM_STATE))
        y_off_g = []
        for g in range(SSM_GROUPS):
            rows = slice(g * D_GROUP, (g + 1) * D_GROUP)
            s_prev = state_in_ref[s, rows, :]
            cm_s = cm[r0:r0 + dec_seq, g * SSM_STATE:(g + 1) * SSM_STATE].astype(BF16)
            y_off_g.append(_dot_nt(cm_s, s_prev.astype(BF16)))
            bm_g = jnp.where(row_id == s, bm[:, g * SSM_STATE:(g + 1) * SSM_STATE], 0.0).astype(BF16)
            decay = jnp.concatenate(
                [jnp.broadcast_to(q_t[e:e + 1, :], (SSM_HEAD_DIM, SSM_STATE))
                 for e in range(g * HEADS_PER_GROUP, (g + 1) * HEADS_PER_GROUP)], axis=0)
            state_ref[s, rows, :] = decay * s_prev + _dot_tn(xw_b[:, rows], bm_g)
        y_off_rows.append(jnp.concatenate(y_off_g, axis=1))
    y = y + jnp.concatenate(y_off_rows, axis=0) * ea_full
    yb = _ssm_tail(y, xs, proj(O_Z, D_SSM), d_full_ref[...], ssm_norm_ref[...])
    mix = _dot(ya.astype(BF16), w_out_ref[0:D_SCONV, :]) + _dot(yb.astype(BF16), w_out_ref[D_SCONV:D_MIX, :])
    h1_ref[...] = h + _rms(mix, npost_ref[...])


def _smix(x_rows, sc_prev, xbc_prev, state, consts, dec_seq):
    n_seq = state.shape[0]
    t = SEQ_TILE * dec_seq
    const_spec = lambda a: pl.BlockSpec(a.shape, lambda i: (0, 0), pipeline_mode=pl.Buffered(1))
    row_spec = lambda width: pl.BlockSpec((t, width), lambda i: (i, 0))
    state_spec = pl.BlockSpec((SEQ_TILE, D_SSM, SSM_STATE), lambda i: (i, 0, 0))
    return pl.pallas_call(
        functools.partial(_smix_kernel, dec_seq=dec_seq),
        grid=(n_seq // SEQ_TILE,),
        in_specs=[row_spec(D_MODEL), row_spec(D_SCONV), row_spec(D_XBC), state_spec]
        + [const_spec(a) for a in consts],
        out_specs=[row_spec(D_MODEL), row_spec(D_SCONV), row_spec(D_XBC), state_spec],
        out_shape=[
            jax.ShapeDtypeStruct((n_seq * dec_seq, D_MODEL), F32),
            jax.ShapeDtypeStruct((n_seq * dec_seq, D_SCONV), F32),
            jax.ShapeDtypeStruct((n_seq * dec_seq, D_XBC), F32),
            jax.ShapeDtypeStruct(state.shape, F32),
        ],
        compiler_params=pltpu.CompilerParams(dimension_semantics=("arbitrary",), vmem_limit_bytes=VMEM_LIMIT),
        name="sample_mixer",
    )(x_rows, sc_prev, xbc_prev, state, *consts)


def _ffn_kernel(hp_ref, hs_ref, npre_ref, w_gate_ref, w_up_ref, w_down_ref, npost_ref, op_ref, os_ref, *,
                prompt_steps):
    def body(h_ref, o_ref):
        h1 = h_ref[...]
        fn = _rms(h1, npre_ref[...]).astype(BF16)
        act = (_silu(_dot(fn, w_gate_ref[...])) * _dot(fn, w_up_ref[...])).astype(BF16)
        o_ref[...] = h1 + _rms(_dot(act, w_down_ref[...]), npost_ref[...])

    step = pl.program_id(0)
    pl.when(step < prompt_steps)(functools.partial(body, hp_ref, op_ref))
    pl.when(step >= prompt_steps)(functools.partial(body, hs_ref, os_ref))


def _ffn(h_prompt, h_sample, ffn_consts):
    mp, ms = h_prompt.shape[0], h_sample.shape[0]
    rows = FFN_ROWS
    p_steps, s_steps = mp // rows, ms // rows
    p_spec = pl.BlockSpec((rows, D_MODEL), lambda i: (jnp.minimum(i, p_steps - 1), 0))
    s_spec = pl.BlockSpec((rows, D_MODEL), lambda i: (jnp.maximum(i - p_steps, 0), 0))
    const_spec = lambda a: pl.BlockSpec(a.shape, lambda i: (0, 0), pipeline_mode=pl.Buffered(1))
    return pl.pallas_call(
        functools.partial(_ffn_kernel, prompt_steps=p_steps),
        grid=(p_steps + s_steps,),
        in_specs=[p_spec, s_spec] + [const_spec(a) for a in ffn_consts],
        out_specs=[p_spec, s_spec],
        out_shape=[jax.ShapeDtypeStruct((mp, D_MODEL), F32), jax.ShapeDtypeStruct((ms, D_MODEL), F32)],
        compiler_params=pltpu.CompilerParams(dimension_semantics=("arbitrary",), vmem_limit_bytes=VMEM_LIMIT),
        name="ffn",
    )(h_prompt, h_sample, *ffn_consts)


def _row(v, width=None):
    v = v.reshape(1, -1).astype(F32)
    if width is not None and v.shape[1] < width:
        v = jnp.pad(v, ((0, 0), (0, width - v.shape[1])))
    return v


def _layer(hp, hs, sc_state, xbc_state, ssm_state, meta_tokens,
           norm_mix_pre, norm_mix_post, w_in, sconv_w, sconv_norm, ssm_conv_w, ssm_conv_b,
           dt_bias, a_log, d_skip, ssm_norm, w_out, norm_ffn_pre, norm_ffn_post, w_gate, w_up, w_down):
    batch, seq, _ = hp.shape
    n_seq, dec_seq, _ = hs.shape

    w_main_b = w_in[:, 0:O_DT].astype(BF16)
    w_dt_b = jnp.pad(w_in[:, O_DT:D_IN], ((0, 0), (0, D_IN_PAD - D_IN))).astype(BF16)
    w_dt_t = w_in[:, O_DT:D_IN].T.astype(BF16)
    w_out_b = w_out.astype(BF16)
    w_gate_b, w_up_b, w_down_b = w_gate.astype(BF16), w_up.astype(BF16), w_down.astype(BF16)
    npre, npost = _row(norm_mix_pre), _row(norm_mix_post)
    fpre, fpost = _row(norm_ffn_pre), _row(norm_ffn_post)
    sconv_w = sconv_w.astype(F32)
    xconv_w = ssm_conv_w.astype(F32)
    xconv_b = _row(ssm_conv_b)
    dt_bias_p, a_log_p = _row(dt_bias, LANES), _row(a_log, LANES)
    dt_bias_t = jnp.broadcast_to(dt_bias.astype(F32)[:, None], (SSM_HEADS, LANES))
    a_log_t = jnp.broadcast_to(a_log.astype(F32)[:, None], (SSM_HEADS, LANES))
    d_full = jnp.repeat(d_skip.astype(F32), SSM_HEAD_DIM).reshape(1, D_SSM)
    sconv_norm_r, ssm_norm_r = _row(sconv_norm), _row(ssm_norm)

    r = jnp.arange(CHUNK)
    tri = (r[:, None] >= r[None, :]).astype(BF16)
    expand = (jnp.arange(LANES)[:, None] == (jnp.arange(D_SSM)[None, :] // SSM_HEAD_DIM)).astype(BF16)

    mixer_consts = (sconv_w, sconv_norm_r, xconv_w, xconv_b, dt_bias_t, a_log_t, d_full, ssm_norm_r)
    ffn_consts = (fpre, w_gate_b, w_up_b, w_down_b, fpost)

    proj_meta = _inproj(meta_tokens.astype(F32), npre, w_main_b, w_dt_b, N_META)
    sc0, xbc0, state0_t = _meta_state(proj_meta, xconv_w, xconv_b, dt_bias_p, a_log_p, tri, expand)

    pm_consts = (npre, w_main_b, w_dt_t) + mixer_consts + (tri, w_out_b, npost)
    h1, sc_tail, xbc_tail, p_state = _pmix(hp, pm_consts, sc0, xbc0, state0_t)
    new_sc_p = sc_tail[:, SUBLANES - (SCONV_WIDTH - 1):, :]
    new_xbc_p = xbc_tail[:, SUBLANES - (SSM_CONV_WIDTH - 1):, :]
    new_ssm_p = p_state.reshape(batch, SSM_GROUPS, HEADS_PER_GROUP, SSM_HEAD_DIM, SSM_STATE)

    hs_rows = hs.reshape(n_seq * dec_seq, D_MODEL)
    sc_prev = jnp.pad(sc_state, ((0, 0), (dec_seq - (SCONV_WIDTH - 1), 0), (0, 0))).reshape(n_seq * dec_seq, D_SCONV)
    xbc_prev = jnp.pad(xbc_state, ((0, 0), (dec_seq - (SSM_CONV_WIDTH - 1), 0), (0, 0))).reshape(n_seq * dec_seq, D_XBC)
    sm_consts = (npre, w_main_b, w_dt_t) + mixer_consts + (w_out_b, npost)
    h1_s, u_s, xbc_s, s_state = _smix(hs_rows, sc_prev, xbc_prev, ssm_state.reshape(n_seq, D_SSM, SSM_STATE),
                                      sm_consts, dec_seq)

    y_prompt, y_sample = _ffn(h1.reshape(batch * seq, D_MODEL), h1_s, ffn_consts)
    y_prompt = y_prompt.reshape(batch, seq, D_MODEL)
    y_sample = y_sample.reshape(n_seq, dec_seq, D_MODEL)
    new_sc_s = u_s.reshape(n_seq, dec_seq, D_SCONV)[:, dec_seq - (SCONV_WIDTH - 1):, :]
    new_xbc_s = xbc_s.reshape(n_seq, dec_seq, D_XBC)[:, dec_seq - (SSM_CONV_WIDTH - 1):, :]
    new_ssm_s = s_state.reshape(n_seq, SSM_GROUPS, HEADS_PER_GROUP, SSM_HEAD_DIM, SSM_STATE)
    return y_prompt, y_sample, new_sc_p, new_xbc_p, new_ssm_p, new_sc_s, new_xbc_s, new_ssm_s


def kernel(x_prompt, x_sample, state_sconv, state_ssm_conv, state_ssm, meta_tokens, norm_mix_pre, norm_mix_post, w_in, sconv_w, sconv_norm, ssm_conv_w, ssm_conv_b, dt_bias, A_log, D_skip, ssm_norm, w_out, norm_ffn_pre, norm_ffn_post, w_gate, w_up, w_down):
    depth = w_in.shape[0]
    assert depth == 1 and x_sample.shape[1] == SUBLANES
    outs = _layer(x_prompt, x_sample, state_sconv[0], state_ssm_conv[0], state_ssm[0], meta_tokens,
                  norm_mix_pre[0], norm_mix_post[0], w_in[0], sconv_w[0], sconv_norm[0], ssm_conv_w[0],
                  ssm_conv_b[0], dt_bias[0], A_log[0], D_skip[0], ssm_norm[0], w_out[0],
                  norm_ffn_pre[0], norm_ffn_post[0], w_gate[0], w_up[0], w_down[0])
    y_prompt, y_sample = outs[0], outs[1]
    return (y_prompt, y_sample) + tuple(o[None] for o in outs[2:])
```

```python
import functools

import jax
import jax.numpy as jnp
from jax import lax
from jax.experimental import pallas as pl
from jax.experimental.pallas import tpu as pltpu

D_MODEL = 1024
N_META = 16
D_SCONV = 512
SCONV_GROUPS = 8
SCONV_WIDTH = 3
SSM_HEAD_DIM = 64
SSM_HEADS = 16
D_SSM = 1024
SSM_GROUPS = 2
HEADS_PER_GROUP = 8
D_GROUP = HEADS_PER_GROUP * SSM_HEAD_DIM
SSM_STATE = 128
SSM_CONV_WIDTH = 4
D_BC = 2 * SSM_GROUPS * SSM_STATE
D_XBC = D_SSM + D_BC
D_MIX = D_SCONV + D_SSM
D_IN = 3 * D_SCONV + D_SSM + D_XBC + SSM_HEADS
D_FF = 2816
EPS = 1e-6

LANES = 128
SUBLANES = 8
D_IN_PAD = D_IN + (LANES - SSM_HEADS)
HEAD_PAIRS = D_SSM // LANES

O_GATE_B = 0
O_GATE_C = D_SCONV
O_HV = 2 * D_SCONV
O_Z = 3 * D_SCONV
O_XS = O_Z + D_SSM
O_BC = O_XS + D_SSM
O_DT = O_BC + D_BC

CHUNK = 128
STAGE_A_COLS = tuple(range(0, O_DT + 1, D_SCONV))
STAGE_A_PLAN = ("sconv", "sconv", "prepare", "prepare", 1, 3, 5, "end", "end")
MIX_YA_AT = "sconv"
SEQ_TILE = 16
FFN_ROWS = 512
VMEM_LIMIT = 60000 * 1024

BF16 = jnp.bfloat16
F32 = jnp.float32


def _resident_block(a):
    return (a.shape[0], O_DT) if a.shape[-1] == D_IN else a.shape


def _dot(a, b):
    return jnp.dot(a, b, preferred_element_type=F32)


def _dot_nt(a, b):
    return lax.dot_general(a, b, (((1,), (1,)), ((), ())), preferred_element_type=F32)


def _dot_tn(a, b):
    return lax.dot_general(a, b, (((0,), (0,)), ((), ())), preferred_element_type=F32)


def _split3(v):
    hi = v.astype(BF16)
    r1 = v - hi.astype(F32)
    mid = r1.astype(BF16)
    lo = (r1 - mid.astype(F32)).astype(BF16)
    return hi, mid, lo


def _dot3(v, m):
    hi, mid, lo = _split3(v)
    return _dot(hi, m) + _dot(mid, m) + _dot(lo, m)


def _dot3_left(m, v):
    hi, mid, lo = _split3(v)
    return _dot(m, hi) + _dot(m, mid) + _dot(m, lo)


def _rms(x, w):
    return x * lax.rsqrt(jnp.mean(x * x, axis=-1, keepdims=True) + EPS) * w


def _silu(x):
    return x * (1.0 / (1.0 + jnp.exp(-x)))


def _softplus(x):
    return jnp.maximum(x, 0.0) + jnp.log1p(jnp.exp(-jnp.abs(x)))


def _shifted(u, prev, j, rows_per_seq):
    t = u.shape[0]
    s = pltpu.roll(u, j, 0)
    if rows_per_seq is None:
        row = lax.broadcasted_iota(jnp.int32, (SUBLANES, u.shape[1]), 0)
        first = jnp.where(row < j, pltpu.roll(prev, j, 0), s[0:SUBLANES])
        if t == SUBLANES:
            return first
        return jnp.concatenate([first, s[SUBLANES:]], axis=0)
    row = lax.broadcasted_iota(jnp.int32, u.shape, 0) % SUBLANES
    return jnp.where(row < j, pltpu.roll(prev, t - SUBLANES + j, 0), s)


def _causal_conv(u, prev, w, rows_per_seq):
    k = w.shape[0]
    out = _shifted(u, prev, k - 1, rows_per_seq) * w[0:1]
    for i in range(1, k - 1):
        out = out + _shifted(u, prev, k - 1 - i, rows_per_seq) * w[i:i + 1]
    return out + u * w[k - 1:k]


def _sconv_mixer(gate_b, gate_c, hv, prev, sconv_w, sconv_norm, rows_per_seq):
    u = gate_c * hv
    conv_u = _causal_conv(u, prev, sconv_w, rows_per_seq)
    return _sconv_norm(gate_b, conv_u, sconv_norm), u


def _sconv_norm(gate_b, conv_u, sconv_norm):
    v = gate_b * conv_u
    first_half = lax.broadcasted_iota(jnp.int32, (v.shape[0], LANES), 1) < D_SCONV // SCONV_GROUPS
    parts = []
    for c in range(0, D_SCONV, LANES):
        vc = v[:, c:c + LANES]
        sq = vc * vc
        s_first = jnp.sum(jnp.where(first_half, sq, 0.0), axis=-1, keepdims=True)
        s_second = jnp.sum(jnp.where(first_half, 0.0, sq), axis=-1, keepdims=True)
        mean_sq = jnp.where(first_half, s_first, s_second) * (SCONV_GROUPS / D_SCONV)
        parts.append(vc * lax.rsqrt(mean_sq + EPS))
    return jnp.concatenate(parts, axis=1) * sconv_norm


def _ssd_prepare(dt_raw, dt_bias, a_log, tri, t):
    lane = lax.broadcasted_iota(jnp.int32, (t, LANES), 1)
    valid = lane < SSM_HEADS
    dt = jnp.where(valid, _softplus(dt_raw + dt_bias), 0.0)
    a = dt * (-jnp.exp(a_log))
    a_cum = _dot3_left(tri, a)
    return dt, a_cum


def _ssd_prepare_rows(dt_raw_t, dt_bias_t, a_log_t, tri, seq_ones=None):
    t = dt_raw_t.shape[1]
    dt_t = _softplus(dt_raw_t + dt_bias_t)
    a_t = dt_t * (-jnp.exp(a_log_t))
    hi, mid, lo = _split3(a_t)
    a_cum_t = _dot_nt(hi, tri) + _dot_nt(mid, tri) + _dot_nt(lo, tri)
    if seq_ones is None:
        a_total_t = jnp.broadcast_to(a_cum_t[:, t - 1:t], (SSM_HEADS, t))
    else:
        a_total_t = _dot_nt(hi, seq_ones) + _dot_nt(mid, seq_ones) + _dot_nt(lo, seq_ones)
    w_t = jnp.exp(a_total_t - a_cum_t) * dt_t
    pad = jnp.zeros((LANES - SSM_HEADS, t), F32)
    a_cum = jnp.concatenate([a_cum_t, pad], axis=0).T
    w = jnp.concatenate([w_t, pad], axis=0).T
    return a_cum, w, a_cum_t, dt_t, a_total_t


def _group_rms_ssm(y, w):
    parts = []
    for g in range(SSM_GROUPS):
        yg = y[:, g * D_GROUP:(g + 1) * D_GROUP]
        parts.append(yg * lax.rsqrt(jnp.mean(yg * yg, axis=-1, keepdims=True) + EPS))
    return jnp.concatenate(parts, axis=1) * w


def _ssd_heads(xs_of, bm_b, cm, a_cum, w, a_cum_t, dt_t, mask, state_t=None, gate=None, after_pair=None):
    t = a_cum.shape[0]
    assert t == LANES
    first_half = lax.broadcasted_iota(jnp.int32, (t, LANES), 1) < SSM_HEAD_DIM
    y_pairs, xw_pairs, ea_pairs, xs_pairs = [], [], [], []
    for g in range(SSM_GROUPS):
        cm_g = cm[:, g * SSM_STATE:(g + 1) * SSM_STATE]
        cb = _dot_nt(cm_g.astype(BF16), bm_b[:, g * SSM_STATE:(g + 1) * SSM_STATE])
        for q in range(HEADS_PER_GROUP // 2):
            pair = g * (HEADS_PER_GROUP // 2) + q
            cols = slice(pair * LANES, (pair + 1) * LANES)
            xs_p = xs_of(pair)
            rhs = xs_p.astype(BF16)
            if state_t is not None:
                rhs = jnp.concatenate([rhs, state_t[:, cols].astype(BF16)], axis=0)
            res, eas, ws = [], [], []
            for e in (2 * pair, 2 * pair + 1):
                a_col = jnp.broadcast_to(a_cum[:, e:e + 1], (t, LANES))
                decay = jnp.exp(jnp.where(mask, a_col - a_cum_t[e:e + 1, :], -jnp.inf))
                lhs = (cb * decay * dt_t[e:e + 1, :]).astype(BF16)
                ea = jnp.exp(a_col)
                if state_t is not None:
                    lhs = jnp.concatenate([lhs, (cm_g * ea).astype(BF16)], axis=1)
                res.append(_dot(lhs, rhs))
                eas.append(ea)
                ws.append(jnp.broadcast_to(w[:, e:e + 1], (t, LANES)))
            y_p = jnp.where(first_half, res[0], res[1])
            y_pairs.append(y_p if gate is None else gate(pair, y_p, xs_p))
            xw_pairs.append((xs_p * jnp.where(first_half, ws[0], ws[1])).astype(BF16))
            ea_pairs.append(jnp.where(first_half, eas[0], eas[1]))
            xs_pairs.append(xs_p)
            if after_pair is not None:
                after_pair(pair, y_pairs, xw_pairs, ea_pairs)
    return (jnp.concatenate(y_pairs, axis=1), jnp.concatenate(xw_pairs, axis=1),
            jnp.concatenate(ea_pairs, axis=1), jnp.concatenate(xs_pairs, axis=1))


def _ssm_tail(y, xs, z, d_full, ssm_norm):
    y = y + d_full * xs
    y = y * _silu(z)
    return _group_rms_ssm(y, ssm_norm)


def _inproj_kernel(x_ref, npre_ref, w_main_ref, w_dt_ref, o_ref):
    hn = _rms(x_ref[...], npre_ref[...]).astype(BF16)
    o_ref[:, 0:O_DT] = _dot(hn, w_main_ref[...])
    o_ref[:, O_DT:D_IN_PAD] = _dot(hn, w_dt_ref[...])


def _inproj(x, npre, w_main_b, w_dt_b, rows):
    m = x.shape[0]
    return pl.pallas_call(
        _inproj_kernel,
        grid=(m // rows,),
        in_specs=[
            pl.BlockSpec((rows, D_MODEL), lambda i: (i, 0)),
            pl.BlockSpec((1, D_MODEL), lambda i: (0, 0)),
            pl.BlockSpec((D_MODEL, O_DT), lambda i: (0, 0)),
            pl.BlockSpec((D_MODEL, LANES), lambda i: (0, 0)),
        ],
        out_specs=pl.BlockSpec((rows, D_IN_PAD), lambda i: (i, 0)),
        out_shape=jax.ShapeDtypeStruct((m, D_IN_PAD), F32),
        compiler_params=pltpu.CompilerParams(dimension_semantics=("arbitrary",), vmem_limit_bytes=VMEM_LIMIT),
        name="inproj",
    )(x, npre, w_main_b, w_dt_b)


def _meta_kernel(proj_ref, xconv_w_ref, xconv_b_ref, dt_bias_ref, a_log_ref, tri_ref, expand_ref,
                 sc_tail_ref, xbc_tail_ref, state_t_ref):
    t = N_META
    proj = proj_ref[...]
    u = proj[:, O_GATE_C:O_GATE_C + D_SCONV] * proj[:, O_HV:O_HV + D_SCONV]
    sc_tail_ref[...] = u[t - SUBLANES:t]
    xbc = proj[:, O_XS:O_XS + D_XBC]
    xbc_tail_ref[...] = xbc[t - SUBLANES:t]
    zeros_prev = jnp.zeros((SUBLANES, D_XBC), F32)
    xbc_c = _silu(_causal_conv(xbc, zeros_prev, xconv_w_ref[...], None) + xconv_b_ref[...])
    xs = xbc_c[:, 0:D_SSM]
    dt, a_cum = _ssd_prepare(proj[:, O_DT:O_DT + LANES], dt_bias_ref[...], a_log_ref[...],
                             tri_ref[0:t, 0:t], t)
    w = jnp.exp(a_cum[t - 1:t, :] - a_cum) * dt
    xw_b = (xs * _dot3(w, expand_ref[...])).astype(BF16)
    for g in range(SSM_GROUPS):
        bm_b = xbc_c[:, D_SSM + g * SSM_STATE:D_SSM + (g + 1) * SSM_STATE].astype(BF16)
        state_t_ref[:, g * D_GROUP:(g + 1) * D_GROUP] = _dot_tn(bm_b, xw_b[:, g * D_GROUP:(g + 1) * D_GROUP])


def _meta_state(proj_meta, xconv_w, xconv_b, dt_bias, a_log, tri, expand):
    full = lambda shape: pl.BlockSpec(shape, lambda: tuple(0 for _ in shape))
    return pl.pallas_call(
        _meta_kernel,
        in_specs=[full(proj_meta.shape), full(xconv_w.shape), full(xconv_b.shape), full(dt_bias.shape),
                  full(a_log.shape), full(tri.shape), full(expand.shape)],
        out_specs=[full((SUBLANES, D_SCONV)), full((SUBLANES, D_XBC)), full((SSM_STATE, D_SSM))],
        out_shape=[jax.ShapeDtypeStruct((SUBLANES, D_SCONV), F32),
                   jax.ShapeDtypeStruct((SUBLANES, D_XBC), F32),
                   jax.ShapeDtypeStruct((SSM_STATE, D_SSM), F32)],
        name="meta_state",
    )(proj_meta, xconv_w, xconv_b, dt_bias, a_log, tri, expand)


def _pmix_kernel(x_ref, npre_ref, w_main_ref, w_dt_ref, sconv_w_ref, sconv_norm_ref,
                 xconv_w_ref, xconv_b_ref, dt_bias_ref, a_log_ref, d_full_ref, ssm_norm_ref,
                 tri_ref, w_out_ref, npost_ref, sc0_ref, xbc0_ref, state0_t_ref,
                 h1_ref, sc_out_ref, xbc_out_ref, state_ref,
                 proj_s0, proj_s1, dt_s0, dt_s1, h_s0, h_s1, hn_s, sc_tail, xbc_tail, state_t_ref, *,
                 chunks_per_seq):
    t = CHUNK
    k = pl.program_id(0)
    chunk = lax.rem(jnp.maximum(k - 1, 0), chunks_per_seq)

    @pl.when(k == 0)
    def _():
        proj_s1[...] = jnp.zeros((t, O_DT), F32)
        dt_s1[...] = jnp.zeros((SSM_HEADS, t), F32)
        h_s1[...] = jnp.zeros((t, D_MODEL), F32)

    @pl.when(chunk == 0)
    def _():
        sc_tail[...] = sc0_ref[...]
        xbc_tail[...] = xbc0_ref[...]
        state_t_ref[...] = state0_t_ref[...]

    for parity in (0, 1):
        pl.when(lax.rem(k, 2) == parity)(functools.partial(
            _pmix_step, parity, x_ref, npre_ref, w_main_ref, w_dt_ref, sconv_w_ref, sconv_norm_ref,
            xconv_w_ref, xconv_b_ref, dt_bias_ref, a_log_ref, d_full_ref, ssm_norm_ref, tri_ref, w_out_ref,
            npost_ref, h1_ref, (proj_s0, proj_s1), (dt_s0, dt_s1), (h_s0, h_s1), hn_s, sc_tail, xbc_tail,
            state_t_ref))

    @pl.when(chunk == chunks_per_seq - 1)
    def _():
        sc_out_ref[0] = sc_tail[...]
        xbc_out_ref[0] = xbc_tail[...]
        for p in range(HEAD_PAIRS):
            state_ref[0, p * LANES:(p + 1) * LANES, :] = state_t_ref[:, p * LANES:(p + 1) * LANES].T


def _pmix_step(slot_a, x_ref, npre_ref, w_main_ref, w_dt_ref, sconv_w_ref, sconv_norm_ref,
               xconv_w_ref, xconv_b_ref, dt_bias_ref, a_log_ref, d_full_ref, ssm_norm_ref, tri_ref, w_out_ref,
               npost_ref, h1_ref, proj_s, dt_s, h_s, hn_s, sc_tail, xbc_tail, state_t_ref):
    t = CHUNK
    slot_b = 1 - slot_a

    h_a = x_ref[0]
    h_s[slot_a][...] = h_a
    hn_s[...] = _rms(h_a, npre_ref[...]).astype(BF16)

    def stage_a(slot_name):
        for piece, planned in enumerate(STAGE_A_PLAN):
            if planned == slot_name:
                if piece + 1 < len(STAGE_A_COLS):
                    off, end = STAGE_A_COLS[piece], STAGE_A_COLS[piece + 1]
                    proj_s[slot_a][:, off:end] = _dot(hn_s[...], w_main_ref[:, off:end])
                else:
                    dt_s[slot_a][...] = _dot_nt(w_dt_ref[...], hn_s[...])

    def proj(off, width):
        return proj_s[slot_b][:, off:off + width]

    ya, u = _sconv_mixer(proj(O_GATE_B, D_SCONV), proj(O_GATE_C, D_SCONV), proj(O_HV, D_SCONV),
                         sc_tail[...], sconv_w_ref[...], sconv_norm_ref[...], None)
    sc_tail[...] = u[t - SUBLANES:t]
    stage_a("sconv")
    if MIX_YA_AT == "sconv":
        mix_ya = _dot(ya.astype(BF16), w_out_ref[0:D_SCONV, :])

    def xconv(col, width):
        cols = slice(col, col + width)
        raw = proj(O_XS + col, width)
        out = _silu(_causal_conv(raw, xbc_tail[:, cols], xconv_w_ref[:, cols], None) + xconv_b_ref[:, cols])
        xbc_tail[:, cols] = raw[t - SUBLANES:t]
        return out

    bc = xconv(D_SSM, D_BC)
    bm_b = bc[:, 0:SSM_GROUPS * SSM_STATE].astype(BF16)
    cm = bc[:, SSM_GROUPS * SSM_STATE:D_BC]
    stage_a("bc")
    a_cum, w, a_cum_t, dt_t, _ = _ssd_prepare_rows(dt_s[slot_b][...], dt_bias_ref[...], a_log_ref[...],
                                                   tri_ref[...])
    causal = lax.broadcasted_iota(jnp.int32, (t, t), 0) >= lax.broadcasted_iota(jnp.int32, (t, t), 1)
    stage_a("prepare")
    if MIX_YA_AT == "prepare":
        mix_ya = _dot(ya.astype(BF16), w_out_ref[0:D_SCONV, :])

    def gate(p, y_p, xs_p):
        cols = slice(p * LANES, (p + 1) * LANES)
        return (y_p + d_full_ref[:, cols] * xs_p) * _silu(proj(O_Z + p * LANES, LANES))

    y, xw_b, ea_full, _ = _ssd_heads(lambda p: xconv(p * LANES, LANES), bm_b, cm, a_cum, w, a_cum_t, dt_t,
                                     causal, state_t_ref, gate, lambda p, *partial: stage_a(p))
    chunk_decay = ea_full[t - 1:t, :]
    for g in range(SSM_GROUPS):
        cols = slice(g * D_GROUP, (g + 1) * D_GROUP)
        state_t_ref[:, cols] = chunk_decay[:, cols] * state_t_ref[:, cols] + _dot_tn(
            bm_b[:, g * SSM_STATE:(g + 1) * SSM_STATE], xw_b[:, cols])
    stage_a("state")
    yb = _group_rms_ssm(y, ssm_norm_ref[...])
    if MIX_YA_AT == "end":
        mix_ya = _dot(ya.astype(BF16), w_out_ref[0:D_SCONV, :])
    mix = mix_ya + _dot(yb.astype(BF16), w_out_ref[D_SCONV:D_MIX, :])
    stage_a("end")
    h1_ref[0] = h_s[slot_b][...] + _rms(mix, npost_ref[...])


def _pmix(x_prompt, consts, sc0, xbc0, state0_t):
    batch, seq, _ = x_prompt.shape
    cps = seq // CHUNK
    n_tiles = batch * cps
    const_spec = lambda a: pl.BlockSpec(_resident_block(a),lambda k: (0, 0))
    ins = list(consts) + [sc0, xbc0, state0_t]

    def tile_a(k):
        tile = jnp.minimum(k, n_tiles - 1)
        return (tile // cps, tile % cps, 0)

    def tile_b(k):
        tile = jnp.maximum(k - 1, 0)
        return (tile // cps, tile % cps, 0)

    def seq_b(k):
        return (jnp.maximum(k - 1, 0) // cps, 0, 0)

    return pl.pallas_call(
        functools.partial(_pmix_kernel, chunks_per_seq=cps),
        grid=(n_tiles + 1,),
        in_specs=[pl.BlockSpec((1, CHUNK, D_MODEL), tile_a)] + [const_spec(a) for a in ins],
        out_specs=[
            pl.BlockSpec((1, CHUNK, D_MODEL), tile_b),
            pl.BlockSpec((1, SUBLANES, D_SCONV), seq_b),
            pl.BlockSpec((1, SUBLANES, D_XBC), seq_b),
            pl.BlockSpec((1, D_SSM, SSM_STATE), seq_b),
        ],
        out_shape=[
            jax.ShapeDtypeStruct((batch, seq, D_MODEL), F32),
            jax.ShapeDtypeStruct((batch, SUBLANES, D_SCONV), F32),
            jax.ShapeDtypeStruct((batch, SUBLANES, D_XBC), F32),
            jax.ShapeDtypeStruct((batch, D_SSM, SSM_STATE), F32),
        ],
        scratch_shapes=[
            pltpu.VMEM((CHUNK, O_DT), F32),
            pltpu.VMEM((CHUNK, O_DT), F32),
            pltpu.VMEM((SSM_HEADS, CHUNK), F32),
            pltpu.VMEM((SSM_HEADS, CHUNK), F32),
            pltpu.VMEM((CHUNK, D_MODEL), F32),
            pltpu.VMEM((CHUNK, D_MODEL), F32),
            pltpu.VMEM((CHUNK, D_MODEL), BF16),
            pltpu.VMEM((SUBLANES, D_SCONV), F32),
            pltpu.VMEM((SUBLANES, D_XBC), F32),
            pltpu.VMEM((SSM_STATE, D_SSM), F32),
        ],
        compiler_params=pltpu.CompilerParams(dimension_semantics=("arbitrary",), vmem_limit_bytes=VMEM_LIMIT),
        name="prompt_mixer",
    )(x_prompt, *ins)


def _smix_kernel(x_ref, sc_prev_ref, xbc_prev_ref, state_in_ref, npre_ref, w_main_ref, w_dt_ref,
                 sconv_w_ref, sconv_norm_ref, xconv_w_ref, xconv_b_ref,
                 dt_bias_ref, a_log_ref, d_full_ref, ssm_norm_ref, w_out_ref, npost_ref,
                 h1_ref, u_ref, xbc_ref, state_ref, *, dec_seq):
    t = SEQ_TILE * dec_seq
    h = x_ref[...]
    hn = _rms(h, npre_ref[...]).astype(BF16)

    def proj(off, width):
        return _dot(hn, w_main_ref[:, off:off + width])

    ya, u = _sconv_mixer(proj(O_GATE_B, D_SCONV), proj(O_GATE_C, D_SCONV), proj(O_HV, D_SCONV),
                         sc_prev_ref[...], sconv_w_ref[...], sconv_norm_ref[...], dec_seq)
    u_ref[...] = u
    xbc = proj(O_XS, D_XBC)
    xbc_ref[...] = xbc
    xbc_c = _silu(_causal_conv(xbc, xbc_prev_ref[...], xconv_w_ref[...], dec_seq) + xconv_b_ref[...])
    xs = xbc_c[:, 0:D_SSM]
    bm = xbc_c[:, D_SSM:D_SSM + SSM_GROUPS * SSM_STATE]
    bm_b = bm.astype(BF16)
    cm = xbc_c[:, D_SSM + SSM_GROUPS * SSM_STATE:D_XBC]

    row = lax.broadcasted_iota(jnp.int32, (t, t), 0)
    col = lax.broadcasted_iota(jnp.int32, (t, t), 1)
    same_seq = row // dec_seq == col // dec_seq
    mask = jnp.logical_and(row >= col, same_seq)
    tri = jnp.where(mask, 1.0, 0.0).astype(BF16)
    seq_ones = jnp.where(same_seq, 1.0, 0.0).astype(BF16)
    a_cum, w, a_cum_t, dt_t, a_total_t = _ssd_prepare_rows(_dot_nt(w_dt_ref[...], hn), dt_bias_ref[...],
                                                           a_log_ref[...], tri, seq_ones)
    y, xw_b, ea_full, _ = _ssd_heads(lambda p: xs[:, p * LANES:(p + 1) * LANES], bm_b, cm, a_cum, w, a_cum_t,
                                     dt_t, mask)

    row_id = lax.broadcasted_iota(jnp.int32, (t, SSM_STATE), 0) // dec_seq
    seq_decay_t = jnp.exp(a_total_t)
    y_off_rows = []
    for s in range(SEQ_TILE):
        r0 = s * dec_seq
        q_t = jnp.broadcast_to(seq_decay_t[:, r0:r0 + 1], (SSM_HEADS, SSM_STATE))
        y_off_g = []
        for g in range(SSM_GROUPS):
            rows = slice(g * D_GROUP, (g + 1) * D_GROUP)
            s_prev = state_in_ref[s, rows, :]
            cm_s = cm[r0:r0 + dec_seq, g * SSM_STATE:(g + 1) * SSM_STATE].astype(BF16)
            y_off_g.append(_dot_nt(cm_s, s_prev.astype(BF16)))
            bm_g = jnp.where(row_id == s, bm[:, g * SSM_STATE:(g + 1) * SSM_STATE], 0.0).astype(BF16)
            decay = jnp.concatenate(
                [jnp.broadcast_to(q_t[e:e + 1, :], (SSM_HEAD_DIM, SSM_STATE))
                 for e in range(g * HEADS_PER_GROUP, (g + 1) * HEADS_PER_GROUP)], axis=0)
            state_ref[s, rows, :] = decay * s_prev + _dot_tn(xw_b[:, rows], bm_g)
        y_off_rows.append(jnp.concatenate(y_off_g, axis=1))
    y = y + jnp.concatenate(y_off_rows, axis=0) * ea_full
    yb = _ssm_tail(y, xs, proj(O_Z, D_SSM), d_full_ref[...], ssm_norm_ref[...])
    mix = _dot(ya.astype(BF16), w_out_ref[0:D_SCONV, :]) + _dot(yb.astype(BF16), w_out_ref[D_SCONV:D_MIX, :])
    h1_ref[...] = h + _rms(mix, npost_ref[...])


def _smix(x_rows, sc_prev, xbc_prev, state, consts, dec_seq):
    n_seq = state.shape[0]
    t = SEQ_TILE * dec_seq
    const_spec = lambda a: pl.BlockSpec(_resident_block(a),lambda i: (0, 0), pipeline_mode=pl.Buffered(1))
    row_spec = lambda width: pl.BlockSpec((t, width), lambda i: (i, 0))
    state_spec = pl.BlockSpec((SEQ_TILE, D_SSM, SSM_STATE), lambda i: (i, 0, 0))
    return pl.pallas_call(
        functools.partial(_smix_kernel, dec_seq=dec_seq),
        grid=(n_seq // SEQ_TILE,),
        in_specs=[row_spec(D_MODEL), row_spec(D_SCONV), row_spec(D_XBC), state_spec]
        + [const_spec(a) for a in consts],
        out_specs=[row_spec(D_MODEL), row_spec(D_SCONV), row_spec(D_XBC), state_spec],
        out_shape=[
            jax.ShapeDtypeStruct((n_seq * dec_seq, D_MODEL), F32),
            jax.ShapeDtypeStruct((n_seq * dec_seq, D_SCONV), F32),
            jax.ShapeDtypeStruct((n_seq * dec_seq, D_XBC), F32),
            jax.ShapeDtypeStruct(state.shape, F32),
        ],
        compiler_params=pltpu.CompilerParams(dimension_semantics=("arbitrary",), vmem_limit_bytes=VMEM_LIMIT),
        name="sample_mixer",
    )(x_rows, sc_prev, xbc_prev, state, *consts)


def _ffn_kernel(hp_ref, hs_ref, npre_ref, w_gate_ref, w_up_ref, w_down_ref, npost_ref, op_ref, os_ref, *,
                prompt_steps):
    def body(h_ref, o_ref):
        h1 = h_ref[...]
        fn = _rms(h1, npre_ref[...]).astype(BF16)
        act = (_silu(_dot(fn, w_gate_ref[...])) * _dot(fn, w_up_ref[...])).astype(BF16)
        o_ref[...] = h1 + _rms(_dot(act, w_down_ref[...]), npost_ref[...])

    step = pl.program_id(0)
    pl.when(step < prompt_steps)(functools.partial(body, hp_ref, op_ref))
    pl.when(step >= prompt_steps)(functools.partial(body, hs_ref, os_ref))


def _ffn(h_prompt, h_sample, ffn_consts):
    mp, ms = h_prompt.shape[0], h_sample.shape[0]
    rows = FFN_ROWS
    p_steps, s_steps = mp // rows, ms // rows
    p_spec = pl.BlockSpec((rows, D_MODEL), lambda i: (jnp.minimum(i, p_steps - 1), 0))
    s_spec = pl.BlockSpec((rows, D_MODEL), lambda i: (jnp.maximum(i - p_steps, 0), 0))
    const_spec = lambda a: pl.BlockSpec(_resident_block(a),lambda i: (0, 0), pipeline_mode=pl.Buffered(1))
    return pl.pallas_call(
        functools.partial(_ffn_kernel, prompt_steps=p_steps),
        grid=(p_steps + s_steps,),
        in_specs=[p_spec, s_spec] + [const_spec(a) for a in ffn_consts],
        out_specs=[p_spec, s_spec],
        out_shape=[jax.ShapeDtypeStruct((mp, D_MODEL), F32), jax.ShapeDtypeStruct((ms, D_MODEL), F32)],
        compiler_params=pltpu.CompilerParams(dimension_semantics=("arbitrary",), vmem_limit_bytes=VMEM_LIMIT),
        name="ffn",
    )(h_prompt, h_sample, *ffn_consts)


def _row(v, width=None):
    v = v.reshape(1, -1).astype(F32)
    if width is not None and v.shape[1] < width:
        v = jnp.pad(v, ((0, 0), (0, width - v.shape[1])))
    return v


def _layer(hp, hs, sc_state, xbc_state, ssm_state, meta_tokens,
           norm_mix_pre, norm_mix_post, w_in, sconv_w, sconv_norm, ssm_conv_w, ssm_conv_b,
           dt_bias, a_log, d_skip, ssm_norm, w_out, norm_ffn_pre, norm_ffn_post, w_gate, w_up, w_down):
    batch, seq, _ = hp.shape
    n_seq, dec_seq, _ = hs.shape

    w_main_b = w_in.astype(BF16)
    w_dt_b = jnp.pad(w_in[:, O_DT:D_IN], ((0, 0), (0, D_IN_PAD - D_IN))).astype(BF16)
    w_dt_t = w_in[:, O_DT:D_IN].T.astype(BF16)
    w_out_b = w_out.astype(BF16)
    w_gate_b, w_up_b, w_down_b = w_gate.astype(BF16), w_up.astype(BF16), w_down.astype(BF16)
    npre, npost = _row(norm_mix_pre), _row(norm_mix_post)
    fpre, fpost = _row(norm_ffn_pre), _row(norm_ffn_post)
    sconv_w = sconv_w.astype(F32)
    xconv_w = ssm_conv_w.astype(F32)
    xconv_b = _row(ssm_conv_b)
    dt_bias_p, a_log_p = _row(dt_bias, LANES), _row(a_log, LANES)
    dt_bias_t = jnp.broadcast_to(dt_bias.astype(F32)[:, None], (SSM_HEADS, LANES))
    a_log_t = jnp.broadcast_to(a_log.astype(F32)[:, None], (SSM_HEADS, LANES))
    d_full = jnp.repeat(d_skip.astype(F32), SSM_HEAD_DIM).reshape(1, D_SSM)
    sconv_norm_r, ssm_norm_r = _row(sconv_norm), _row(ssm_norm)

    r = jnp.arange(CHUNK)
    tri = (r[:, None] >= r[None, :]).astype(BF16)
    expand = (jnp.arange(LANES)[:, None] == (jnp.arange(D_SSM)[None, :] // SSM_HEAD_DIM)).astype(BF16)

    mixer_consts = (sconv_w, sconv_norm_r, xconv_w, xconv_b, dt_bias_t, a_log_t, d_full, ssm_norm_r)
    ffn_consts = (fpre, w_gate_b, w_up_b, w_down_b, fpost)

    proj_meta = _inproj(meta_tokens.astype(F32), npre, w_main_b, w_dt_b, N_META)
    sc0, xbc0, state0_t = _meta_state(proj_meta, xconv_w, xconv_b, dt_bias_p, a_log_p, tri, expand)

    pm_consts = (npre, w_main_b, w_dt_t) + mixer_consts + (tri, w_out_b, npost)
    h1, sc_tail, xbc_tail, p_state = _pmix(hp, pm_consts, sc0, xbc0, state0_t)
    new_sc_p = sc_tail[:, SUBLANES - (SCONV_WIDTH - 1):, :]
    new_xbc_p = xbc_tail[:, SUBLANES - (SSM_CONV_WIDTH - 1):, :]
    new_ssm_p = p_state.reshape(batch, SSM_GROUPS, HEADS_PER_GROUP, SSM_HEAD_DIM, SSM_STATE)

    hs_rows = hs.reshape(n_seq * dec_seq, D_MODEL)
    sc_prev = jnp.pad(sc_state, ((0, 0), (dec_seq - (SCONV_WIDTH - 1), 0), (0, 0))).reshape(n_seq * dec_seq, D_SCONV)
    xbc_prev = jnp.pad(xbc_state, ((0, 0), (dec_seq - (SSM_CONV_WIDTH - 1), 0), (0, 0))).reshape(n_seq * dec_seq, D_XBC)
    sm_consts = (npre, w_main_b, w_dt_t) + mixer_consts + (w_out_b, npost)
    h1_s, u_s, xbc_s, s_state = _smix(hs_rows, sc_prev, xbc_prev, ssm_state.reshape(n_seq, D_SSM, SSM_STATE),
                                      sm_consts, dec_seq)

    y_prompt, y_sample = _ffn(h1.reshape(batch * seq, D_MODEL), h1_s, ffn_consts)
    y_prompt = y_prompt.reshape(batch, seq, D_MODEL)
    y_sample = y_sample.reshape(n_seq, dec_seq, D_MODEL)
    new_sc_s = u_s.reshape(n_seq, dec_seq, D_SCONV)[:, dec_seq - (SCONV_WIDTH - 1):, :]
    new_xbc_s = xbc_s.reshape(n_seq, dec_seq, D_XBC)[:, dec_seq - (SSM_CONV_WIDTH - 1):, :]
    new_ssm_s = s_state.reshape(n_seq, SSM_GROUPS, HEADS_PER_GROUP, SSM_HEAD_DIM, SSM_STATE)
    return y_prompt, y_sample, new_sc_p, new_xbc_p, new_ssm_p, new_sc_s, new_xbc_s, new_ssm_s


def kernel(x_prompt, x_sample, state_sconv, state_ssm_conv, state_ssm, meta_tokens, norm_mix_pre, norm_mix_post, w_in, sconv_w, sconv_norm, ssm_conv_w, ssm_conv_b, dt_bias, A_log, D_skip, ssm_norm, w_out, norm_ffn_pre, norm_ffn_post, w_gate, w_up, w_down):
    depth = w_in.shape[0]
    assert depth == 1 and x_sample.shape[1] == SUBLANES
    outs = _layer(x_prompt, x_sample, state_sconv[0], state_ssm_conv[0], state_ssm[0], meta_tokens,
                  norm_mix_pre[0], norm_mix_post[0], w_in[0], sconv_w[0], sconv_norm[0], ssm_conv_w[0],
                  ssm_conv_b[0], dt_bias[0], A_log[0], D_skip[0], ssm_norm[0], w_out[0],
                  norm_ffn_pre[0], norm_ffn_post[0], w_gate[0], w_up[0], w_down[0])
    y_prompt, y_sample = outs[0], outs[1]
    return (y_prompt, y_sample) + tuple(o[None] for o in outs[2:])
```

```python
import functools

import jax
import jax.numpy as jnp
from jax import lax
from jax.experimental import pallas as pl
from jax.experimental.pallas import tpu as pltpu

D_MODEL = 1024
N_META = 16
D_SCONV = 512
SCONV_GROUPS = 8
SCONV_WIDTH = 3
SSM_HEAD_DIM = 64
SSM_HEADS = 16
D_SSM = 1024
SSM_GROUPS = 2
HEADS_PER_GROUP = 8
D_GROUP = HEADS_PER_GROUP * SSM_HEAD_DIM
SSM_STATE = 128
SSM_CONV_WIDTH = 4
D_BC = 2 * SSM_GROUPS * SSM_STATE
D_XBC = D_SSM + D_BC
D_MIX = D_SCONV + D_SSM
D_IN = 3 * D_SCONV + D_SSM + D_XBC + SSM_HEADS
D_FF = 2816
EPS = 1e-6

LANES = 128
SUBLANES = 8
D_IN_PAD = D_IN + (LANES - SSM_HEADS)
HEAD_PAIRS = D_SSM // LANES

O_GATE_B = 0
O_GATE_C = D_SCONV
O_HV = 2 * D_SCONV
O_Z = 3 * D_SCONV
O_XS = O_Z + D_SSM
O_BC = O_XS + D_SSM
O_DT = O_BC + D_BC

CHUNK = 128
STAGE_A_COLS = tuple(range(0, O_DT + 1, D_SCONV))
STAGE_A_PLAN = ("sconv", "sconv", "prepare", "prepare", 1, 3, 5, "end", "end")
MIX_YA_AT = "sconv"
SEQ_TILE = 16
FFN_ROWS = 512
VMEM_LIMIT = 60000 * 1024

BF16 = jnp.bfloat16
F32 = jnp.float32


def _dot(a, b):
    return jnp.dot(a, b, preferred_element_type=F32)


def _dot_nt(a, b):
    return lax.dot_general(a, b, (((1,), (1,)), ((), ())), preferred_element_type=F32)


def _dot_tn(a, b):
    return lax.dot_general(a, b, (((0,), (0,)), ((), ())), preferred_element_type=F32)


def _split3(v):
    hi = v.astype(BF16)
    r1 = v - hi.astype(F32)
    mid = r1.astype(BF16)
    lo = (r1 - mid.astype(F32)).astype(BF16)
    return hi, mid, lo


def _dot3(v, m):
    hi, mid, lo = _split3(v)
    return _dot(hi, m) + _dot(mid, m) + _dot(lo, m)


def _dot3_left(m, v):
    hi, mid, lo = _split3(v)
    return _dot(m, hi) + _dot(m, mid) + _dot(m, lo)


def _rms(x, w):
    return x * lax.rsqrt(jnp.mean(x * x, axis=-1, keepdims=True) + EPS) * w


def _silu(x):
    return x * (1.0 / (1.0 + jnp.exp(-x)))


def _softplus(x):
    return jnp.maximum(x, 0.0) + jnp.log1p(jnp.exp(-jnp.abs(x)))


def _shifted(u, prev, j, rows_per_seq):
    t = u.shape[0]
    s = pltpu.roll(u, j, 0)
    if rows_per_seq is None:
        row = lax.broadcasted_iota(jnp.int32, (SUBLANES, u.shape[1]), 0)
        first = jnp.where(row < j, pltpu.roll(prev, j, 0), s[0:SUBLANES])
        if t == SUBLANES:
            return first
        return jnp.concatenate([first, s[SUBLANES:]], axis=0)
    row = lax.broadcasted_iota(jnp.int32, u.shape, 0) % SUBLANES
    return jnp.where(row < j, pltpu.roll(prev, t - SUBLANES + j, 0), s)


def _causal_conv(u, prev, w, rows_per_seq):
    k = w.shape[0]
    out = _shifted(u, prev, k - 1, rows_per_seq) * w[0:1]
    for i in range(1, k - 1):
        out = out + _shifted(u, prev, k - 1 - i, rows_per_seq) * w[i:i + 1]
    return out + u * w[k - 1:k]


def _sconv_mixer(gate_b, gate_c, hv, prev, sconv_w, sconv_norm, rows_per_seq):
    u = gate_c * hv
    conv_u = _causal_conv(u, prev, sconv_w, rows_per_seq)
    return _sconv_norm(gate_b, conv_u, sconv_norm), u


def _sconv_norm(gate_b, conv_u, sconv_norm):
    v = gate_b * conv_u
    first_half = lax.broadcasted_iota(jnp.int32, (v.shape[0], LANES), 1) < D_SCONV // SCONV_GROUPS
    parts = []
    for c in range(0, D_SCONV, LANES):
        vc = v[:, c:c + LANES]
        sq = vc * vc
        s_first = jnp.sum(jnp.where(first_half, sq, 0.0), axis=-1, keepdims=True)
        s_second = jnp.sum(jnp.where(first_half, 0.0, sq), axis=-1, keepdims=True)
        mean_sq = jnp.where(first_half, s_first, s_second) * (SCONV_GROUPS / D_SCONV)
        parts.append(vc * lax.rsqrt(mean_sq + EPS))
    return jnp.concatenate(parts, axis=1) * sconv_norm


def _ssd_prepare(dt_raw, dt_bias, a_log, tri, t):
    lane = lax.broadcasted_iota(jnp.int32, (t, LANES), 1)
    valid = lane < SSM_HEADS
    dt = jnp.where(valid, _softplus(dt_raw + dt_bias), 0.0)
    a = dt * (-jnp.exp(a_log))
    a_cum = _dot3_left(tri, a)
    return dt, a_cum


def _ssd_prepare_rows(dt_raw_t, dt_bias_t, a_log_t, tri, seq_ones=None):
    t = dt_raw_t.shape[1]
    dt_t = _softplus(dt_raw_t + dt_bias_t)
    a_t = dt_t * (-jnp.exp(a_log_t))
    hi, mid, lo = _split3(a_t)
    a_cum_t = _dot_nt(hi, tri) + _dot_nt(mid, tri) + _dot_nt(lo, tri)
    if seq_ones is None:
        a_total_t = jnp.broadcast_to(a_cum_t[:, t - 1:t], (SSM_HEADS, t))
    else:
        a_total_t = _dot_nt(hi, seq_ones) + _dot_nt(mid, seq_ones) + _dot_nt(lo, seq_ones)
    w_t = jnp.exp(a_total_t - a_cum_t) * dt_t
    pad = jnp.zeros((LANES - SSM_HEADS, t), F32)
    a_cum = jnp.concatenate([a_cum_t, pad], axis=0).T
    w = jnp.concatenate([w_t, pad], axis=0).T
    return a_cum, w, a_cum_t, dt_t, a_total_t


def _group_rms_ssm(y, w):
    parts = []
    for g in range(SSM_GROUPS):
        yg = y[:, g * D_GROUP:(g + 1) * D_GROUP]
        parts.append(yg * lax.rsqrt(jnp.mean(yg * yg, axis=-1, keepdims=True) + EPS))
    return jnp.concatenate(parts, axis=1) * w


def _ssd_heads(xs_of, bm_b, cm, a_cum, w, a_cum_t, dt_t, mask, state_t=None, gate=None, after_pair=None):
    t = a_cum.shape[0]
    assert t == LANES
    first_half = lax.broadcasted_iota(jnp.int32, (t, LANES), 1) < SSM_HEAD_DIM
    y_pairs, xw_pairs, ea_pairs, xs_pairs = [], [], [], []
    for g in range(SSM_GROUPS):
        cm_g = cm[:, g * SSM_STATE:(g + 1) * SSM_STATE]
        cb = _dot_nt(cm_g.astype(BF16), bm_b[:, g * SSM_STATE:(g + 1) * SSM_STATE])
        for q in range(HEADS_PER_GROUP // 2):
            pair = g * (HEADS_PER_GROUP // 2) + q
            cols = slice(pair * LANES, (pair + 1) * LANES)
            xs_p = xs_of(pair)
            rhs = xs_p.astype(BF16)
            if state_t is not None:
                rhs = jnp.concatenate([rhs, state_t[:, cols].astype(BF16)], axis=0)
            res, eas, ws = [], [], []
            for e in (2 * pair, 2 * pair + 1):
                a_col = jnp.broadcast_to(a_cum[:, e:e + 1], (t, LANES))
                decay = jnp.exp(jnp.where(mask, a_col - a_cum_t[e:e + 1, :], -jnp.inf))
                lhs = (cb * decay * dt_t[e:e + 1, :]).astype(BF16)
                ea = jnp.exp(a_col)
                if state_t is not None:
                    lhs = jnp.concatenate([lhs, (cm_g * ea).astype(BF16)], axis=1)
                res.append(_dot(lhs, rhs))
                eas.append(ea)
                ws.append(jnp.broadcast_to(w[:, e:e + 1], (t, LANES)))
            y_p = jnp.where(first_half, res[0], res[1])
            y_pairs.append(y_p if gate is None else gate(pair, y_p, xs_p))
            xw_pairs.append((xs_p * jnp.where(first_half, ws[0], ws[1])).astype(BF16))
            ea_pairs.append(jnp.where(first_half, eas[0], eas[1]))
            xs_pairs.append(xs_p)
            if after_pair is not None:
                after_pair(pair, y_pairs, xw_pairs, ea_pairs)
    return (jnp.concatenate(y_pairs, axis=1), jnp.concatenate(xw_pairs, axis=1),
            jnp.concatenate(ea_pairs, axis=1), jnp.concatenate(xs_pairs, axis=1))


def _ssm_tail(y, xs, z, d_full, ssm_norm):
    y = y + d_full * xs
    y = y * _silu(z)
    return _group_rms_ssm(y, ssm_norm)


def _inproj_kernel(x_ref, npre_ref, w_in_ref, w_dt_ref, o_ref, w_main_b_ref):
    hn = _rms(x_ref[...], npre_ref[...]).astype(BF16)
    for off in range(0, O_DT, D_SCONV):
        w_piece = w_in_ref[:, off:off + D_SCONV].astype(BF16)
        w_main_b_ref[:, off:off + D_SCONV] = w_piece
        o_ref[:, off:off + D_SCONV] = _dot(hn, w_piece)
    o_ref[:, O_DT:D_IN_PAD] = _dot(hn, w_dt_ref[...])


def _inproj(x, npre, w_in, w_dt_b):
    m = x.shape[0]
    once = lambda shape: pl.BlockSpec(shape, lambda i: (0, 0), pipeline_mode=pl.Buffered(1))
    return pl.pallas_call(
        _inproj_kernel,
        grid=(1,),
        in_specs=[once((m, D_MODEL)), once((1, D_MODEL)), once((D_MODEL, O_DT)), once((D_MODEL, LANES))],
        out_specs=[once((m, D_IN_PAD)), once((D_MODEL, O_DT))],
        out_shape=[jax.ShapeDtypeStruct((m, D_IN_PAD), F32), jax.ShapeDtypeStruct((D_MODEL, O_DT), BF16)],
        compiler_params=pltpu.CompilerParams(dimension_semantics=("arbitrary",), vmem_limit_bytes=VMEM_LIMIT),
        name="inproj",
    )(x, npre, w_in, w_dt_b)


def _meta_kernel(proj_ref, xconv_w_ref, xconv_b_ref, dt_bias_ref, a_log_ref, tri_ref, expand_ref,
                 sc_tail_ref, xbc_tail_ref, state_t_ref):
    t = N_META
    proj = proj_ref[...]
    u = proj[:, O_GATE_C:O_GATE_C + D_SCONV] * proj[:, O_HV:O_HV + D_SCONV]
    sc_tail_ref[...] = u[t - SUBLANES:t]
    xbc = proj[:, O_XS:O_XS + D_XBC]
    xbc_tail_ref[...] = xbc[t - SUBLANES:t]
    zeros_prev = jnp.zeros((SUBLANES, D_XBC), F32)
    xbc_c = _silu(_causal_conv(xbc, zeros_prev, xconv_w_ref[...], None) + xconv_b_ref[...])
    xs = xbc_c[:, 0:D_SSM]
    dt, a_cum = _ssd_prepare(proj[:, O_DT:O_DT + LANES], dt_bias_ref[...], a_log_ref[...],
                             tri_ref[0:t, 0:t], t)
    w = jnp.exp(a_cum[t - 1:t, :] - a_cum) * dt
    xw_b = (xs * _dot3(w, expand_ref[...])).astype(BF16)
    for g in range(SSM_GROUPS):
        bm_b = xbc_c[:, D_SSM + g * SSM_STATE:D_SSM + (g + 1) * SSM_STATE].astype(BF16)
        state_t_ref[:, g * D_GROUP:(g + 1) * D_GROUP] = _dot_tn(bm_b, xw_b[:, g * D_GROUP:(g + 1) * D_GROUP])


def _meta_state(proj_meta, xconv_w, xconv_b, dt_bias, a_log, tri, expand):
    full = lambda shape: pl.BlockSpec(shape, lambda: tuple(0 for _ in shape))
    return pl.pallas_call(
        _meta_kernel,
        in_specs=[full(proj_meta.shape), full(xconv_w.shape), full(xconv_b.shape), full(dt_bias.shape),
                  full(a_log.shape), full(tri.shape), full(expand.shape)],
        out_specs=[full((SUBLANES, D_SCONV)), full((SUBLANES, D_XBC)), full((SSM_STATE, D_SSM))],
        out_shape=[jax.ShapeDtypeStruct((SUBLANES, D_SCONV), F32),
                   jax.ShapeDtypeStruct((SUBLANES, D_XBC), F32),
                   jax.ShapeDtypeStruct((SSM_STATE, D_SSM), F32)],
        name="meta_state",
    )(proj_meta, xconv_w, xconv_b, dt_bias, a_log, tri, expand)


def _pmix_kernel(x_ref, npre_ref, w_main_ref, w_dt_ref, sconv_w_ref, sconv_norm_ref,
                 xconv_w_ref, xconv_b_ref, dt_bias_ref, a_log_ref, d_full_ref, ssm_norm_ref,
                 tri_ref, w_out_ref, npost_ref, sc0_ref, xbc0_ref, state0_t_ref,
                 h1_ref, sc_out_ref, xbc_out_ref, state_ref,
                 proj_s0, proj_s1, dt_s0, dt_s1, h_s0, h_s1, hn_s, sc_tail, xbc_tail, state_t_ref, *,
                 chunks_per_seq):
    t = CHUNK
    k = pl.program_id(0)
    chunk = lax.rem(jnp.maximum(k - 1, 0), chunks_per_seq)

    @pl.when(k == 0)
    def _():
        proj_s1[...] = jnp.zeros((t, O_DT), F32)
        dt_s1[...] = jnp.zeros((SSM_HEADS, t), F32)
        h_s1[...] = jnp.zeros((t, D_MODEL), F32)

    @pl.when(chunk == 0)
    def _():
        sc_tail[...] = sc0_ref[...]
        xbc_tail[...] = xbc0_ref[...]
        state_t_ref[...] = state0_t_ref[...]

    for parity in (0, 1):
        pl.when(lax.rem(k, 2) == parity)(functools.partial(
            _pmix_step, parity, x_ref, npre_ref, w_main_ref, w_dt_ref, sconv_w_ref, sconv_norm_ref,
            xconv_w_ref, xconv_b_ref, dt_bias_ref, a_log_ref, d_full_ref, ssm_norm_ref, tri_ref, w_out_ref,
            npost_ref, h1_ref, (proj_s0, proj_s1), (dt_s0, dt_s1), (h_s0, h_s1), hn_s, sc_tail, xbc_tail,
            state_t_ref))

    @pl.when(chunk == chunks_per_seq - 1)
    def _():
        sc_out_ref[0] = sc_tail[...]
        xbc_out_ref[0] = xbc_tail[...]
        for p in range(HEAD_PAIRS):
            state_ref[0, p * LANES:(p + 1) * LANES, :] = state_t_ref[:, p * LANES:(p + 1) * LANES].T


def _pmix_step(slot_a, x_ref, npre_ref, w_main_ref, w_dt_ref, sconv_w_ref, sconv_norm_ref,
               xconv_w_ref, xconv_b_ref, dt_bias_ref, a_log_ref, d_full_ref, ssm_norm_ref, tri_ref, w_out_ref,
               npost_ref, h1_ref, proj_s, dt_s, h_s, hn_s, sc_tail, xbc_tail, state_t_ref):
    t = CHUNK
    slot_b = 1 - slot_a

    h_a = x_ref[0]
    h_s[slot_a][...] = h_a
    hn_s[...] = _rms(h_a, npre_ref[...]).astype(BF16)

    def stage_a(slot_name):
        for piece, planned in enumerate(STAGE_A_PLAN):
            if planned == slot_name:
                if piece + 1 < len(STAGE_A_COLS):
                    off, end = STAGE_A_COLS[piece], STAGE_A_COLS[piece + 1]
                    proj_s[slot_a][:, off:end] = _dot(hn_s[...], w_main_ref[:, off:end])
                else:
                    dt_s[slot_a][...] = _dot_nt(w_dt_ref[...], hn_s[...])

    def proj(off, width):
        return proj_s[slot_b][:, off:off + width]

    ya, u = _sconv_mixer(proj(O_GATE_B, D_SCONV), proj(O_GATE_C, D_SCONV), proj(O_HV, D_SCONV),
                         sc_tail[...], sconv_w_ref[...], sconv_norm_ref[...], None)
    sc_tail[...] = u[t - SUBLANES:t]
    stage_a("sconv")
    if MIX_YA_AT == "sconv":
        mix_ya = _dot(ya.astype(BF16), w_out_ref[0:D_SCONV, :])

    def xconv(col, width):
        cols = slice(col, col + width)
        raw = proj(O_XS + col, width)
        out = _silu(_causal_conv(raw, xbc_tail[:, cols], xconv_w_ref[:, cols], None) + xconv_b_ref[:, cols])
        xbc_tail[:, cols] = raw[t - SUBLANES:t]
        return out

    bc = xconv(D_SSM, D_BC)
    bm_b = bc[:, 0:SSM_GROUPS * SSM_STATE].astype(BF16)
    cm = bc[:, SSM_GROUPS * SSM_STATE:D_BC]
    stage_a("bc")
    a_cum, w, a_cum_t, dt_t, _ = _ssd_prepare_rows(dt_s[slot_b][...], dt_bias_ref[...], a_log_ref[...],
                                                   tri_ref[...])
    causal = lax.broadcasted_iota(jnp.int32, (t, t), 0) >= lax.broadcasted_iota(jnp.int32, (t, t), 1)
    stage_a("prepare")
    if MIX_YA_AT == "prepare":
        mix_ya = _dot(ya.astype(BF16), w_out_ref[0:D_SCONV, :])

    def gate(p, y_p, xs_p):
        cols = slice(p * LANES, (p + 1) * LANES)
        return (y_p + d_full_ref[:, cols] * xs_p) * _silu(proj(O_Z + p * LANES, LANES))

    y, xw_b, ea_full, _ = _ssd_heads(lambda p: xconv(p * LANES, LANES), bm_b, cm, a_cum, w, a_cum_t, dt_t,
                                     causal, state_t_ref, gate, lambda p, *partial: stage_a(p))
    chunk_decay = ea_full[t - 1:t, :]
    for g in range(SSM_GROUPS):
        cols = slice(g * D_GROUP, (g + 1) * D_GROUP)
        state_t_ref[:, cols] = chunk_decay[:, cols] * state_t_ref[:, cols] + _dot_tn(
            bm_b[:, g * SSM_STATE:(g + 1) * SSM_STATE], xw_b[:, cols])
    stage_a("state")
    yb = _group_rms_ssm(y, ssm_norm_ref[...])
    if MIX_YA_AT == "end":
        mix_ya = _dot(ya.astype(BF16), w_out_ref[0:D_SCONV, :])
    mix = mix_ya + _dot(yb.astype(BF16), w_out_ref[D_SCONV:D_MIX, :])
    stage_a("end")
    h1_ref[0] = h_s[slot_b][...] + _rms(mix, npost_ref[...])


def _pmix(x_prompt, consts, sc0, xbc0, state0_t):
    batch, seq, _ = x_prompt.shape
    cps = seq // CHUNK
    n_tiles = batch * cps
    const_spec = lambda a: pl.BlockSpec(a.shape, lambda k: (0, 0))
    ins = list(consts) + [sc0, xbc0, state0_t]

    def tile_a(k):
        tile = jnp.minimum(k, n_tiles - 1)
        return (tile // cps, tile % cps, 0)

    def tile_b(k):
        tile = jnp.maximum(k - 1, 0)
        return (tile // cps, tile % cps, 0)

    def seq_b(k):
        return (jnp.maximum(k - 1, 0) // cps, 0, 0)

    return pl.pallas_call(
        functools.partial(_pmix_kernel, chunks_per_seq=cps),
        grid=(n_tiles + 1,),
        in_specs=[pl.BlockSpec((1, CHUNK, D_MODEL), tile_a)] + [const_spec(a) for a in ins],
        out_specs=[
            pl.BlockSpec((1, CHUNK, D_MODEL), tile_b),
            pl.BlockSpec((1, SUBLANES, D_SCONV), seq_b),
            pl.BlockSpec((1, SUBLANES, D_XBC), seq_b),
            pl.BlockSpec((1, D_SSM, SSM_STATE), seq_b),
        ],
        out_shape=[
            jax.ShapeDtypeStruct((batch, seq, D_MODEL), F32),
            jax.ShapeDtypeStruct((batch, SUBLANES, D_SCONV), F32),
            jax.ShapeDtypeStruct((batch, SUBLANES, D_XBC), F32),
            jax.ShapeDtypeStruct((batch, D_SSM, SSM_STATE), F32),
        ],
        scratch_shapes=[
            pltpu.VMEM((CHUNK, O_DT), F32),
            pltpu.VMEM((CHUNK, O_DT), F32),
            pltpu.VMEM((SSM_HEADS, CHUNK), F32),
            pltpu.VMEM((SSM_HEADS, CHUNK), F32),
            pltpu.VMEM((CHUNK, D_MODEL), F32),
            pltpu.VMEM((CHUNK, D_MODEL), F32),
            pltpu.VMEM((CHUNK, D_MODEL), BF16),
            pltpu.VMEM((SUBLANES, D_SCONV), F32),
            pltpu.VMEM((SUBLANES, D_XBC), F32),
            pltpu.VMEM((SSM_STATE, D_SSM), F32),
        ],
        compiler_params=pltpu.CompilerParams(dimension_semantics=("arbitrary",), vmem_limit_bytes=VMEM_LIMIT),
        name="prompt_mixer",
    )(x_prompt, *ins)


def _smix_kernel(x_ref, sc_prev_ref, xbc_prev_ref, state_in_ref, npre_ref, w_main_ref, w_dt_ref,
                 sconv_w_ref, sconv_norm_ref, xconv_w_ref, xconv_b_ref,
                 dt_bias_ref, a_log_ref, d_full_ref, ssm_norm_ref, w_out_ref, npost_ref,
                 h1_ref, u_ref, xbc_ref, state_ref, *, dec_seq):
    t = SEQ_TILE * dec_seq
    h = x_ref[...]
    hn = _rms(h, npre_ref[...]).astype(BF16)

    def proj(off, width):
        return _dot(hn, w_main_ref[:, off:off + width])

    ya, u = _sconv_mixer(proj(O_GATE_B, D_SCONV), proj(O_GATE_C, D_SCONV), proj(O_HV, D_SCONV),
                         sc_prev_ref[...], sconv_w_ref[...], sconv_norm_ref[...], dec_seq)
    u_ref[...] = u
    xbc = proj(O_XS, D_XBC)
    xbc_ref[...] = xbc
    xbc_c = _silu(_causal_conv(xbc, xbc_prev_ref[...], xconv_w_ref[...], dec_seq) + xconv_b_ref[...])
    xs = xbc_c[:, 0:D_SSM]
    bm = xbc_c[:, D_SSM:D_SSM + SSM_GROUPS * SSM_STATE]
    bm_b = bm.astype(BF16)
    cm = xbc_c[:, D_SSM + SSM_GROUPS * SSM_STATE:D_XBC]

    row = lax.broadcasted_iota(jnp.int32, (t, t), 0)
    col = lax.broadcasted_iota(jnp.int32, (t, t), 1)
    same_seq = row // dec_seq == col // dec_seq
    mask = jnp.logical_and(row >= col, same_seq)
    tri = jnp.where(mask, 1.0, 0.0).astype(BF16)
    seq_ones = jnp.where(same_seq, 1.0, 0.0).astype(BF16)
    a_cum, w, a_cum_t, dt_t, a_total_t = _ssd_prepare_rows(_dot_nt(w_dt_ref[...], hn), dt_bias_ref[...],
                                                           a_log_ref[...], tri, seq_ones)
    y, xw_b, ea_full, _ = _ssd_heads(lambda p: xs[:, p * LANES:(p + 1) * LANES], bm_b, cm, a_cum, w, a_cum_t,
                                     dt_t, mask)

    row_id = lax.broadcasted_iota(jnp.int32, (t, SSM_STATE), 0) // dec_seq
    seq_decay_t = jnp.exp(a_total_t)
    y_off_rows = []
    for s in range(SEQ_TILE):
        r0 = s * dec_seq
        q_t = jnp.broadcast_to(seq_decay_t[:, r0:r0 + 1], (SSM_HEADS, SSM_STATE))
        y_off_g = []
        for g in range(SSM_GROUPS):
            rows = slice(g * D_GROUP, (g + 1) * D_GROUP)
            s_prev = state_in_ref[s, rows, :]
            cm_s = cm[r0:r0 + dec_seq, g * SSM_STATE:(g + 1) * SSM_STATE].astype(BF16)
            y_off_g.append(_dot_nt(cm_s, s_prev.astype(BF16)))
            bm_g = jnp.where(row_id == s, bm[:, g * SSM_STATE:(g + 1) * SSM_STATE], 0.0).astype(BF16)
            decay = jnp.concatenate(
                [jnp.broadcast_to(q_t[e:e + 1, :], (SSM_HEAD_DIM, SSM_STATE))
                 for e in range(g * HEADS_PER_GROUP, (g + 1) * HEADS_PER_GROUP)], axis=0)
            state_ref[s, rows, :] = decay * s_prev + _dot_tn(xw_b[:, rows], bm_g)
        y_off_rows.append(jnp.concatenate(y_off_g, axis=1))
    y = y + jnp.concatenate(y_off_rows, axis=0) * ea_full
    yb = _ssm_tail(y, xs, proj(O_Z, D_SSM), d_full_ref[...], ssm_norm_ref[...])
    mix = _dot(ya.astype(BF16), w_out_ref[0:D_SCONV, :]) + _dot(yb.astype(BF16), w_out_ref[D_SCONV:D_MIX, :])
    h1_ref[...] = h + _rms(mix, npost_ref[...])


def _smix(x_rows, sc_prev, xbc_prev, state, consts, dec_seq):
    n_seq = state.shape[0]
    t = SEQ_TILE * dec_seq
    const_spec = lambda a: pl.BlockSpec(a.shape, lambda i: (0, 0), pipeline_mode=pl.Buffered(1))
    row_spec = lambda width: pl.BlockSpec((t, width), lambda i: (i, 0))
    state_spec = pl.BlockSpec((SEQ_TILE, D_SSM, SSM_STATE), lambda i: (i, 0, 0))
    return pl.pallas_call(
        functools.partial(_smix_kernel, dec_seq=dec_seq),
        grid=(n_seq // SEQ_TILE,),
        in_specs=[row_spec(D_MODEL), row_spec(D_SCONV), row_spec(D_XBC), state_spec]
        + [const_spec(a) for a in consts],
        out_specs=[row_spec(D_MODEL), row_spec(D_SCONV), row_spec(D_XBC), state_spec],
        out_shape=[
            jax.ShapeDtypeStruct((n_seq * dec_seq, D_MODEL), F32),
            jax.ShapeDtypeStruct((n_seq * dec_seq, D_SCONV), F32),
            jax.ShapeDtypeStruct((n_seq * dec_seq, D_XBC), F32),
            jax.ShapeDtypeStruct(state.shape, F32),
        ],
        compiler_params=pltpu.CompilerParams(dimension_semantics=("arbitrary",), vmem_limit_bytes=VMEM_LIMIT),
        name="sample_mixer",
    )(x_rows, sc_prev, xbc_prev, state, *consts)


def _ffn_kernel(hp_ref, hs_ref, npre_ref, w_gate_ref, w_up_ref, w_down_ref, npost_ref, op_ref, os_ref, *,
                prompt_steps):
    def body(h_ref, o_ref):
        h1 = h_ref[...]
        fn = _rms(h1, npre_ref[...]).astype(BF16)
        act = (_silu(_dot(fn, w_gate_ref[...])) * _dot(fn, w_up_ref[...])).astype(BF16)
        o_ref[...] = h1 + _rms(_dot(act, w_down_ref[...]), npost_ref[...])

    step = pl.program_id(0)
    pl.when(step < prompt_steps)(functools.partial(body, hp_ref, op_ref))
    pl.when(step >= prompt_steps)(functools.partial(body, hs_ref, os_ref))


def _ffn(h_prompt, h_sample, ffn_consts):
    mp, ms = h_prompt.shape[0], h_sample.shape[0]
    rows = FFN_ROWS
    p_steps, s_steps = mp // rows, ms // rows
    p_spec = pl.BlockSpec((rows, D_MODEL), lambda i: (jnp.minimum(i, p_steps - 1), 0))
    s_spec = pl.BlockSpec((rows, D_MODEL), lambda i: (jnp.maximum(i - p_steps, 0), 0))
    const_spec = lambda a: pl.BlockSpec(a.shape, lambda i: (0, 0), pipeline_mode=pl.Buffered(1))
    return pl.pallas_call(
        functools.partial(_ffn_kernel, prompt_steps=p_steps),
        grid=(p_steps + s_steps,),
        in_specs=[p_spec, s_spec] + [const_spec(a) for a in ffn_consts],
        out_specs=[p_spec, s_spec],
        out_shape=[jax.ShapeDtypeStruct((mp, D_MODEL), F32), jax.ShapeDtypeStruct((ms, D_MODEL), F32)],
        compiler_params=pltpu.CompilerParams(dimension_semantics=("arbitrary",), vmem_limit_bytes=VMEM_LIMIT),
        name="ffn",
    )(h_prompt, h_sample, *ffn_consts)


def _row(v, width=None):
    v = v.reshape(1, -1).astype(F32)
    if width is not None and v.shape[1] < width:
        v = jnp.pad(v, ((0, 0), (0, width - v.shape[1])))
    return v


def _layer(hp, hs, sc_state, xbc_state, ssm_state, meta_tokens,
           norm_mix_pre, norm_mix_post, w_in, sconv_w, sconv_norm, ssm_conv_w, ssm_conv_b,
           dt_bias, a_log, d_skip, ssm_norm, w_out, norm_ffn_pre, norm_ffn_post, w_gate, w_up, w_down):
    batch, seq, _ = hp.shape
    n_seq, dec_seq, _ = hs.shape

    w_dt_b = jnp.pad(w_in[:, O_DT:D_IN], ((0, 0), (0, D_IN_PAD - D_IN))).astype(BF16)
    w_dt_t = w_in[:, O_DT:D_IN].T.astype(BF16)
    w_out_b = w_out.astype(BF16)
    w_gate_b, w_up_b, w_down_b = w_gate.astype(BF16), w_up.astype(BF16), w_down.astype(BF16)
    npre, npost = _row(norm_mix_pre), _row(norm_mix_post)
    fpre, fpost = _row(norm_ffn_pre), _row(norm_ffn_post)
    sconv_w = sconv_w.astype(F32)
    xconv_w = ssm_conv_w.astype(F32)
    xconv_b = _row(ssm_conv_b)
    dt_bias_p, a_log_p = _row(dt_bias, LANES), _row(a_log, LANES)
    dt_bias_t = jnp.broadcast_to(dt_bias.astype(F32)[:, None], (SSM_HEADS, LANES))
    a_log_t = jnp.broadcast_to(a_log.astype(F32)[:, None], (SSM_HEADS, LANES))
    d_full = jnp.repeat(d_skip.astype(F32), SSM_HEAD_DIM).reshape(1, D_SSM)
    sconv_norm_r, ssm_norm_r = _row(sconv_norm), _row(ssm_norm)

    r = jnp.arange(CHUNK)
    tri = (r[:, None] >= r[None, :]).astype(BF16)
    expand = (jnp.arange(LANES)[:, None] == (jnp.arange(D_SSM)[None, :] // SSM_HEAD_DIM)).astype(BF16)

    mixer_consts = (sconv_w, sconv_norm_r, xconv_w, xconv_b, dt_bias_t, a_log_t, d_full, ssm_norm_r)
    ffn_consts = (fpre, w_gate_b, w_up_b, w_down_b, fpost)

    proj_meta, w_main_b = _inproj(meta_tokens.astype(F32), npre, w_in.astype(F32), w_dt_b)
    sc0, xbc0, state0_t = _meta_state(proj_meta, xconv_w, xconv_b, dt_bias_p, a_log_p, tri, expand)

    pm_consts = (npre, w_main_b, w_dt_t) + mixer_consts + (tri, w_out_b, npost)
    h1, sc_tail, xbc_tail, p_state = _pmix(hp, pm_consts, sc0, xbc0, state0_t)
    new_sc_p = sc_tail[:, SUBLANES - (SCONV_WIDTH - 1):, :]
    new_xbc_p = xbc_tail[:, SUBLANES - (SSM_CONV_WIDTH - 1):, :]
    new_ssm_p = p_state.reshape(batch, SSM_GROUPS, HEADS_PER_GROUP, SSM_HEAD_DIM, SSM_STATE)

    hs_rows = hs.reshape(n_seq * dec_seq, D_MODEL)
    sc_prev = jnp.pad(sc_state, ((0, 0), (dec_seq - (SCONV_WIDTH - 1), 0), (0, 0))).reshape(n_seq * dec_seq, D_SCONV)
    xbc_prev = jnp.pad(xbc_state, ((0, 0), (dec_seq - (SSM_CONV_WIDTH - 1), 0), (0, 0))).reshape(n_seq * dec_seq, D_XBC)
    sm_consts = (npre, w_main_b, w_dt_t) + mixer_consts + (w_out_b, npost)
    h1_s, u_s, xbc_s, s_state = _smix(hs_rows, sc_prev, xbc_prev, ssm_state.reshape(n_seq, D_SSM, SSM_STATE),
                                      sm_consts, dec_seq)

    y_prompt, y_sample = _ffn(h1.reshape(batch * seq, D_MODEL), h1_s, ffn_consts)
    y_prompt = y_prompt.reshape(batch, seq, D_MODEL)
    y_sample = y_sample.reshape(n_seq, dec_seq, D_MODEL)
    new_sc_s = u_s.reshape(n_seq, dec_seq, D_SCONV)[:, dec_seq - (SCONV_WIDTH - 1):, :]
    new_xbc_s = xbc_s.reshape(n_seq, dec_seq, D_XBC)[:, dec_seq - (SSM_CONV_WIDTH - 1):, :]
    new_ssm_s = s_state.reshape(n_seq, SSM_GROUPS, HEADS_PER_GROUP, SSM_HEAD_DIM, SSM_STATE)
    return y_prompt, y_sample, new_sc_p, new_xbc_p, new_ssm_p, new_sc_s, new_xbc_s, new_ssm_s


def kernel(x_prompt, x_sample, state_sconv, state_ssm_conv, state_ssm, meta_tokens, norm_mix_pre, norm_mix_post, w_in, sconv_w, sconv_norm, ssm_conv_w, ssm_conv_b, dt_bias, A_log, D_skip, ssm_norm, w_out, norm_ffn_pre, norm_ffn_post, w_gate, w_up, w_down):
    depth = w_in.shape[0]
    assert depth == 1 and x_sample.shape[1] == SUBLANES
    outs = _layer(x_prompt, x_sample, state_sconv[0], state_ssm_conv[0], state_ssm[0], meta_tokens,
                  norm_mix_pre[0], norm_mix_post[0], w_in[0], sconv_w[0], sconv_norm[0], ssm_conv_w[0],
                  ssm_conv_b[0], dt_bias[0], A_log[0], D_skip[0], ssm_norm[0], w_out[0],
                  norm_ffn_pre[0], norm_ffn_post[0], w_gate[0], w_up[0], w_down[0])
    y_prompt, y_sample = outs[0], outs[1]
    return (y_prompt, y_sample) + tuple(o[None] for o in outs[2:])
```

```python
import functools

import jax
import jax.numpy as jnp
from jax import lax
from jax.experimental import pallas as pl
from jax.experimental.pallas import tpu as pltpu

D_MODEL = 1024
N_META = 16
D_SCONV = 512
SCONV_GROUPS = 8
SCONV_WIDTH = 3
SSM_HEAD_DIM = 64
SSM_HEADS = 16
D_SSM = 1024
SSM_GROUPS = 2
HEADS_PER_GROUP = 8
D_GROUP = HEADS_PER_GROUP * SSM_HEAD_DIM
SSM_STATE = 128
SSM_CONV_WIDTH = 4
D_BC = 2 * SSM_GROUPS * SSM_STATE
D_XBC = D_SSM + D_BC
D_MIX = D_SCONV + D_SSM
D_IN = 3 * D_SCONV + D_SSM + D_XBC + SSM_HEADS
D_FF = 2816
EPS = 1e-6

LANES = 128
SUBLANES = 8
D_IN_PAD = D_IN + (LANES - SSM_HEADS)
HEAD_PAIRS = D_SSM // LANES

O_GATE_B = 0
O_GATE_C = D_SCONV
O_HV = 2 * D_SCONV
O_Z = 3 * D_SCONV
O_XS = O_Z + D_SSM
O_BC = O_XS + D_SSM
O_DT = O_BC + D_BC

CHUNK = 128
STAGE_A_COLS = tuple(range(0, O_DT + 1, D_SCONV))
STAGE_A_PLAN = ("sconv", "sconv", "prepare", "prepare", 1, 3, 5, "end", "end")
MIX_YA_AT = "sconv"
SEQ_TILE = 16
FFN_ROWS = 512
VMEM_LIMIT = 60000 * 1024

BF16 = jnp.bfloat16
F32 = jnp.float32


def _dot(a, b):
    return jnp.dot(a, b, preferred_element_type=F32)


def _dot_nt(a, b):
    return lax.dot_general(a, b, (((1,), (1,)), ((), ())), preferred_element_type=F32)


def _dot_tn(a, b):
    return lax.dot_general(a, b, (((0,), (0,)), ((), ())), preferred_element_type=F32)


def _split3(v):
    hi = v.astype(BF16)
    r1 = v - hi.astype(F32)
    mid = r1.astype(BF16)
    lo = (r1 - mid.astype(F32)).astype(BF16)
    return hi, mid, lo


def _dot3(v, m):
    hi, mid, lo = _split3(v)
    return _dot(hi, m) + _dot(mid, m) + _dot(lo, m)


def _dot3_left(m, v):
    hi, mid, lo = _split3(v)
    return _dot(m, hi) + _dot(m, mid) + _dot(m, lo)


def _rms(x, w):
    return x * lax.rsqrt(jnp.mean(x * x, axis=-1, keepdims=True) + EPS) * w


def _silu(x):
    return x * (1.0 / (1.0 + jnp.exp(-x)))


def _softplus(x):
    return jnp.maximum(x, 0.0) + jnp.log1p(jnp.exp(-jnp.abs(x)))


def _shifted(u, prev, j, rows_per_seq):
    t = u.shape[0]
    s = pltpu.roll(u, j, 0)
    if rows_per_seq is None:
        row = lax.broadcasted_iota(jnp.int32, (SUBLANES, u.shape[1]), 0)
        first = jnp.where(row < j, pltpu.roll(prev, j, 0), s[0:SUBLANES])
        if t == SUBLANES:
            return first
        return jnp.concatenate([first, s[SUBLANES:]], axis=0)
    row = lax.broadcasted_iota(jnp.int32, u.shape, 0) % SUBLANES
    return jnp.where(row < j, pltpu.roll(prev, t - SUBLANES + j, 0), s)


def _causal_conv(u, prev, w, rows_per_seq):
    k = w.shape[0]
    out = _shifted(u, prev, k - 1, rows_per_seq) * w[0:1]
    for i in range(1, k - 1):
        out = out + _shifted(u, prev, k - 1 - i, rows_per_seq) * w[i:i + 1]
    return out + u * w[k - 1:k]


def _sconv_mixer(gate_b, gate_c, hv, prev, sconv_w, sconv_norm, rows_per_seq):
    u = gate_c * hv
    conv_u = _causal_conv(u, prev, sconv_w, rows_per_seq)
    return _sconv_norm(gate_b, conv_u, sconv_norm), u


def _sconv_norm(gate_b, conv_u, sconv_norm):
    v = gate_b * conv_u
    first_half = lax.broadcasted_iota(jnp.int32, (v.shape[0], LANES), 1) < D_SCONV // SCONV_GROUPS
    parts = []
    for c in range(0, D_SCONV, LANES):
        vc = v[:, c:c + LANES]
        sq = vc * vc
        s_first = jnp.sum(jnp.where(first_half, sq, 0.0), axis=-1, keepdims=True)
        s_second = jnp.sum(jnp.where(first_half, 0.0, sq), axis=-1, keepdims=True)
        mean_sq = jnp.where(first_half, s_first, s_second) * (SCONV_GROUPS / D_SCONV)
        parts.append(vc * lax.rsqrt(mean_sq + EPS))
    return jnp.concatenate(parts, axis=1) * sconv_norm


def _ssd_prepare(dt_raw, dt_bias, a_log, tri, t):
    lane = lax.broadcasted_iota(jnp.int32, (t, LANES), 1)
    valid = lane < SSM_HEADS
    dt = jnp.where(valid, _softplus(dt_raw + dt_bias), 0.0)
    a = dt * (-jnp.exp(a_log))
    a_cum = _dot3_left(tri, a)
    return dt, a_cum


def _ssd_prepare_rows(dt_raw_t, dt_bias_t, a_log_t, tri, seq_ones=None):
    t = dt_raw_t.shape[1]
    dt_t = _softplus(dt_raw_t + dt_bias_t)
    a_t = dt_t * (-jnp.exp(a_log_t))
    hi, mid, lo = _split3(a_t)
    a_cum_t = _dot_nt(hi, tri) + _dot_nt(mid, tri) + _dot_nt(lo, tri)
    if seq_ones is None:
        a_total_t = jnp.broadcast_to(a_cum_t[:, t - 1:t], (SSM_HEADS, t))
    else:
        a_total_t = _dot_nt(hi, seq_ones) + _dot_nt(mid, seq_ones) + _dot_nt(lo, seq_ones)
    w_t = jnp.exp(a_total_t - a_cum_t) * dt_t
    pad = jnp.zeros((LANES - SSM_HEADS, t), F32)
    a_cum = jnp.concatenate([a_cum_t, pad], axis=0).T
    w = jnp.concatenate([w_t, pad], axis=0).T
    return a_cum, w, a_cum_t, dt_t, a_total_t


def _group_rms_ssm(y, w):
    parts = []
    for g in range(SSM_GROUPS):
        yg = y[:, g * D_GROUP:(g + 1) * D_GROUP]
        parts.append(yg * lax.rsqrt(jnp.mean(yg * yg, axis=-1, keepdims=True) + EPS))
    return jnp.concatenate(parts, axis=1) * w


def _ssd_heads(xs_of, bm_b, cm, a_cum, w, a_cum_t, dt_t, mask, state_t=None, gate=None, after_pair=None):
    t = a_cum.shape[0]
    assert t == LANES
    first_half = lax.broadcasted_iota(jnp.int32, (t, LANES), 1) < SSM_HEAD_DIM
    y_pairs, xw_pairs, ea_pairs, xs_pairs = [], [], [], []
    for g in range(SSM_GROUPS):
        cm_g = cm[:, g * SSM_STATE:(g + 1) * SSM_STATE]
        cb = _dot_nt(cm_g.astype(BF16), bm_b[:, g * SSM_STATE:(g + 1) * SSM_STATE])
        for q in range(HEADS_PER_GROUP // 2):
            pair = g * (HEADS_PER_GROUP // 2) + q
            cols = slice(pair * LANES, (pair + 1) * LANES)
            xs_p = xs_of(pair)
            rhs = xs_p.astype(BF16)
            if state_t is not None:
                rhs = jnp.concatenate([rhs, state_t[:, cols].astype(BF16)], axis=0)
            res, eas, ws = [], [], []
            for e in (2 * pair, 2 * pair + 1):
                a_col = jnp.broadcast_to(a_cum[:, e:e + 1], (t, LANES))
                decay = jnp.exp(jnp.where(mask, a_col - a_cum_t[e:e + 1, :], -jnp.inf))
                lhs = (cb * decay * dt_t[e:e + 1, :]).astype(BF16)
                ea = jnp.exp(a_col)
                if state_t is not None:
                    lhs = jnp.concatenate([lhs, (cm_g * ea).astype(BF16)], axis=1)
                res.append(_dot(lhs, rhs))
                eas.append(ea)
                ws.append(jnp.broadcast_to(w[:, e:e + 1], (t, LANES)))
            y_p = jnp.where(first_half, res[0], res[1])
            y_pairs.append(y_p if gate is None else gate(pair, y_p, xs_p))
            xw_pairs.append((xs_p * jnp.where(first_half, ws[0], ws[1])).astype(BF16))
            ea_pairs.append(jnp.where(first_half, eas[0], eas[1]))
            xs_pairs.append(xs_p)
            if after_pair is not None:
                after_pair(pair, y_pairs, xw_pairs, ea_pairs)
    return (jnp.concatenate(y_pairs, axis=1), jnp.concatenate(xw_pairs, axis=1),
            jnp.concatenate(ea_pairs, axis=1), jnp.concatenate(xs_pairs, axis=1))


def _ssm_tail(y, xs, z, d_full, ssm_norm):
    y = y + d_full * xs
    y = y * _silu(z)
    return _group_rms_ssm(y, ssm_norm)


def _w_main_spec(index_map, **kwargs):
    return pl.BlockSpec((1, D_MODEL, O_DT), index_map, **kwargs)


def _inproj_kernel(x_ref, npre_ref, w_main_ref, w_dt_ref, o_ref):
    hn = _rms(x_ref[...], npre_ref[...]).astype(BF16)
    o_ref[:, 0:O_DT] = _dot(hn, w_main_ref[0])
    o_ref[:, O_DT:D_IN_PAD] = _dot(hn, w_dt_ref[...])


def _inproj(x, npre, w_main_b, w_dt_b):
    m = x.shape[0]
    once = lambda shape: pl.BlockSpec(shape, lambda i: (0, 0), pipeline_mode=pl.Buffered(1))
    return pl.pallas_call(
        _inproj_kernel,
        grid=(1,),
        in_specs=[once((m, D_MODEL)), once((1, D_MODEL)),
                  _w_main_spec(lambda i: (0, 0, 0), pipeline_mode=pl.Buffered(1)), once((D_MODEL, LANES))],
        out_specs=once((m, D_IN_PAD)),
        out_shape=jax.ShapeDtypeStruct((m, D_IN_PAD), F32),
        compiler_params=pltpu.CompilerParams(dimension_semantics=("arbitrary",), vmem_limit_bytes=VMEM_LIMIT),
        name="inproj",
    )(x, npre, w_main_b, w_dt_b)


def _meta_kernel(proj_ref, xconv_w_ref, xconv_b_ref, dt_bias_ref, a_log_ref, tri_ref, expand_ref,
                 sc_tail_ref, xbc_tail_ref, state_t_ref):
    t = N_META
    proj = proj_ref[...]
    u = proj[:, O_GATE_C:O_GATE_C + D_SCONV] * proj[:, O_HV:O_HV + D_SCONV]
    sc_tail_ref[...] = u[t - SUBLANES:t]
    xbc = proj[:, O_XS:O_XS + D_XBC]
    xbc_tail_ref[...] = xbc[t - SUBLANES:t]
    zeros_prev = jnp.zeros((SUBLANES, D_XBC), F32)
    xbc_c = _silu(_causal_conv(xbc, zeros_prev, xconv_w_ref[...], None) + xconv_b_ref[...])
    xs = xbc_c[:, 0:D_SSM]
    dt, a_cum = _ssd_prepare(proj[:, O_DT:O_DT + LANES], dt_bias_ref[...], a_log_ref[...],
                             tri_ref[0:t, 0:t], t)
    w = jnp.exp(a_cum[t - 1:t, :] - a_cum) * dt
    xw_b = (xs * _dot3(w, expand_ref[...])).astype(BF16)
    for g in range(SSM_GROUPS):
        bm_b = xbc_c[:, D_SSM + g * SSM_STATE:D_SSM + (g + 1) * SSM_STATE].astype(BF16)
        state_t_ref[:, g * D_GROUP:(g + 1) * D_GROUP] = _dot_tn(bm_b, xw_b[:, g * D_GROUP:(g + 1) * D_GROUP])


def _meta_state(proj_meta, xconv_w, xconv_b, dt_bias, a_log, tri, expand):
    full = lambda shape: pl.BlockSpec(shape, lambda: tuple(0 for _ in shape))
    return pl.pallas_call(
        _meta_kernel,
        in_specs=[full(proj_meta.shape), full(xconv_w.shape), full(xconv_b.shape), full(dt_bias.shape),
                  full(a_log.shape), full(tri.shape), full(expand.shape)],
        out_specs=[full((SUBLANES, D_SCONV)), full((SUBLANES, D_XBC)), full((SSM_STATE, D_SSM))],
        out_shape=[jax.ShapeDtypeStruct((SUBLANES, D_SCONV), F32),
                   jax.ShapeDtypeStruct((SUBLANES, D_XBC), F32),
                   jax.ShapeDtypeStruct((SSM_STATE, D_SSM), F32)],
        name="meta_state",
    )(proj_meta, xconv_w, xconv_b, dt_bias, a_log, tri, expand)


def _pmix_kernel(x_ref, npre_ref, w_main_ref, w_dt_ref, sconv_w_ref, sconv_norm_ref,
                 xconv_w_ref, xconv_b_ref, dt_bias_ref, a_log_ref, d_full_ref, ssm_norm_ref,
                 tri_ref, w_out_ref, npost_ref, sc0_ref, xbc0_ref, state0_t_ref,
                 h1_ref, sc_out_ref, xbc_out_ref, state_ref,
                 proj_s0, proj_s1, dt_s0, dt_s1, h_s0, h_s1, hn_s, sc_tail, xbc_tail, state_t_ref, *,
                 chunks_per_seq):
    t = CHUNK
    k = pl.program_id(0)
    chunk = lax.rem(jnp.maximum(k - 1, 0), chunks_per_seq)

    @pl.when(k == 0)
    def _():
        proj_s1[...] = jnp.zeros((t, O_DT), F32)
        dt_s1[...] = jnp.zeros((SSM_HEADS, t), F32)
        h_s1[...] = jnp.zeros((t, D_MODEL), F32)

    @pl.when(chunk == 0)
    def _():
        sc_tail[...] = sc0_ref[...]
        xbc_tail[...] = xbc0_ref[...]
        state_t_ref[...] = state0_t_ref[...]

    for parity in (0, 1):
        pl.when(lax.rem(k, 2) == parity)(functools.partial(
            _pmix_step, parity, x_ref, npre_ref, w_main_ref, w_dt_ref, sconv_w_ref, sconv_norm_ref,
            xconv_w_ref, xconv_b_ref, dt_bias_ref, a_log_ref, d_full_ref, ssm_norm_ref, tri_ref, w_out_ref,
            npost_ref, h1_ref, (proj_s0, proj_s1), (dt_s0, dt_s1), (h_s0, h_s1), hn_s, sc_tail, xbc_tail,
            state_t_ref))

    @pl.when(chunk == chunks_per_seq - 1)
    def _():
        sc_out_ref[0] = sc_tail[...]
        xbc_out_ref[0] = xbc_tail[...]
        for p in range(HEAD_PAIRS):
            state_ref[0, p * LANES:(p + 1) * LANES, :] = state_t_ref[:, p * LANES:(p + 1) * LANES].T


def _pmix_step(slot_a, x_ref, npre_ref, w_main_ref, w_dt_ref, sconv_w_ref, sconv_norm_ref,
               xconv_w_ref, xconv_b_ref, dt_bias_ref, a_log_ref, d_full_ref, ssm_norm_ref, tri_ref, w_out_ref,
               npost_ref, h1_ref, proj_s, dt_s, h_s, hn_s, sc_tail, xbc_tail, state_t_ref):
    t = CHUNK
    slot_b = 1 - slot_a

    h_a = x_ref[0]
    h_s[slot_a][...] = h_a
    hn_s[...] = _rms(h_a, npre_ref[...]).astype(BF16)

    def stage_a(slot_name):
        for piece, planned in enumerate(STAGE_A_PLAN):
            if planned == slot_name:
                if piece + 1 < len(STAGE_A_COLS):
                    off, end = STAGE_A_COLS[piece], STAGE_A_COLS[piece + 1]
                    proj_s[slot_a][:, off:end] = _dot(hn_s[...], w_main_ref[0, :, off:end])
                else:
                    dt_s[slot_a][...] = _dot_nt(w_dt_ref[...], hn_s[...])

    def proj(off, width):
        return proj_s[slot_b][:, off:off + width]

    ya, u = _sconv_mixer(proj(O_GATE_B, D_SCONV), proj(O_GATE_C, D_SCONV), proj(O_HV, D_SCONV),
                         sc_tail[...], sconv_w_ref[...], sconv_norm_ref[...], None)
    sc_tail[...] = u[t - SUBLANES:t]
    stage_a("sconv")
    if MIX_YA_AT == "sconv":
        mix_ya = _dot(ya.astype(BF16), w_out_ref[0:D_SCONV, :])

    def xconv(col, width):
        cols = slice(col, col + width)
        raw = proj(O_XS + col, width)
        out = _silu(_causal_conv(raw, xbc_tail[:, cols], xconv_w_ref[:, cols], None) + xconv_b_ref[:, cols])
        xbc_tail[:, cols] = raw[t - SUBLANES:t]
        return out

    bc = xconv(D_SSM, D_BC)
    bm_b = bc[:, 0:SSM_GROUPS * SSM_STATE].astype(BF16)
    cm = bc[:, SSM_GROUPS * SSM_STATE:D_BC]
    stage_a("bc")
    a_cum, w, a_cum_t, dt_t, _ = _ssd_prepare_rows(dt_s[slot_b][...], dt_bias_ref[...], a_log_ref[...],
                                                   tri_ref[...])
    causal = lax.broadcasted_iota(jnp.int32, (t, t), 0) >= lax.broadcasted_iota(jnp.int32, (t, t), 1)
    stage_a("prepare")
    if MIX_YA_AT == "prepare":
        mix_ya = _dot(ya.astype(BF16), w_out_ref[0:D_SCONV, :])

    def gate(p, y_p, xs_p):
        cols = slice(p * LANES, (p + 1) * LANES)
        return (y_p + d_full_ref[:, cols] * xs_p) * _silu(proj(O_Z + p * LANES, LANES))

    y, xw_b, ea_full, _ = _ssd_heads(lambda p: xconv(p * LANES, LANES), bm_b, cm, a_cum, w, a_cum_t, dt_t,
                                     causal, state_t_ref, gate, lambda p, *partial: stage_a(p))
    chunk_decay = ea_full[t - 1:t, :]
    for g in range(SSM_GROUPS):
        cols = slice(g * D_GROUP, (g + 1) * D_GROUP)
        state_t_ref[:, cols] = chunk_decay[:, cols] * state_t_ref[:, cols] + _dot_tn(
            bm_b[:, g * SSM_STATE:(g + 1) * SSM_STATE], xw_b[:, cols])
    stage_a("state")
    yb = _group_rms_ssm(y, ssm_norm_ref[...])
    if MIX_YA_AT == "end":
        mix_ya = _dot(ya.astype(BF16), w_out_ref[0:D_SCONV, :])
    mix = mix_ya + _dot(yb.astype(BF16), w_out_ref[D_SCONV:D_MIX, :])
    stage_a("end")
    h1_ref[0] = h_s[slot_b][...] + _rms(mix, npost_ref[...])


def _pmix(x_prompt, consts, sc0, xbc0, state0_t):
    batch, seq, _ = x_prompt.shape
    cps = seq // CHUNK
    n_tiles = batch * cps
    const_spec = lambda a: (pl.BlockSpec(a.shape, lambda k: (0, 0)) if a.ndim == 2
                            else _w_main_spec(lambda k: (0, 0, 0)))
    ins = list(consts) + [sc0, xbc0, state0_t]

    def tile_a(k):
        tile = jnp.minimum(k, n_tiles - 1)
        return (tile // cps, tile % cps, 0)

    def tile_b(k):
        tile = jnp.maximum(k - 1, 0)
        return (tile // cps, tile % cps, 0)

    def seq_b(k):
        return (jnp.maximum(k - 1, 0) // cps, 0, 0)

    return pl.pallas_call(
        functools.partial(_pmix_kernel, chunks_per_seq=cps),
        grid=(n_tiles + 1,),
        in_specs=[pl.BlockSpec((1, CHUNK, D_MODEL), tile_a)] + [const_spec(a) for a in ins],
        out_specs=[
            pl.BlockSpec((1, CHUNK, D_MODEL), tile_b),
            pl.BlockSpec((1, SUBLANES, D_SCONV), seq_b),
            pl.BlockSpec((1, SUBLANES, D_XBC), seq_b),
            pl.BlockSpec((1, D_SSM, SSM_STATE), seq_b),
        ],
        out_shape=[
            jax.ShapeDtypeStruct((batch, seq, D_MODEL), F32),
            jax.ShapeDtypeStruct((batch, SUBLANES, D_SCONV), F32),
            jax.ShapeDtypeStruct((batch, SUBLANES, D_XBC), F32),
            jax.ShapeDtypeStruct((batch, D_SSM, SSM_STATE), F32),
        ],
        scratch_shapes=[
            pltpu.VMEM((CHUNK, O_DT), F32),
            pltpu.VMEM((CHUNK, O_DT), F32),
            pltpu.VMEM((SSM_HEADS, CHUNK), F32),
            pltpu.VMEM((SSM_HEADS, CHUNK), F32),
            pltpu.VMEM((CHUNK, D_MODEL), F32),
            pltpu.VMEM((CHUNK, D_MODEL), F32),
            pltpu.VMEM((CHUNK, D_MODEL), BF16),
            pltpu.VMEM((SUBLANES, D_SCONV), F32),
            pltpu.VMEM((SUBLANES, D_XBC), F32),
            pltpu.VMEM((SSM_STATE, D_SSM), F32),
        ],
        compiler_params=pltpu.CompilerParams(dimension_semantics=("arbitrary",), vmem_limit_bytes=VMEM_LIMIT),
        name="prompt_mixer",
    )(x_prompt, *ins)


def _smix_kernel(x_ref, sc_prev_ref, xbc_prev_ref, state_in_ref, npre_ref, w_main_ref, w_dt_ref,
                 sconv_w_ref, sconv_norm_ref, xconv_w_ref, xconv_b_ref,
                 dt_bias_ref, a_log_ref, d_full_ref, ssm_norm_ref, w_out_ref, npost_ref,
                 h1_ref, u_ref, xbc_ref, state_ref, *, dec_seq):
    t = SEQ_TILE * dec_seq
    h = x_ref[...]
    hn = _rms(h, npre_ref[...]).astype(BF16)

    def proj(off, width):
        return _dot(hn, w_main_ref[0, :, off:off + width])

    ya, u = _sconv_mixer(proj(O_GATE_B, D_SCONV), proj(O_GATE_C, D_SCONV), proj(O_HV, D_SCONV),
                         sc_prev_ref[...], sconv_w_ref[...], sconv_norm_ref[...], dec_seq)
    u_ref[...] = u
    xbc = proj(O_XS, D_XBC)
    xbc_ref[...] = xbc
    xbc_c = _silu(_causal_conv(xbc, xbc_prev_ref[...], xconv_w_ref[...], dec_seq) + xconv_b_ref[...])
    xs = xbc_c[:, 0:D_SSM]
    bm = xbc_c[:, D_SSM:D_SSM + SSM_GROUPS * SSM_STATE]
    bm_b = bm.astype(BF16)
    cm = xbc_c[:, D_SSM + SSM_GROUPS * SSM_STATE:D_XBC]

    row = lax.broadcasted_iota(jnp.int32, (t, t), 0)
    col = lax.broadcasted_iota(jnp.int32, (t, t), 1)
    same_seq = row // dec_seq == col // dec_seq
    mask = jnp.logical_and(row >= col, same_seq)
    tri = jnp.where(mask, 1.0, 0.0).astype(BF16)
    seq_ones = jnp.where(same_seq, 1.0, 0.0).astype(BF16)
    a_cum, w, a_cum_t, dt_t, a_total_t = _ssd_prepare_rows(_dot_nt(w_dt_ref[...], hn), dt_bias_ref[...],
                                                           a_log_ref[...], tri, seq_ones)
    y, xw_b, ea_full, _ = _ssd_heads(lambda p: xs[:, p * LANES:(p + 1) * LANES], bm_b, cm, a_cum, w, a_cum_t,
                                     dt_t, mask)

    row_id = lax.broadcasted_iota(jnp.int32, (t, SSM_STATE), 0) // dec_seq
    seq_decay_t = jnp.exp(a_total_t)
    y_off_rows = []
    for s in range(SEQ_TILE):
        r0 = s * dec_seq
        q_t = jnp.broadcast_to(seq_decay_t[:, r0:r0 + 1], (SSM_HEADS, SSM_STATE))
        y_off_g = []
        for g in range(SSM_GROUPS):
            rows = slice(g * D_GROUP, (g + 1) * D_GROUP)
            s_prev = state_in_ref[s, rows, :]
            cm_s = cm[r0:r0 + dec_seq, g * SSM_STATE:(g + 1) * SSM_STATE].astype(BF16)
            y_off_g.append(_dot_nt(cm_s, s_prev.astype(BF16)))
            bm_g = jnp.where(row_id == s, bm[:, g * SSM_STATE:(g + 1) * SSM_STATE], 0.0).astype(BF16)
            decay = jnp.concatenate(
                [jnp.broadcast_to(q_t[e:e + 1, :], (SSM_HEAD_DIM, SSM_STATE))
                 for e in range(g * HEADS_PER_GROUP, (g + 1) * HEADS_PER_GROUP)], axis=0)
            state_ref[s, rows, :] = decay * s_prev + _dot_tn(xw_b[:, rows], bm_g)
        y_off_rows.append(jnp.concatenate(y_off_g, axis=1))
    y = y + jnp.concatenate(y_off_rows, axis=0) * ea_full
    yb = _ssm_tail(y, xs, proj(O_Z, D_SSM), d_full_ref[...], ssm_norm_ref[...])
    mix = _dot(ya.astype(BF16), w_out_ref[0:D_SCONV, :]) + _dot(yb.astype(BF16), w_out_ref[D_SCONV:D_MIX, :])
    h1_ref[...] = h + _rms(mix, npost_ref[...])


def _smix(x_rows, sc_prev, xbc_prev, state, consts, dec_seq):
    n_seq = state.shape[0]
    t = SEQ_TILE * dec_seq
    const_spec = lambda a: (pl.BlockSpec(a.shape, lambda i: (0, 0), pipeline_mode=pl.Buffered(1)) if a.ndim == 2
                            else _w_main_spec(lambda i: (0, 0, 0), pipeline_mode=pl.Buffered(1)))
    row_spec = lambda width: pl.BlockSpec((t, width), lambda i: (i, 0))
    state_spec = pl.BlockSpec((SEQ_TILE, D_SSM, SSM_STATE), lambda i: (i, 0, 0))
    return pl.pallas_call(
        functools.partial(_smix_kernel, dec_seq=dec_seq),
        grid=(n_seq // SEQ_TILE,),
        in_specs=[row_spec(D_MODEL), row_spec(D_SCONV), row_spec(D_XBC), state_spec]
        + [const_spec(a) for a in consts],
        out_specs=[row_spec(D_MODEL), row_spec(D_SCONV), row_spec(D_XBC), state_spec],
        out_shape=[
            jax.ShapeDtypeStruct((n_seq * dec_seq, D_MODEL), F32),
            jax.ShapeDtypeStruct((n_seq * dec_seq, D_SCONV), F32),
            jax.ShapeDtypeStruct((n_seq * dec_seq, D_XBC), F32),
            jax.ShapeDtypeStruct(state.shape, F32),
        ],
        compiler_params=pltpu.CompilerParams(dimension_semantics=("arbitrary",), vmem_limit_bytes=VMEM_LIMIT),
        name="sample_mixer",
    )(x_rows, sc_prev, xbc_prev, state, *consts)


def _ffn_kernel(hp_ref, hs_ref, npre_ref, w_gate_ref, w_up_ref, w_down_ref, npost_ref, op_ref, os_ref, *,
                prompt_steps):
    def body(h_ref, o_ref):
        h1 = h_ref[...]
        fn = _rms(h1, npre_ref[...]).astype(BF16)
        act = (_silu(_dot(fn, w_gate_ref[...])) * _dot(fn, w_up_ref[...])).astype(BF16)
        o_ref[...] = h1 + _rms(_dot(act, w_down_ref[...]), npost_ref[...])

    step = pl.program_id(0)
    pl.when(step < prompt_steps)(functools.partial(body, hp_ref, op_ref))
    pl.when(step >= prompt_steps)(functools.partial(body, hs_ref, os_ref))


def _ffn(h_prompt, h_sample, ffn_consts):
    mp, ms = h_prompt.shape[0], h_sample.shape[0]
    rows = FFN_ROWS
    p_steps, s_steps = mp // rows, ms // rows
    p_spec = pl.BlockSpec((rows, D_MODEL), lambda i: (jnp.minimum(i, p_steps - 1), 0))
    s_spec = pl.BlockSpec((rows, D_MODEL), lambda i: (jnp.maximum(i - p_steps, 0), 0))
    const_spec = lambda a: pl.BlockSpec(a.shape, lambda i: (0, 0), pipeline_mode=pl.Buffered(1))
    return pl.pallas_call(
        functools.partial(_ffn_kernel, prompt_steps=p_steps),
        grid=(p_steps + s_steps,),
        in_specs=[p_spec, s_spec] + [const_spec(a) for a in ffn_consts],
        out_specs=[p_spec, s_spec],
        out_shape=[jax.ShapeDtypeStruct((mp, D_MODEL), F32), jax.ShapeDtypeStruct((ms, D_MODEL), F32)],
        compiler_params=pltpu.CompilerParams(dimension_semantics=("arbitrary",), vmem_limit_bytes=VMEM_LIMIT),
        name="ffn",
    )(h_prompt, h_sample, *ffn_consts)


def _row(v, width=None):
    v = v.reshape(1, -1).astype(F32)
    if width is not None and v.shape[1] < width:
        v = jnp.pad(v, ((0, 0), (0, width - v.shape[1])))
    return v


def _layer(hp, hs, sc_state, xbc_state, ssm_state, meta_tokens,
           norm_mix_pre, norm_mix_post, w_in_all, sconv_w, sconv_norm, ssm_conv_w, ssm_conv_b,
           dt_bias, a_log, d_skip, ssm_norm, w_out, norm_ffn_pre, norm_ffn_post, w_gate, w_up, w_down):
    batch, seq, _ = hp.shape
    w_in = w_in_all[0]
    n_seq, dec_seq, _ = hs.shape

    w_main_b = w_in_all.astype(BF16)
    w_dt_b = jnp.pad(w_in[:, O_DT:D_IN], ((0, 0), (0, D_IN_PAD - D_IN))).astype(BF16)
    w_dt_t = w_in[:, O_DT:D_IN].T.astype(BF16)
    w_out_b = w_out.astype(BF16)
    w_gate_b, w_up_b, w_down_b = w_gate.astype(BF16), w_up.astype(BF16), w_down.astype(BF16)
    npre, npost = _row(norm_mix_pre), _row(norm_mix_post)
    fpre, fpost = _row(norm_ffn_pre), _row(norm_ffn_post)
    sconv_w = sconv_w.astype(F32)
    xconv_w = ssm_conv_w.astype(F32)
    xconv_b = _row(ssm_conv_b)
    dt_bias_p, a_log_p = _row(dt_bias, LANES), _row(a_log, LANES)
    dt_bias_t = jnp.broadcast_to(dt_bias.astype(F32)[:, None], (SSM_HEADS, LANES))
    a_log_t = jnp.broadcast_to(a_log.astype(F32)[:, None], (SSM_HEADS, LANES))
    d_full = jnp.repeat(d_skip.astype(F32), SSM_HEAD_DIM).reshape(1, D_SSM)
    sconv_norm_r, ssm_norm_r = _row(sconv_norm), _row(ssm_norm)

    r = jnp.arange(CHUNK)
    tri = (r[:, None] >= r[None, :]).astype(BF16)
    expand = (jnp.arange(LANES)[:, None] == (jnp.arange(D_SSM)[None, :] // SSM_HEAD_DIM)).astype(BF16)

    mixer_consts = (sconv_w, sconv_norm_r, xconv_w, xconv_b, dt_bias_t, a_log_t, d_full, ssm_norm_r)
    ffn_consts = (fpre, w_gate_b, w_up_b, w_down_b, fpost)

    proj_meta = _inproj(meta_tokens.astype(F32), npre, w_main_b, w_dt_b)
    sc0, xbc0, state0_t = _meta_state(proj_meta, xconv_w, xconv_b, dt_bias_p, a_log_p, tri, expand)

    pm_consts = (npre, w_main_b, w_dt_t) + mixer_consts + (tri, w_out_b, npost)
    h1, sc_tail, xbc_tail, p_state = _pmix(hp, pm_consts, sc0, xbc0, state0_t)
    new_sc_p = sc_tail[:, SUBLANES - (SCONV_WIDTH - 1):, :]
    new_xbc_p = xbc_tail[:, SUBLANES - (SSM_CONV_WIDTH - 1):, :]
    new_ssm_p = p_state.reshape(batch, SSM_GROUPS, HEADS_PER_GROUP, SSM_HEAD_DIM, SSM_STATE)

    hs_rows = hs.reshape(n_seq * dec_seq, D_MODEL)
    sc_prev = jnp.pad(sc_state, ((0, 0), (dec_seq - (SCONV_WIDTH - 1), 0), (0, 0))).reshape(n_seq * dec_seq, D_SCONV)
    xbc_prev = jnp.pad(xbc_state, ((0, 0), (dec_seq - (SSM_CONV_WIDTH - 1), 0), (0, 0))).reshape(n_seq * dec_seq, D_XBC)
    sm_consts = (npre, w_main_b, w_dt_t) + mixer_consts + (w_out_b, npost)
    h1_s, u_s, xbc_s, s_state = _smix(hs_rows, sc_prev, xbc_prev, ssm_state.reshape(n_seq, D_SSM, SSM_STATE),
                                      sm_consts, dec_seq)

    y_prompt, y_sample = _ffn(h1.reshape(batch * seq, D_MODEL), h1_s, ffn_consts)
    y_prompt = y_prompt.reshape(batch, seq, D_MODEL)
    y_sample = y_sample.reshape(n_seq, dec_seq, D_MODEL)
    new_sc_s = u_s.reshape(n_seq, dec_seq, D_SCONV)[:, dec_seq - (SCONV_WIDTH - 1):, :]
    new_xbc_s = xbc_s.reshape(n_seq, dec_seq, D_XBC)[:, dec_seq - (SSM_CONV_WIDTH - 1):, :]
    new_ssm_s = s_state.reshape(n_seq, SSM_GROUPS, HEADS_PER_GROUP, SSM_HEAD_DIM, SSM_STATE)
    return y_prompt, y_sample, new_sc_p, new_xbc_p, new_ssm_p, new_sc_s, new_xbc_s, new_ssm_s


def kernel(x_prompt, x_sample, state_sconv, state_ssm_conv, state_ssm, meta_tokens, norm_mix_pre, norm_mix_post, w_in, sconv_w, sconv_norm, ssm_conv_w, ssm_conv_b, dt_bias, A_log, D_skip, ssm_norm, w_out, norm_ffn_pre, norm_ffn_post, w_gate, w_up, w_down):
    depth = w_in.shape[0]
    assert depth == 1 and x_sample.shape[1] == SUBLANES
    outs = _layer(x_prompt, x_sample, state_sconv[0], state_ssm_conv[0], state_ssm[0], meta_tokens,
                  norm_mix_pre[0], norm_mix_post[0], w_in, sconv_w[0], sconv_norm[0], ssm_conv_w[0],
                  ssm_conv_b[0], dt_bias[0], A_log[0], D_skip[0], ssm_norm[0], w_out[0],
                  norm_ffn_pre[0], norm_ffn_post[0], w_gate[0], w_up[0], w_down[0])
    y_prompt, y_sample = outs[0], outs[1]
    return (y_prompt, y_sample) + tuple(o[None] for o in outs[2:])
```

```python
import functools

import jax
import jax.numpy as jnp
from jax import lax
from jax.experimental import pallas as pl
from jax.experimental.pallas import tpu as pltpu

D_MODEL = 1024
N_META = 16
D_SCONV = 512
SCONV_GROUPS = 8
SCONV_WIDTH = 3
SSM_HEAD_DIM = 64
SSM_HEADS = 16
D_SSM = 1024
SSM_GROUPS = 2
HEADS_PER_GROUP = 8
D_GROUP = HEADS_PER_GROUP * SSM_HEAD_DIM
SSM_STATE = 128
SSM_CONV_WIDTH = 4
D_BC = 2 * SSM_GROUPS * SSM_STATE
D_XBC = D_SSM + D_BC
D_MIX = D_SCONV + D_SSM
D_IN = 3 * D_SCONV + D_SSM + D_XBC + SSM_HEADS
D_FF = 2816
EPS = 1e-6

LANES = 128
SUBLANES = 8
D_IN_PAD = D_IN + (LANES - SSM_HEADS)
HEAD_PAIRS = D_SSM // LANES

O_GATE_B = 0
O_GATE_C = D_SCONV
O_HV = 2 * D_SCONV
O_Z = 3 * D_SCONV
O_XS = O_Z + D_SSM
O_BC = O_XS + D_SSM
O_DT = O_BC + D_BC

CHUNK = 128
TILE_CHUNKS = 2
STAGE_A_COLS = tuple(range(0, O_DT + 1, D_SCONV))
STAGE_A_PLAN = ((0, "sconv"), (0, "prepare"), (0, 1), (0, 3), (0, 5), (1, "prepare"), (1, 1), (1, 3), (1, 5))
SEQ_TILE = 16
FFN_ROWS = 512
VMEM_LIMIT = 60000 * 1024

BF16 = jnp.bfloat16
F32 = jnp.float32


def _dot(a, b):
    return jnp.dot(a, b, preferred_element_type=F32)


def _dot_nt(a, b):
    return lax.dot_general(a, b, (((1,), (1,)), ((), ())), preferred_element_type=F32)


def _dot_tn(a, b):
    return lax.dot_general(a, b, (((0,), (0,)), ((), ())), preferred_element_type=F32)


def _split3(v):
    hi = v.astype(BF16)
    r1 = v - hi.astype(F32)
    mid = r1.astype(BF16)
    lo = (r1 - mid.astype(F32)).astype(BF16)
    return hi, mid, lo


def _dot3(v, m):
    hi, mid, lo = _split3(v)
    return _dot(hi, m) + _dot(mid, m) + _dot(lo, m)


def _dot3_left(m, v):
    hi, mid, lo = _split3(v)
    return _dot(m, hi) + _dot(m, mid) + _dot(m, lo)


def _rms(x, w):
    return x * lax.rsqrt(jnp.mean(x * x, axis=-1, keepdims=True) + EPS) * w


def _silu(x):
    return x * (1.0 / (1.0 + jnp.exp(-x)))


def _softplus(x):
    return jnp.maximum(x, 0.0) + jnp.log1p(jnp.exp(-jnp.abs(x)))


def _shifted(u, prev, j, rows_per_seq):
    t = u.shape[0]
    s = pltpu.roll(u, j, 0)
    if rows_per_seq is None:
        row = lax.broadcasted_iota(jnp.int32, (SUBLANES, u.shape[1]), 0)
        first = jnp.where(row < j, pltpu.roll(prev, j, 0), s[0:SUBLANES])
        if t == SUBLANES:
            return first
        return jnp.concatenate([first, s[SUBLANES:]], axis=0)
    row = lax.broadcasted_iota(jnp.int32, u.shape, 0) % SUBLANES
    return jnp.where(row < j, pltpu.roll(prev, t - SUBLANES + j, 0), s)


def _causal_conv(u, prev, w, rows_per_seq):
    k = w.shape[0]
    out = _shifted(u, prev, k - 1, rows_per_seq) * w[0:1]
    for i in range(1, k - 1):
        out = out + _shifted(u, prev, k - 1 - i, rows_per_seq) * w[i:i + 1]
    return out + u * w[k - 1:k]


def _sconv_mixer(gate_b, gate_c, hv, prev, sconv_w, sconv_norm, rows_per_seq):
    u = gate_c * hv
    conv_u = _causal_conv(u, prev, sconv_w, rows_per_seq)
    return _sconv_norm(gate_b, conv_u, sconv_norm), u


def _sconv_norm(gate_b, conv_u, sconv_norm):
    v = gate_b * conv_u
    first_half = lax.broadcasted_iota(jnp.int32, (v.shape[0], LANES), 1) < D_SCONV // SCONV_GROUPS
    parts = []
    for c in range(0, D_SCONV, LANES):
        vc = v[:, c:c + LANES]
        sq = vc * vc
        s_first = jnp.sum(jnp.where(first_half, sq, 0.0), axis=-1, keepdims=True)
        s_second = jnp.sum(jnp.where(first_half, 0.0, sq), axis=-1, keepdims=True)
        mean_sq = jnp.where(first_half, s_first, s_second) * (SCONV_GROUPS / D_SCONV)
        parts.append(vc * lax.rsqrt(mean_sq + EPS))
    return jnp.concatenate(parts, axis=1) * sconv_norm


def _ssd_prepare(dt_raw, dt_bias, a_log, tri, t):
    lane = lax.broadcasted_iota(jnp.int32, (t, LANES), 1)
    valid = lane < SSM_HEADS
    dt = jnp.where(valid, _softplus(dt_raw + dt_bias), 0.0)
    a = dt * (-jnp.exp(a_log))
    a_cum = _dot3_left(tri, a)
    return dt, a_cum


def _ssd_prepare_rows(dt_raw_t, dt_bias_t, a_log_t, tri, seq_ones=None):
    t = dt_raw_t.shape[1]
    dt_t = _softplus(dt_raw_t + dt_bias_t)
    a_t = dt_t * (-jnp.exp(a_log_t))
    hi, mid, lo = _split3(a_t)
    a_cum_t = _dot_nt(hi, tri) + _dot_nt(mid, tri) + _dot_nt(lo, tri)
    if seq_ones is None:
        a_total_t = jnp.broadcast_to(a_cum_t[:, t - 1:t], (SSM_HEADS, t))
    else:
        a_total_t = _dot_nt(hi, seq_ones) + _dot_nt(mid, seq_ones) + _dot_nt(lo, seq_ones)
    w_t = jnp.exp(a_total_t - a_cum_t) * dt_t
    pad = jnp.zeros((LANES - SSM_HEADS, t), F32)
    a_cum = jnp.concatenate([a_cum_t, pad], axis=0).T
    w = jnp.concatenate([w_t, pad], axis=0).T
    return a_cum, w, a_cum_t, dt_t, a_total_t


def _group_rms_ssm(y, w):
    parts = []
    for g in range(SSM_GROUPS):
        yg = y[:, g * D_GROUP:(g + 1) * D_GROUP]
        parts.append(yg * lax.rsqrt(jnp.mean(yg * yg, axis=-1, keepdims=True) + EPS))
    return jnp.concatenate(parts, axis=1) * w


def _ssd_heads(xs_of, bm_b, cm, a_cum, w, a_cum_t, dt_t, mask, state_t=None, gate=None, after_pair=None):
    t = a_cum.shape[0]
    assert t == LANES
    first_half = lax.broadcasted_iota(jnp.int32, (t, LANES), 1) < SSM_HEAD_DIM
    y_pairs, xw_pairs, ea_pairs, xs_pairs = [], [], [], []
    for g in range(SSM_GROUPS):
        cm_g = cm[:, g * SSM_STATE:(g + 1) * SSM_STATE]
        cb = _dot_nt(cm_g.astype(BF16), bm_b[:, g * SSM_STATE:(g + 1) * SSM_STATE])
        for q in range(HEADS_PER_GROUP // 2):
            pair = g * (HEADS_PER_GROUP // 2) + q
            cols = slice(pair * LANES, (pair + 1) * LANES)
            xs_p = xs_of(pair)
            rhs = xs_p.astype(BF16)
            if state_t is not None:
                rhs = jnp.concatenate([rhs, state_t[:, cols].astype(BF16)], axis=0)
            res, eas, ws = [], [], []
            for e in (2 * pair, 2 * pair + 1):
                a_col = jnp.broadcast_to(a_cum[:, e:e + 1], (t, LANES))
                decay = jnp.exp(jnp.where(mask, a_col - a_cum_t[e:e + 1, :], -jnp.inf))
                lhs = (cb * decay * dt_t[e:e + 1, :]).astype(BF16)
                ea = jnp.exp(a_col)
                if state_t is not None:
                    lhs = jnp.concatenate([lhs, (cm_g * ea).astype(BF16)], axis=1)
                res.append(_dot(lhs, rhs))
                eas.append(ea)
                ws.append(jnp.broadcast_to(w[:, e:e + 1], (t, LANES)))
            y_p = jnp.where(first_half, res[0], res[1])
            y_pairs.append(y_p if gate is None else gate(pair, y_p, xs_p))
            xw_pairs.append((xs_p * jnp.where(first_half, ws[0], ws[1])).astype(BF16))
            ea_pairs.append(jnp.where(first_half, eas[0], eas[1]))
            xs_pairs.append(xs_p)
            if after_pair is not None:
                after_pair(pair, y_pairs, xw_pairs, ea_pairs)
    return (jnp.concatenate(y_pairs, axis=1), jnp.concatenate(xw_pairs, axis=1),
            jnp.concatenate(ea_pairs, axis=1), jnp.concatenate(xs_pairs, axis=1))


def _ssm_tail(y, xs, z, d_full, ssm_norm):
    y = y + d_full * xs
    y = y * _silu(z)
    return _group_rms_ssm(y, ssm_norm)


def _w_main_spec(index_map, **kwargs):
    return pl.BlockSpec((1, D_MODEL, O_DT), index_map, **kwargs)


def _inproj_kernel(x_ref, npre_ref, w_main_ref, w_dt_ref, o_ref):
    hn = _rms(x_ref[...], npre_ref[...]).astype(BF16)
    o_ref[:, 0:O_DT] = _dot(hn, w_main_ref[0])
    o_ref[:, O_DT:D_IN_PAD] = _dot(hn, w_dt_ref[...])


def _inproj(x, npre, w_main_b, w_dt_b):
    m = x.shape[0]
    once = lambda shape: pl.BlockSpec(shape, lambda i: (0, 0), pipeline_mode=pl.Buffered(1))
    return pl.pallas_call(
        _inproj_kernel,
        grid=(1,),
        in_specs=[once((m, D_MODEL)), once((1, D_MODEL)),
                  _w_main_spec(lambda i: (0, 0, 0), pipeline_mode=pl.Buffered(1)), once((D_MODEL, LANES))],
        out_specs=once((m, D_IN_PAD)),
        out_shape=jax.ShapeDtypeStruct((m, D_IN_PAD), F32),
        compiler_params=pltpu.CompilerParams(dimension_semantics=("arbitrary",), vmem_limit_bytes=VMEM_LIMIT),
        name="inproj",
    )(x, npre, w_main_b, w_dt_b)


def _meta_kernel(proj_ref, xconv_w_ref, xconv_b_ref, dt_bias_ref, a_log_ref, tri_ref, expand_ref,
                 sc_tail_ref, xbc_tail_ref, state_t_ref):
    t = N_META
    proj = proj_ref[...]
    u = proj[:, O_GATE_C:O_GATE_C + D_SCONV] * proj[:, O_HV:O_HV + D_SCONV]
    sc_tail_ref[...] = u[t - SUBLANES:t]
    xbc = proj[:, O_XS:O_XS + D_XBC]
    xbc_tail_ref[...] = xbc[t - SUBLANES:t]
    zeros_prev = jnp.zeros((SUBLANES, D_XBC), F32)
    xbc_c = _silu(_causal_conv(xbc, zeros_prev, xconv_w_ref[...], None) + xconv_b_ref[...])
    xs = xbc_c[:, 0:D_SSM]
    dt, a_cum = _ssd_prepare(proj[:, O_DT:O_DT + LANES], dt_bias_ref[...], a_log_ref[...],
                             tri_ref[0:t, 0:t], t)
    w = jnp.exp(a_cum[t - 1:t, :] - a_cum) * dt
    xw_b = (xs * _dot3(w, expand_ref[...])).astype(BF16)
    for g in range(SSM_GROUPS):
        bm_b = xbc_c[:, D_SSM + g * SSM_STATE:D_SSM + (g + 1) * SSM_STATE].astype(BF16)
        state_t_ref[:, g * D_GROUP:(g + 1) * D_GROUP] = _dot_tn(bm_b, xw_b[:, g * D_GROUP:(g + 1) * D_GROUP])


def _meta_state(proj_meta, xconv_w, xconv_b, dt_bias, a_log, tri, expand):
    full = lambda shape: pl.BlockSpec(shape, lambda: tuple(0 for _ in shape))
    return pl.pallas_call(
        _meta_kernel,
        in_specs=[full(proj_meta.shape), full(xconv_w.shape), full(xconv_b.shape), full(dt_bias.shape),
                  full(a_log.shape), full(tri.shape), full(expand.shape)],
        out_specs=[full((SUBLANES, D_SCONV)), full((SUBLANES, D_XBC)), full((SSM_STATE, D_SSM))],
        out_shape=[jax.ShapeDtypeStruct((SUBLANES, D_SCONV), F32),
                   jax.ShapeDtypeStruct((SUBLANES, D_XBC), F32),
                   jax.ShapeDtypeStruct((SSM_STATE, D_SSM), F32)],
        name="meta_state",
    )(proj_meta, xconv_w, xconv_b, dt_bias, a_log, tri, expand)


def _pmix_kernel(x_ref, npre_ref, w_main_ref, w_dt_ref, sconv_w_ref, sconv_norm_ref,
                 xconv_w_ref, xconv_b_ref, dt_bias_ref, a_log_ref, d_full_ref, ssm_norm_ref,
                 tri_ref, w_out_ref, npost_ref, sc0_ref, xbc0_ref, state0_t_ref,
                 h1_ref, sc_out_ref, xbc_out_ref, state_ref,
                 proj_s0, proj_s1, dt_s0, dt_s1, h_s0, h_s1, hn_s, sc_tail, xbc_tail, state_t_ref, *,
                 steps_per_seq):
    rows = TILE_CHUNKS * CHUNK
    k = pl.program_id(0)
    step_in_seq = lax.rem(jnp.maximum(k - 1, 0), steps_per_seq)

    @pl.when(k == 0)
    def _():
        proj_s1[...] = jnp.zeros((rows, O_DT), F32)
        dt_s1[...] = jnp.zeros((SSM_HEADS, rows), F32)
        h_s1[...] = jnp.zeros((rows, D_MODEL), F32)

    @pl.when(step_in_seq == 0)
    def _():
        sc_tail[...] = sc0_ref[...]
        xbc_tail[...] = xbc0_ref[...]
        state_t_ref[...] = state0_t_ref[...]

    for parity in (0, 1):
        pl.when(lax.rem(k, 2) == parity)(functools.partial(
            _pmix_step, parity, x_ref, npre_ref, w_main_ref, w_dt_ref, sconv_w_ref, sconv_norm_ref,
            xconv_w_ref, xconv_b_ref, dt_bias_ref, a_log_ref, d_full_ref, ssm_norm_ref, tri_ref, w_out_ref,
            npost_ref, h1_ref, (proj_s0, proj_s1), (dt_s0, dt_s1), (h_s0, h_s1), hn_s, sc_tail, xbc_tail,
            state_t_ref))

    @pl.when(step_in_seq == steps_per_seq - 1)
    def _():
        sc_out_ref[0] = sc_tail[...]
        xbc_out_ref[0] = xbc_tail[...]
        for p in range(HEAD_PAIRS):
            state_ref[0, p * LANES:(p + 1) * LANES, :] = state_t_ref[:, p * LANES:(p + 1) * LANES].T


def _pmix_step(slot_a, x_ref, npre_ref, w_main_ref, w_dt_ref, sconv_w_ref, sconv_norm_ref,
               xconv_w_ref, xconv_b_ref, dt_bias_ref, a_log_ref, d_full_ref, ssm_norm_ref, tri_ref, w_out_ref,
               npost_ref, h1_ref, proj_s, dt_s, h_s, hn_s, sc_tail, xbc_tail, state_t_ref):
    t = CHUNK
    slot_b = 1 - slot_a

    h_a = x_ref[0]
    h_s[slot_a][...] = h_a
    hn_s[...] = _rms(h_a, npre_ref[...]).astype(BF16)

    def stage_a(point):
        for piece, planned in enumerate(STAGE_A_PLAN):
            if planned == point:
                if piece + 1 < len(STAGE_A_COLS):
                    off, end = STAGE_A_COLS[piece], STAGE_A_COLS[piece + 1]
                    proj_s[slot_a][:, off:end] = _dot(hn_s[...], w_main_ref[0, :, off:end])
                else:
                    dt_s[slot_a][...] = _dot_nt(w_dt_ref[...], hn_s[...])

    for sub in range(TILE_CHUNKS):
        r = slice(sub * t, (sub + 1) * t)

        def proj(off, width):
            return proj_s[slot_b][r, off:off + width]

        ya, u = _sconv_mixer(proj(O_GATE_B, D_SCONV), proj(O_GATE_C, D_SCONV), proj(O_HV, D_SCONV),
                             sc_tail[...], sconv_w_ref[...], sconv_norm_ref[...], None)
        sc_tail[...] = u[t - SUBLANES:t]
        stage_a((sub, "sconv"))
        mix_ya = _dot(ya.astype(BF16), w_out_ref[0:D_SCONV, :])

        def xconv(col, width):
            cols = slice(col, col + width)
            raw = proj(O_XS + col, width)
            out = _silu(_causal_conv(raw, xbc_tail[:, cols], xconv_w_ref[:, cols], None) + xconv_b_ref[:, cols])
            xbc_tail[:, cols] = raw[t - SUBLANES:t]
            return out

        bc = xconv(D_SSM, D_BC)
        bm_b = bc[:, 0:SSM_GROUPS * SSM_STATE].astype(BF16)
        cm = bc[:, SSM_GROUPS * SSM_STATE:D_BC]
        a_cum, w, a_cum_t, dt_t, _ = _ssd_prepare_rows(dt_s[slot_b][:, r], dt_bias_ref[...], a_log_ref[...],
                                                       tri_ref[...])
        causal = lax.broadcasted_iota(jnp.int32, (t, t), 0) >= lax.broadcasted_iota(jnp.int32, (t, t), 1)
        stage_a((sub, "prepare"))

        def gate(p, y_p, xs_p):
            cols = slice(p * LANES, (p + 1) * LANES)
            return (y_p + d_full_ref[:, cols] * xs_p) * _silu(proj(O_Z + p * LANES, LANES))

        y, xw_b, ea_full, _ = _ssd_heads(lambda p: xconv(p * LANES, LANES), bm_b, cm, a_cum, w, a_cum_t, dt_t,
                                         causal, state_t_ref, gate, lambda p, *partial: stage_a((sub, p)))
        chunk_decay = ea_full[t - 1:t, :]
        for g in range(SSM_GROUPS):
            cols = slice(g * D_GROUP, (g + 1) * D_GROUP)
            state_t_ref[:, cols] = chunk_decay[:, cols] * state_t_ref[:, cols] + _dot_tn(
                bm_b[:, g * SSM_STATE:(g + 1) * SSM_STATE], xw_b[:, cols])
        yb = _group_rms_ssm(y, ssm_norm_ref[...])
        mix = mix_ya + _dot(yb.astype(BF16), w_out_ref[D_SCONV:D_MIX, :])
        stage_a((sub, "end"))
        h1_ref[0, r, :] = h_s[slot_b][r, :] + _rms(mix, npost_ref[...])


def _pmix(x_prompt, consts, sc0, xbc0, state0_t):
    batch, seq, _ = x_prompt.shape
    rows = TILE_CHUNKS * CHUNK
    sps = seq // rows
    n_tiles = batch * sps
    const_spec = lambda a: (pl.BlockSpec(a.shape, lambda k: (0, 0)) if a.ndim == 2
                            else _w_main_spec(lambda k: (0, 0, 0)))
    ins = list(consts) + [sc0, xbc0, state0_t]

    def tile_a(k):
        tile = jnp.minimum(k, n_tiles - 1)
        return (tile // sps, tile % sps, 0)

    def tile_b(k):
        tile = jnp.maximum(k - 1, 0)
        return (tile // sps, tile % sps, 0)

    def seq_b(k):
        return (jnp.maximum(k - 1, 0) // sps, 0, 0)

    return pl.pallas_call(
        functools.partial(_pmix_kernel, steps_per_seq=sps),
        grid=(n_tiles + 1,),
        in_specs=[pl.BlockSpec((1, rows, D_MODEL), tile_a)] + [const_spec(a) for a in ins],
        out_specs=[
            pl.BlockSpec((1, rows, D_MODEL), tile_b),
            pl.BlockSpec((1, SUBLANES, D_SCONV), seq_b),
            pl.BlockSpec((1, SUBLANES, D_XBC), seq_b),
            pl.BlockSpec((1, D_SSM, SSM_STATE), seq_b),
        ],
        out_shape=[
            jax.ShapeDtypeStruct((batch, seq, D_MODEL), F32),
            jax.ShapeDtypeStruct((batch, SUBLANES, D_SCONV), F32),
            jax.ShapeDtypeStruct((batch, SUBLANES, D_XBC), F32),
            jax.ShapeDtypeStruct((batch, D_SSM, SSM_STATE), F32),
        ],
        scratch_shapes=[
            pltpu.VMEM((rows, O_DT), F32),
            pltpu.VMEM((rows, O_DT), F32),
            pltpu.VMEM((SSM_HEADS, rows), F32),
            pltpu.VMEM((SSM_HEADS, rows), F32),
            pltpu.VMEM((rows, D_MODEL), F32),
            pltpu.VMEM((rows, D_MODEL), F32),
            pltpu.VMEM((rows, D_MODEL), BF16),
            pltpu.VMEM((SUBLANES, D_SCONV), F32),
            pltpu.VMEM((SUBLANES, D_XBC), F32),
            pltpu.VMEM((SSM_STATE, D_SSM), F32),
        ],
        compiler_params=pltpu.CompilerParams(dimension_semantics=("arbitrary",), vmem_limit_bytes=VMEM_LIMIT),
        name="prompt_mixer",
    )(x_prompt, *ins)


def _smix_kernel(x_ref, sc_prev_ref, xbc_prev_ref, state_in_ref, npre_ref, w_main_ref, w_dt_ref,
                 sconv_w_ref, sconv_norm_ref, xconv_w_ref, xconv_b_ref,
                 dt_bias_ref, a_log_ref, d_full_ref, ssm_norm_ref, w_out_ref, npost_ref,
                 h1_ref, u_ref, xbc_ref, state_ref, *, dec_seq):
    t = SEQ_TILE * dec_seq
    h = x_ref[...]
    hn = _rms(h, npre_ref[...]).astype(BF16)

    def proj(off, width):
        return _dot(hn, w_main_ref[0, :, off:off + width])

    ya, u = _sconv_mixer(proj(O_GATE_B, D_SCONV), proj(O_GATE_C, D_SCONV), proj(O_HV, D_SCONV),
                         sc_prev_ref[...], sconv_w_ref[...], sconv_norm_ref[...], dec_seq)
    u_ref[...] = u
    xbc = proj(O_XS, D_XBC)
    xbc_ref[...] = xbc
    xbc_c = _silu(_causal_conv(xbc, xbc_prev_ref[...], xconv_w_ref[...], dec_seq) + xconv_b_ref[...])
    xs = xbc_c[:, 0:D_SSM]
    bm = xbc_c[:, D_SSM:D_SSM + SSM_GROUPS * SSM_STATE]
    bm_b = bm.astype(BF16)
    cm = xbc_c[:, D_SSM + SSM_GROUPS * SSM_STATE:D_XBC]

    row = lax.broadcasted_iota(jnp.int32, (t, t), 0)
    col = lax.broadcasted_iota(jnp.int32, (t, t), 1)
    same_seq = row // dec_seq == col // dec_seq
    mask = jnp.logical_and(row >= col, same_seq)
    tri = jnp.where(mask, 1.0, 0.0).astype(BF16)
    seq_ones = jnp.where(same_seq, 1.0, 0.0).astype(BF16)
    a_cum, w, a_cum_t, dt_t, a_total_t = _ssd_prepare_rows(_dot_nt(w_dt_ref[...], hn), dt_bias_ref[...],
                                                           a_log_ref[...], tri, seq_ones)
    y, xw_b, ea_full, _ = _ssd_heads(lambda p: xs[:, p * LANES:(p + 1) * LANES], bm_b, cm, a_cum, w, a_cum_t,
                                     dt_t, mask)

    row_id = lax.broadcasted_iota(jnp.int32, (t, SSM_STATE), 0) // dec_seq
    seq_decay_t = jnp.exp(a_total_t)
    y_off_rows = []
    for s in range(SEQ_TILE):
        r0 = s * dec_seq
        q_t = jnp.broadcast_to(seq_decay_t[:, r0:r0 + 1], (SSM_HEADS, SSM_STATE))
        y_off_g = []
        for g in range(SSM_GROUPS):
            rows = slice(g * D_GROUP, (g + 1) * D_GROUP)
            s_prev = state_in_ref[s, rows, :]
            cm_s = cm[r0:r0 + dec_seq, g * SSM_STATE:(g + 1) * SSM_STATE].astype(BF16)
            y_off_g.append(_dot_nt(cm_s, s_prev.astype(BF16)))
            bm_g = jnp.where(row_id == s, bm[:, g * SSM_STATE:(g + 1) * SSM_STATE], 0.0).astype(BF16)
            decay = jnp.concatenate(
                [jnp.broadcast_to(q_t[e:e + 1, :], (SSM_HEAD_DIM, SSM_STATE))
                 for e in range(g * HEADS_PER_GROUP, (g + 1) * HEADS_PER_GROUP)], axis=0)
            state_ref[s, rows, :] = decay * s_prev + _dot_tn(xw_b[:, rows], bm_g)
        y_off_rows.append(jnp.concatenate(y_off_g, axis=1))
    y = y + jnp.concatenate(y_off_rows, axis=0) * ea_full
    yb = _ssm_tail(y, xs, proj(O_Z, D_SSM), d_full_ref[...], ssm_norm_ref[...])
    mix = _dot(ya.astype(BF16), w_out_ref[0:D_SCONV, :]) + _dot(yb.astype(BF16), w_out_ref[D_SCONV:D_MIX, :])
    h1_ref[...] = h + _rms(mix, npost_ref[...])


def _smix(x_rows, sc_prev, xbc_prev, state, consts, dec_seq):
    n_seq = state.shape[0]
    t = SEQ_TILE * dec_seq
    const_spec = lambda a: (pl.BlockSpec(a.shape, lambda i: (0, 0), pipeline_mode=pl.Buffered(1)) if a.ndim == 2
                            else _w_main_spec(lambda i: (0, 0, 0), pipeline_mode=pl.Buffered(1)))
    row_spec = lambda width: pl.BlockSpec((t, width), lambda i: (i, 0))
    state_spec = pl.BlockSpec((SEQ_TILE, D_SSM, SSM_STATE), lambda i: (i, 0, 0))
    return pl.pallas_call(
        functools.partial(_smix_kernel, dec_seq=dec_seq),
        grid=(n_seq // SEQ_TILE,),
        in_specs=[row_spec(D_MODEL), row_spec(D_SCONV), row_spec(D_XBC), state_spec]
        + [const_spec(a) for a in consts],
        out_specs=[row_spec(D_MODEL), row_spec(D_SCONV), row_spec(D_XBC), state_spec],
        out_shape=[
            jax.ShapeDtypeStruct((n_seq * dec_seq, D_MODEL), F32),
            jax.ShapeDtypeStruct((n_seq * dec_seq, D_SCONV), F32),
            jax.ShapeDtypeStruct((n_seq * dec_seq, D_XBC), F32),
            jax.ShapeDtypeStruct(state.shape, F32),
        ],
        compiler_params=pltpu.CompilerParams(dimension_semantics=("arbitrary",), vmem_limit_bytes=VMEM_LIMIT),
        name="sample_mixer",
    )(x_rows, sc_prev, xbc_prev, state, *consts)


def _ffn_kernel(hp_ref, hs_ref, npre_ref, w_gate_ref, w_up_ref, w_down_ref, npost_ref, op_ref, os_ref, *,
                prompt_steps):
    def body(h_ref, o_ref):
        h1 = h_ref[...]
        fn = _rms(h1, npre_ref[...]).astype(BF16)
        act = (_silu(_dot(fn, w_gate_ref[...])) * _dot(fn, w_up_ref[...])).astype(BF16)
        o_ref[...] = h1 + _rms(_dot(act, w_down_ref[...]), npost_ref[...])

    step = pl.program_id(0)
    pl.when(step < prompt_steps)(functools.partial(body, hp_ref, op_ref))
    pl.when(step >= prompt_steps)(functools.partial(body, hs_ref, os_ref))


def _ffn(h_prompt, h_sample, ffn_consts):
    mp, ms = h_prompt.shape[0], h_sample.shape[0]
    rows = FFN_ROWS
    p_steps, s_steps = mp // rows, ms // rows
    p_spec = pl.BlockSpec((rows, D_MODEL), lambda i: (jnp.minimum(i, p_steps - 1), 0))
    s_spec = pl.BlockSpec((rows, D_MODEL), lambda i: (jnp.maximum(i - p_steps, 0), 0))
    const_spec = lambda a: pl.BlockSpec(a.shape, lambda i: (0, 0), pipeline_mode=pl.Buffered(1))
    return pl.pallas_call(
        functools.partial(_ffn_kernel, prompt_steps=p_steps),
        grid=(p_steps + s_steps,),
        in_specs=[p_spec, s_spec] + [const_spec(a) for a in ffn_consts],
        out_specs=[p_spec, s_spec],
        out_shape=[jax.ShapeDtypeStruct((mp, D_MODEL), F32), jax.ShapeDtypeStruct((ms, D_MODEL), F32)],
        compiler_params=pltpu.CompilerParams(dimension_semantics=("arbitrary",), vmem_limit_bytes=VMEM_LIMIT),
        name="ffn",
    )(h_prompt, h_sample, *ffn_consts)


def _row(v, width=None):
    v = v.reshape(1, -1).astype(F32)
    if width is not None and v.shape[1] < width:
        v = jnp.pad(v, ((0, 0), (0, width - v.shape[1])))
    return v


def _layer(hp, hs, sc_state, xbc_state, ssm_state, meta_tokens,
           norm_mix_pre, norm_mix_post, w_in_all, sconv_w, sconv_norm, ssm_conv_w, ssm_conv_b,
           dt_bias, a_log, d_skip, ssm_norm, w_out, norm_ffn_pre, norm_ffn_post, w_gate, w_up, w_down):
    batch, seq, _ = hp.shape
    w_in = w_in_all[0]
    n_seq, dec_seq, _ = hs.shape

    w_main_b = w_in_all.astype(BF16)
    w_dt_b = jnp.pad(w_in[:, O_DT:D_IN], ((0, 0), (0, D_IN_PAD - D_IN))).astype(BF16)
    w_dt_t = w_in[:, O_DT:D_IN].T.astype(BF16)
    w_out_b = w_out.astype(BF16)
    w_gate_b, w_up_b, w_down_b = w_gate.astype(BF16), w_up.astype(BF16), w_down.astype(BF16)
    npre, npost = _row(norm_mix_pre), _row(norm_mix_post)
    fpre, fpost = _row(norm_ffn_pre), _row(norm_ffn_post)
    sconv_w = sconv_w.astype(F32)
    xconv_w = ssm_conv_w.astype(F32)
    xconv_b = _row(ssm_conv_b)
    dt_bias_p, a_log_p = _row(dt_bias, LANES), _row(a_log, LANES)
    dt_bias_t = jnp.broadcast_to(dt_bias.astype(F32)[:, None], (SSM_HEADS, LANES))
    a_log_t = jnp.broadcast_to(a_log.astype(F32)[:, None], (SSM_HEADS, LANES))
    d_full = jnp.repeat(d_skip.astype(F32), SSM_HEAD_DIM).reshape(1, D_SSM)
    sconv_norm_r, ssm_norm_r = _row(sconv_norm), _row(ssm_norm)

    r = jnp.arange(CHUNK)
    tri = (r[:, None] >= r[None, :]).astype(BF16)
    expand = (jnp.arange(LANES)[:, None] == (jnp.arange(D_SSM)[None, :] // SSM_HEAD_DIM)).astype(BF16)

    mixer_consts = (sconv_w, sconv_norm_r, xconv_w, xconv_b, dt_bias_t, a_log_t, d_full, ssm_norm_r)
    ffn_consts = (fpre, w_gate_b, w_up_b, w_down_b, fpost)

    proj_meta = _inproj(meta_tokens.astype(F32), npre, w_main_b, w_dt_b)
    sc0, xbc0, state0_t = _meta_state(proj_meta, xconv_w, xconv_b, dt_bias_p, a_log_p, tri, expand)

    pm_consts = (npre, w_main_b, w_dt_t) + mixer_consts + (tri, w_out_b, npost)
    h1, sc_tail, xbc_tail, p_state = _pmix(hp, pm_consts, sc0, xbc0, state0_t)
    new_sc_p = sc_tail[:, SUBLANES - (SCONV_WIDTH - 1):, :]
    new_xbc_p = xbc_tail[:, SUBLANES - (SSM_CONV_WIDTH - 1):, :]
    new_ssm_p = p_state.reshape(batch, SSM_GROUPS, HEADS_PER_GROUP, SSM_HEAD_DIM, SSM_STATE)

    hs_rows = hs.reshape(n_seq * dec_seq, D_MODEL)
    sc_prev = jnp.pad(sc_state, ((0, 0), (dec_seq - (SCONV_WIDTH - 1), 0), (0, 0))).reshape(n_seq * dec_seq, D_SCONV)
    xbc_prev = jnp.pad(xbc_state, ((0, 0), (dec_seq - (SSM_CONV_WIDTH - 1), 0), (0, 0))).reshape(n_seq * dec_seq, D_XBC)
    sm_consts = (npre, w_main_b, w_dt_t) + mixer_consts + (w_out_b, npost)
    h1_s, u_s, xbc_s, s_state = _smix(hs_rows, sc_prev, xbc_prev, ssm_state.reshape(n_seq, D_SSM, SSM_STATE),
                                      sm_consts, dec_seq)

    y_prompt, y_sample = _ffn(h1.reshape(batch * seq, D_MODEL), h1_s, ffn_consts)
    y_prompt = y_prompt.reshape(batch, seq, D_MODEL)
    y_sample = y_sample.reshape(n_seq, dec_seq, D_MODEL)
    new_sc_s = u_s.reshape(n_seq, dec_seq, D_SCONV)[:, dec_seq - (SCONV_WIDTH - 1):, :]
    new_xbc_s = xbc_s.reshape(n_seq, dec_seq, D_XBC)[:, dec_seq - (SSM_CONV_WIDTH - 1):, :]
    new_ssm_s = s_state.reshape(n_seq, SSM_GROUPS, HEADS_PER_GROUP, SSM_HEAD_DIM, SSM_STATE)
    return y_prompt, y_sample, new_sc_p, new_xbc_p, new_ssm_p, new_sc_s, new_xbc_s, new_ssm_s


def kernel(x_prompt, x_sample, state_sconv, state_ssm_conv, state_ssm, meta_tokens, norm_mix_pre, norm_mix_post, w_in, sconv_w, sconv_norm, ssm_conv_w, ssm_conv_b, dt_bias, A_log, D_skip, ssm_norm, w_out, norm_ffn_pre, norm_ffn_post, w_gate, w_up, w_down):
    depth = w_in.shape[0]
    assert depth == 1 and x_sample.shape[1] == SUBLANES
    outs = _layer(x_prompt, x_sample, state_sconv[0], state_ssm_conv[0], state_ssm[0], meta_tokens,
                  norm_mix_pre[0], norm_mix_post[0], w_in, sconv_w[0], sconv_norm[0], ssm_conv_w[0],
                  ssm_conv_b[0], dt_bias[0], A_log[0], D_skip[0], ssm_norm[0], w_out[0],
                  norm_ffn_pre[0], norm_ffn_post[0], w_gate[0], w_up[0], w_down[0])
    y_prompt, y_sample = outs[0], outs[1]
    return (y_prompt, y_sample) + tuple(o[None] for o in outs[2:])
```

```python
import functools

import jax
import jax.numpy as jnp
from jax import lax
from jax.experimental import pallas as pl
from jax.experimental.pallas import tpu as pltpu

D_MODEL = 1024
N_META = 16
D_SCONV = 512
SCONV_GROUPS = 8
SCONV_WIDTH = 3
SSM_HEAD_DIM = 64
SSM_HEADS = 16
D_SSM = 1024
SSM_GROUPS = 2
HEADS_PER_GROUP = 8
D_GROUP = HEADS_PER_GROUP * SSM_HEAD_DIM
SSM_STATE = 128
SSM_CONV_WIDTH = 4
D_BC = 2 * SSM_GROUPS * SSM_STATE
D_XBC = D_SSM + D_BC
D_MIX = D_SCONV + D_SSM
D_IN = 3 * D_SCONV + D_SSM + D_XBC + SSM_HEADS
D_FF = 2816
EPS = 1e-6

LANES = 128
SUBLANES = 8
D_IN_PAD = D_IN + (LANES - SSM_HEADS)
HEAD_PAIRS = D_SSM // LANES

O_GATE_B = 0
O_GATE_C = D_SCONV
O_HV = 2 * D_SCONV
O_Z = 3 * D_SCONV
O_XS = O_Z + D_SSM
O_BC = O_XS + D_SSM
O_DT = O_BC + D_BC

CHUNK = 128
TILE_CHUNKS = 2
STAGE_A_COLS = tuple(range(0, O_DT + 1, D_SCONV))
STAGE_A_PLAN = ((0, "sconv"), (0, "prepare"), (0, 1), (0, 3), (0, 5), (1, "prepare"), (1, 1), (1, 3), (1, 5))
SEQ_TILE = 16
FFN_ROWS = 1024
FFN_SUB_ROWS = 512
VMEM_LIMIT = 60000 * 1024

BF16 = jnp.bfloat16
F32 = jnp.float32


def _dot(a, b):
    return jnp.dot(a, b, preferred_element_type=F32)


def _dot_nt(a, b):
    return lax.dot_general(a, b, (((1,), (1,)), ((), ())), preferred_element_type=F32)


def _dot_tn(a, b):
    return lax.dot_general(a, b, (((0,), (0,)), ((), ())), preferred_element_type=F32)


def _split3(v):
    hi = v.astype(BF16)
    r1 = v - hi.astype(F32)
    mid = r1.astype(BF16)
    lo = (r1 - mid.astype(F32)).astype(BF16)
    return hi, mid, lo


def _dot3(v, m):
    hi, mid, lo = _split3(v)
    return _dot(hi, m) + _dot(mid, m) + _dot(lo, m)


def _dot3_left(m, v):
    hi, mid, lo = _split3(v)
    return _dot(m, hi) + _dot(m, mid) + _dot(m, lo)


def _rms(x, w):
    return x * lax.rsqrt(jnp.mean(x * x, axis=-1, keepdims=True) + EPS) * w


def _silu(x):
    return x * (1.0 / (1.0 + jnp.exp(-x)))


def _softplus(x):
    return jnp.maximum(x, 0.0) + jnp.log1p(jnp.exp(-jnp.abs(x)))


def _shifted(u, prev, j, rows_per_seq):
    t = u.shape[0]
    s = pltpu.roll(u, j, 0)
    if rows_per_seq is None:
        row = lax.broadcasted_iota(jnp.int32, (SUBLANES, u.shape[1]), 0)
        first = jnp.where(row < j, pltpu.roll(prev, j, 0), s[0:SUBLANES])
        if t == SUBLANES:
            return first
        return jnp.concatenate([first, s[SUBLANES:]], axis=0)
    row = lax.broadcasted_iota(jnp.int32, u.shape, 0) % SUBLANES
    return jnp.where(row < j, pltpu.roll(prev, t - SUBLANES + j, 0), s)


def _causal_conv(u, prev, w, rows_per_seq):
    k = w.shape[0]
    out = _shifted(u, prev, k - 1, rows_per_seq) * w[0:1]
    for i in range(1, k - 1):
        out = out + _shifted(u, prev, k - 1 - i, rows_per_seq) * w[i:i + 1]
    return out + u * w[k - 1:k]


def _sconv_mixer(gate_b, gate_c, hv, prev, sconv_w, sconv_norm, rows_per_seq):
    u = gate_c * hv
    conv_u = _causal_conv(u, prev, sconv_w, rows_per_seq)
    return _sconv_norm(gate_b, conv_u, sconv_norm), u


def _sconv_norm(gate_b, conv_u, sconv_norm):
    v = gate_b * conv_u
    first_half = lax.broadcasted_iota(jnp.int32, (v.shape[0], LANES), 1) < D_SCONV // SCONV_GROUPS
    parts = []
    for c in range(0, D_SCONV, LANES):
        vc = v[:, c:c + LANES]
        sq = vc * vc
        s_first = jnp.sum(jnp.where(first_half, sq, 0.0), axis=-1, keepdims=True)
        s_second = jnp.sum(jnp.where(first_half, 0.0, sq), axis=-1, keepdims=True)
        mean_sq = jnp.where(first_half, s_first, s_second) * (SCONV_GROUPS / D_SCONV)
        parts.append(vc * lax.rsqrt(mean_sq + EPS))
    return jnp.concatenate(parts, axis=1) * sconv_norm


def _ssd_prepare(dt_raw, dt_bias, a_log, tri, t):
    lane = lax.broadcasted_iota(jnp.int32, (t, LANES), 1)
    valid = lane < SSM_HEADS
    dt = jnp.where(valid, _softplus(dt_raw + dt_bias), 0.0)
    a = dt * (-jnp.exp(a_log))
    a_cum = _dot3_left(tri, a)
    return dt, a_cum


def _ssd_prepare_rows(dt_raw_t, dt_bias_t, a_log_t, tri, seq_ones=None):
    t = dt_raw_t.shape[1]
    dt_t = _softplus(dt_raw_t + dt_bias_t)
    a_t = dt_t * (-jnp.exp(a_log_t))
    hi, mid, lo = _split3(a_t)
    a_cum_t = _dot_nt(hi, tri) + _dot_nt(mid, tri) + _dot_nt(lo, tri)
    if seq_ones is None:
        a_total_t = jnp.broadcast_to(a_cum_t[:, t - 1:t], (SSM_HEADS, t))
    else:
        a_total_t = _dot_nt(hi, seq_ones) + _dot_nt(mid, seq_ones) + _dot_nt(lo, seq_ones)
    w_t = jnp.exp(a_total_t - a_cum_t) * dt_t
    pad = jnp.zeros((LANES - SSM_HEADS, t), F32)
    a_cum = jnp.concatenate([a_cum_t, pad], axis=0).T
    w = jnp.concatenate([w_t, pad], axis=0).T
    return a_cum, w, a_cum_t, dt_t, a_total_t


def _group_rms_ssm(y, w):
    parts = []
    for g in range(SSM_GROUPS):
        yg = y[:, g * D_GROUP:(g + 1) * D_GROUP]
        parts.append(yg * lax.rsqrt(jnp.mean(yg * yg, axis=-1, keepdims=True) + EPS))
    return jnp.concatenate(parts, axis=1) * w


def _ssd_heads(xs_of, bm_b, cm, a_cum, w, a_cum_t, dt_t, mask, state_t=None, gate=None, after_pair=None):
    t = a_cum.shape[0]
    assert t == LANES
    first_half = lax.broadcasted_iota(jnp.int32, (t, LANES), 1) < SSM_HEAD_DIM
    y_pairs, xw_pairs, ea_pairs, xs_pairs = [], [], [], []
    for g in range(SSM_GROUPS):
        cm_g = cm[:, g * SSM_STATE:(g + 1) * SSM_STATE]
        cb = _dot_nt(cm_g.astype(BF16), bm_b[:, g * SSM_STATE:(g + 1) * SSM_STATE])
        for q in range(HEADS_PER_GROUP // 2):
            pair = g * (HEADS_PER_GROUP // 2) + q
            cols = slice(pair * LANES, (pair + 1) * LANES)
            xs_p = xs_of(pair)
            rhs = xs_p.astype(BF16)
            if state_t is not None:
                rhs = jnp.concatenate([rhs, state_t[:, cols].astype(BF16)], axis=0)
            res, eas, ws = [], [], []
            for e in (2 * pair, 2 * pair + 1):
                a_col = jnp.broadcast_to(a_cum[:, e:e + 1], (t, LANES))
                decay = jnp.exp(jnp.where(mask, a_col - a_cum_t[e:e + 1, :], -jnp.inf))
                lhs = (cb * decay * dt_t[e:e + 1, :]).astype(BF16)
                ea = jnp.exp(a_col)
                if state_t is not None:
                    lhs = jnp.concatenate([lhs, (cm_g * ea).astype(BF16)], axis=1)
                res.append(_dot(lhs, rhs))
                eas.append(ea)
                ws.append(jnp.broadcast_to(w[:, e:e + 1], (t, LANES)))
            y_p = jnp.where(first_half, res[0], res[1])
            y_pairs.append(y_p if gate is None else gate(pair, y_p, xs_p))
            xw_pairs.append((xs_p * jnp.where(first_half, ws[0], ws[1])).astype(BF16))
            ea_pairs.append(jnp.where(first_half, eas[0], eas[1]))
            xs_pairs.append(xs_p)
            if after_pair is not None:
                after_pair(pair, y_pairs, xw_pairs, ea_pairs)
    return (jnp.concatenate(y_pairs, axis=1), jnp.concatenate(xw_pairs, axis=1),
            jnp.concatenate(ea_pairs, axis=1), jnp.concatenate(xs_pairs, axis=1))


def _ssm_tail(y, xs, z, d_full, ssm_norm):
    y = y + d_full * xs
    y = y * _silu(z)
    return _group_rms_ssm(y, ssm_norm)


def _w_main_spec(index_map, **kwargs):
    return pl.BlockSpec((1, D_MODEL, O_DT), index_map, **kwargs)


def _inproj_kernel(x_ref, npre_ref, w_main_ref, w_dt_ref, o_ref):
    hn = _rms(x_ref[...], npre_ref[...]).astype(BF16)
    o_ref[:, 0:O_DT] = _dot(hn, w_main_ref[0])
    o_ref[:, O_DT:D_IN_PAD] = _dot(hn, w_dt_ref[...])


def _inproj(x, npre, w_main_b, w_dt_b):
    m = x.shape[0]
    once = lambda shape: pl.BlockSpec(shape, lambda i: (0, 0), pipeline_mode=pl.Buffered(1))
    return pl.pallas_call(
        _inproj_kernel,
        grid=(1,),
        in_specs=[once((m, D_MODEL)), once((1, D_MODEL)),
                  _w_main_spec(lambda i: (0, 0, 0), pipeline_mode=pl.Buffered(1)), once((D_MODEL, LANES))],
        out_specs=once((m, D_IN_PAD)),
        out_shape=jax.ShapeDtypeStruct((m, D_IN_PAD), F32),
        compiler_params=pltpu.CompilerParams(dimension_semantics=("arbitrary",), vmem_limit_bytes=VMEM_LIMIT),
        name="inproj",
    )(x, npre, w_main_b, w_dt_b)


def _meta_kernel(proj_ref, xconv_w_ref, xconv_b_ref, dt_bias_ref, a_log_ref, tri_ref, expand_ref,
                 sc_tail_ref, xbc_tail_ref, state_t_ref):
    t = N_META
    proj = proj_ref[...]
    u = proj[:, O_GATE_C:O_GATE_C + D_SCONV] * proj[:, O_HV:O_HV + D_SCONV]
    sc_tail_ref[...] = u[t - SUBLANES:t]
    xbc = proj[:, O_XS:O_XS + D_XBC]
    xbc_tail_ref[...] = xbc[t - SUBLANES:t]
    zeros_prev = jnp.zeros((SUBLANES, D_XBC), F32)
    xbc_c = _silu(_causal_conv(xbc, zeros_prev, xconv_w_ref[...], None) + xconv_b_ref[...])
    xs = xbc_c[:, 0:D_SSM]
    dt, a_cum = _ssd_prepare(proj[:, O_DT:O_DT + LANES], dt_bias_ref[...], a_log_ref[...],
                             tri_ref[0:t, 0:t], t)
    w = jnp.exp(a_cum[t - 1:t, :] - a_cum) * dt
    xw_b = (xs * _dot3(w, expand_ref[...])).astype(BF16)
    for g in range(SSM_GROUPS):
        bm_b = xbc_c[:, D_SSM + g * SSM_STATE:D_SSM + (g + 1) * SSM_STATE].astype(BF16)
        state_t_ref[:, g * D_GROUP:(g + 1) * D_GROUP] = _dot_tn(bm_b, xw_b[:, g * D_GROUP:(g + 1) * D_GROUP])


def _meta_state(proj_meta, xconv_w, xconv_b, dt_bias, a_log, tri, expand):
    full = lambda shape: pl.BlockSpec(shape, lambda: tuple(0 for _ in shape))
    return pl.pallas_call(
        _meta_kernel,
        in_specs=[full(proj_meta.shape), full(xconv_w.shape), full(xconv_b.shape), full(dt_bias.shape),
                  full(a_log.shape), full(tri.shape), full(expand.shape)],
        out_specs=[full((SUBLANES, D_SCONV)), full((SUBLANES, D_XBC)), full((SSM_STATE, D_SSM))],
        out_shape=[jax.ShapeDtypeStruct((SUBLANES, D_SCONV), F32),
                   jax.ShapeDtypeStruct((SUBLANES, D_XBC), F32),
                   jax.ShapeDtypeStruct((SSM_STATE, D_SSM), F32)],
        name="meta_state",
    )(proj_meta, xconv_w, xconv_b, dt_bias, a_log, tri, expand)


def _pmix_kernel(x_ref, npre_ref, w_main_ref, w_dt_ref, sconv_w_ref, sconv_norm_ref,
                 xconv_w_ref, xconv_b_ref, dt_bias_ref, a_log_ref, d_full_ref, ssm_norm_ref,
                 tri_ref, w_out_ref, npost_ref, sc0_ref, xbc0_ref, state0_t_ref,
                 h1_ref, sc_out_ref, xbc_out_ref, state_ref,
                 proj_s0, proj_s1, dt_s0, dt_s1, h_s0, h_s1, hn_s, sc_tail, xbc_tail, state_t_ref, *,
                 steps_per_seq):
    rows = TILE_CHUNKS * CHUNK
    k = pl.program_id(0)
    step_in_seq = lax.rem(jnp.maximum(k - 1, 0), steps_per_seq)

    @pl.when(k == 0)
    def _():
        proj_s1[...] = jnp.zeros((rows, O_DT), F32)
        dt_s1[...] = jnp.zeros((SSM_HEADS, rows), F32)
        h_s1[...] = jnp.zeros((rows, D_MODEL), F32)

    @pl.when(step_in_seq == 0)
    def _():
        sc_tail[...] = sc0_ref[...]
        xbc_tail[...] = xbc0_ref[...]
        state_t_ref[...] = state0_t_ref[...]

    for parity in (0, 1):
        pl.when(lax.rem(k, 2) == parity)(functools.partial(
            _pmix_step, parity, x_ref, npre_ref, w_main_ref, w_dt_ref, sconv_w_ref, sconv_norm_ref,
            xconv_w_ref, xconv_b_ref, dt_bias_ref, a_log_ref, d_full_ref, ssm_norm_ref, tri_ref, w_out_ref,
            npost_ref, h1_ref, (proj_s0, proj_s1), (dt_s0, dt_s1), (h_s0, h_s1), hn_s, sc_tail, xbc_tail,
            state_t_ref))

    @pl.when(step_in_seq == steps_per_seq - 1)
    def _():
        sc_out_ref[0] = sc_tail[SUBLANES - (SCONV_WIDTH - 1):SUBLANES, :]
        xbc_out_ref[0] = xbc_tail[SUBLANES - (SSM_CONV_WIDTH - 1):SUBLANES, :]
        for p in range(HEAD_PAIRS):
            state_ref[0, p * LANES:(p + 1) * LANES, :] = state_t_ref[:, p * LANES:(p + 1) * LANES].T


def _pmix_step(slot_a, x_ref, npre_ref, w_main_ref, w_dt_ref, sconv_w_ref, sconv_norm_ref,
               xconv_w_ref, xconv_b_ref, dt_bias_ref, a_log_ref, d_full_ref, ssm_norm_ref, tri_ref, w_out_ref,
               npost_ref, h1_ref, proj_s, dt_s, h_s, hn_s, sc_tail, xbc_tail, state_t_ref):
    t = CHUNK
    slot_b = 1 - slot_a

    h_a = x_ref[0]
    h_s[slot_a][...] = h_a
    hn_s[...] = _rms(h_a, npre_ref[...]).astype(BF16)

    def stage_a(point):
        for piece, planned in enumerate(STAGE_A_PLAN):
            if planned == point:
                if piece + 1 < len(STAGE_A_COLS):
                    off, end = STAGE_A_COLS[piece], STAGE_A_COLS[piece + 1]
                    proj_s[slot_a][:, off:end] = _dot(hn_s[...], w_main_ref[0, :, off:end])
                else:
                    dt_s[slot_a][...] = _dot_nt(w_dt_ref[...], hn_s[...])

    for sub in range(TILE_CHUNKS):
        r = slice(sub * t, (sub + 1) * t)

        def proj(off, width):
            return proj_s[slot_b][r, off:off + width]

        ya, u = _sconv_mixer(proj(O_GATE_B, D_SCONV), proj(O_GATE_C, D_SCONV), proj(O_HV, D_SCONV),
                             sc_tail[...], sconv_w_ref[...], sconv_norm_ref[...], None)
        sc_tail[...] = u[t - SUBLANES:t]
        stage_a((sub, "sconv"))
        mix_ya = _dot(ya.astype(BF16), w_out_ref[0:D_SCONV, :])

        def xconv(col, width):
            cols = slice(col, col + width)
            raw = proj(O_XS + col, width)
            out = _silu(_causal_conv(raw, xbc_tail[:, cols], xconv_w_ref[:, cols], None) + xconv_b_ref[:, cols])
            xbc_tail[:, cols] = raw[t - SUBLANES:t]
            return out

        bc = xconv(D_SSM, D_BC)
        bm_b = bc[:, 0:SSM_GROUPS * SSM_STATE].astype(BF16)
        cm = bc[:, SSM_GROUPS * SSM_STATE:D_BC]
        a_cum, w, a_cum_t, dt_t, _ = _ssd_prepare_rows(dt_s[slot_b][:, r], dt_bias_ref[...], a_log_ref[...],
                                                       tri_ref[...])
        causal = lax.broadcasted_iota(jnp.int32, (t, t), 0) >= lax.broadcasted_iota(jnp.int32, (t, t), 1)
        stage_a((sub, "prepare"))

        def gate(p, y_p, xs_p):
            cols = slice(p * LANES, (p + 1) * LANES)
            return (y_p + d_full_ref[:, cols] * xs_p) * _silu(proj(O_Z + p * LANES, LANES))

        y, xw_b, ea_full, _ = _ssd_heads(lambda p: xconv(p * LANES, LANES), bm_b, cm, a_cum, w, a_cum_t, dt_t,
                                         causal, state_t_ref, gate, lambda p, *partial: stage_a((sub, p)))
        chunk_decay = ea_full[t - 1:t, :]
        for g in range(SSM_GROUPS):
            cols = slice(g * D_GROUP, (g + 1) * D_GROUP)
            state_t_ref[:, cols] = chunk_decay[:, cols] * state_t_ref[:, cols] + _dot_tn(
                bm_b[:, g * SSM_STATE:(g + 1) * SSM_STATE], xw_b[:, cols])
        yb = _group_rms_ssm(y, ssm_norm_ref[...])
        mix = mix_ya + _dot(yb.astype(BF16), w_out_ref[D_SCONV:D_MIX, :])
        stage_a((sub, "end"))
        h1_ref[0, r, :] = h_s[slot_b][r, :] + _rms(mix, npost_ref[...])


def _pmix(x_prompt, consts, sc0, xbc0, state0_t):
    batch, seq, _ = x_prompt.shape
    rows = TILE_CHUNKS * CHUNK
    sps = seq // rows
    n_tiles = batch * sps
    const_spec = lambda a: (pl.BlockSpec(a.shape, lambda k: (0, 0)) if a.ndim == 2
                            else _w_main_spec(lambda k: (0, 0, 0)))
    ins = list(consts) + [sc0, xbc0, state0_t]

    def tile_a(k):
        tile = jnp.minimum(k, n_tiles - 1)
        return (tile // sps, tile % sps, 0)

    def tile_b(k):
        tile = jnp.maximum(k - 1, 0)
        return (tile // sps, tile % sps, 0)

    def seq_b(k):
        return (jnp.maximum(k - 1, 0) // sps, 0, 0)

    return pl.pallas_call(
        functools.partial(_pmix_kernel, steps_per_seq=sps),
        grid=(n_tiles + 1,),
        in_specs=[pl.BlockSpec((1, rows, D_MODEL), tile_a)] + [const_spec(a) for a in ins],
        out_specs=[
            pl.BlockSpec((1, rows, D_MODEL), tile_b),
            pl.BlockSpec((1, SCONV_WIDTH - 1, D_SCONV), seq_b),
            pl.BlockSpec((1, SSM_CONV_WIDTH - 1, D_XBC), seq_b),
            pl.BlockSpec((1, D_SSM, SSM_STATE), seq_b),
        ],
        out_shape=[
            jax.ShapeDtypeStruct((batch, seq, D_MODEL), F32),
            jax.ShapeDtypeStruct((batch, SCONV_WIDTH - 1, D_SCONV), F32),
            jax.ShapeDtypeStruct((batch, SSM_CONV_WIDTH - 1, D_XBC), F32),
            jax.ShapeDtypeStruct((batch, D_SSM, SSM_STATE), F32),
        ],
        scratch_shapes=[
            pltpu.VMEM((rows, O_DT), F32),
            pltpu.VMEM((rows, O_DT), F32),
            pltpu.VMEM((SSM_HEADS, rows), F32),
            pltpu.VMEM((SSM_HEADS, rows), F32),
            pltpu.VMEM((rows, D_MODEL), F32),
            pltpu.VMEM((rows, D_MODEL), F32),
            pltpu.VMEM((rows, D_MODEL), BF16),
            pltpu.VMEM((SUBLANES, D_SCONV), F32),
            pltpu.VMEM((SUBLANES, D_XBC), F32),
            pltpu.VMEM((SSM_STATE, D_SSM), F32),
        ],
        compiler_params=pltpu.CompilerParams(dimension_semantics=("arbitrary",), vmem_limit_bytes=VMEM_LIMIT),
        name="prompt_mixer",
    )(x_prompt, *ins)


def _smix_kernel(x_ref, sc_prev_ref, xbc_prev_ref, state_in_ref, npre_ref, w_main_ref, w_dt_ref,
                 sconv_w_ref, sconv_norm_ref, xconv_w_ref, xconv_b_ref,
                 dt_bias_ref, a_log_ref, d_full_ref, ssm_norm_ref, w_out_ref, npost_ref,
                 h1_ref, u_ref, xbc_ref, state_ref, *, dec_seq):
    t = SEQ_TILE * dec_seq
    h = x_ref[...]
    hn = _rms(h, npre_ref[...]).astype(BF16)

    def proj(off, width):
        return _dot(hn, w_main_ref[0, :, off:off + width])

    ya, u = _sconv_mixer(proj(O_GATE_B, D_SCONV), proj(O_GATE_C, D_SCONV), proj(O_HV, D_SCONV),
                         sc_prev_ref[...], sconv_w_ref[...], sconv_norm_ref[...], dec_seq)
    for s in range(SEQ_TILE):
        u_ref[s] = u[(s + 1) * dec_seq - (SCONV_WIDTH - 1):(s + 1) * dec_seq, :]
    xbc = proj(O_XS, D_XBC)
    for s in range(SEQ_TILE):
        xbc_ref[s] = xbc[(s + 1) * dec_seq - (SSM_CONV_WIDTH - 1):(s + 1) * dec_seq, :]
    xbc_c = _silu(_causal_conv(xbc, xbc_prev_ref[...], xconv_w_ref[...], dec_seq) + xconv_b_ref[...])
    xs = xbc_c[:, 0:D_SSM]
    bm = xbc_c[:, D_SSM:D_SSM + SSM_GROUPS * SSM_STATE]
    bm_b = bm.astype(BF16)
    cm = xbc_c[:, D_SSM + SSM_GROUPS * SSM_STATE:D_XBC]

    row = lax.broadcasted_iota(jnp.int32, (t, t), 0)
    col = lax.broadcasted_iota(jnp.int32, (t, t), 1)
    same_seq = row // dec_seq == col // dec_seq
    mask = jnp.logical_and(row >= col, same_seq)
    tri = jnp.where(mask, 1.0, 0.0).astype(BF16)
    seq_ones = jnp.where(same_seq, 1.0, 0.0).astype(BF16)
    a_cum, w, a_cum_t, dt_t, a_total_t = _ssd_prepare_rows(_dot_nt(w_dt_ref[...], hn), dt_bias_ref[...],
                                                           a_log_ref[...], tri, seq_ones)
    y, xw_b, ea_full, _ = _ssd_heads(lambda p: xs[:, p * LANES:(p + 1) * LANES], bm_b, cm, a_cum, w, a_cum_t,
                                     dt_t, mask)

    row_id = lax.broadcasted_iota(jnp.int32, (t, SSM_STATE), 0) // dec_seq
    seq_decay_t = jnp.exp(a_total_t)
    y_off_rows = []
    for s in range(SEQ_TILE):
        r0 = s * dec_seq
        q_t = jnp.broadcast_to(seq_decay_t[:, r0:r0 + 1], (SSM_HEADS, SSM_STATE))
        y_off_g = []
        for g in range(SSM_GROUPS):
            rows = slice(g * D_GROUP, (g + 1) * D_GROUP)
            s_prev = state_in_ref[s, rows, :]
            cm_s = cm[r0:r0 + dec_seq, g * SSM_STATE:(g + 1) * SSM_STATE].astype(BF16)
            y_off_g.append(_dot_nt(cm_s, s_prev.astype(BF16)))
            bm_g = jnp.where(row_id == s, bm[:, g * SSM_STATE:(g + 1) * SSM_STATE], 0.0).astype(BF16)
            decay = jnp.concatenate(
                [jnp.broadcast_to(q_t[e:e + 1, :], (SSM_HEAD_DIM, SSM_STATE))
                 for e in range(g * HEADS_PER_GROUP, (g + 1) * HEADS_PER_GROUP)], axis=0)
            state_ref[s, rows, :] = decay * s_prev + _dot_tn(xw_b[:, rows], bm_g)
        y_off_rows.append(jnp.concatenate(y_off_g, axis=1))
    y = y + jnp.concatenate(y_off_rows, axis=0) * ea_full
    yb = _ssm_tail(y, xs, proj(O_Z, D_SSM), d_full_ref[...], ssm_norm_ref[...])
    mix = _dot(ya.astype(BF16), w_out_ref[0:D_SCONV, :]) + _dot(yb.astype(BF16), w_out_ref[D_SCONV:D_MIX, :])
    h1_ref[...] = h + _rms(mix, npost_ref[...])


def _smix(x_rows, sc_prev, xbc_prev, state, consts, dec_seq):
    n_seq = state.shape[0]
    t = SEQ_TILE * dec_seq
    const_spec = lambda a: (pl.BlockSpec(a.shape, lambda i: (0, 0), pipeline_mode=pl.Buffered(1)) if a.ndim == 2
                            else _w_main_spec(lambda i: (0, 0, 0), pipeline_mode=pl.Buffered(1)))
    row_spec = lambda width: pl.BlockSpec((t, width), lambda i: (i, 0))
    state_spec = pl.BlockSpec((SEQ_TILE, D_SSM, SSM_STATE), lambda i: (i, 0, 0))
    return pl.pallas_call(
        functools.partial(_smix_kernel, dec_seq=dec_seq),
        grid=(n_seq // SEQ_TILE,),
        in_specs=[row_spec(D_MODEL), row_spec(D_SCONV), row_spec(D_XBC), state_spec]
        + [const_spec(a) for a in consts],
        out_specs=[row_spec(D_MODEL),
                   pl.BlockSpec((SEQ_TILE, SCONV_WIDTH - 1, D_SCONV), lambda i: (i, 0, 0)),
                   pl.BlockSpec((SEQ_TILE, SSM_CONV_WIDTH - 1, D_XBC), lambda i: (i, 0, 0)), state_spec],
        out_shape=[
            jax.ShapeDtypeStruct((n_seq * dec_seq, D_MODEL), F32),
            jax.ShapeDtypeStruct((n_seq, SCONV_WIDTH - 1, D_SCONV), F32),
            jax.ShapeDtypeStruct((n_seq, SSM_CONV_WIDTH - 1, D_XBC), F32),
            jax.ShapeDtypeStruct(state.shape, F32),
        ],
        compiler_params=pltpu.CompilerParams(dimension_semantics=("arbitrary",), vmem_limit_bytes=VMEM_LIMIT),
        name="sample_mixer",
    )(x_rows, sc_prev, xbc_prev, state, *consts)


def _ffn_kernel(hp_ref, hs_ref, npre_ref, w_gate_ref, w_up_ref, w_down_ref, npost_ref, op_ref, os_ref, *,
                prompt_steps):
    def body(h_ref, o_ref):
        for r0 in range(0, FFN_ROWS, FFN_SUB_ROWS):
            r = slice(r0, r0 + FFN_SUB_ROWS)
            h1 = h_ref[r, :]
            fn = _rms(h1, npre_ref[...]).astype(BF16)
            act = (_silu(_dot(fn, w_gate_ref[...])) * _dot(fn, w_up_ref[...])).astype(BF16)
            o_ref[r, :] = h1 + _rms(_dot(act, w_down_ref[...]), npost_ref[...])

    step = pl.program_id(0)
    pl.when(step < prompt_steps)(functools.partial(body, hp_ref, op_ref))
    pl.when(step >= prompt_steps)(functools.partial(body, hs_ref, os_ref))


def _ffn(h_prompt, h_sample, ffn_consts):
    mp, ms = h_prompt.shape[0], h_sample.shape[0]
    rows = FFN_ROWS
    p_steps, s_steps = mp // rows, ms // rows
    p_spec = pl.BlockSpec((rows, D_MODEL), lambda i: (jnp.minimum(i, p_steps - 1), 0))
    s_spec = pl.BlockSpec((rows, D_MODEL), lambda i: (jnp.maximum(i - p_steps, 0), 0))
    const_spec = lambda a: pl.BlockSpec(a.shape, lambda i: (0, 0), pipeline_mode=pl.Buffered(1))
    return pl.pallas_call(
        functools.partial(_ffn_kernel, prompt_steps=p_steps),
        grid=(p_steps + s_steps,),
        in_specs=[p_spec, s_spec] + [const_spec(a) for a in ffn_consts],
        out_specs=[p_spec, s_spec],
        out_shape=[jax.ShapeDtypeStruct((mp, D_MODEL), F32), jax.ShapeDtypeStruct((ms, D_MODEL), F32)],
        compiler_params=pltpu.CompilerParams(dimension_semantics=("arbitrary",), vmem_limit_bytes=VMEM_LIMIT),
        name="ffn",
    )(h_prompt, h_sample, *ffn_consts)


def _row(v, width=None):
    v = v.reshape(1, -1).astype(F32)
    if width is not None and v.shape[1] < width:
        v = jnp.pad(v, ((0, 0), (0, width - v.shape[1])))
    return v


def _layer(hp, hs, sc_state, xbc_state, ssm_state, meta_tokens,
           norm_mix_pre, norm_mix_post, w_in_all, sconv_w, sconv_norm, ssm_conv_w, ssm_conv_b,
           dt_bias, a_log, d_skip, ssm_norm, w_out, norm_ffn_pre, norm_ffn_post, w_gate, w_up, w_down):
    batch, seq, _ = hp.shape
    w_in = w_in_all[0]
    n_seq, dec_seq, _ = hs.shape

    w_main_b = w_in_all.astype(BF16)
    w_dt_b = jnp.pad(w_in[:, O_DT:D_IN], ((0, 0), (0, D_IN_PAD - D_IN))).astype(BF16)
    w_dt_t = w_in[:, O_DT:D_IN].T.astype(BF16)
    w_out_b = w_out.astype(BF16)
    w_gate_b, w_up_b, w_down_b = w_gate.astype(BF16), w_up.astype(BF16), w_down.astype(BF16)
    npre, npost = _row(norm_mix_pre), _row(norm_mix_post)
    fpre, fpost = _row(norm_ffn_pre), _row(norm_ffn_post)
    sconv_w = sconv_w.astype(F32)
    xconv_w = ssm_conv_w.astype(F32)
    xconv_b = _row(ssm_conv_b)
    dt_bias_p, a_log_p = _row(dt_bias, LANES), _row(a_log, LANES)
    dt_bias_t = jnp.broadcast_to(dt_bias.astype(F32)[:, None], (SSM_HEADS, LANES))
    a_log_t = jnp.broadcast_to(a_log.astype(F32)[:, None], (SSM_HEADS, LANES))
    d_full = jnp.repeat(d_skip.astype(F32), SSM_HEAD_DIM).reshape(1, D_SSM)
    sconv_norm_r, ssm_norm_r = _row(sconv_norm), _row(ssm_norm)

    r = jnp.arange(CHUNK)
    tri = (r[:, None] >= r[None, :]).astype(BF16)
    expand = (jnp.arange(LANES)[:, None] == (jnp.arange(D_SSM)[None, :] // SSM_HEAD_DIM)).astype(BF16)

    mixer_consts = (sconv_w, sconv_norm_r, xconv_w, xconv_b, dt_bias_t, a_log_t, d_full, ssm_norm_r)
    ffn_consts = (fpre, w_gate_b, w_up_b, w_down_b, fpost)

    proj_meta = _inproj(meta_tokens.astype(F32), npre, w_main_b, w_dt_b)
    sc0, xbc0, state0_t = _meta_state(proj_meta, xconv_w, xconv_b, dt_bias_p, a_log_p, tri, expand)

    pm_consts = (npre, w_main_b, w_dt_t) + mixer_consts + (tri, w_out_b, npost)
    h1, new_sc_p, new_xbc_p, p_state = _pmix(hp, pm_consts, sc0, xbc0, state0_t)
    new_ssm_p = p_state.reshape(batch, SSM_GROUPS, HEADS_PER_GROUP, SSM_HEAD_DIM, SSM_STATE)

    hs_rows = hs.reshape(n_seq * dec_seq, D_MODEL)
    sc_prev = jnp.pad(sc_state, ((0, 0), (dec_seq - (SCONV_WIDTH - 1), 0), (0, 0))).reshape(n_seq * dec_seq, D_SCONV)
    xbc_prev = jnp.pad(xbc_state, ((0, 0), (dec_seq - (SSM_CONV_WIDTH - 1), 0), (0, 0))).reshape(n_seq * dec_seq, D_XBC)
    sm_consts = (npre, w_main_b, w_dt_t) + mixer_consts + (w_out_b, npost)
    h1_s, new_sc_s, new_xbc_s, s_state = _smix(hs_rows, sc_prev, xbc_prev, ssm_state.reshape(n_seq, D_SSM, SSM_STATE),
                                      sm_consts, dec_seq)

    y_prompt, y_sample = _ffn(h1.reshape(batch * seq, D_MODEL), h1_s, ffn_consts)
    y_prompt = y_prompt.reshape(batch, seq, D_MODEL)
    y_sample = y_sample.reshape(n_seq, dec_seq, D_MODEL)
    new_ssm_s = s_state.reshape(n_seq, SSM_GROUPS, HEADS_PER_GROUP, SSM_HEAD_DIM, SSM_STATE)
    return y_prompt, y_sample, new_sc_p, new_xbc_p, new_ssm_p, new_sc_s, new_xbc_s, new_ssm_s


def kernel(x_prompt, x_sample, state_sconv, state_ssm_conv, state_ssm, meta_tokens, norm_mix_pre, norm_mix_post, w_in, sconv_w, sconv_norm, ssm_conv_w, ssm_conv_b, dt_bias, A_log, D_skip, ssm_norm, w_out, norm_ffn_pre, norm_ffn_post, w_gate, w_up, w_down):
    depth = w_in.shape[0]
    assert depth == 1 and x_sample.shape[1] == SUBLANES
    outs = _layer(x_prompt, x_sample, state_sconv[0], state_ssm_conv[0], state_ssm[0], meta_tokens,
                  norm_mix_pre[0], norm_mix_post[0], w_in, sconv_w[0], sconv_norm[0], ssm_conv_w[0],
                  ssm_conv_b[0], dt_bias[0], A_log[0], D_skip[0], ssm_norm[0], w_out[0],
                  norm_ffn_pre[0], norm_ffn_post[0], w_gate[0], w_up[0], w_down[0])
    y_prompt, y_sample = outs[0], outs[1]
    return (y_prompt, y_sample) + tuple(o[None] for o in outs[2:])
```

```python
import functools

import jax
import jax.numpy as jnp
from jax import lax
from jax.experimental import pallas as pl
from jax.experimental.pallas import tpu as pltpu

D_MODEL = 1024
N_META = 16
D_SCONV = 512
SCONV_GROUPS = 8
SCONV_WIDTH = 3
SSM_HEAD_DIM = 64
SSM_HEADS = 16
D_SSM = 1024
SSM_GROUPS = 2
HEADS_PER_GROUP = 8
D_GROUP = HEADS_PER_GROUP * SSM_HEAD_DIM
SSM_STATE = 128
SSM_CONV_WIDTH = 4
D_BC = 2 * SSM_GROUPS * SSM_STATE
D_XBC = D_SSM + D_BC
D_MIX = D_SCONV + D_SSM
D_IN = 3 * D_SCONV + D_SSM + D_XBC + SSM_HEADS
D_FF = 2816
EPS = 1e-6

LANES = 128
SUBLANES = 8
D_IN_PAD = D_IN + (LANES - SSM_HEADS)
HEAD_PAIRS = D_SSM // LANES

O_GATE_B = 0
O_GATE_C = D_SCONV
O_HV = 2 * D_SCONV
O_Z = 3 * D_SCONV
O_XS = O_Z + D_SSM
O_BC = O_XS + D_SSM
O_DT = O_BC + D_BC

CHUNK = 128
TILE_CHUNKS = 2
STAGE_A_COLS = tuple(range(0, O_DT + 1, D_SCONV))
STAGE_A_PLAN = ((0, "sconv"), (0, "prepare"), (0, 1), (0, 3), (0, 5), (1, "prepare"), (1, 1), (1, 3), (1, 5))
SEQ_TILE = 16
FFN_ROWS = 512
VMEM_LIMIT = 60000 * 1024

BF16 = jnp.bfloat16
F32 = jnp.float32


def _dot(a, b):
    return jnp.dot(a, b, preferred_element_type=F32)


def _dot_nt(a, b):
    return lax.dot_general(a, b, (((1,), (1,)), ((), ())), preferred_element_type=F32)


def _dot_tn(a, b):
    return lax.dot_general(a, b, (((0,), (0,)), ((), ())), preferred_element_type=F32)


def _split3(v):
    hi = v.astype(BF16)
    r1 = v - hi.astype(F32)
    mid = r1.astype(BF16)
    lo = (r1 - mid.astype(F32)).astype(BF16)
    return hi, mid, lo


def _dot3(v, m):
    hi, mid, lo = _split3(v)
    return _dot(hi, m) + _dot(mid, m) + _dot(lo, m)


def _dot3_left(m, v):
    hi, mid, lo = _split3(v)
    return _dot(m, hi) + _dot(m, mid) + _dot(m, lo)


def _rms(x, w):
    return x * lax.rsqrt(jnp.mean(x * x, axis=-1, keepdims=True) + EPS) * w


def _silu(x):
    return x * (1.0 / (1.0 + jnp.exp(-x)))


def _softplus(x):
    return jnp.maximum(x, 0.0) + jnp.log1p(jnp.exp(-jnp.abs(x)))


def _shifted(u, prev, j, rows_per_seq):
    t = u.shape[0]
    s = pltpu.roll(u, j, 0)
    if rows_per_seq is None:
        row = lax.broadcasted_iota(jnp.int32, (SUBLANES, u.shape[1]), 0)
        first = jnp.where(row < j, pltpu.roll(prev, j, 0), s[0:SUBLANES])
        if t == SUBLANES:
            return first
        return jnp.concatenate([first, s[SUBLANES:]], axis=0)
    row = lax.broadcasted_iota(jnp.int32, u.shape, 0) % SUBLANES
    return jnp.where(row < j, pltpu.roll(prev, t - SUBLANES + j, 0), s)


def _causal_conv(u, prev, w, rows_per_seq):
    k = w.shape[0]
    out = _shifted(u, prev, k - 1, rows_per_seq) * w[0:1]
    for i in range(1, k - 1):
        out = out + _shifted(u, prev, k - 1 - i, rows_per_seq) * w[i:i + 1]
    return out + u * w[k - 1:k]


def _sconv_mixer(gate_b, gate_c, hv, prev, sconv_w, sconv_norm, rows_per_seq):
    u = gate_c * hv
    conv_u = _causal_conv(u, prev, sconv_w, rows_per_seq)
    return _sconv_norm(gate_b, conv_u, sconv_norm), u


def _sconv_norm(gate_b, conv_u, sconv_norm):
    v = gate_b * conv_u
    first_half = lax.broadcasted_iota(jnp.int32, (v.shape[0], LANES), 1) < D_SCONV // SCONV_GROUPS
    parts = []
    for c in range(0, D_SCONV, LANES):
        vc = v[:, c:c + LANES]
        sq = vc * vc
        s_first = jnp.sum(jnp.where(first_half, sq, 0.0), axis=-1, keepdims=True)
        s_second = jnp.sum(jnp.where(first_half, 0.0, sq), axis=-1, keepdims=True)
        mean_sq = jnp.where(first_half, s_first, s_second) * (SCONV_GROUPS / D_SCONV)
        parts.append(vc * lax.rsqrt(mean_sq + EPS))
    return jnp.concatenate(parts, axis=1) * sconv_norm


def _ssd_prepare(dt_raw, dt_bias, a_log, tri, t):
    lane = lax.broadcasted_iota(jnp.int32, (t, LANES), 1)
    valid = lane < SSM_HEADS
    dt = jnp.where(valid, _softplus(dt_raw + dt_bias), 0.0)
    a = dt * (-jnp.exp(a_log))
    a_cum = _dot3_left(tri, a)
    return dt, a_cum


def _ssd_prepare_rows(dt_raw_t, dt_bias_t, a_log_t, tri, seq_ones=None):
    t = dt_raw_t.shape[1]
    dt_t = _softplus(dt_raw_t + dt_bias_t)
    a_t = dt_t * (-jnp.exp(a_log_t))
    hi, mid, lo = _split3(a_t)
    a_cum_t = _dot_nt(hi, tri) + _dot_nt(mid, tri) + _dot_nt(lo, tri)
    if seq_ones is None:
        a_total_t = jnp.broadcast_to(a_cum_t[:, t - 1:t], (SSM_HEADS, t))
    else:
        a_total_t = _dot_nt(hi, seq_ones) + _dot_nt(mid, seq_ones) + _dot_nt(lo, seq_ones)
    w_t = jnp.exp(a_total_t - a_cum_t) * dt_t
    pad = jnp.zeros((LANES - SSM_HEADS, t), F32)
    a_cum = jnp.concatenate([a_cum_t, pad], axis=0).T
    w = jnp.concatenate([w_t, pad], axis=0).T
    return a_cum, w, a_cum_t, dt_t, a_total_t


def _group_rms_ssm(y, w):
    parts = []
    for g in range(SSM_GROUPS):
        yg = y[:, g * D_GROUP:(g + 1) * D_GROUP]
        parts.append(yg * lax.rsqrt(jnp.mean(yg * yg, axis=-1, keepdims=True) + EPS))
    return jnp.concatenate(parts, axis=1) * w


def _ssd_heads(xs_of, bm_b, cm, a_cum, w, a_cum_t, dt_t, mask, state_t=None, gate=None, after_pair=None):
    t = a_cum.shape[0]
    assert t == LANES
    first_half = lax.broadcasted_iota(jnp.int32, (t, LANES), 1) < SSM_HEAD_DIM
    y_pairs, xw_pairs, ea_pairs, xs_pairs = [], [], [], []
    for g in range(SSM_GROUPS):
        cm_g = cm[:, g * SSM_STATE:(g + 1) * SSM_STATE]
        cb = _dot_nt(cm_g.astype(BF16), bm_b[:, g * SSM_STATE:(g + 1) * SSM_STATE])
        for q in range(HEADS_PER_GROUP // 2):
            pair = g * (HEADS_PER_GROUP // 2) + q
            cols = slice(pair * LANES, (pair + 1) * LANES)
            xs_p = xs_of(pair)
            rhs = xs_p.astype(BF16)
            if state_t is not None:
                rhs = jnp.concatenate([rhs, state_t[:, cols].astype(BF16)], axis=0)
            res, eas, ws = [], [], []
            for e in (2 * pair, 2 * pair + 1):
                a_col = jnp.broadcast_to(a_cum[:, e:e + 1], (t, LANES))
                decay = jnp.exp(jnp.where(mask, a_col - a_cum_t[e:e + 1, :], -jnp.inf))
                lhs = (cb * decay * dt_t[e:e + 1, :]).astype(BF16)
                ea = jnp.exp(a_col)
                if state_t is not None:
                    lhs = jnp.concatenate([lhs, (cm_g * ea).astype(BF16)], axis=1)
                res.append(_dot(lhs, rhs))
                eas.append(ea)
                ws.append(jnp.broadcast_to(w[:, e:e + 1], (t, LANES)))
            y_p = jnp.where(first_half, res[0], res[1])
            y_pairs.append(y_p if gate is None else gate(pair, y_p, xs_p))
            xw_pairs.append((xs_p * jnp.where(first_half, ws[0], ws[1])).astype(BF16))
            ea_pairs.append(jnp.where(first_half, eas[0], eas[1]))
            xs_pairs.append(xs_p)
            if after_pair is not None:
                after_pair(pair, y_pairs, xw_pairs, ea_pairs)
    return (jnp.concatenate(y_pairs, axis=1), jnp.concatenate(xw_pairs, axis=1),
            jnp.concatenate(ea_pairs, axis=1), jnp.concatenate(xs_pairs, axis=1))


def _ssm_tail(y, xs, z, d_full, ssm_norm):
    y = y + d_full * xs
    y = y * _silu(z)
    return _group_rms_ssm(y, ssm_norm)


def _w_main_spec(index_map, **kwargs):
    return pl.BlockSpec((1, D_MODEL, O_DT), index_map, **kwargs)


def _inproj_kernel(x_ref, npre_ref, w_main_ref, w_dt_ref, o_ref):
    hn = _rms(x_ref[...], npre_ref[...]).astype(BF16)
    o_ref[:, 0:O_DT] = _dot(hn, w_main_ref[0])
    o_ref[:, O_DT:D_IN_PAD] = _dot(hn, w_dt_ref[...])


def _inproj(x, npre, w_main_b, w_dt_b):
    m = x.shape[0]
    once = lambda shape: pl.BlockSpec(shape, lambda i: (0, 0), pipeline_mode=pl.Buffered(1))
    return pl.pallas_call(
        _inproj_kernel,
        grid=(1,),
        in_specs=[once((m, D_MODEL)), once((1, D_MODEL)),
                  _w_main_spec(lambda i: (0, 0, 0), pipeline_mode=pl.Buffered(1)), once((D_MODEL, LANES))],
        out_specs=once((m, D_IN_PAD)),
        out_shape=jax.ShapeDtypeStruct((m, D_IN_PAD), F32),
        compiler_params=pltpu.CompilerParams(dimension_semantics=("arbitrary",), vmem_limit_bytes=VMEM_LIMIT),
        name="inproj",
    )(x, npre, w_main_b, w_dt_b)


def _meta_kernel(proj_ref, xconv_w_ref, xconv_b_ref, dt_bias_ref, a_log_ref, tri_ref, expand_ref,
                 sc_tail_ref, xbc_tail_ref, state_t_ref):
    t = N_META
    proj = proj_ref[...]
    u = proj[:, O_GATE_C:O_GATE_C + D_SCONV] * proj[:, O_HV:O_HV + D_SCONV]
    sc_tail_ref[...] = u[t - SUBLANES:t]
    xbc = proj[:, O_XS:O_XS + D_XBC]
    xbc_tail_ref[...] = xbc[t - SUBLANES:t]
    zeros_prev = jnp.zeros((SUBLANES, D_XBC), F32)
    xbc_c = _silu(_causal_conv(xbc, zeros_prev, xconv_w_ref[...], None) + xconv_b_ref[...])
    xs = xbc_c[:, 0:D_SSM]
    dt, a_cum = _ssd_prepare(proj[:, O_DT:O_DT + LANES], dt_bias_ref[...], a_log_ref[...],
                             tri_ref[0:t, 0:t], t)
    w = jnp.exp(a_cum[t - 1:t, :] - a_cum) * dt
    xw_b = (xs * _dot3(w, expand_ref[...])).astype(BF16)
    for g in range(SSM_GROUPS):
        bm_b = xbc_c[:, D_SSM + g * SSM_STATE:D_SSM + (g + 1) * SSM_STATE].astype(BF16)
        state_t_ref[:, g * D_GROUP:(g + 1) * D_GROUP] = _dot_tn(bm_b, xw_b[:, g * D_GROUP:(g + 1) * D_GROUP])


def _meta_state(proj_meta, xconv_w, xconv_b, dt_bias, a_log, tri, expand):
    full = lambda shape: pl.BlockSpec(shape, lambda: tuple(0 for _ in shape))
    return pl.pallas_call(
        _meta_kernel,
        in_specs=[full(proj_meta.shape), full(xconv_w.shape), full(xconv_b.shape), full(dt_bias.shape),
                  full(a_log.shape), full(tri.shape), full(expand.shape)],
        out_specs=[full((SUBLANES, D_SCONV)), full((SUBLANES, D_XBC)), full((SSM_STATE, D_SSM))],
        out_shape=[jax.ShapeDtypeStruct((SUBLANES, D_SCONV), F32),
                   jax.ShapeDtypeStruct((SUBLANES, D_XBC), F32),
                   jax.ShapeDtypeStruct((SSM_STATE, D_SSM), F32)],
        name="meta_state",
    )(proj_meta, xconv_w, xconv_b, dt_bias, a_log, tri, expand)


def _pmix_kernel(x_ref, npre_ref, w_main_ref, w_dt_ref, sconv_w_ref, sconv_norm_ref,
                 xconv_w_ref, xconv_b_ref, dt_bias_ref, a_log_ref, d_full_ref, ssm_norm_ref,
                 tri_ref, w_out_ref, npost_ref, sc0_ref, xbc0_ref, state0_t_ref,
                 h1_ref, sc_out_ref, xbc_out_ref, state_ref,
                 proj_s0, proj_s1, dt_s0, dt_s1, h_s0, h_s1, hn_s, sc_tail, xbc_tail, state_t_ref, *,
                 steps_per_seq):
    rows = TILE_CHUNKS * CHUNK
    k = pl.program_id(0)
    step_in_seq = lax.rem(jnp.maximum(k - 1, 0), steps_per_seq)

    @pl.when(k == 0)
    def _():
        proj_s1[...] = jnp.zeros((rows, O_DT), F32)
        dt_s1[...] = jnp.zeros((SSM_HEADS, rows), F32)
        h_s1[...] = jnp.zeros((rows, D_MODEL), F32)

    @pl.when(step_in_seq == 0)
    def _():
        sc_tail[...] = sc0_ref[...]
        xbc_tail[...] = xbc0_ref[...]
        state_t_ref[...] = state0_t_ref[...]

    for parity in (0, 1):
        pl.when(lax.rem(k, 2) == parity)(functools.partial(
            _pmix_step, parity, x_ref, npre_ref, w_main_ref, w_dt_ref, sconv_w_ref, sconv_norm_ref,
            xconv_w_ref, xconv_b_ref, dt_bias_ref, a_log_ref, d_full_ref, ssm_norm_ref, tri_ref, w_out_ref,
            npost_ref, h1_ref, (proj_s0, proj_s1), (dt_s0, dt_s1), (h_s0, h_s1), hn_s, sc_tail, xbc_tail,
            state_t_ref))

    @pl.when(step_in_seq == steps_per_seq - 1)
    def _():
        sc_out_ref[0] = sc_tail[SUBLANES - (SCONV_WIDTH - 1):SUBLANES, :]
        xbc_out_ref[0] = xbc_tail[SUBLANES - (SSM_CONV_WIDTH - 1):SUBLANES, :]
        for p in range(HEAD_PAIRS):
            state_ref[0, p * LANES:(p + 1) * LANES, :] = state_t_ref[:, p * LANES:(p + 1) * LANES].T


def _pmix_step(slot_a, x_ref, npre_ref, w_main_ref, w_dt_ref, sconv_w_ref, sconv_norm_ref,
               xconv_w_ref, xconv_b_ref, dt_bias_ref, a_log_ref, d_full_ref, ssm_norm_ref, tri_ref, w_out_ref,
               npost_ref, h1_ref, proj_s, dt_s, h_s, hn_s, sc_tail, xbc_tail, state_t_ref):
    t = CHUNK
    slot_b = 1 - slot_a

    h_a = x_ref[0]
    h_s[slot_a][...] = h_a
    hn_s[...] = _rms(h_a, npre_ref[...]).astype(BF16)

    def stage_a(point):
        for piece, planned in enumerate(STAGE_A_PLAN):
            if planned == point:
                if piece + 1 < len(STAGE_A_COLS):
                    off, end = STAGE_A_COLS[piece], STAGE_A_COLS[piece + 1]
                    proj_s[slot_a][:, off:end] = _dot(hn_s[...], w_main_ref[0, :, off:end])
                else:
                    dt_s[slot_a][...] = _dot_nt(w_dt_ref[...], hn_s[...])

    for sub in range(TILE_CHUNKS):
        r = slice(sub * t, (sub + 1) * t)

        def proj(off, width):
            return proj_s[slot_b][r, off:off + width]

        ya, u = _sconv_mixer(proj(O_GATE_B, D_SCONV), proj(O_GATE_C, D_SCONV), proj(O_HV, D_SCONV),
                             sc_tail[...], sconv_w_ref[...], sconv_norm_ref[...], None)
        sc_tail[...] = u[t - SUBLANES:t]
        stage_a((sub, "sconv"))
        mix_ya = _dot(ya.astype(BF16), w_out_ref[0:D_SCONV, :])

        def xconv(col, width):
            cols = slice(col, col + width)
            raw = proj(O_XS + col, width)
            out = _silu(_causal_conv(raw, xbc_tail[:, cols], xconv_w_ref[:, cols], None) + xconv_b_ref[:, cols])
            xbc_tail[:, cols] = raw[t - SUBLANES:t]
            return out

        bc = xconv(D_SSM, D_BC)
        bm_b = bc[:, 0:SSM_GROUPS * SSM_STATE].astype(BF16)
        cm = bc[:, SSM_GROUPS * SSM_STATE:D_BC]
        a_cum, w, a_cum_t, dt_t, _ = _ssd_prepare_rows(dt_s[slot_b][:, r], dt_bias_ref[...], a_log_ref[...],
                                                       tri_ref[...])
        causal = lax.broadcasted_iota(jnp.int32, (t, t), 0) >= lax.broadcasted_iota(jnp.int32, (t, t), 1)
        stage_a((sub, "prepare"))

        def gate(p, y_p, xs_p):
            cols = slice(p * LANES, (p + 1) * LANES)
            return (y_p + d_full_ref[:, cols] * xs_p) * _silu(proj(O_Z + p * LANES, LANES))

        y, xw_b, ea_full, _ = _ssd_heads(lambda p: xconv(p * LANES, LANES), bm_b, cm, a_cum, w, a_cum_t, dt_t,
                                         causal, state_t_ref, gate, lambda p, *partial: stage_a((sub, p)))
        chunk_decay = ea_full[t - 1:t, :]
        for g in range(SSM_GROUPS):
            cols = slice(g * D_GROUP, (g + 1) * D_GROUP)
            state_t_ref[:, cols] = chunk_decay[:, cols] * state_t_ref[:, cols] + _dot_tn(
                bm_b[:, g * SSM_STATE:(g + 1) * SSM_STATE], xw_b[:, cols])
        yb = _group_rms_ssm(y, ssm_norm_ref[...])
        mix = mix_ya + _dot(yb.astype(BF16), w_out_ref[D_SCONV:D_MIX, :])
        stage_a((sub, "end"))
        h1_ref[0, r, :] = h_s[slot_b][r, :] + _rms(mix, npost_ref[...])


def _pmix(x_prompt, consts, sc0, xbc0, state0_t):
    batch, seq, _ = x_prompt.shape
    rows = TILE_CHUNKS * CHUNK
    sps = seq // rows
    n_tiles = batch * sps
    const_spec = lambda a: (pl.BlockSpec(a.shape, lambda k: (0, 0)) if a.ndim == 2
                            else _w_main_spec(lambda k: (0, 0, 0)))
    ins = list(consts) + [sc0, xbc0, state0_t]

    def tile_a(k):
        tile = jnp.minimum(k, n_tiles - 1)
        return (tile // sps, tile % sps, 0)

    def tile_b(k):
        tile = jnp.maximum(k - 1, 0)
        return (tile // sps, tile % sps, 0)

    def seq_b(k):
        return (jnp.maximum(k - 1, 0) // sps, 0, 0)

    return pl.pallas_call(
        functools.partial(_pmix_kernel, steps_per_seq=sps),
        grid=(n_tiles + 1,),
        in_specs=[pl.BlockSpec((1, rows, D_MODEL), tile_a)] + [const_spec(a) for a in ins],
        out_specs=[
            pl.BlockSpec((1, rows, D_MODEL), tile_b),
            pl.BlockSpec((1, SCONV_WIDTH - 1, D_SCONV), seq_b),
            pl.BlockSpec((1, SSM_CONV_WIDTH - 1, D_XBC), seq_b),
            pl.BlockSpec((1, D_SSM, SSM_STATE), seq_b),
        ],
        out_shape=[
            jax.ShapeDtypeStruct((batch, seq, D_MODEL), F32),
            jax.ShapeDtypeStruct((batch, SCONV_WIDTH - 1, D_SCONV), F32),
            jax.ShapeDtypeStruct((batch, SSM_CONV_WIDTH - 1, D_XBC), F32),
            jax.ShapeDtypeStruct((batch, D_SSM, SSM_STATE), F32),
        ],
        scratch_shapes=[
            pltpu.VMEM((rows, O_DT), F32),
            pltpu.VMEM((rows, O_DT), F32),
            pltpu.VMEM((SSM_HEADS, rows), F32),
            pltpu.VMEM((SSM_HEADS, rows), F32),
            pltpu.VMEM((rows, D_MODEL), F32),
            pltpu.VMEM((rows, D_MODEL), F32),
            pltpu.VMEM((rows, D_MODEL), BF16),
            pltpu.VMEM((SUBLANES, D_SCONV), F32),
            pltpu.VMEM((SUBLANES, D_XBC), F32),
            pltpu.VMEM((SSM_STATE, D_SSM), F32),
        ],
        compiler_params=pltpu.CompilerParams(dimension_semantics=("arbitrary",), vmem_limit_bytes=VMEM_LIMIT),
        name="prompt_mixer",
    )(x_prompt, *ins)


def _smix_kernel(x_ref, sc_prev_ref, xbc_prev_ref, state_in_ref, npre_ref, w_main_ref, w_dt_ref,
                 sconv_w_ref, sconv_norm_ref, xconv_w_ref, xconv_b_ref,
                 dt_bias_ref, a_log_ref, d_full_ref, ssm_norm_ref, w_out_ref, npost_ref,
                 h1_ref, u_ref, xbc_ref, state_ref, *, dec_seq):
    t = SEQ_TILE * dec_seq
    h = x_ref[...]
    hn = _rms(h, npre_ref[...]).astype(BF16)

    def proj(off, width):
        return _dot(hn, w_main_ref[0, :, off:off + width])

    ya, u = _sconv_mixer(proj(O_GATE_B, D_SCONV), proj(O_GATE_C, D_SCONV), proj(O_HV, D_SCONV),
                         sc_prev_ref[...], sconv_w_ref[...], sconv_norm_ref[...], dec_seq)
    for s in range(SEQ_TILE):
        u_ref[s] = u[(s + 1) * dec_seq - (SCONV_WIDTH - 1):(s + 1) * dec_seq, :]
    xbc = proj(O_XS, D_XBC)
    for s in range(SEQ_TILE):
        xbc_ref[s] = xbc[(s + 1) * dec_seq - (SSM_CONV_WIDTH - 1):(s + 1) * dec_seq, :]
    xbc_c = _silu(_causal_conv(xbc, xbc_prev_ref[...], xconv_w_ref[...], dec_seq) + xconv_b_ref[...])
    xs = xbc_c[:, 0:D_SSM]
    bm = xbc_c[:, D_SSM:D_SSM + SSM_GROUPS * SSM_STATE]
    bm_b = bm.astype(BF16)
    cm = xbc_c[:, D_SSM + SSM_GROUPS * SSM_STATE:D_XBC]

    row = lax.broadcasted_iota(jnp.int32, (t, t), 0)
    col = lax.broadcasted_iota(jnp.int32, (t, t), 1)
    same_seq = row // dec_seq == col // dec_seq
    mask = jnp.logical_and(row >= col, same_seq)
    tri = jnp.where(mask, 1.0, 0.0).astype(BF16)
    seq_ones = jnp.where(same_seq, 1.0, 0.0).astype(BF16)
    a_cum, w, a_cum_t, dt_t, a_total_t = _ssd_prepare_rows(_dot_nt(w_dt_ref[...], hn), dt_bias_ref[...],
                                                           a_log_ref[...], tri, seq_ones)
    y, xw_b, ea_full, _ = _ssd_heads(lambda p: xs[:, p * LANES:(p + 1) * LANES], bm_b, cm, a_cum, w, a_cum_t,
                                     dt_t, mask)

    row_id = lax.broadcasted_iota(jnp.int32, (t, SSM_STATE), 0) // dec_seq
    seq_decay_t = jnp.exp(a_total_t)
    y_off_rows = []
    for s in range(SEQ_TILE):
        r0 = s * dec_seq
        q_t = jnp.broadcast_to(seq_decay_t[:, r0:r0 + 1], (SSM_HEADS, SSM_STATE))
        y_off_g = []
        for g in range(SSM_GROUPS):
            rows = slice(g * D_GROUP, (g + 1) * D_GROUP)
            s_prev = state_in_ref[s, rows, :]
            cm_s = cm[r0:r0 + dec_seq, g * SSM_STATE:(g + 1) * SSM_STATE].astype(BF16)
            y_off_g.append(_dot_nt(cm_s, s_prev.astype(BF16)))
            bm_g = jnp.where(row_id == s, bm[:, g * SSM_STATE:(g + 1) * SSM_STATE], 0.0).astype(BF16)
            decay = jnp.concatenate(
                [jnp.broadcast_to(q_t[e:e + 1, :], (SSM_HEAD_DIM, SSM_STATE))
                 for e in range(g * HEADS_PER_GROUP, (g + 1) * HEADS_PER_GROUP)], axis=0)
            state_ref[s, rows, :] = decay * s_prev + _dot_tn(xw_b[:, rows], bm_g)
        y_off_rows.append(jnp.concatenate(y_off_g, axis=1))
    y = y + jnp.concatenate(y_off_rows, axis=0) * ea_full
    yb = _ssm_tail(y, xs, proj(O_Z, D_SSM), d_full_ref[...], ssm_norm_ref[...])
    mix = _dot(ya.astype(BF16), w_out_ref[0:D_SCONV, :]) + _dot(yb.astype(BF16), w_out_ref[D_SCONV:D_MIX, :])
    h1_ref[...] = h + _rms(mix, npost_ref[...])


def _smix(x_rows, sc_prev, xbc_prev, state, consts, dec_seq):
    n_seq = state.shape[0]
    t = SEQ_TILE * dec_seq
    const_spec = lambda a: (pl.BlockSpec(a.shape, lambda i: (0, 0), pipeline_mode=pl.Buffered(1)) if a.ndim == 2
                            else _w_main_spec(lambda i: (0, 0, 0), pipeline_mode=pl.Buffered(1)))
    row_spec = lambda width: pl.BlockSpec((t, width), lambda i: (i, 0))
    state_spec = pl.BlockSpec((SEQ_TILE, D_SSM, SSM_STATE), lambda i: (i, 0, 0))
    return pl.pallas_call(
        functools.partial(_smix_kernel, dec_seq=dec_seq),
        grid=(n_seq // SEQ_TILE,),
        in_specs=[row_spec(D_MODEL), row_spec(D_SCONV), row_spec(D_XBC), state_spec]
        + [const_spec(a) for a in consts],
        out_specs=[row_spec(D_MODEL),
                   pl.BlockSpec((SEQ_TILE, SCONV_WIDTH - 1, D_SCONV), lambda i: (i, 0, 0)),
                   pl.BlockSpec((SEQ_TILE, SSM_CONV_WIDTH - 1, D_XBC), lambda i: (i, 0, 0)), state_spec],
        out_shape=[
            jax.ShapeDtypeStruct((n_seq * dec_seq, D_MODEL), F32),
            jax.ShapeDtypeStruct((n_seq, SCONV_WIDTH - 1, D_SCONV), F32),
            jax.ShapeDtypeStruct((n_seq, SSM_CONV_WIDTH - 1, D_XBC), F32),
            jax.ShapeDtypeStruct(state.shape, F32),
        ],
        compiler_params=pltpu.CompilerParams(dimension_semantics=("arbitrary",), vmem_limit_bytes=VMEM_LIMIT),
        name="sample_mixer",
    )(x_rows, sc_prev, xbc_prev, state, *consts)


def _ffn_kernel(hp_ref, hs_ref, npre_ref, w_gate_ref, w_up_ref, w_down_ref, npost_ref, op_ref, os_ref, *,
                prompt_steps):
    def body(h_ref, o_ref):
        h1 = h_ref[...]
        fn = _rms(h1, npre_ref[...]).astype(BF16)
        act = (_silu(_dot(fn, w_gate_ref[...])) * _dot(fn, w_up_ref[...])).astype(BF16)
        o_ref[...] = h1 + _rms(_dot(act, w_down_ref[...]), npost_ref[...])

    step = pl.program_id(0)
    pl.when(step < prompt_steps)(functools.partial(body, hp_ref, op_ref))
    pl.when(step >= prompt_steps)(functools.partial(body, hs_ref, os_ref))


def _ffn(h_prompt, h_sample, ffn_consts):
    mp, ms = h_prompt.shape[0], h_sample.shape[0]
    rows = FFN_ROWS
    p_steps, s_steps = mp // rows, ms // rows
    p_spec = pl.BlockSpec((rows, D_MODEL), lambda i: (jnp.minimum(i, p_steps - 1), 0))
    s_spec = pl.BlockSpec((rows, D_MODEL), lambda i: (jnp.maximum(i - p_steps, 0), 0))
    const_spec = lambda a: pl.BlockSpec(a.shape, lambda i: (0, 0), pipeline_mode=pl.Buffered(1))
    return pl.pallas_call(
        functools.partial(_ffn_kernel, prompt_steps=p_steps),
        grid=(p_steps + s_steps,),
        in_specs=[p_spec, s_spec] + [const_spec(a) for a in ffn_consts],
        out_specs=[p_spec, s_spec],
        out_shape=[jax.ShapeDtypeStruct((mp, D_MODEL), F32), jax.ShapeDtypeStruct((ms, D_MODEL), F32)],
        compiler_params=pltpu.CompilerParams(dimension_semantics=("arbitrary",), vmem_limit_bytes=VMEM_LIMIT),
        name="ffn",
    )(h_prompt, h_sample, *ffn_consts)


def _row(v, width=None):
    v = v.reshape(1, -1).astype(F32)
    if width is not None and v.shape[1] < width:
        v = jnp.pad(v, ((0, 0), (0, width - v.shape[1])))
    return v


def _layer(hp, hs, sc_state, xbc_state, ssm_state, meta_tokens,
           norm_mix_pre, norm_mix_post, w_in_all, sconv_w, sconv_norm, ssm_conv_w, ssm_conv_b,
           dt_bias, a_log, d_skip, ssm_norm, w_out, norm_ffn_pre, norm_ffn_post, w_gate, w_up, w_down):
    batch, seq, _ = hp.shape
    w_in = w_in_all[0]
    n_seq, dec_seq, _ = hs.shape

    w_main_b = w_in_all.astype(BF16)
    w_dt_b = jnp.pad(w_in[:, O_DT:D_IN], ((0, 0), (0, D_IN_PAD - D_IN))).astype(BF16)
    w_dt_t = w_in[:, O_DT:D_IN].T.astype(BF16)
    w_out_b = w_out.astype(BF16)
    w_gate_b, w_up_b, w_down_b = w_gate.astype(BF16), w_up.astype(BF16), w_down.astype(BF16)
    npre, npost = _row(norm_mix_pre), _row(norm_mix_post)
    fpre, fpost = _row(norm_ffn_pre), _row(norm_ffn_post)
    sconv_w = sconv_w.astype(F32)
    xconv_w = ssm_conv_w.astype(F32)
    xconv_b = _row(ssm_conv_b)
    dt_bias_p, a_log_p = _row(dt_bias, LANES), _row(a_log, LANES)
    dt_bias_t = jnp.broadcast_to(dt_bias.astype(F32)[:, None], (SSM_HEADS, LANES))
    a_log_t = jnp.broadcast_to(a_log.astype(F32)[:, None], (SSM_HEADS, LANES))
    d_full = jnp.repeat(d_skip.astype(F32), SSM_HEAD_DIM).reshape(1, D_SSM)
    sconv_norm_r, ssm_norm_r = _row(sconv_norm), _row(ssm_norm)

    r = jnp.arange(CHUNK)
    tri = (r[:, None] >= r[None, :]).astype(BF16)
    expand = (jnp.arange(LANES)[:, None] == (jnp.arange(D_SSM)[None, :] // SSM_HEAD_DIM)).astype(BF16)

    mixer_consts = (sconv_w, sconv_norm_r, xconv_w, xconv_b, dt_bias_t, a_log_t, d_full, ssm_norm_r)
    ffn_consts = (fpre, w_gate_b, w_up_b, w_down_b, fpost)

    proj_meta = _inproj(meta_tokens.astype(F32), npre, w_main_b, w_dt_b)
    sc0, xbc0, state0_t = _meta_state(proj_meta, xconv_w, xconv_b, dt_bias_p, a_log_p, tri, expand)

    pm_consts = (npre, w_main_b, w_dt_t) + mixer_consts + (tri, w_out_b, npost)
    h1, new_sc_p, new_xbc_p, p_state = _pmix(hp, pm_consts, sc0, xbc0, state0_t)
    new_ssm_p = p_state.reshape(batch, SSM_GROUPS, HEADS_PER_GROUP, SSM_HEAD_DIM, SSM_STATE)

    hs_rows = hs.reshape(n_seq * dec_seq, D_MODEL)
    sc_prev = jnp.pad(sc_state, ((0, 0), (dec_seq - (SCONV_WIDTH - 1), 0), (0, 0))).reshape(n_seq * dec_seq, D_SCONV)
    xbc_prev = jnp.pad(xbc_state, ((0, 0), (dec_seq - (SSM_CONV_WIDTH - 1), 0), (0, 0))).reshape(n_seq * dec_seq, D_XBC)
    sm_consts = (npre, w_main_b, w_dt_t) + mixer_consts + (w_out_b, npost)
    h1_s, new_sc_s, new_xbc_s, s_state = _smix(hs_rows, sc_prev, xbc_prev, ssm_state.reshape(n_seq, D_SSM, SSM_STATE),
                                      sm_consts, dec_seq)

    y_prompt, y_sample = _ffn(h1.reshape(batch * seq, D_MODEL), h1_s, ffn_consts)
    y_prompt = y_prompt.reshape(batch, seq, D_MODEL)
    y_sample = y_sample.reshape(n_seq, dec_seq, D_MODEL)
    new_ssm_s = s_state.reshape(n_seq, SSM_GROUPS, HEADS_PER_GROUP, SSM_HEAD_DIM, SSM_STATE)
    return y_prompt, y_sample, new_sc_p, new_xbc_p, new_ssm_p, new_sc_s, new_xbc_s, new_ssm_s


def kernel(x_prompt, x_sample, state_sconv, state_ssm_conv, state_ssm, meta_tokens, norm_mix_pre, norm_mix_post, w_in, sconv_w, sconv_norm, ssm_conv_w, ssm_conv_b, dt_bias, A_log, D_skip, ssm_norm, w_out, norm_ffn_pre, norm_ffn_post, w_gate, w_up, w_down):
    depth = w_in.shape[0]
    assert depth == 1 and x_sample.shape[1] == SUBLANES
    outs = _layer(x_prompt, x_sample, state_sconv[0], state_ssm_conv[0], state_ssm[0], meta_tokens,
                  norm_mix_pre[0], norm_mix_post[0], w_in, sconv_w[0], sconv_norm[0], ssm_conv_w[0],
                  ssm_conv_b[0], dt_bias[0], A_log[0], D_skip[0], ssm_norm[0], w_out[0],
                  norm_ffn_pre[0], norm_ffn_post[0], w_gate[0], w_up[0], w_down[0])
    y_prompt, y_sample = outs[0], outs[1]
    return (y_prompt, y_sample) + tuple(o[None] for o in outs[2:])
```

```python
import functools

import jax
import jax.numpy as jnp
from jax import lax
from jax.experimental import pallas as pl
from jax.experimental.pallas import tpu as pltpu

D_MODEL = 1024
N_META = 16
D_SCONV = 512
SCONV_GROUPS = 8
SCONV_WIDTH = 3
SSM_HEAD_DIM = 64
SSM_HEADS = 16
D_SSM = 1024
SSM_GROUPS = 2
HEADS_PER_GROUP = 8
D_GROUP = HEADS_PER_GROUP * SSM_HEAD_DIM
SSM_STATE = 128
SSM_CONV_WIDTH = 4
D_BC = 2 * SSM_GROUPS * SSM_STATE
D_XBC = D_SSM + D_BC
D_MIX = D_SCONV + D_SSM
D_IN = 3 * D_SCONV + D_SSM + D_XBC + SSM_HEADS
D_FF = 2816
EPS = 1e-6

LANES = 128
SUBLANES = 8
D_IN_PAD = D_IN + (LANES - SSM_HEADS)
HEAD_PAIRS = D_SSM // LANES

O_GATE_B = 0
O_GATE_C = D_SCONV
O_HV = 2 * D_SCONV
O_Z = 3 * D_SCONV
O_XS = O_Z + D_SSM
O_BC = O_XS + D_SSM
O_DT = O_BC + D_BC

CHUNK = 128
TILE_CHUNKS = 4
STAGE_A_COLS = tuple(range(0, O_DT + 1, D_SCONV))
STAGE_A_PLAN = ((0, "sconv"), (0, 1), (0, 5), (1, 1), (1, 5), (2, 1), (2, 5), (3, 1), (3, 5))
SEQ_TILE = 16
FFN_ROWS = 256
VMEM_LIMIT = 60000 * 1024

BF16 = jnp.bfloat16
F32 = jnp.float32


def _dot(a, b):
    return jnp.dot(a, b, preferred_element_type=F32)


def _dot_nt(a, b):
    return lax.dot_general(a, b, (((1,), (1,)), ((), ())), preferred_element_type=F32)


def _dot_tn(a, b):
    return lax.dot_general(a, b, (((0,), (0,)), ((), ())), preferred_element_type=F32)


def _split3(v):
    hi = v.astype(BF16)
    r1 = v - hi.astype(F32)
    mid = r1.astype(BF16)
    lo = (r1 - mid.astype(F32)).astype(BF16)
    return hi, mid, lo


def _dot3(v, m):
    hi, mid, lo = _split3(v)
    return _dot(hi, m) + _dot(mid, m) + _dot(lo, m)


def _dot3_left(m, v):
    hi, mid, lo = _split3(v)
    return _dot(m, hi) + _dot(m, mid) + _dot(m, lo)


def _rms(x, w):
    return x * lax.rsqrt(jnp.mean(x * x, axis=-1, keepdims=True) + EPS) * w


def _silu(x):
    return x * (1.0 / (1.0 + jnp.exp(-x)))


def _softplus(x):
    return jnp.maximum(x, 0.0) + jnp.log1p(jnp.exp(-jnp.abs(x)))


def _shifted(u, prev, j, rows_per_seq):
    t = u.shape[0]
    s = pltpu.roll(u, j, 0)
    if rows_per_seq is None:
        row = lax.broadcasted_iota(jnp.int32, (SUBLANES, u.shape[1]), 0)
        first = jnp.where(row < j, pltpu.roll(prev, j, 0), s[0:SUBLANES])
        if t == SUBLANES:
            return first
        return jnp.concatenate([first, s[SUBLANES:]], axis=0)
    row = lax.broadcasted_iota(jnp.int32, u.shape, 0) % SUBLANES
    return jnp.where(row < j, pltpu.roll(prev, t - SUBLANES + j, 0), s)


def _causal_conv(u, prev, w, rows_per_seq):
    k = w.shape[0]
    out = _shifted(u, prev, k - 1, rows_per_seq) * w[0:1]
    for i in range(1, k - 1):
        out = out + _shifted(u, prev, k - 1 - i, rows_per_seq) * w[i:i + 1]
    return out + u * w[k - 1:k]


def _sconv_mixer(gate_b, gate_c, hv, prev, sconv_w, sconv_norm, rows_per_seq):
    u = gate_c * hv
    conv_u = _causal_conv(u, prev, sconv_w, rows_per_seq)
    return _sconv_norm(gate_b, conv_u, sconv_norm), u


def _sconv_norm(gate_b, conv_u, sconv_norm):
    v = gate_b * conv_u
    first_half = lax.broadcasted_iota(jnp.int32, (v.shape[0], LANES), 1) < D_SCONV // SCONV_GROUPS
    parts = []
    for c in range(0, D_SCONV, LANES):
        vc = v[:, c:c + LANES]
        sq = vc * vc
        s_first = jnp.sum(jnp.where(first_half, sq, 0.0), axis=-1, keepdims=True)
        s_second = jnp.sum(jnp.where(first_half, 0.0, sq), axis=-1, keepdims=True)
        mean_sq = jnp.where(first_half, s_first, s_second) * (SCONV_GROUPS / D_SCONV)
        parts.append(vc * lax.rsqrt(mean_sq + EPS))
    return jnp.concatenate(parts, axis=1) * sconv_norm


def _ssd_prepare(dt_raw, dt_bias, a_log, tri, t):
    lane = lax.broadcasted_iota(jnp.int32, (t, LANES), 1)
    valid = lane < SSM_HEADS
    dt = jnp.where(valid, _softplus(dt_raw + dt_bias), 0.0)
    a = dt * (-jnp.exp(a_log))
    a_cum = _dot3_left(tri, a)
    return dt, a_cum


def _ssd_prepare_rows(dt_raw_t, dt_bias_t, a_log_t, tri, seq_ones=None):
    t = dt_raw_t.shape[1]
    dt_t = _softplus(dt_raw_t + dt_bias_t)
    a_t = dt_t * (-jnp.exp(a_log_t))
    hi, mid, lo = _split3(a_t)
    a_cum_t = _dot_nt(hi, tri) + _dot_nt(mid, tri) + _dot_nt(lo, tri)
    if seq_ones is None:
        a_total_t = jnp.broadcast_to(a_cum_t[:, t - 1:t], (SSM_HEADS, t))
    else:
        a_total_t = _dot_nt(hi, seq_ones) + _dot_nt(mid, seq_ones) + _dot_nt(lo, seq_ones)
    w_t = jnp.exp(a_total_t - a_cum_t) * dt_t
    pad = jnp.zeros((LANES - SSM_HEADS, t), F32)
    a_cum = jnp.concatenate([a_cum_t, pad], axis=0).T
    w = jnp.concatenate([w_t, pad], axis=0).T
    return a_cum, w, a_cum_t, dt_t, a_total_t


def _group_rms_ssm(y, w):
    parts = []
    for g in range(SSM_GROUPS):
        yg = y[:, g * D_GROUP:(g + 1) * D_GROUP]
        parts.append(yg * lax.rsqrt(jnp.mean(yg * yg, axis=-1, keepdims=True) + EPS))
    return jnp.concatenate(parts, axis=1) * w


def _ssd_heads(xs_of, bm_b, cm, a_cum, w, a_cum_t, dt_t, mask, state_t=None, gate=None, after_pair=None):
    t = a_cum.shape[0]
    assert t == LANES
    first_half = lax.broadcasted_iota(jnp.int32, (t, LANES), 1) < SSM_HEAD_DIM
    y_pairs, xw_pairs, ea_pairs, xs_pairs = [], [], [], []
    for g in range(SSM_GROUPS):
        cm_g = cm[:, g * SSM_STATE:(g + 1) * SSM_STATE]
        cb = _dot_nt(cm_g.astype(BF16), bm_b[:, g * SSM_STATE:(g + 1) * SSM_STATE])
        for q in range(HEADS_PER_GROUP // 2):
            pair = g * (HEADS_PER_GROUP // 2) + q
            cols = slice(pair * LANES, (pair + 1) * LANES)
            xs_p = xs_of(pair)
            rhs = xs_p.astype(BF16)
            if state_t is not None:
                rhs = jnp.concatenate([rhs, state_t[:, cols].astype(BF16)], axis=0)
            res, eas, ws = [], [], []
            for e in (2 * pair, 2 * pair + 1):
                a_col = jnp.broadcast_to(a_cum[:, e:e + 1], (t, LANES))
                decay = jnp.exp(jnp.where(mask, a_col - a_cum_t[e:e + 1, :], -jnp.inf))
                lhs = (cb * decay * dt_t[e:e + 1, :]).astype(BF16)
                ea = jnp.exp(a_col)
                if state_t is not None:
                    lhs = jnp.concatenate([lhs, (cm_g * ea).astype(BF16)], axis=1)
                res.append(_dot(lhs, rhs))
                eas.append(ea)
                ws.append(jnp.broadcast_to(w[:, e:e + 1], (t, LANES)))
            y_p = jnp.where(first_half, res[0], res[1])
            y_pairs.append(y_p if gate is None else gate(pair, y_p, xs_p))
            xw_pairs.append((xs_p * jnp.where(first_half, ws[0], ws[1])).astype(BF16))
            ea_pairs.append(jnp.where(first_half, eas[0], eas[1]))
            xs_pairs.append(xs_p)
            if after_pair is not None:
                after_pair(pair, y_pairs, xw_pairs, ea_pairs)
    return (jnp.concatenate(y_pairs, axis=1), jnp.concatenate(xw_pairs, axis=1),
            jnp.concatenate(ea_pairs, axis=1), jnp.concatenate(xs_pairs, axis=1))


def _ssm_tail(y, xs, z, d_full, ssm_norm):
    y = y + d_full * xs
    y = y * _silu(z)
    return _group_rms_ssm(y, ssm_norm)


def _w_main_spec(index_map, **kwargs):
    return pl.BlockSpec((1, D_MODEL, O_DT), index_map, **kwargs)


def _inproj_kernel(x_ref, npre_ref, w_main_ref, w_dt_ref, o_ref):
    hn = _rms(x_ref[...], npre_ref[...]).astype(BF16)
    o_ref[:, 0:O_DT] = _dot(hn, w_main_ref[0])
    o_ref[:, O_DT:D_IN_PAD] = _dot(hn, w_dt_ref[...])


def _inproj(x, npre, w_main_b, w_dt_b):
    m = x.shape[0]
    once = lambda shape: pl.BlockSpec(shape, lambda i: (0, 0), pipeline_mode=pl.Buffered(1))
    return pl.pallas_call(
        _inproj_kernel,
        grid=(1,),
        in_specs=[once((m, D_MODEL)), once((1, D_MODEL)),
                  _w_main_spec(lambda i: (0, 0, 0), pipeline_mode=pl.Buffered(1)), once((D_MODEL, LANES))],
        out_specs=once((m, D_IN_PAD)),
        out_shape=jax.ShapeDtypeStruct((m, D_IN_PAD), F32),
        compiler_params=pltpu.CompilerParams(dimension_semantics=("arbitrary",), vmem_limit_bytes=VMEM_LIMIT),
        name="inproj",
    )(x, npre, w_main_b, w_dt_b)


def _meta_kernel(proj_ref, xconv_w_ref, xconv_b_ref, dt_bias_ref, a_log_ref, tri_ref, expand_ref,
                 sc_tail_ref, xbc_tail_ref, state_t_ref):
    t = N_META
    proj = proj_ref[...]
    u = proj[:, O_GATE_C:O_GATE_C + D_SCONV] * proj[:, O_HV:O_HV + D_SCONV]
    sc_tail_ref[...] = u[t - SUBLANES:t]
    xbc = proj[:, O_XS:O_XS + D_XBC]
    xbc_tail_ref[...] = xbc[t - SUBLANES:t]
    zeros_prev = jnp.zeros((SUBLANES, D_XBC), F32)
    xbc_c = _silu(_causal_conv(xbc, zeros_prev, xconv_w_ref[...], None) + xconv_b_ref[...])
    xs = xbc_c[:, 0:D_SSM]
    dt, a_cum = _ssd_prepare(proj[:, O_DT:O_DT + LANES], dt_bias_ref[...], a_log_ref[...],
                             tri_ref[0:t, 0:t], t)
    w = jnp.exp(a_cum[t - 1:t, :] - a_cum) * dt
    xw_b = (xs * _dot3(w, expand_ref[...])).astype(BF16)
    for g in range(SSM_GROUPS):
        bm_b = xbc_c[:, D_SSM + g * SSM_STATE:D_SSM + (g + 1) * SSM_STATE].astype(BF16)
        state_t_ref[:, g * D_GROUP:(g + 1) * D_GROUP] = _dot_tn(bm_b, xw_b[:, g * D_GROUP:(g + 1) * D_GROUP])


def _meta_state(proj_meta, xconv_w, xconv_b, dt_bias, a_log, tri, expand):
    full = lambda shape: pl.BlockSpec(shape, lambda: tuple(0 for _ in shape))
    return pl.pallas_call(
        _meta_kernel,
        in_specs=[full(proj_meta.shape), full(xconv_w.shape), full(xconv_b.shape), full(dt_bias.shape),
                  full(a_log.shape), full(tri.shape), full(expand.shape)],
        out_specs=[full((SUBLANES, D_SCONV)), full((SUBLANES, D_XBC)), full((SSM_STATE, D_SSM))],
        out_shape=[jax.ShapeDtypeStruct((SUBLANES, D_SCONV), F32),
                   jax.ShapeDtypeStruct((SUBLANES, D_XBC), F32),
                   jax.ShapeDtypeStruct((SSM_STATE, D_SSM), F32)],
        name="meta_state",
    )(proj_meta, xconv_w, xconv_b, dt_bias, a_log, tri, expand)


def _pmix_kernel(x_ref, npre_ref, w_main_ref, w_dt_ref, sconv_w_ref, sconv_norm_ref,
                 xconv_w_ref, xconv_b_ref, dt_bias_ref, a_log_ref, d_full_ref, ssm_norm_ref,
                 tri_ref, w_out_ref, npost_ref, sc0_ref, xbc0_ref, state0_t_ref,
                 h1_ref, sc_out_ref, xbc_out_ref, state_ref,
                 proj_s0, proj_s1, dt_s0, dt_s1, h_s0, h_s1, hn_s, sc_tail, xbc_tail, state_t_ref, *,
                 steps_per_seq):
    rows = TILE_CHUNKS * CHUNK
    k = pl.program_id(0)
    step_in_seq = lax.rem(jnp.maximum(k - 1, 0), steps_per_seq)

    @pl.when(k == 0)
    def _():
        proj_s1[...] = jnp.zeros((rows, O_DT), F32)
        dt_s1[...] = jnp.zeros((SSM_HEADS, rows), F32)
        h_s1[...] = jnp.zeros((rows, D_MODEL), F32)

    @pl.when(step_in_seq == 0)
    def _():
        sc_tail[...] = sc0_ref[...]
        xbc_tail[...] = xbc0_ref[...]
        state_t_ref[...] = state0_t_ref[...]

    for parity in (0, 1):
        pl.when(lax.rem(k, 2) == parity)(functools.partial(
            _pmix_step, parity, x_ref, npre_ref, w_main_ref, w_dt_ref, sconv_w_ref, sconv_norm_ref,
            xconv_w_ref, xconv_b_ref, dt_bias_ref, a_log_ref, d_full_ref, ssm_norm_ref, tri_ref, w_out_ref,
            npost_ref, h1_ref, (proj_s0, proj_s1), (dt_s0, dt_s1), (h_s0, h_s1), hn_s, sc_tail, xbc_tail,
            state_t_ref))

    @pl.when(step_in_seq == steps_per_seq - 1)
    def _():
        sc_out_ref[0] = sc_tail[SUBLANES - (SCONV_WIDTH - 1):SUBLANES, :]
        xbc_out_ref[0] = xbc_tail[SUBLANES - (SSM_CONV_WIDTH - 1):SUBLANES, :]
        for p in range(HEAD_PAIRS):
            state_ref[0, p * LANES:(p + 1) * LANES, :] = state_t_ref[:, p * LANES:(p + 1) * LANES].T


def _pmix_step(slot_a, x_ref, npre_ref, w_main_ref, w_dt_ref, sconv_w_ref, sconv_norm_ref,
               xconv_w_ref, xconv_b_ref, dt_bias_ref, a_log_ref, d_full_ref, ssm_norm_ref, tri_ref, w_out_ref,
               npost_ref, h1_ref, proj_s, dt_s, h_s, hn_s, sc_tail, xbc_tail, state_t_ref):
    t = CHUNK
    slot_b = 1 - slot_a

    h_a = x_ref[0]
    h_s[slot_a][...] = h_a
    hn_s[...] = _rms(h_a, npre_ref[...]).astype(BF16)

    def stage_a(point):
        for piece, planned in enumerate(STAGE_A_PLAN):
            if planned == point:
                if piece + 1 < len(STAGE_A_COLS):
                    off, end = STAGE_A_COLS[piece], STAGE_A_COLS[piece + 1]
                    proj_s[slot_a][:, off:end] = _dot(hn_s[...], w_main_ref[0, :, off:end])
                else:
                    dt_s[slot_a][...] = _dot_nt(w_dt_ref[...], hn_s[...])

    for sub in range(TILE_CHUNKS):
        r = slice(sub * t, (sub + 1) * t)

        def proj(off, width):
            return proj_s[slot_b][r, off:off + width]

        ya, u = _sconv_mixer(proj(O_GATE_B, D_SCONV), proj(O_GATE_C, D_SCONV), proj(O_HV, D_SCONV),
                             sc_tail[...], sconv_w_ref[...], sconv_norm_ref[...], None)
        sc_tail[...] = u[t - SUBLANES:t]
        stage_a((sub, "sconv"))
        mix_ya = _dot(ya.astype(BF16), w_out_ref[0:D_SCONV, :])

        def xconv(col, width):
            cols = slice(col, col + width)
            raw = proj(O_XS + col, width)
            out = _silu(_causal_conv(raw, xbc_tail[:, cols], xconv_w_ref[:, cols], None) + xconv_b_ref[:, cols])
            xbc_tail[:, cols] = raw[t - SUBLANES:t]
            return out

        bc = xconv(D_SSM, D_BC)
        bm_b = bc[:, 0:SSM_GROUPS * SSM_STATE].astype(BF16)
        cm = bc[:, SSM_GROUPS * SSM_STATE:D_BC]
        a_cum, w, a_cum_t, dt_t, _ = _ssd_prepare_rows(dt_s[slot_b][:, r], dt_bias_ref[...], a_log_ref[...],
                                                       tri_ref[...])
        causal = lax.broadcasted_iota(jnp.int32, (t, t), 0) >= lax.broadcasted_iota(jnp.int32, (t, t), 1)
        stage_a((sub, "prepare"))

        def gate(p, y_p, xs_p):
            cols = slice(p * LANES, (p + 1) * LANES)
            return (y_p + d_full_ref[:, cols] * xs_p) * _silu(proj(O_Z + p * LANES, LANES))

        y, xw_b, ea_full, _ = _ssd_heads(lambda p: xconv(p * LANES, LANES), bm_b, cm, a_cum, w, a_cum_t, dt_t,
                                         causal, state_t_ref, gate, lambda p, *partial: stage_a((sub, p)))
        chunk_decay = ea_full[t - 1:t, :]
        for g in range(SSM_GROUPS):
            cols = slice(g * D_GROUP, (g + 1) * D_GROUP)
            state_t_ref[:, cols] = chunk_decay[:, cols] * state_t_ref[:, cols] + _dot_tn(
                bm_b[:, g * SSM_STATE:(g + 1) * SSM_STATE], xw_b[:, cols])
        yb = _group_rms_ssm(y, ssm_norm_ref[...])
        mix = mix_ya + _dot(yb.astype(BF16), w_out_ref[D_SCONV:D_MIX, :])
        stage_a((sub, "end"))
        h1_ref[0, r, :] = h_s[slot_b][r, :] + _rms(mix, npost_ref[...])


def _pmix(x_prompt, consts, sc0, xbc0, state0_t):
    batch, seq, _ = x_prompt.shape
    rows = TILE_CHUNKS * CHUNK
    sps = seq // rows
    n_tiles = batch * sps
    const_spec = lambda a: (pl.BlockSpec(a.shape, lambda k: (0, 0)) if a.ndim == 2
                            else _w_main_spec(lambda k: (0, 0, 0)))
    ins = list(consts) + [sc0, xbc0, state0_t]

    def tile_a(k):
        tile = jnp.minimum(k, n_tiles - 1)
        return (tile // sps, tile % sps, 0)

    def tile_b(k):
        tile = jnp.maximum(k - 1, 0)
        return (tile // sps, tile % sps, 0)

    def seq_b(k):
        return (jnp.maximum(k - 1, 0) // sps, 0, 0)

    return pl.pallas_call(
        functools.partial(_pmix_kernel, steps_per_seq=sps),
        grid=(n_tiles + 1,),
        in_specs=[pl.BlockSpec((1, rows, D_MODEL), tile_a)] + [const_spec(a) for a in ins],
        out_specs=[
            pl.BlockSpec((1, rows, D_MODEL), tile_b),
            pl.BlockSpec((1, SCONV_WIDTH - 1, D_SCONV), seq_b),
            pl.BlockSpec((1, SSM_CONV_WIDTH - 1, D_XBC), seq_b),
            pl.BlockSpec((1, D_SSM, SSM_STATE), seq_b),
        ],
        out_shape=[
            jax.ShapeDtypeStruct((batch, seq, D_MODEL), F32),
            jax.ShapeDtypeStruct((batch, SCONV_WIDTH - 1, D_SCONV), F32),
            jax.ShapeDtypeStruct((batch, SSM_CONV_WIDTH - 1, D_XBC), F32),
            jax.ShapeDtypeStruct((batch, D_SSM, SSM_STATE), F32),
        ],
        scratch_shapes=[
            pltpu.VMEM((rows, O_DT), F32),
            pltpu.VMEM((rows, O_DT), F32),
            pltpu.VMEM((SSM_HEADS, rows), F32),
            pltpu.VMEM((SSM_HEADS, rows), F32),
            pltpu.VMEM((rows, D_MODEL), F32),
            pltpu.VMEM((rows, D_MODEL), F32),
            pltpu.VMEM((rows, D_MODEL), BF16),
            pltpu.VMEM((SUBLANES, D_SCONV), F32),
            pltpu.VMEM((SUBLANES, D_XBC), F32),
            pltpu.VMEM((SSM_STATE, D_SSM), F32),
        ],
        compiler_params=pltpu.CompilerParams(dimension_semantics=("arbitrary",), vmem_limit_bytes=VMEM_LIMIT),
        name="prompt_mixer",
    )(x_prompt, *ins)


def _smix_kernel(x_ref, sc_prev_ref, xbc_prev_ref, state_in_ref, npre_ref, w_main_ref, w_dt_ref,
                 sconv_w_ref, sconv_norm_ref, xconv_w_ref, xconv_b_ref,
                 dt_bias_ref, a_log_ref, d_full_ref, ssm_norm_ref, w_out_ref, npost_ref,
                 h1_ref, u_ref, xbc_ref, state_ref, *, dec_seq):
    t = SEQ_TILE * dec_seq
    h = x_ref[...]
    hn = _rms(h, npre_ref[...]).astype(BF16)

    def proj(off, width):
        return _dot(hn, w_main_ref[0, :, off:off + width])

    ya, u = _sconv_mixer(proj(O_GATE_B, D_SCONV), proj(O_GATE_C, D_SCONV), proj(O_HV, D_SCONV),
                         sc_prev_ref[...], sconv_w_ref[...], sconv_norm_ref[...], dec_seq)
    for s in range(SEQ_TILE):
        u_ref[s] = u[(s + 1) * dec_seq - (SCONV_WIDTH - 1):(s + 1) * dec_seq, :]
    xbc = proj(O_XS, D_XBC)
    for s in range(SEQ_TILE):
        xbc_ref[s] = xbc[(s + 1) * dec_seq - (SSM_CONV_WIDTH - 1):(s + 1) * dec_seq, :]
    xbc_c = _silu(_causal_conv(xbc, xbc_prev_ref[...], xconv_w_ref[...], dec_seq) + xconv_b_ref[...])
    xs = xbc_c[:, 0:D_SSM]
    bm = xbc_c[:, D_SSM:D_SSM + SSM_GROUPS * SSM_STATE]
    bm_b = bm.astype(BF16)
    cm = xbc_c[:, D_SSM + SSM_GROUPS * SSM_STATE:D_XBC]

    row = lax.broadcasted_iota(jnp.int32, (t, t), 0)
    col = lax.broadcasted_iota(jnp.int32, (t, t), 1)
    same_seq = row // dec_seq == col // dec_seq
    mask = jnp.logical_and(row >= col, same_seq)
    tri = jnp.where(mask, 1.0, 0.0).astype(BF16)
    seq_ones = jnp.where(same_seq, 1.0, 0.0).astype(BF16)
    a_cum, w, a_cum_t, dt_t, a_total_t = _ssd_prepare_rows(_dot_nt(w_dt_ref[...], hn), dt_bias_ref[...],
                                                           a_log_ref[...], tri, seq_ones)
    y, xw_b, ea_full, _ = _ssd_heads(lambda p: xs[:, p * LANES:(p + 1) * LANES], bm_b, cm, a_cum, w, a_cum_t,
                                     dt_t, mask)

    row_id = lax.broadcasted_iota(jnp.int32, (t, SSM_STATE), 0) // dec_seq
    seq_decay_t = jnp.exp(a_total_t)
    y_off_rows = []
    for s in range(SEQ_TILE):
        r0 = s * dec_seq
        q_t = jnp.broadcast_to(seq_decay_t[:, r0:r0 + 1], (SSM_HEADS, SSM_STATE))
        y_off_g = []
        for g in range(SSM_GROUPS):
            rows = slice(g * D_GROUP, (g + 1) * D_GROUP)
            s_prev = state_in_ref[s, rows, :]
            cm_s = cm[r0:r0 + dec_seq, g * SSM_STATE:(g + 1) * SSM_STATE].astype(BF16)
            y_off_g.append(_dot_nt(cm_s, s_prev.astype(BF16)))
            bm_g = jnp.where(row_id == s, bm[:, g * SSM_STATE:(g + 1) * SSM_STATE], 0.0).astype(BF16)
            decay = jnp.concatenate(
                [jnp.broadcast_to(q_t[e:e + 1, :], (SSM_HEAD_DIM, SSM_STATE))
                 for e in range(g * HEADS_PER_GROUP, (g + 1) * HEADS_PER_GROUP)], axis=0)
            state_ref[s, rows, :] = decay * s_prev + _dot_tn(xw_b[:, rows], bm_g)
        y_off_rows.append(jnp.concatenate(y_off_g, axis=1))
    y = y + jnp.concatenate(y_off_rows, axis=0) * ea_full
    yb = _ssm_tail(y, xs, proj(O_Z, D_SSM), d_full_ref[...], ssm_norm_ref[...])
    mix = _dot(ya.astype(BF16), w_out_ref[0:D_SCONV, :]) + _dot(yb.astype(BF16), w_out_ref[D_SCONV:D_MIX, :])
    h1_ref[...] = h + _rms(mix, npost_ref[...])


def _smix(x_rows, sc_prev, xbc_prev, state, consts, dec_seq):
    n_seq = state.shape[0]
    t = SEQ_TILE * dec_seq
    const_spec = lambda a: (pl.BlockSpec(a.shape, lambda i: (0, 0), pipeline_mode=pl.Buffered(1)) if a.ndim == 2
                            else _w_main_spec(lambda i: (0, 0, 0), pipeline_mode=pl.Buffered(1)))
    row_spec = lambda width: pl.BlockSpec((t, width), lambda i: (i, 0))
    state_spec = pl.BlockSpec((SEQ_TILE, D_SSM, SSM_STATE), lambda i: (i, 0, 0))
    return pl.pallas_call(
        functools.partial(_smix_kernel, dec_seq=dec_seq),
        grid=(n_seq // SEQ_TILE,),
        in_specs=[row_spec(D_MODEL), row_spec(D_SCONV), row_spec(D_XBC), state_spec]
        + [const_spec(a) for a in consts],
        out_specs=[row_spec(D_MODEL),
                   pl.BlockSpec((SEQ_TILE, SCONV_WIDTH - 1, D_SCONV), lambda i: (i, 0, 0)),
                   pl.BlockSpec((SEQ_TILE, SSM_CONV_WIDTH - 1, D_XBC), lambda i: (i, 0, 0)), state_spec],
        out_shape=[
            jax.ShapeDtypeStruct((n_seq * dec_seq, D_MODEL), F32),
            jax.ShapeDtypeStruct((n_seq, SCONV_WIDTH - 1, D_SCONV), F32),
            jax.ShapeDtypeStruct((n_seq, SSM_CONV_WIDTH - 1, D_XBC), F32),
            jax.ShapeDtypeStruct(state.shape, F32),
        ],
        compiler_params=pltpu.CompilerParams(dimension_semantics=("arbitrary",), vmem_limit_bytes=VMEM_LIMIT),
        name="sample_mixer",
    )(x_rows, sc_prev, xbc_prev, state, *consts)


def _ffn_kernel(hp_ref, hs_ref, npre_ref, w_gate_ref, w_up_ref, w_down_ref, npost_ref, op_ref, os_ref, *,
                prompt_steps):
    def body(h_ref, o_ref):
        h1 = h_ref[...]
        fn = _rms(h1, npre_ref[...]).astype(BF16)
        act = (_silu(_dot(fn, w_gate_ref[...])) * _dot(fn, w_up_ref[...])).astype(BF16)
        o_ref[...] = h1 + _rms(_dot(act, w_down_ref[...]), npost_ref[...])

    step = pl.program_id(0)
    pl.when(step < prompt_steps)(functools.partial(body, hp_ref, op_ref))
    pl.when(step >= prompt_steps)(functools.partial(body, hs_ref, os_ref))


def _ffn(h_prompt, h_sample, ffn_consts):
    mp, ms = h_prompt.shape[0], h_sample.shape[0]
    rows = FFN_ROWS
    p_steps, s_steps = mp // rows, ms // rows
    p_spec = pl.BlockSpec((rows, D_MODEL), lambda i: (jnp.minimum(i, p_steps - 1), 0))
    s_spec = pl.BlockSpec((rows, D_MODEL), lambda i: (jnp.maximum(i - p_steps, 0), 0))
    const_spec = lambda a: pl.BlockSpec(a.shape, lambda i: (0, 0), pipeline_mode=pl.Buffered(1))
    return pl.pallas_call(
        functools.partial(_ffn_kernel, prompt_steps=p_steps),
        grid=(p_steps + s_steps,),
        in_specs=[p_spec, s_spec] + [const_spec(a) for a in ffn_consts],
        out_specs=[p_spec, s_spec],
        out_shape=[jax.ShapeDtypeStruct((mp, D_MODEL), F32), jax.ShapeDtypeStruct((ms, D_MODEL), F32)],
        compiler_params=pltpu.CompilerParams(dimension_semantics=("arbitrary",), vmem_limit_bytes=VMEM_LIMIT),
        name="ffn",
    )(h_prompt, h_sample, *ffn_consts)


def _row(v, width=None):
    v = v.reshape(1, -1).astype(F32)
    if width is not None and v.shape[1] < width:
        v = jnp.pad(v, ((0, 0), (0, width - v.shape[1])))
    return v


def _layer(hp, hs, sc_state, xbc_state, ssm_state, meta_tokens,
           norm_mix_pre, norm_mix_post, w_in_all, sconv_w, sconv_norm, ssm_conv_w, ssm_conv_b,
           dt_bias, a_log, d_skip, ssm_norm, w_out, norm_ffn_pre, norm_ffn_post, w_gate, w_up, w_down):
    batch, seq, _ = hp.shape
    w_in = w_in_all[0]
    n_seq, dec_seq, _ = hs.shape

    w_main_b = w_in_all.astype(BF16)
    w_dt_b = jnp.pad(w_in[:, O_DT:D_IN], ((0, 0), (0, D_IN_PAD - D_IN))).astype(BF16)
    w_dt_t = w_in[:, O_DT:D_IN].T.astype(BF16)
    w_out_b = w_out.astype(BF16)
    w_gate_b, w_up_b, w_down_b = w_gate.astype(BF16), w_up.astype(BF16), w_down.astype(BF16)
    npre, npost = _row(norm_mix_pre), _row(norm_mix_post)
    fpre, fpost = _row(norm_ffn_pre), _row(norm_ffn_post)
    sconv_w = sconv_w.astype(F32)
    xconv_w = ssm_conv_w.astype(F32)
    xconv_b = _row(ssm_conv_b)
    dt_bias_p, a_log_p = _row(dt_bias, LANES), _row(a_log, LANES)
    dt_bias_t = jnp.broadcast_to(dt_bias.astype(F32)[:, None], (SSM_HEADS, LANES))
    a_log_t = jnp.broadcast_to(a_log.astype(F32)[:, None], (SSM_HEADS, LANES))
    d_full = jnp.repeat(d_skip.astype(F32), SSM_HEAD_DIM).reshape(1, D_SSM)
    sconv_norm_r, ssm_norm_r = _row(sconv_norm), _row(ssm_norm)

    r = jnp.arange(CHUNK)
    tri = (r[:, None] >= r[None, :]).astype(BF16)
    expand = (jnp.arange(LANES)[:, None] == (jnp.arange(D_SSM)[None, :] // SSM_HEAD_DIM)).astype(BF16)

    mixer_consts = (sconv_w, sconv_norm_r, xconv_w, xconv_b, dt_bias_t, a_log_t, d_full, ssm_norm_r)
    ffn_consts = (fpre, w_gate_b, w_up_b, w_down_b, fpost)

    proj_meta = _inproj(meta_tokens.astype(F32), npre, w_main_b, w_dt_b)
    sc0, xbc0, state0_t = _meta_state(proj_meta, xconv_w, xconv_b, dt_bias_p, a_log_p, tri, expand)

    pm_consts = (npre, w_main_b, w_dt_t) + mixer_consts + (tri, w_out_b, npost)
    h1, new_sc_p, new_xbc_p, p_state = _pmix(hp, pm_consts, sc0, xbc0, state0_t)
    new_ssm_p = p_state.reshape(batch, SSM_GROUPS, HEADS_PER_GROUP, SSM_HEAD_DIM, SSM_STATE)

    hs_rows = hs.reshape(n_seq * dec_seq, D_MODEL)
    sc_prev = jnp.pad(sc_state, ((0, 0), (dec_seq - (SCONV_WIDTH - 1), 0), (0, 0))).reshape(n_seq * dec_seq, D_SCONV)
    xbc_prev = jnp.pad(xbc_state, ((0, 0), (dec_seq - (SSM_CONV_WIDTH - 1), 0), (0, 0))).reshape(n_seq * dec_seq, D_XBC)
    sm_consts = (npre, w_main_b, w_dt_t) + mixer_consts + (w_out_b, npost)
    h1_s, new_sc_s, new_xbc_s, s_state = _smix(hs_rows, sc_prev, xbc_prev, ssm_state.reshape(n_seq, D_SSM, SSM_STATE),
                                      sm_consts, dec_seq)

    y_prompt, y_sample = _ffn(h1.reshape(batch * seq, D_MODEL), h1_s, ffn_consts)
    y_prompt = y_prompt.reshape(batch, seq, D_MODEL)
    y_sample = y_sample.reshape(n_seq, dec_seq, D_MODEL)
    new_ssm_s = s_state.reshape(n_seq, SSM_GROUPS, HEADS_PER_GROUP, SSM_HEAD_DIM, SSM_STATE)
    return y_prompt, y_sample, new_sc_p, new_xbc_p, new_ssm_p, new_sc_s, new_xbc_s, new_ssm_s


def kernel(x_prompt, x_sample, state_sconv, state_ssm_conv, state_ssm, meta_tokens, norm_mix_pre, norm_mix_post, w_in, sconv_w, sconv_norm, ssm_conv_w, ssm_conv_b, dt_bias, A_log, D_skip, ssm_norm, w_out, norm_ffn_pre, norm_ffn_post, w_gate, w_up, w_down):
    depth = w_in.shape[0]
    assert depth == 1 and x_sample.shape[1] == SUBLANES
    outs = _layer(x_prompt, x_sample, state_sconv[0], state_ssm_conv[0], state_ssm[0], meta_tokens,
                  norm_mix_pre[0], norm_mix_post[0], w_in, sconv_w[0], sconv_norm[0], ssm_conv_w[0],
                  ssm_conv_b[0], dt_bias[0], A_log[0], D_skip[0], ssm_norm[0], w_out[0],
                  norm_ffn_pre[0], norm_ffn_post[0], w_gate[0], w_up[0], w_down[0])
    y_prompt, y_sample = outs[0], outs[1]
    return (y_prompt, y_sample) + tuple(o[None] for o in outs[2:])
```

```python
import functools

import jax
import jax.numpy as jnp
from jax import lax
from jax.experimental import pallas as pl
from jax.experimental.pallas import tpu as pltpu

D_MODEL = 1024
N_META = 16
D_SCONV = 512
SCONV_GROUPS = 8
SCONV_WIDTH = 3
SSM_HEAD_DIM = 64
SSM_HEADS = 16
D_SSM = 1024
SSM_GROUPS = 2
HEADS_PER_GROUP = 8
D_GROUP = HEADS_PER_GROUP * SSM_HEAD_DIM
SSM_STATE = 128
SSM_CONV_WIDTH = 4
D_BC = 2 * SSM_GROUPS * SSM_STATE
D_XBC = D_SSM + D_BC
D_MIX = D_SCONV + D_SSM
D_IN = 3 * D_SCONV + D_SSM + D_XBC + SSM_HEADS
D_FF = 2816
EPS = 1e-6
LOG2_E = 1.4426950408889634

LANES = 128
SUBLANES = 8
D_IN_PAD = D_IN + (LANES - SSM_HEADS)
HEAD_PAIRS = D_SSM // LANES

O_GATE_B = 0
O_GATE_C = D_SCONV
O_HV = 2 * D_SCONV
O_Z = 3 * D_SCONV
O_XS = O_Z + D_SSM
O_BC = O_XS + D_SSM
O_DT = O_BC + D_BC

CHUNK = 128
TILE_CHUNKS = 2
STAGE_A_COLS = tuple(range(0, O_DT + 1, D_SCONV))
STAGE_A_PLAN = ((0, "sconv"), (0, "prepare"), (0, 1), (0, 3), (0, 5), (1, "prepare"), (1, 1), (1, 3), (1, 5))
SEQ_TILE = 16
FFN_ROWS = 512
VMEM_LIMIT = 60000 * 1024

BF16 = jnp.bfloat16
F32 = jnp.float32


def _dot(a, b):
    return jnp.dot(a, b, preferred_element_type=F32)


def _dot_nt(a, b):
    return lax.dot_general(a, b, (((1,), (1,)), ((), ())), preferred_element_type=F32)


def _dot_tn(a, b):
    return lax.dot_general(a, b, (((0,), (0,)), ((), ())), preferred_element_type=F32)


def _split3(v):
    hi = v.astype(BF16)
    r1 = v - hi.astype(F32)
    mid = r1.astype(BF16)
    lo = (r1 - mid.astype(F32)).astype(BF16)
    return hi, mid, lo


def _dot3(v, m):
    hi, mid, lo = _split3(v)
    return _dot(hi, m) + _dot(mid, m) + _dot(lo, m)


def _dot3_left(m, v):
    hi, mid, lo = _split3(v)
    return _dot(m, hi) + _dot(m, mid) + _dot(m, lo)


def _rms(x, w):
    return x * lax.rsqrt(jnp.mean(x * x, axis=-1, keepdims=True) + EPS) * w


def _silu(x):
    hx = 0.5 * x
    return hx + hx * jnp.tanh(hx)


def _softplus(x):
    return jnp.maximum(x, 0.0) + jnp.log1p(jnp.exp(-jnp.abs(x)))


def _shifted(u, prev, j, rows_per_seq):
    t = u.shape[0]
    s = pltpu.roll(u, j, 0)
    if rows_per_seq is None:
        row = lax.broadcasted_iota(jnp.int32, (SUBLANES, u.shape[1]), 0)
        first = jnp.where(row < j, pltpu.roll(prev, j, 0), s[0:SUBLANES])
        if t == SUBLANES:
            return first
        return jnp.concatenate([first, s[SUBLANES:]], axis=0)
    row = lax.broadcasted_iota(jnp.int32, u.shape, 0) % SUBLANES
    return jnp.where(row < j, pltpu.roll(prev, t - SUBLANES + j, 0), s)


def _causal_conv(u, prev, w, rows_per_seq):
    k = w.shape[0]
    out = _shifted(u, prev, k - 1, rows_per_seq) * w[0:1]
    for i in range(1, k - 1):
        out = out + _shifted(u, prev, k - 1 - i, rows_per_seq) * w[i:i + 1]
    return out + u * w[k - 1:k]


def _sconv_mixer(gate_b, gate_c, hv, prev, sconv_w, sconv_norm, rows_per_seq):
    u = gate_c * hv
    conv_u = _causal_conv(u, prev, sconv_w, rows_per_seq)
    return _sconv_norm(gate_b, conv_u, sconv_norm), u


def _sconv_norm(gate_b, conv_u, sconv_norm):
    v = gate_b * conv_u
    first_half = lax.broadcasted_iota(jnp.int32, (v.shape[0], LANES), 1) < D_SCONV // SCONV_GROUPS
    parts = []
    for c in range(0, D_SCONV, LANES):
        vc = v[:, c:c + LANES]
        sq = vc * vc
        s_first = jnp.sum(jnp.where(first_half, sq, 0.0), axis=-1, keepdims=True)
        s_second = jnp.sum(jnp.where(first_half, 0.0, sq), axis=-1, keepdims=True)
        mean_sq = jnp.where(first_half, s_first, s_second) * (SCONV_GROUPS / D_SCONV)
        parts.append(vc * lax.rsqrt(mean_sq + EPS))
    return jnp.concatenate(parts, axis=1) * sconv_norm


def _ssd_prepare(dt_raw, dt_bias, a_log, tri, t):
    lane = lax.broadcasted_iota(jnp.int32, (t, LANES), 1)
    valid = lane < SSM_HEADS
    dt = jnp.where(valid, _softplus(dt_raw + dt_bias), 0.0)
    a = dt * (-jnp.exp(a_log))
    a_cum = _dot3_left(tri, a)
    return dt, a_cum


def _ssd_prepare_rows(dt_raw_t, dt_bias_t, a_log_t, tri, seq_ones=None):
    t = dt_raw_t.shape[1]
    dt_t = _softplus(dt_raw_t + dt_bias_t)
    a_t = dt_t * (-(jnp.exp(a_log_t) * LOG2_E))
    hi, mid, lo = _split3(a_t)
    a_cum_t = _dot_nt(hi, tri) + _dot_nt(mid, tri) + _dot_nt(lo, tri)
    if seq_ones is None:
        a_total_t = jnp.broadcast_to(a_cum_t[:, t - 1:t], (SSM_HEADS, t))
    else:
        a_total_t = _dot_nt(hi, seq_ones) + _dot_nt(mid, seq_ones) + _dot_nt(lo, seq_ones)
    w_t = jnp.exp2(a_total_t - a_cum_t) * dt_t
    pad = jnp.zeros((LANES - SSM_HEADS, t), F32)
    a_cum = jnp.concatenate([a_cum_t, pad], axis=0).T
    w = jnp.concatenate([w_t, pad], axis=0).T
    return a_cum, w, a_cum_t, dt_t, a_total_t


def _group_rms_ssm(y, w):
    parts = []
    for g in range(SSM_GROUPS):
        yg = y[:, g * D_GROUP:(g + 1) * D_GROUP]
        parts.append(yg * lax.rsqrt(jnp.mean(yg * yg, axis=-1, keepdims=True) + EPS))
    return jnp.concatenate(parts, axis=1) * w


def _ssd_heads(xs_of, bm_b, cm, a_cum, w, a_cum_t, dt_t, mask, state_t=None, gate=None, after_pair=None):
    t = a_cum.shape[0]
    assert t == LANES
    first_half = lax.broadcasted_iota(jnp.int32, (t, LANES), 1) < SSM_HEAD_DIM
    y_pairs, xw_pairs, ea_pairs, xs_pairs = [], [], [], []
    for g in range(SSM_GROUPS):
        cm_g = cm[:, g * SSM_STATE:(g + 1) * SSM_STATE]
        cb = _dot_nt(cm_g.astype(BF16), bm_b[:, g * SSM_STATE:(g + 1) * SSM_STATE])
        for q in range(HEADS_PER_GROUP // 2):
            pair = g * (HEADS_PER_GROUP // 2) + q
            cols = slice(pair * LANES, (pair + 1) * LANES)
            xs_p = xs_of(pair)
            rhs = xs_p.astype(BF16)
            if state_t is not None:
                rhs = jnp.concatenate([rhs, state_t[:, cols].astype(BF16)], axis=0)
            res, eas, ws = [], [], []
            for e in (2 * pair, 2 * pair + 1):
                a_col = jnp.broadcast_to(a_cum[:, e:e + 1], (t, LANES))
                decay = jnp.exp2(jnp.where(mask, a_col - a_cum_t[e:e + 1, :], -jnp.inf))
                lhs = (cb * decay * dt_t[e:e + 1, :]).astype(BF16)
                ea = jnp.exp2(a_col)
                if state_t is not None:
                    lhs = jnp.concatenate([lhs, (cm_g * ea).astype(BF16)], axis=1)
                res.append(_dot(lhs, rhs))
                eas.append(ea)
                ws.append(jnp.broadcast_to(w[:, e:e + 1], (t, LANES)))
            y_p = jnp.where(first_half, res[0], res[1])
            y_pairs.append(y_p if gate is None else gate(pair, y_p, xs_p))
            xw_pairs.append((xs_p * jnp.where(first_half, ws[0], ws[1])).astype(BF16))
            ea_pairs.append(jnp.where(first_half, eas[0], eas[1]))
            xs_pairs.append(xs_p)
            if after_pair is not None:
                after_pair(pair, y_pairs, xw_pairs, ea_pairs)
    return (jnp.concatenate(y_pairs, axis=1), jnp.concatenate(xw_pairs, axis=1),
            jnp.concatenate(ea_pairs, axis=1), jnp.concatenate(xs_pairs, axis=1))


def _ssm_tail(y, xs, z, d_full, ssm_norm):
    y = y + d_full * xs
    y = y * _silu(z)
    return _group_rms_ssm(y, ssm_norm)


def _w_main_spec(index_map, **kwargs):
    return pl.BlockSpec((1, D_MODEL, O_DT), index_map, **kwargs)


def _inproj_kernel(x_ref, npre_ref, w_main_ref, w_dt_ref, o_ref):
    hn = _rms(x_ref[...], npre_ref[...]).astype(BF16)
    o_ref[:, 0:O_DT] = _dot(hn, w_main_ref[0])
    o_ref[:, O_DT:D_IN_PAD] = _dot(hn, w_dt_ref[...])


def _inproj(x, npre, w_main_b, w_dt_b):
    m = x.shape[0]
    once = lambda shape: pl.BlockSpec(shape, lambda i: (0, 0), pipeline_mode=pl.Buffered(1))
    return pl.pallas_call(
        _inproj_kernel,
        grid=(1,),
        in_specs=[once((m, D_MODEL)), once((1, D_MODEL)),
                  _w_main_spec(lambda i: (0, 0, 0), pipeline_mode=pl.Buffered(1)), once((D_MODEL, LANES))],
        out_specs=once((m, D_IN_PAD)),
        out_shape=jax.ShapeDtypeStruct((m, D_IN_PAD), F32),
        compiler_params=pltpu.CompilerParams(dimension_semantics=("arbitrary",), vmem_limit_bytes=VMEM_LIMIT),
        name="inproj",
    )(x, npre, w_main_b, w_dt_b)


def _meta_kernel(proj_ref, xconv_w_ref, xconv_b_ref, dt_bias_ref, a_log_ref, tri_ref, expand_ref,
                 sc_tail_ref, xbc_tail_ref, state_t_ref):
    t = N_META
    proj = proj_ref[...]
    u = proj[:, O_GATE_C:O_GATE_C + D_SCONV] * proj[:, O_HV:O_HV + D_SCONV]
    sc_tail_ref[...] = u[t - SUBLANES:t]
    xbc = proj[:, O_XS:O_XS + D_XBC]
    xbc_tail_ref[...] = xbc[t - SUBLANES:t]
    zeros_prev = jnp.zeros((SUBLANES, D_XBC), F32)
    xbc_c = _silu(_causal_conv(xbc, zeros_prev, xconv_w_ref[...], None) + xconv_b_ref[...])
    xs = xbc_c[:, 0:D_SSM]
    dt, a_cum = _ssd_prepare(proj[:, O_DT:O_DT + LANES], dt_bias_ref[...], a_log_ref[...],
                             tri_ref[0:t, 0:t], t)
    w = jnp.exp(a_cum[t - 1:t, :] - a_cum) * dt
    xw_b = (xs * _dot3(w, expand_ref[...])).astype(BF16)
    for g in range(SSM_GROUPS):
        bm_b = xbc_c[:, D_SSM + g * SSM_STATE:D_SSM + (g + 1) * SSM_STATE].astype(BF16)
        state_t_ref[:, g * D_GROUP:(g + 1) * D_GROUP] = _dot_tn(bm_b, xw_b[:, g * D_GROUP:(g + 1) * D_GROUP])


def _meta_state(proj_meta, xconv_w, xconv_b, dt_bias, a_log, tri, expand):
    full = lambda shape: pl.BlockSpec(shape, lambda: tuple(0 for _ in shape))
    return pl.pallas_call(
        _meta_kernel,
        in_specs=[full(proj_meta.shape), full(xconv_w.shape), full(xconv_b.shape), full(dt_bias.shape),
                  full(a_log.shape), full(tri.shape), full(expand.shape)],
        out_specs=[full((SUBLANES, D_SCONV)), full((SUBLANES, D_XBC)), full((SSM_STATE, D_SSM))],
        out_shape=[jax.ShapeDtypeStruct((SUBLANES, D_SCONV), F32),
                   jax.ShapeDtypeStruct((SUBLANES, D_XBC), F32),
                   jax.ShapeDtypeStruct((SSM_STATE, D_SSM), F32)],
        name="meta_state",
    )(proj_meta, xconv_w, xconv_b, dt_bias, a_log, tri, expand)


def _pmix_kernel(x_ref, npre_ref, w_main_ref, w_dt_ref, sconv_w_ref, sconv_norm_ref,
                 xconv_w_ref, xconv_b_ref, dt_bias_ref, a_log_ref, d_full_ref, ssm_norm_ref,
                 tri_ref, w_out_ref, npost_ref, sc0_ref, xbc0_ref, state0_t_ref,
                 h1_ref, sc_out_ref, xbc_out_ref, state_ref,
                 proj_s0, proj_s1, dt_s0, dt_s1, h_s0, h_s1, hn_s, sc_tail, xbc_tail, state_t_ref, *,
                 steps_per_seq):
    rows = TILE_CHUNKS * CHUNK
    k = pl.program_id(0)
    step_in_seq = lax.rem(jnp.maximum(k - 1, 0), steps_per_seq)

    @pl.when(k == 0)
    def _():
        proj_s1[...] = jnp.zeros((rows, O_DT), F32)
        dt_s1[...] = jnp.zeros((SSM_HEADS, rows), F32)
        h_s1[...] = jnp.zeros((rows, D_MODEL), F32)

    @pl.when(step_in_seq == 0)
    def _():
        sc_tail[...] = sc0_ref[...]
        xbc_tail[...] = xbc0_ref[...]
        state_t_ref[...] = state0_t_ref[...]

    for parity in (0, 1):
        pl.when(lax.rem(k, 2) == parity)(functools.partial(
            _pmix_step, parity, x_ref, npre_ref, w_main_ref, w_dt_ref, sconv_w_ref, sconv_norm_ref,
            xconv_w_ref, xconv_b_ref, dt_bias_ref, a_log_ref, d_full_ref, ssm_norm_ref, tri_ref, w_out_ref,
            npost_ref, h1_ref, (proj_s0, proj_s1), (dt_s0, dt_s1), (h_s0, h_s1), hn_s, sc_tail, xbc_tail,
            state_t_ref))

    @pl.when(step_in_seq == steps_per_seq - 1)
    def _():
        sc_out_ref[0] = sc_tail[SUBLANES - (SCONV_WIDTH - 1):SUBLANES, :]
        xbc_out_ref[0] = xbc_tail[SUBLANES - (SSM_CONV_WIDTH - 1):SUBLANES, :]
        for p in range(HEAD_PAIRS):
            state_ref[0, p * LANES:(p + 1) * LANES, :] = state_t_ref[:, p * LANES:(p + 1) * LANES].T


def _pmix_step(slot_a, x_ref, npre_ref, w_main_ref, w_dt_ref, sconv_w_ref, sconv_norm_ref,
               xconv_w_ref, xconv_b_ref, dt_bias_ref, a_log_ref, d_full_ref, ssm_norm_ref, tri_ref, w_out_ref,
               npost_ref, h1_ref, proj_s, dt_s, h_s, hn_s, sc_tail, xbc_tail, state_t_ref):
    t = CHUNK
    slot_b = 1 - slot_a

    h_a = x_ref[0]
    h_s[slot_a][...] = h_a
    hn_s[...] = _rms(h_a, npre_ref[...]).astype(BF16)

    def stage_a(point):
        for piece, planned in enumerate(STAGE_A_PLAN):
            if planned == point:
                if piece + 1 < len(STAGE_A_COLS):
                    off, end = STAGE_A_COLS[piece], STAGE_A_COLS[piece + 1]
                    proj_s[slot_a][:, off:end] = _dot(hn_s[...], w_main_ref[0, :, off:end])
                else:
                    dt_s[slot_a][...] = _dot_nt(w_dt_ref[...], hn_s[...])

    for sub in range(TILE_CHUNKS):
        r = slice(sub * t, (sub + 1) * t)

        def proj(off, width):
            return proj_s[slot_b][r, off:off + width]

        ya, u = _sconv_mixer(proj(O_GATE_B, D_SCONV), proj(O_GATE_C, D_SCONV), proj(O_HV, D_SCONV),
                             sc_tail[...], sconv_w_ref[...], sconv_norm_ref[...], None)
        sc_tail[...] = u[t - SUBLANES:t]
        stage_a((sub, "sconv"))
        mix_ya = _dot(ya.astype(BF16), w_out_ref[0:D_SCONV, :])

        def xconv(col, width):
            cols = slice(col, col + width)
            raw = proj(O_XS + col, width)
            out = _silu(_causal_conv(raw, xbc_tail[:, cols], xconv_w_ref[:, cols], None) + xconv_b_ref[:, cols])
            xbc_tail[:, cols] = raw[t - SUBLANES:t]
            return out

        bc = xconv(D_SSM, D_BC)
        bm_b = bc[:, 0:SSM_GROUPS * SSM_STATE].astype(BF16)
        cm = bc[:, SSM_GROUPS * SSM_STATE:D_BC]
        a_cum, w, a_cum_t, dt_t, _ = _ssd_prepare_rows(dt_s[slot_b][:, r], dt_bias_ref[...], a_log_ref[...],
                                                       tri_ref[...])
        causal = lax.broadcasted_iota(jnp.int32, (t, t), 0) >= lax.broadcasted_iota(jnp.int32, (t, t), 1)
        stage_a((sub, "prepare"))

        def gate(p, y_p, xs_p):
            cols = slice(p * LANES, (p + 1) * LANES)
            return (y_p + d_full_ref[:, cols] * xs_p) * _silu(proj(O_Z + p * LANES, LANES))

        y, xw_b, ea_full, _ = _ssd_heads(lambda p: xconv(p * LANES, LANES), bm_b, cm, a_cum, w, a_cum_t, dt_t,
                                         causal, state_t_ref, gate, lambda p, *partial: stage_a((sub, p)))
        chunk_decay = ea_full[t - 1:t, :]
        for g in range(SSM_GROUPS):
            cols = slice(g * D_GROUP, (g + 1) * D_GROUP)
            state_t_ref[:, cols] = chunk_decay[:, cols] * state_t_ref[:, cols] + _dot_tn(
                bm_b[:, g * SSM_STATE:(g + 1) * SSM_STATE], xw_b[:, cols])
        yb = _group_rms_ssm(y, ssm_norm_ref[...])
        mix = mix_ya + _dot(yb.astype(BF16), w_out_ref[D_SCONV:D_MIX, :])
        stage_a((sub, "end"))
        h1_ref[0, r, :] = h_s[slot_b][r, :] + _rms(mix, npost_ref[...])


def _pmix(x_prompt, consts, sc0, xbc0, state0_t):
    batch, seq, _ = x_prompt.shape
    rows = TILE_CHUNKS * CHUNK
    sps = seq // rows
    n_tiles = batch * sps
    const_spec = lambda a: (pl.BlockSpec(a.shape, lambda k: (0, 0)) if a.ndim == 2
                            else _w_main_spec(lambda k: (0, 0, 0)))
    ins = list(consts) + [sc0, xbc0, state0_t]

    def tile_a(k):
        tile = jnp.minimum(k, n_tiles - 1)
        return (tile // sps, tile % sps, 0)

    def tile_b(k):
        tile = jnp.maximum(k - 1, 0)
        return (tile // sps, tile % sps, 0)

    def seq_b(k):
        return (jnp.maximum(k - 1, 0) // sps, 0, 0)

    return pl.pallas_call(
        functools.partial(_pmix_kernel, steps_per_seq=sps),
        grid=(n_tiles + 1,),
        in_specs=[pl.BlockSpec((1, rows, D_MODEL), tile_a)] + [const_spec(a) for a in ins],
        out_specs=[
            pl.BlockSpec((1, rows, D_MODEL), tile_b),
            pl.BlockSpec((1, SCONV_WIDTH - 1, D_SCONV), seq_b),
            pl.BlockSpec((1, SSM_CONV_WIDTH - 1, D_XBC), seq_b),
            pl.BlockSpec((1, D_SSM, SSM_STATE), seq_b),
        ],
        out_shape=[
            jax.ShapeDtypeStruct((batch, seq, D_MODEL), F32),
            jax.ShapeDtypeStruct((batch, SCONV_WIDTH - 1, D_SCONV), F32),
            jax.ShapeDtypeStruct((batch, SSM_CONV_WIDTH - 1, D_XBC), F32),
            jax.ShapeDtypeStruct((batch, D_SSM, SSM_STATE), F32),
        ],
        scratch_shapes=[
            pltpu.VMEM((rows, O_DT), F32),
            pltpu.VMEM((rows, O_DT), F32),
            pltpu.VMEM((SSM_HEADS, rows), F32),
            pltpu.VMEM((SSM_HEADS, rows), F32),
            pltpu.VMEM((rows, D_MODEL), F32),
            pltpu.VMEM((rows, D_MODEL), F32),
            pltpu.VMEM((rows, D_MODEL), BF16),
            pltpu.VMEM((SUBLANES, D_SCONV), F32),
            pltpu.VMEM((SUBLANES, D_XBC), F32),
            pltpu.VMEM((SSM_STATE, D_SSM), F32),
        ],
        compiler_params=pltpu.CompilerParams(dimension_semantics=("arbitrary",), vmem_limit_bytes=VMEM_LIMIT),
        name="prompt_mixer",
    )(x_prompt, *ins)


def _smix_kernel(x_ref, sc_prev_ref, xbc_prev_ref, state_in_ref, npre_ref, w_main_ref, w_dt_ref,
                 sconv_w_ref, sconv_norm_ref, xconv_w_ref, xconv_b_ref,
                 dt_bias_ref, a_log_ref, d_full_ref, ssm_norm_ref, w_out_ref, npost_ref,
                 h1_ref, u_ref, xbc_ref, state_ref, *, dec_seq):
    t = SEQ_TILE * dec_seq
    h = x_ref[...]
    hn = _rms(h, npre_ref[...]).astype(BF16)

    def proj(off, width):
        return _dot(hn, w_main_ref[0, :, off:off + width])

    ya, u = _sconv_mixer(proj(O_GATE_B, D_SCONV), proj(O_GATE_C, D_SCONV), proj(O_HV, D_SCONV),
                         sc_prev_ref[...], sconv_w_ref[...], sconv_norm_ref[...], dec_seq)
    for s in range(SEQ_TILE):
        u_ref[s] = u[(s + 1) * dec_seq - (SCONV_WIDTH - 1):(s + 1) * dec_seq, :]
    xbc = proj(O_XS, D_XBC)
    for s in range(SEQ_TILE):
        xbc_ref[s] = xbc[(s + 1) * dec_seq - (SSM_CONV_WIDTH - 1):(s + 1) * dec_seq, :]
    xbc_c = _silu(_causal_conv(xbc, xbc_prev_ref[...], xconv_w_ref[...], dec_seq) + xconv_b_ref[...])
    xs = xbc_c[:, 0:D_SSM]
    bm = xbc_c[:, D_SSM:D_SSM + SSM_GROUPS * SSM_STATE]
    bm_b = bm.astype(BF16)
    cm = xbc_c[:, D_SSM + SSM_GROUPS * SSM_STATE:D_XBC]

    row = lax.broadcasted_iota(jnp.int32, (t, t), 0)
    col = lax.broadcasted_iota(jnp.int32, (t, t), 1)
    same_seq = row // dec_seq == col // dec_seq
    mask = jnp.logical_and(row >= col, same_seq)
    tri = jnp.where(mask, 1.0, 0.0).astype(BF16)
    seq_ones = jnp.where(same_seq, 1.0, 0.0).astype(BF16)
    a_cum, w, a_cum_t, dt_t, a_total_t = _ssd_prepare_rows(_dot_nt(w_dt_ref[...], hn), dt_bias_ref[...],
                                                           a_log_ref[...], tri, seq_ones)
    y, xw_b, ea_full, _ = _ssd_heads(lambda p: xs[:, p * LANES:(p + 1) * LANES], bm_b, cm, a_cum, w, a_cum_t,
                                     dt_t, mask)

    row_id = lax.broadcasted_iota(jnp.int32, (t, SSM_STATE), 0) // dec_seq
    seq_decay_t = jnp.exp2(a_total_t)
    y_off_rows = []
    for s in range(SEQ_TILE):
        r0 = s * dec_seq
        q_t = jnp.broadcast_to(seq_decay_t[:, r0:r0 + 1], (SSM_HEADS, SSM_STATE))
        y_off_g = []
        for g in range(SSM_GROUPS):
            rows = slice(g * D_GROUP, (g + 1) * D_GROUP)
            s_prev = state_in_ref[s, rows, :]
            cm_s = cm[r0:r0 + dec_seq, g * SSM_STATE:(g + 1) * SSM_STATE].astype(BF16)
            y_off_g.append(_dot_nt(cm_s, s_prev.astype(BF16)))
            bm_g = jnp.where(row_id == s, bm[:, g * SSM_STATE:(g + 1) * SSM_STATE], 0.0).astype(BF16)
            decay = jnp.concatenate(
                [jnp.broadcast_to(q_t[e:e + 1, :], (SSM_HEAD_DIM, SSM_STATE))
                 for e in range(g * HEADS_PER_GROUP, (g + 1) * HEADS_PER_GROUP)], axis=0)
            state_ref[s, rows, :] = decay * s_prev + _dot_tn(xw_b[:, rows], bm_g)
        y_off_rows.append(jnp.concatenate(y_off_g, axis=1))
    y = y + jnp.concatenate(y_off_rows, axis=0) * ea_full
    yb = _ssm_tail(y, xs, proj(O_Z, D_SSM), d_full_ref[...], ssm_norm_ref[...])
    mix = _dot(ya.astype(BF16), w_out_ref[0:D_SCONV, :]) + _dot(yb.astype(BF16), w_out_ref[D_SCONV:D_MIX, :])
    h1_ref[...] = h + _rms(mix, npost_ref[...])


def _smix(x_rows, sc_prev, xbc_prev, state, consts, dec_seq):
    n_seq = state.shape[0]
    t = SEQ_TILE * dec_seq
    const_spec = lambda a: (pl.BlockSpec(a.shape, lambda i: (0, 0), pipeline_mode=pl.Buffered(1)) if a.ndim == 2
                            else _w_main_spec(lambda i: (0, 0, 0), pipeline_mode=pl.Buffered(1)))
    row_spec = lambda width: pl.BlockSpec((t, width), lambda i: (i, 0))
    state_spec = pl.BlockSpec((SEQ_TILE, D_SSM, SSM_STATE), lambda i: (i, 0, 0))
    return pl.pallas_call(
        functools.partial(_smix_kernel, dec_seq=dec_seq),
        grid=(n_seq // SEQ_TILE,),
        in_specs=[row_spec(D_MODEL), row_spec(D_SCONV), row_spec(D_XBC), state_spec]
        + [const_spec(a) for a in consts],
        out_specs=[row_spec(D_MODEL),
                   pl.BlockSpec((SEQ_TILE, SCONV_WIDTH - 1, D_SCONV), lambda i: (i, 0, 0)),
                   pl.BlockSpec((SEQ_TILE, SSM_CONV_WIDTH - 1, D_XBC), lambda i: (i, 0, 0)), state_spec],
        out_shape=[
            jax.ShapeDtypeStruct((n_seq * dec_seq, D_MODEL), F32),
            jax.ShapeDtypeStruct((n_seq, SCONV_WIDTH - 1, D_SCONV), F32),
            jax.ShapeDtypeStruct((n_seq, SSM_CONV_WIDTH - 1, D_XBC), F32),
            jax.ShapeDtypeStruct(state.shape, F32),
        ],
        compiler_params=pltpu.CompilerParams(dimension_semantics=("arbitrary",), vmem_limit_bytes=VMEM_LIMIT),
        name="sample_mixer",
    )(x_rows, sc_prev, xbc_prev, state, *consts)


def _ffn_kernel(hp_ref, hs_ref, npre_ref, w_gate_ref, w_up_ref, w_down_ref, npost_ref, op_ref, os_ref, *,
                prompt_steps):
    def body(h_ref, o_ref):
        h1 = h_ref[...]
        fn = _rms(h1, npre_ref[...]).astype(BF16)
        act = (_silu(_dot(fn, w_gate_ref[...])) * _dot(fn, w_up_ref[...])).astype(BF16)
        o_ref[...] = h1 + _rms(_dot(act, w_down_ref[...]), npost_ref[...])

    step = pl.program_id(0)
    pl.when(step < prompt_steps)(functools.partial(body, hp_ref, op_ref))
    pl.when(step >= prompt_steps)(functools.partial(body, hs_ref, os_ref))


def _ffn(h_prompt, h_sample, ffn_consts):
    mp, ms = h_prompt.shape[0], h_sample.shape[0]
    rows = FFN_ROWS
    p_steps, s_steps = mp // rows, ms // rows
    p_spec = pl.BlockSpec((rows, D_MODEL), lambda i: (jnp.minimum(i, p_steps - 1), 0))
    s_spec = pl.BlockSpec((rows, D_MODEL), lambda i: (jnp.maximum(i - p_steps, 0), 0))
    const_spec = lambda a: pl.BlockSpec(a.shape, lambda i: (0, 0), pipeline_mode=pl.Buffered(1))
    return pl.pallas_call(
        functools.partial(_ffn_kernel, prompt_steps=p_steps),
        grid=(p_steps + s_steps,),
        in_specs=[p_spec, s_spec] + [const_spec(a) for a in ffn_consts],
        out_specs=[p_spec, s_spec],
        out_shape=[jax.ShapeDtypeStruct((mp, D_MODEL), F32), jax.ShapeDtypeStruct((ms, D_MODEL), F32)],
        compiler_params=pltpu.CompilerParams(dimension_semantics=("arbitrary",), vmem_limit_bytes=VMEM_LIMIT),
        name="ffn",
    )(h_prompt, h_sample, *ffn_consts)


def _row(v, width=None):
    v = v.reshape(1, -1).astype(F32)
    if width is not None and v.shape[1] < width:
        v = jnp.pad(v, ((0, 0), (0, width - v.shape[1])))
    return v


def _layer(hp, hs, sc_state, xbc_state, ssm_state, meta_tokens,
           norm_mix_pre, norm_mix_post, w_in_all, sconv_w, sconv_norm, ssm_conv_w, ssm_conv_b,
           dt_bias, a_log, d_skip, ssm_norm, w_out, norm_ffn_pre, norm_ffn_post, w_gate, w_up, w_down):
    batch, seq, _ = hp.shape
    w_in = w_in_all[0]
    n_seq, dec_seq, _ = hs.shape

    w_main_b = w_in_all.astype(BF16)
    w_dt_b = jnp.pad(w_in[:, O_DT:D_IN], ((0, 0), (0, D_IN_PAD - D_IN))).astype(BF16)
    w_dt_t = w_in[:, O_DT:D_IN].T.astype(BF16)
    w_out_b = w_out.astype(BF16)
    w_gate_b, w_up_b, w_down_b = w_gate.astype(BF16), w_up.astype(BF16), w_down.astype(BF16)
    npre, npost = _row(norm_mix_pre), _row(norm_mix_post)
    fpre, fpost = _row(norm_ffn_pre), _row(norm_ffn_post)
    sconv_w = sconv_w.astype(F32)
    xconv_w = ssm_conv_w.astype(F32)
    xconv_b = _row(ssm_conv_b)
    dt_bias_p, a_log_p = _row(dt_bias, LANES), _row(a_log, LANES)
    dt_bias_t = jnp.broadcast_to(dt_bias.astype(F32)[:, None], (SSM_HEADS, LANES))
    a_log_t = jnp.broadcast_to(a_log.astype(F32)[:, None], (SSM_HEADS, LANES))
    d_full = jnp.repeat(d_skip.astype(F32), SSM_HEAD_DIM).reshape(1, D_SSM)
    sconv_norm_r, ssm_norm_r = _row(sconv_norm), _row(ssm_norm)

    r = jnp.arange(CHUNK)
    tri = (r[:, None] >= r[None, :]).astype(BF16)
    expand = (jnp.arange(LANES)[:, None] == (jnp.arange(D_SSM)[None, :] // SSM_HEAD_DIM)).astype(BF16)

    mixer_consts = (sconv_w, sconv_norm_r, xconv_w, xconv_b, dt_bias_t, a_log_t, d_full, ssm_norm_r)
    ffn_consts = (fpre, w_gate_b, w_up_b, w_down_b, fpost)

    proj_meta = _inproj(meta_tokens.astype(F32), npre, w_main_b, w_dt_b)
    sc0, xbc0, state0_t = _meta_state(proj_meta, xconv_w, xconv_b, dt_bias_p, a_log_p, tri, expand)

    pm_consts = (npre, w_main_b, w_dt_t) + mixer_consts + (tri, w_out_b, npost)
    h1, new_sc_p, new_xbc_p, p_state = _pmix(hp, pm_consts, sc0, xbc0, state0_t)
    new_ssm_p = p_state.reshape(batch, SSM_GROUPS, HEADS_PER_GROUP, SSM_HEAD_DIM, SSM_STATE)

    hs_rows = hs.reshape(n_seq * dec_seq, D_MODEL)
    sc_prev = jnp.pad(sc_state, ((0, 0), (dec_seq - (SCONV_WIDTH - 1), 0), (0, 0))).reshape(n_seq * dec_seq, D_SCONV)
    xbc_prev = jnp.pad(xbc_state, ((0, 0), (dec_seq - (SSM_CONV_WIDTH - 1), 0), (0, 0))).reshape(n_seq * dec_seq, D_XBC)
    sm_consts = (npre, w_main_b, w_dt_t) + mixer_consts + (w_out_b, npost)
    h1_s, new_sc_s, new_xbc_s, s_state = _smix(hs_rows, sc_prev, xbc_prev, ssm_state.reshape(n_seq, D_SSM, SSM_STATE),
                                      sm_consts, dec_seq)

    y_prompt, y_sample = _ffn(h1.reshape(batch * seq, D_MODEL), h1_s, ffn_consts)
    y_prompt = y_prompt.reshape(batch, seq, D_MODEL)
    y_sample = y_sample.reshape(n_seq, dec_seq, D_MODEL)
    new_ssm_s = s_state.reshape(n_seq, SSM_GROUPS, HEADS_PER_GROUP, SSM_HEAD_DIM, SSM_STATE)
    return y_prompt, y_sample, new_sc_p, new_xbc_p, new_ssm_p, new_sc_s, new_xbc_s, new_ssm_s


def kernel(x_prompt, x_sample, state_sconv, state_ssm_conv, state_ssm, meta_tokens, norm_mix_pre, norm_mix_post, w_in, sconv_w, sconv_norm, ssm_conv_w, ssm_conv_b, dt_bias, A_log, D_skip, ssm_norm, w_out, norm_ffn_pre, norm_ffn_post, w_gate, w_up, w_down):
    depth = w_in.shape[0]
    assert depth == 1 and x_sample.shape[1] == SUBLANES
    outs = _layer(x_prompt, x_sample, state_sconv[0], state_ssm_conv[0], state_ssm[0], meta_tokens,
                  norm_mix_pre[0], norm_mix_post[0], w_in, sconv_w[0], sconv_norm[0], ssm_conv_w[0],
                  ssm_conv_b[0], dt_bias[0], A_log[0], D_skip[0], ssm_norm[0], w_out[0],
                  norm_ffn_pre[0], norm_ffn_post[0], w_gate[0], w_up[0], w_down[0])
    y_prompt, y_sample = outs[0], outs[1]
    return (y_prompt, y_sample) + tuple(o[None] for o in outs[2:])
```

```python
import functools

import jax
import jax.numpy as jnp
from jax import lax
from jax.experimental import pallas as pl
from jax.experimental.pallas import tpu as pltpu

D_MODEL = 1024
N_META = 16
D_SCONV = 512
SCONV_GROUPS = 8
SCONV_WIDTH = 3
SSM_HEAD_DIM = 64
SSM_HEADS = 16
D_SSM = 1024
SSM_GROUPS = 2
HEADS_PER_GROUP = 8
D_GROUP = HEADS_PER_GROUP * SSM_HEAD_DIM
SSM_STATE = 128
SSM_CONV_WIDTH = 4
D_BC = 2 * SSM_GROUPS * SSM_STATE
D_XBC = D_SSM + D_BC
D_MIX = D_SCONV + D_SSM
D_IN = 3 * D_SCONV + D_SSM + D_XBC + SSM_HEADS
D_FF = 2816
EPS = 1e-6
LOG2_E = 1.4426950408889634

LANES = 128
SUBLANES = 8
D_IN_PAD = D_IN + (LANES - SSM_HEADS)
HEAD_PAIRS = D_SSM // LANES

O_GATE_B = 0
O_GATE_C = D_SCONV
O_HV = 2 * D_SCONV
O_Z = 3 * D_SCONV
O_XS = O_Z + D_SSM
O_BC = O_XS + D_SSM
O_DT = O_BC + D_BC

CHUNK = 128
TILE_CHUNKS = 2
STAGE_A_COLS = tuple(range(0, O_DT + 1, D_SCONV))
STAGE_A_PLAN = ((0, "sconv"), (0, "prepare"), (0, 1), (0, 3), (0, 5), (1, "prepare"), (1, 1), (1, 3), (1, 5))
SEQ_TILE = 16
FFN_ROWS = 512
FFN_SUB_ROWS = 256
VMEM_LIMIT = 60000 * 1024

BF16 = jnp.bfloat16
F32 = jnp.float32


def _dot(a, b):
    return jnp.dot(a, b, preferred_element_type=F32)


def _dot_nt(a, b):
    return lax.dot_general(a, b, (((1,), (1,)), ((), ())), preferred_element_type=F32)


def _dot_tn(a, b):
    return lax.dot_general(a, b, (((0,), (0,)), ((), ())), preferred_element_type=F32)


def _split3(v):
    hi = v.astype(BF16)
    r1 = v - hi.astype(F32)
    mid = r1.astype(BF16)
    lo = (r1 - mid.astype(F32)).astype(BF16)
    return hi, mid, lo


def _dot3(v, m):
    hi, mid, lo = _split3(v)
    return _dot(hi, m) + _dot(mid, m) + _dot(lo, m)


def _dot3_left(m, v):
    hi, mid, lo = _split3(v)
    return _dot(m, hi) + _dot(m, mid) + _dot(m, lo)


def _rms(x, w):
    return x * lax.rsqrt(jnp.mean(x * x, axis=-1, keepdims=True) + EPS) * w


def _silu(x):
    hx = 0.5 * x
    return hx + hx * jnp.tanh(hx)


def _softplus(x):
    return jnp.maximum(x, 0.0) + jnp.log1p(jnp.exp(-jnp.abs(x)))


def _shifted(u, prev, j, rows_per_seq):
    t = u.shape[0]
    s = pltpu.roll(u, j, 0)
    if rows_per_seq is None:
        row = lax.broadcasted_iota(jnp.int32, (SUBLANES, u.shape[1]), 0)
        first = jnp.where(row < j, pltpu.roll(prev, j, 0), s[0:SUBLANES])
        if t == SUBLANES:
            return first
        return jnp.concatenate([first, s[SUBLANES:]], axis=0)
    row = lax.broadcasted_iota(jnp.int32, u.shape, 0) % SUBLANES
    return jnp.where(row < j, pltpu.roll(prev, t - SUBLANES + j, 0), s)


def _causal_conv(u, prev, w, rows_per_seq):
    k = w.shape[0]
    out = _shifted(u, prev, k - 1, rows_per_seq) * w[0:1]
    for i in range(1, k - 1):
        out = out + _shifted(u, prev, k - 1 - i, rows_per_seq) * w[i:i + 1]
    return out + u * w[k - 1:k]


def _sconv_mixer(gate_b, gate_c, hv, prev, sconv_w, sconv_norm, rows_per_seq):
    u = gate_c * hv
    conv_u = _causal_conv(u, prev, sconv_w, rows_per_seq)
    return _sconv_norm(gate_b, conv_u, sconv_norm), u


def _sconv_norm(gate_b, conv_u, sconv_norm):
    v = gate_b * conv_u
    first_half = lax.broadcasted_iota(jnp.int32, (v.shape[0], LANES), 1) < D_SCONV // SCONV_GROUPS
    parts = []
    for c in range(0, D_SCONV, LANES):
        vc = v[:, c:c + LANES]
        sq = vc * vc
        s_first = jnp.sum(jnp.where(first_half, sq, 0.0), axis=-1, keepdims=True)
        s_second = jnp.sum(jnp.where(first_half, 0.0, sq), axis=-1, keepdims=True)
        mean_sq = jnp.where(first_half, s_first, s_second) * (SCONV_GROUPS / D_SCONV)
        parts.append(vc * lax.rsqrt(mean_sq + EPS))
    return jnp.concatenate(parts, axis=1) * sconv_norm


def _ssd_prepare(dt_raw, dt_bias, a_log, tri, t):
    lane = lax.broadcasted_iota(jnp.int32, (t, LANES), 1)
    valid = lane < SSM_HEADS
    dt = jnp.where(valid, _softplus(dt_raw + dt_bias), 0.0)
    a = dt * (-jnp.exp(a_log))
    a_cum = _dot3_left(tri, a)
    return dt, a_cum


def _ssd_prepare_rows(dt_raw_t, dt_bias_t, a_log_t, tri, seq_ones=None):
    t = dt_raw_t.shape[1]
    dt_t = _softplus(dt_raw_t + dt_bias_t)
    a_t = dt_t * (-(jnp.exp(a_log_t) * LOG2_E))
    hi, mid, lo = _split3(a_t)
    a_cum_t = _dot_nt(hi, tri) + _dot_nt(mid, tri) + _dot_nt(lo, tri)
    if seq_ones is None:
        a_total_t = jnp.broadcast_to(a_cum_t[:, t - 1:t], (SSM_HEADS, t))
    else:
        a_total_t = _dot_nt(hi, seq_ones) + _dot_nt(mid, seq_ones) + _dot_nt(lo, seq_ones)
    w_t = jnp.exp2(a_total_t - a_cum_t) * dt_t
    pad = jnp.zeros((LANES - SSM_HEADS, t), F32)
    a_cum = jnp.concatenate([a_cum_t, pad], axis=0).T
    w = jnp.concatenate([w_t, pad], axis=0).T
    return a_cum, w, a_cum_t, dt_t, a_total_t


def _group_rms_ssm(y, w):
    parts = []
    for g in range(SSM_GROUPS):
        yg = y[:, g * D_GROUP:(g + 1) * D_GROUP]
        parts.append(yg * lax.rsqrt(jnp.mean(yg * yg, axis=-1, keepdims=True) + EPS))
    return jnp.concatenate(parts, axis=1) * w


def _ssd_heads(xs_of, bm_b, cm, a_cum, w, a_cum_t, dt_t, mask, state_t=None, gate=None, after_pair=None):
    t = a_cum.shape[0]
    assert t == LANES
    first_half = lax.broadcasted_iota(jnp.int32, (t, LANES), 1) < SSM_HEAD_DIM
    y_pairs, xw_pairs, ea_pairs, xs_pairs = [], [], [], []
    for g in range(SSM_GROUPS):
        cm_g = cm[:, g * SSM_STATE:(g + 1) * SSM_STATE]
        cb = _dot_nt(cm_g.astype(BF16), bm_b[:, g * SSM_STATE:(g + 1) * SSM_STATE])
        for q in range(HEADS_PER_GROUP // 2):
            pair = g * (HEADS_PER_GROUP // 2) + q
            cols = slice(pair * LANES, (pair + 1) * LANES)
            xs_p = xs_of(pair)
            rhs = xs_p.astype(BF16)
            if state_t is not None:
                rhs = jnp.concatenate([rhs, state_t[:, cols].astype(BF16)], axis=0)
            res, eas, ws = [], [], []
            for e in (2 * pair, 2 * pair + 1):
                a_col = jnp.broadcast_to(a_cum[:, e:e + 1], (t, LANES))
                decay = jnp.exp2(jnp.where(mask, a_col - a_cum_t[e:e + 1, :], -jnp.inf))
                lhs = (cb * decay * dt_t[e:e + 1, :]).astype(BF16)
                ea = jnp.exp2(a_col)
                if state_t is not None:
                    lhs = jnp.concatenate([lhs, (cm_g * ea).astype(BF16)], axis=1)
                res.append(_dot(lhs, rhs))
                eas.append(ea)
                ws.append(jnp.broadcast_to(w[:, e:e + 1], (t, LANES)))
            y_p = jnp.where(first_half, res[0], res[1])
            y_pairs.append(y_p if gate is None else gate(pair, y_p, xs_p))
            xw_pairs.append((xs_p * jnp.where(first_half, ws[0], ws[1])).astype(BF16))
            ea_pairs.append(jnp.where(first_half, eas[0], eas[1]))
            xs_pairs.append(xs_p)
            if after_pair is not None:
                after_pair(pair, y_pairs, xw_pairs, ea_pairs)
    return (jnp.concatenate(y_pairs, axis=1), jnp.concatenate(xw_pairs, axis=1),
            jnp.concatenate(ea_pairs, axis=1), jnp.concatenate(xs_pairs, axis=1))


def _ssm_tail(y, xs, z, d_full, ssm_norm):
    y = y + d_full * xs
    y = y * _silu(z)
    return _group_rms_ssm(y, ssm_norm)


def _w_main_spec(index_map, **kwargs):
    return pl.BlockSpec((1, D_MODEL, O_DT), index_map, **kwargs)


def _inproj_kernel(x_ref, npre_ref, w_main_ref, w_dt_ref, o_ref):
    hn = _rms(x_ref[...], npre_ref[...]).astype(BF16)
    o_ref[:, 0:O_DT] = _dot(hn, w_main_ref[0])
    o_ref[:, O_DT:D_IN_PAD] = _dot(hn, w_dt_ref[...])


def _inproj(x, npre, w_main_b, w_dt_b):
    m = x.shape[0]
    once = lambda shape: pl.BlockSpec(shape, lambda i: (0, 0), pipeline_mode=pl.Buffered(1))
    return pl.pallas_call(
        _inproj_kernel,
        grid=(1,),
        in_specs=[once((m, D_MODEL)), once((1, D_MODEL)),
                  _w_main_spec(lambda i: (0, 0, 0), pipeline_mode=pl.Buffered(1)), once((D_MODEL, LANES))],
        out_specs=once((m, D_IN_PAD)),
        out_shape=jax.ShapeDtypeStruct((m, D_IN_PAD), F32),
        compiler_params=pltpu.CompilerParams(dimension_semantics=("arbitrary",), vmem_limit_bytes=VMEM_LIMIT),
        name="inproj",
    )(x, npre, w_main_b, w_dt_b)


def _meta_kernel(proj_ref, xconv_w_ref, xconv_b_ref, dt_bias_ref, a_log_ref, tri_ref, expand_ref,
                 sc_tail_ref, xbc_tail_ref, state_t_ref):
    t = N_META
    proj = proj_ref[...]
    u = proj[:, O_GATE_C:O_GATE_C + D_SCONV] * proj[:, O_HV:O_HV + D_SCONV]
    sc_tail_ref[...] = u[t - SUBLANES:t]
    xbc = proj[:, O_XS:O_XS + D_XBC]
    xbc_tail_ref[...] = xbc[t - SUBLANES:t]
    zeros_prev = jnp.zeros((SUBLANES, D_XBC), F32)
    xbc_c = _silu(_causal_conv(xbc, zeros_prev, xconv_w_ref[...], None) + xconv_b_ref[...])
    xs = xbc_c[:, 0:D_SSM]
    dt, a_cum = _ssd_prepare(proj[:, O_DT:O_DT + LANES], dt_bias_ref[...], a_log_ref[...],
                             tri_ref[0:t, 0:t], t)
    w = jnp.exp(a_cum[t - 1:t, :] - a_cum) * dt
    xw_b = (xs * _dot3(w, expand_ref[...])).astype(BF16)
    for g in range(SSM_GROUPS):
        bm_b = xbc_c[:, D_SSM + g * SSM_STATE:D_SSM + (g + 1) * SSM_STATE].astype(BF16)
        state_t_ref[:, g * D_GROUP:(g + 1) * D_GROUP] = _dot_tn(bm_b, xw_b[:, g * D_GROUP:(g + 1) * D_GROUP])


def _meta_state(proj_meta, xconv_w, xconv_b, dt_bias, a_log, tri, expand):
    full = lambda shape: pl.BlockSpec(shape, lambda: tuple(0 for _ in shape))
    return pl.pallas_call(
        _meta_kernel,
        in_specs=[full(proj_meta.shape), full(xconv_w.shape), full(xconv_b.shape), full(dt_bias.shape),
                  full(a_log.shape), full(tri.shape), full(expand.shape)],
        out_specs=[full((SUBLANES, D_SCONV)), full((SUBLANES, D_XBC)), full((SSM_STATE, D_SSM))],
        out_shape=[jax.ShapeDtypeStruct((SUBLANES, D_SCONV), F32),
                   jax.ShapeDtypeStruct((SUBLANES, D_XBC), F32),
                   jax.ShapeDtypeStruct((SSM_STATE, D_SSM), F32)],
        name="meta_state",
    )(proj_meta, xconv_w, xconv_b, dt_bias, a_log, tri, expand)


def _pmix_kernel(x_ref, npre_ref, w_main_ref, w_dt_ref, sconv_w_ref, sconv_norm_ref,
                 xconv_w_ref, xconv_b_ref, dt_bias_ref, a_log_ref, d_full_ref, ssm_norm_ref,
                 tri_ref, w_out_ref, npost_ref, sc0_ref, xbc0_ref, state0_t_ref,
                 h1_ref, sc_out_ref, xbc_out_ref, state_ref,
                 proj_s0, proj_s1, dt_s0, dt_s1, h_s0, h_s1, hn_s, sc_tail, xbc_tail, state_t_ref, *,
                 steps_per_seq):
    rows = TILE_CHUNKS * CHUNK
    k = pl.program_id(0)
    step_in_seq = lax.rem(jnp.maximum(k - 1, 0), steps_per_seq)

    @pl.when(k == 0)
    def _():
        proj_s1[...] = jnp.zeros((rows, O_DT), F32)
        dt_s1[...] = jnp.zeros((SSM_HEADS, rows), F32)
        h_s1[...] = jnp.zeros((rows, D_MODEL), F32)

    @pl.when(step_in_seq == 0)
    def _():
        sc_tail[...] = sc0_ref[...]
        xbc_tail[...] = xbc0_ref[...]
        state_t_ref[...] = state0_t_ref[...]

    for parity in (0, 1):
        pl.when(lax.rem(k, 2) == parity)(functools.partial(
            _pmix_step, parity, x_ref, npre_ref, w_main_ref, w_dt_ref, sconv_w_ref, sconv_norm_ref,
            xconv_w_ref, xconv_b_ref, dt_bias_ref, a_log_ref, d_full_ref, ssm_norm_ref, tri_ref, w_out_ref,
            npost_ref, h1_ref, (proj_s0, proj_s1), (dt_s0, dt_s1), (h_s0, h_s1), hn_s, sc_tail, xbc_tail,
            state_t_ref))

    @pl.when(step_in_seq == steps_per_seq - 1)
    def _():
        sc_out_ref[0] = sc_tail[SUBLANES - (SCONV_WIDTH - 1):SUBLANES, :]
        xbc_out_ref[0] = xbc_tail[SUBLANES - (SSM_CONV_WIDTH - 1):SUBLANES, :]
        for p in range(HEAD_PAIRS):
            state_ref[0, p * LANES:(p + 1) * LANES, :] = state_t_ref[:, p * LANES:(p + 1) * LANES].T


def _pmix_step(slot_a, x_ref, npre_ref, w_main_ref, w_dt_ref, sconv_w_ref, sconv_norm_ref,
               xconv_w_ref, xconv_b_ref, dt_bias_ref, a_log_ref, d_full_ref, ssm_norm_ref, tri_ref, w_out_ref,
               npost_ref, h1_ref, proj_s, dt_s, h_s, hn_s, sc_tail, xbc_tail, state_t_ref):
    t = CHUNK
    slot_b = 1 - slot_a

    h_a = x_ref[0]
    h_s[slot_a][...] = h_a
    hn_s[...] = _rms(h_a, npre_ref[...]).astype(BF16)

    def stage_a(point):
        for piece, planned in enumerate(STAGE_A_PLAN):
            if planned == point:
                if piece + 1 < len(STAGE_A_COLS):
                    off, end = STAGE_A_COLS[piece], STAGE_A_COLS[piece + 1]
                    proj_s[slot_a][:, off:end] = _dot(hn_s[...], w_main_ref[0, :, off:end])
                else:
                    dt_s[slot_a][...] = _dot_nt(w_dt_ref[...], hn_s[...])

    for sub in range(TILE_CHUNKS):
        r = slice(sub * t, (sub + 1) * t)

        def proj(off, width):
            return proj_s[slot_b][r, off:off + width]

        ya, u = _sconv_mixer(proj(O_GATE_B, D_SCONV), proj(O_GATE_C, D_SCONV), proj(O_HV, D_SCONV),
                             sc_tail[...], sconv_w_ref[...], sconv_norm_ref[...], None)
        sc_tail[...] = u[t - SUBLANES:t]
        stage_a((sub, "sconv"))
        mix_ya = _dot(ya.astype(BF16), w_out_ref[0:D_SCONV, :])

        def xconv(col, width):
            cols = slice(col, col + width)
            raw = proj(O_XS + col, width)
            out = _silu(_causal_conv(raw, xbc_tail[:, cols], xconv_w_ref[:, cols], None) + xconv_b_ref[:, cols])
            xbc_tail[:, cols] = raw[t - SUBLANES:t]
            return out

        bc = xconv(D_SSM, D_BC)
        bm_b = bc[:, 0:SSM_GROUPS * SSM_STATE].astype(BF16)
        cm = bc[:, SSM_GROUPS * SSM_STATE:D_BC]
        a_cum, w, a_cum_t, dt_t, _ = _ssd_prepare_rows(dt_s[slot_b][:, r], dt_bias_ref[...], a_log_ref[...],
                                                       tri_ref[...])
        causal = lax.broadcasted_iota(jnp.int32, (t, t), 0) >= lax.broadcasted_iota(jnp.int32, (t, t), 1)
        stage_a((sub, "prepare"))

        def gate(p, y_p, xs_p):
            cols = slice(p * LANES, (p + 1) * LANES)
            return (y_p + d_full_ref[:, cols] * xs_p) * _silu(proj(O_Z + p * LANES, LANES))

        y, xw_b, ea_full, _ = _ssd_heads(lambda p: xconv(p * LANES, LANES), bm_b, cm, a_cum, w, a_cum_t, dt_t,
                                         causal, state_t_ref, gate, lambda p, *partial: stage_a((sub, p)))
        chunk_decay = ea_full[t - 1:t, :]
        for g in range(SSM_GROUPS):
            cols = slice(g * D_GROUP, (g + 1) * D_GROUP)
            state_t_ref[:, cols] = chunk_decay[:, cols] * state_t_ref[:, cols] + _dot_tn(
                bm_b[:, g * SSM_STATE:(g + 1) * SSM_STATE], xw_b[:, cols])
        yb = _group_rms_ssm(y, ssm_norm_ref[...])
        mix = mix_ya + _dot(yb.astype(BF16), w_out_ref[D_SCONV:D_MIX, :])
        stage_a((sub, "end"))
        h1_ref[0, r, :] = h_s[slot_b][r, :] + _rms(mix, npost_ref[...])


def _pmix(x_prompt, consts, sc0, xbc0, state0_t):
    batch, seq, _ = x_prompt.shape
    rows = TILE_CHUNKS * CHUNK
    sps = seq // rows
    n_tiles = batch * sps
    const_spec = lambda a: (pl.BlockSpec(a.shape, lambda k: (0, 0)) if a.ndim == 2
                            else _w_main_spec(lambda k: (0, 0, 0)))
    ins = list(consts) + [sc0, xbc0, state0_t]

    def tile_a(k):
        tile = jnp.minimum(k, n_tiles - 1)
        return (tile // sps, tile % sps, 0)

    def tile_b(k):
        tile = jnp.maximum(k - 1, 0)
        return (tile // sps, tile % sps, 0)

    def seq_b(k):
        return (jnp.maximum(k - 1, 0) // sps, 0, 0)

    return pl.pallas_call(
        functools.partial(_pmix_kernel, steps_per_seq=sps),
        grid=(n_tiles + 1,),
        in_specs=[pl.BlockSpec((1, rows, D_MODEL), tile_a)] + [const_spec(a) for a in ins],
        out_specs=[
            pl.BlockSpec((1, rows, D_MODEL), tile_b),
            pl.BlockSpec((1, SCONV_WIDTH - 1, D_SCONV), seq_b),
            pl.BlockSpec((1, SSM_CONV_WIDTH - 1, D_XBC), seq_b),
            pl.BlockSpec((1, D_SSM, SSM_STATE), seq_b),
        ],
        out_shape=[
            jax.ShapeDtypeStruct((batch, seq, D_MODEL), F32),
            jax.ShapeDtypeStruct((batch, SCONV_WIDTH - 1, D_SCONV), F32),
            jax.ShapeDtypeStruct((batch, SSM_CONV_WIDTH - 1, D_XBC), F32),
            jax.ShapeDtypeStruct((batch, D_SSM, SSM_STATE), F32),
        ],
        scratch_shapes=[
            pltpu.VMEM((rows, O_DT), F32),
            pltpu.VMEM((rows, O_DT), F32),
            pltpu.VMEM((SSM_HEADS, rows), F32),
            pltpu.VMEM((SSM_HEADS, rows), F32),
            pltpu.VMEM((rows, D_MODEL), F32),
            pltpu.VMEM((rows, D_MODEL), F32),
            pltpu.VMEM((rows, D_MODEL), BF16),
            pltpu.VMEM((SUBLANES, D_SCONV), F32),
            pltpu.VMEM((SUBLANES, D_XBC), F32),
            pltpu.VMEM((SSM_STATE, D_SSM), F32),
        ],
        compiler_params=pltpu.CompilerParams(dimension_semantics=("arbitrary",), vmem_limit_bytes=VMEM_LIMIT),
        name="prompt_mixer",
    )(x_prompt, *ins)


def _smix_kernel(x_ref, sc_prev_ref, xbc_prev_ref, state_in_ref, npre_ref, w_main_ref, w_dt_ref,
                 sconv_w_ref, sconv_norm_ref, xconv_w_ref, xconv_b_ref,
                 dt_bias_ref, a_log_ref, d_full_ref, ssm_norm_ref, w_out_ref, npost_ref,
                 h1_ref, u_ref, xbc_ref, state_ref, *, dec_seq):
    t = SEQ_TILE * dec_seq
    h = x_ref[...]
    hn = _rms(h, npre_ref[...]).astype(BF16)

    def proj(off, width):
        return _dot(hn, w_main_ref[0, :, off:off + width])

    ya, u = _sconv_mixer(proj(O_GATE_B, D_SCONV), proj(O_GATE_C, D_SCONV), proj(O_HV, D_SCONV),
                         sc_prev_ref[...], sconv_w_ref[...], sconv_norm_ref[...], dec_seq)
    for s in range(SEQ_TILE):
        u_ref[s] = u[(s + 1) * dec_seq - (SCONV_WIDTH - 1):(s + 1) * dec_seq, :]
    xbc = proj(O_XS, D_XBC)
    for s in range(SEQ_TILE):
        xbc_ref[s] = xbc[(s + 1) * dec_seq - (SSM_CONV_WIDTH - 1):(s + 1) * dec_seq, :]
    xbc_c = _silu(_causal_conv(xbc, xbc_prev_ref[...], xconv_w_ref[...], dec_seq) + xconv_b_ref[...])
    xs = xbc_c[:, 0:D_SSM]
    bm = xbc_c[:, D_SSM:D_SSM + SSM_GROUPS * SSM_STATE]
    bm_b = bm.astype(BF16)
    cm = xbc_c[:, D_SSM + SSM_GROUPS * SSM_STATE:D_XBC]

    row = lax.broadcasted_iota(jnp.int32, (t, t), 0)
    col = lax.broadcasted_iota(jnp.int32, (t, t), 1)
    same_seq = row // dec_seq == col // dec_seq
    mask = jnp.logical_and(row >= col, same_seq)
    tri = jnp.where(mask, 1.0, 0.0).astype(BF16)
    seq_ones = jnp.where(same_seq, 1.0, 0.0).astype(BF16)
    a_cum, w, a_cum_t, dt_t, a_total_t = _ssd_prepare_rows(_dot_nt(w_dt_ref[...], hn), dt_bias_ref[...],
                                                           a_log_ref[...], tri, seq_ones)
    y, xw_b, ea_full, _ = _ssd_heads(lambda p: xs[:, p * LANES:(p + 1) * LANES], bm_b, cm, a_cum, w, a_cum_t,
                                     dt_t, mask)

    row_id = lax.broadcasted_iota(jnp.int32, (t, SSM_STATE), 0) // dec_seq
    seq_decay_t = jnp.exp2(a_total_t)
    y_off_rows = []
    for s in range(SEQ_TILE):
        r0 = s * dec_seq
        q_t = jnp.broadcast_to(seq_decay_t[:, r0:r0 + 1], (SSM_HEADS, SSM_STATE))
        y_off_g = []
        for g in range(SSM_GROUPS):
            rows = slice(g * D_GROUP, (g + 1) * D_GROUP)
            s_prev = state_in_ref[s, rows, :]
            cm_s = cm[r0:r0 + dec_seq, g * SSM_STATE:(g + 1) * SSM_STATE].astype(BF16)
            y_off_g.append(_dot_nt(cm_s, s_prev.astype(BF16)))
            bm_g = jnp.where(row_id == s, bm[:, g * SSM_STATE:(g + 1) * SSM_STATE], 0.0).astype(BF16)
            decay = jnp.concatenate(
                [jnp.broadcast_to(q_t[e:e + 1, :], (SSM_HEAD_DIM, SSM_STATE))
                 for e in range(g * HEADS_PER_GROUP, (g + 1) * HEADS_PER_GROUP)], axis=0)
            state_ref[s, rows, :] = decay * s_prev + _dot_tn(xw_b[:, rows], bm_g)
        y_off_rows.append(jnp.concatenate(y_off_g, axis=1))
    y = y + jnp.concatenate(y_off_rows, axis=0) * ea_full
    yb = _ssm_tail(y, xs, proj(O_Z, D_SSM), d_full_ref[...], ssm_norm_ref[...])
    mix = _dot(ya.astype(BF16), w_out_ref[0:D_SCONV, :]) + _dot(yb.astype(BF16), w_out_ref[D_SCONV:D_MIX, :])
    h1_ref[...] = h + _rms(mix, npost_ref[...])


def _smix(x_rows, sc_prev, xbc_prev, state, consts, dec_seq):
    n_seq = state.shape[0]
    t = SEQ_TILE * dec_seq
    const_spec = lambda a: (pl.BlockSpec(a.shape, lambda i: (0, 0), pipeline_mode=pl.Buffered(1)) if a.ndim == 2
                            else _w_main_spec(lambda i: (0, 0, 0), pipeline_mode=pl.Buffered(1)))
    row_spec = lambda width: pl.BlockSpec((t, width), lambda i: (i, 0))
    state_spec = pl.BlockSpec((SEQ_TILE, D_SSM, SSM_STATE), lambda i: (i, 0, 0))
    return pl.pallas_call(
        functools.partial(_smix_kernel, dec_seq=dec_seq),
        grid=(n_seq // SEQ_TILE,),
        in_specs=[row_spec(D_MODEL), row_spec(D_SCONV), row_spec(D_XBC), state_spec]
        + [const_spec(a) for a in consts],
        out_specs=[row_spec(D_MODEL),
                   pl.BlockSpec((SEQ_TILE, SCONV_WIDTH - 1, D_SCONV), lambda i: (i, 0, 0)),
                   pl.BlockSpec((SEQ_TILE, SSM_CONV_WIDTH - 1, D_XBC), lambda i: (i, 0, 0)), state_spec],
        out_shape=[
            jax.ShapeDtypeStruct((n_seq * dec_seq, D_MODEL), F32),
            jax.ShapeDtypeStruct((n_seq, SCONV_WIDTH - 1, D_SCONV), F32),
            jax.ShapeDtypeStruct((n_seq, SSM_CONV_WIDTH - 1, D_XBC), F32),
            jax.ShapeDtypeStruct(state.shape, F32),
        ],
        compiler_params=pltpu.CompilerParams(dimension_semantics=("arbitrary",), vmem_limit_bytes=VMEM_LIMIT),
        name="sample_mixer",
    )(x_rows, sc_prev, xbc_prev, state, *consts)


def _ffn_kernel(hp_ref, hs_ref, npre_ref, w_gate_ref, w_up_ref, w_down_ref, npost_ref, op_ref, os_ref, *,
                prompt_steps):
    def body(h_ref, o_ref):
        for r0 in range(0, FFN_ROWS, FFN_SUB_ROWS):
            r = slice(r0, r0 + FFN_SUB_ROWS)
            h1 = h_ref[r, :]
            fn = _rms(h1, npre_ref[...]).astype(BF16)
            act = (_silu(_dot(fn, w_gate_ref[...])) * _dot(fn, w_up_ref[...])).astype(BF16)
            o_ref[r, :] = h1 + _rms(_dot(act, w_down_ref[...]), npost_ref[...])

    step = pl.program_id(0)
    pl.when(step < prompt_steps)(functools.partial(body, hp_ref, op_ref))
    pl.when(step >= prompt_steps)(functools.partial(body, hs_ref, os_ref))


def _ffn(h_prompt, h_sample, ffn_consts):
    mp, ms = h_prompt.shape[0], h_sample.shape[0]
    rows = FFN_ROWS
    p_steps, s_steps = mp // rows, ms // rows
    p_spec = pl.BlockSpec((rows, D_MODEL), lambda i: (jnp.minimum(i, p_steps - 1), 0))
    s_spec = pl.BlockSpec((rows, D_MODEL), lambda i: (jnp.maximum(i - p_steps, 0), 0))
    const_spec = lambda a: pl.BlockSpec(a.shape, lambda i: (0, 0), pipeline_mode=pl.Buffered(1))
    return pl.pallas_call(
        functools.partial(_ffn_kernel, prompt_steps=p_steps),
        grid=(p_steps + s_steps,),
        in_specs=[p_spec, s_spec] + [const_spec(a) for a in ffn_consts],
        out_specs=[p_spec, s_spec],
        out_shape=[jax.ShapeDtypeStruct((mp, D_MODEL), F32), jax.ShapeDtypeStruct((ms, D_MODEL), F32)],
        compiler_params=pltpu.CompilerParams(dimension_semantics=("arbitrary",), vmem_limit_bytes=VMEM_LIMIT),
        name="ffn",
    )(h_prompt, h_sample, *ffn_consts)


def _row(v, width=None):
    v = v.reshape(1, -1).astype(F32)
    if width is not None and v.shape[1] < width:
        v = jnp.pad(v, ((0, 0), (0, width - v.shape[1])))
    return v


def _layer(hp, hs, sc_state, xbc_state, ssm_state, meta_tokens,
           norm_mix_pre, norm_mix_post, w_in_all, sconv_w, sconv_norm, ssm_conv_w, ssm_conv_b,
           dt_bias, a_log, d_skip, ssm_norm, w_out, norm_ffn_pre, norm_ffn_post, w_gate, w_up, w_down):
    batch, seq, _ = hp.shape
    w_in = w_in_all[0]
    n_seq, dec_seq, _ = hs.shape

    w_main_b = w_in_all.astype(BF16)
    w_dt_b = jnp.pad(w_in[:, O_DT:D_IN], ((0, 0), (0, D_IN_PAD - D_IN))).astype(BF16)
    w_dt_t = w_in[:, O_DT:D_IN].T.astype(BF16)
    w_out_b = w_out.astype(BF16)
    w_gate_b, w_up_b, w_down_b = w_gate.astype(BF16), w_up.astype(BF16), w_down.astype(BF16)
    npre, npost = _row(norm_mix_pre), _row(norm_mix_post)
    fpre, fpost = _row(norm_ffn_pre), _row(norm_ffn_post)
    sconv_w = sconv_w.astype(F32)
    xconv_w = ssm_conv_w.astype(F32)
    xconv_b = _row(ssm_conv_b)
    dt_bias_p, a_log_p = _row(dt_bias, LANES), _row(a_log, LANES)
    dt_bias_t = jnp.broadcast_to(dt_bias.astype(F32)[:, None], (SSM_HEADS, LANES))
    a_log_t = jnp.broadcast_to(a_log.astype(F32)[:, None], (SSM_HEADS, LANES))
    d_full = jnp.repeat(d_skip.astype(F32), SSM_HEAD_DIM).reshape(1, D_SSM)
    sconv_norm_r, ssm_norm_r = _row(sconv_norm), _row(ssm_norm)

    r = jnp.arange(CHUNK)
    tri = (r[:, None] >= r[None, :]).astype(BF16)
    expand = (jnp.arange(LANES)[:, None] == (jnp.arange(D_SSM)[None, :] // SSM_HEAD_DIM)).astype(BF16)

    mixer_consts = (sconv_w, sconv_norm_r, xconv_w, xconv_b, dt_bias_t, a_log_t, d_full, ssm_norm_r)
    ffn_consts = (fpre, w_gate_b, w_up_b, w_down_b, fpost)

    proj_meta = _inproj(meta_tokens.astype(F32), npre, w_main_b, w_dt_b)
    sc0, xbc0, state0_t = _meta_state(proj_meta, xconv_w, xconv_b, dt_bias_p, a_log_p, tri, expand)

    pm_consts = (npre, w_main_b, w_dt_t) + mixer_consts + (tri, w_out_b, npost)
    h1, new_sc_p, new_xbc_p, p_state = _pmix(hp, pm_consts, sc0, xbc0, state0_t)
    new_ssm_p = p_state.reshape(batch, SSM_GROUPS, HEADS_PER_GROUP, SSM_HEAD_DIM, SSM_STATE)

    hs_rows = hs.reshape(n_seq * dec_seq, D_MODEL)
    sc_prev = jnp.pad(sc_state, ((0, 0), (dec_seq - (SCONV_WIDTH - 1), 0), (0, 0))).reshape(n_seq * dec_seq, D_SCONV)
    xbc_prev = jnp.pad(xbc_state, ((0, 0), (dec_seq - (SSM_CONV_WIDTH - 1), 0), (0, 0))).reshape(n_seq * dec_seq, D_XBC)
    sm_consts = (npre, w_main_b, w_dt_t) + mixer_consts + (w_out_b, npost)
    h1_s, new_sc_s, new_xbc_s, s_state = _smix(hs_rows, sc_prev, xbc_prev, ssm_state.reshape(n_seq, D_SSM, SSM_STATE),
                                      sm_consts, dec_seq)

    y_prompt, y_sample = _ffn(h1.reshape(batch * seq, D_MODEL), h1_s, ffn_consts)
    y_prompt = y_prompt.reshape(batch, seq, D_MODEL)
    y_sample = y_sample.reshape(n_seq, dec_seq, D_MODEL)
    new_ssm_s = s_state.reshape(n_seq, SSM_GROUPS, HEADS_PER_GROUP, SSM_HEAD_DIM, SSM_STATE)
    return y_prompt, y_sample, new_sc_p, new_xbc_p, new_ssm_p, new_sc_s, new_xbc_s, new_ssm_s


def kernel(x_prompt, x_sample, state_sconv, state_ssm_conv, state_ssm, meta_tokens, norm_mix_pre, norm_mix_post, w_in, sconv_w, sconv_norm, ssm_conv_w, ssm_conv_b, dt_bias, A_log, D_skip, ssm_norm, w_out, norm_ffn_pre, norm_ffn_post, w_gate, w_up, w_down):
    depth = w_in.shape[0]
    assert depth == 1 and x_sample.shape[1] == SUBLANES
    outs = _layer(x_prompt, x_sample, state_sconv[0], state_ssm_conv[0], state_ssm[0], meta_tokens,
                  norm_mix_pre[0], norm_mix_post[0], w_in, sconv_w[0], sconv_norm[0], ssm_conv_w[0],
                  ssm_conv_b[0], dt_bias[0], A_log[0], D_skip[0], ssm_norm[0], w_out[0],
                  norm_ffn_pre[0], norm_ffn_post[0], w_gate[0], w_up[0], w_down[0])
    y_prompt, y_sample = outs[0], outs[1]
    return (y_prompt, y_sample) + tuple(o[None] for o in outs[2:])
```

```python
import functools

import jax
import jax.numpy as jnp
from jax import lax
from jax.experimental import pallas as pl
from jax.experimental.pallas import tpu as pltpu

D_MODEL = 1024
N_META = 16
D_SCONV = 512
SCONV_GROUPS = 8
SCONV_WIDTH = 3
SSM_HEAD_DIM = 64
SSM_HEADS = 16
D_SSM = 1024
SSM_GROUPS = 2
HEADS_PER_GROUP = 8
D_GROUP = HEADS_PER_GROUP * SSM_HEAD_DIM
SSM_STATE = 128
SSM_CONV_WIDTH = 4
D_BC = 2 * SSM_GROUPS * SSM_STATE
D_XBC = D_SSM + D_BC
D_MIX = D_SCONV + D_SSM
D_IN = 3 * D_SCONV + D_SSM + D_XBC + SSM_HEADS
D_FF = 2816
EPS = 1e-6
LOG2_E = 1.4426950408889634

LANES = 128
SUBLANES = 8
D_IN_PAD = D_IN + (LANES - SSM_HEADS)
HEAD_PAIRS = D_SSM // LANES

O_GATE_B = 0
O_GATE_C = D_SCONV
O_HV = 2 * D_SCONV
O_Z = 3 * D_SCONV
O_XS = O_Z + D_SSM
O_BC = O_XS + D_SSM
O_DT = O_BC + D_BC

CHUNK = 128
TILE_CHUNKS = 2
STAGE_A_COLS = tuple(range(0, O_DT + 1, D_SCONV))
STAGE_A_PLAN = ((0, "sconv"), (0, "prepare"), (0, 1), (0, 3), (0, 5), (1, "prepare"), (1, 1), (1, 3), (1, 5))
SEQ_TILE = 16
FFN_ROWS = 512
FFN_SUB_ROWS = 256
VMEM_LIMIT = 60000 * 1024

BF16 = jnp.bfloat16
F32 = jnp.float32


def _dot(a, b):
    return jnp.dot(a, b, preferred_element_type=F32)


def _dot_nt(a, b):
    return lax.dot_general(a, b, (((1,), (1,)), ((), ())), preferred_element_type=F32)


def _dot_tn(a, b):
    return lax.dot_general(a, b, (((0,), (0,)), ((), ())), preferred_element_type=F32)


def _split3(v):
    hi = v.astype(BF16)
    r1 = v - hi.astype(F32)
    mid = r1.astype(BF16)
    lo = (r1 - mid.astype(F32)).astype(BF16)
    return hi, mid, lo


def _dot3(v, m):
    hi, mid, lo = _split3(v)
    return _dot(hi, m) + _dot(mid, m) + _dot(lo, m)


def _dot3_left(m, v):
    hi, mid, lo = _split3(v)
    return _dot(m, hi) + _dot(m, mid) + _dot(m, lo)


def _rms(x, w):
    return x * lax.rsqrt(jnp.mean(x * x, axis=-1, keepdims=True) + EPS) * w


def _silu(x):
    hx = 0.5 * x
    return hx + hx * jnp.tanh(hx)


def _softplus(x):
    return jnp.maximum(x, 0.0) + jnp.log1p(jnp.exp(-jnp.abs(x)))


def _shifted(u, prev, j, rows_per_seq):
    t = u.shape[0]
    s = pltpu.roll(u, j, 0)
    if rows_per_seq is None:
        row = lax.broadcasted_iota(jnp.int32, (SUBLANES, u.shape[1]), 0)
        first = jnp.where(row < j, pltpu.roll(prev, j, 0), s[0:SUBLANES])
        if t == SUBLANES:
            return first
        return jnp.concatenate([first, s[SUBLANES:]], axis=0)
    row = lax.broadcasted_iota(jnp.int32, u.shape, 0) % rows_per_seq
    n_buf = prev.shape[1]
    for tt in range(j):
        buf_row = prev[:, n_buf - j + tt, :]
        spread = jnp.broadcast_to(buf_row[:, None, :], (buf_row.shape[0], rows_per_seq, u.shape[1])).reshape(u.shape)
        s = jnp.where(row == tt, spread, s)
    return s


def _causal_conv(u, prev, w, rows_per_seq):
    k = w.shape[0]
    out = _shifted(u, prev, k - 1, rows_per_seq) * w[0:1]
    for i in range(1, k - 1):
        out = out + _shifted(u, prev, k - 1 - i, rows_per_seq) * w[i:i + 1]
    return out + u * w[k - 1:k]


def _sconv_mixer(gate_b, gate_c, hv, prev, sconv_w, sconv_norm, rows_per_seq):
    u = gate_c * hv
    conv_u = _causal_conv(u, prev, sconv_w, rows_per_seq)
    return _sconv_norm(gate_b, conv_u, sconv_norm), u


def _sconv_norm(gate_b, conv_u, sconv_norm):
    v = gate_b * conv_u
    first_half = lax.broadcasted_iota(jnp.int32, (v.shape[0], LANES), 1) < D_SCONV // SCONV_GROUPS
    parts = []
    for c in range(0, D_SCONV, LANES):
        vc = v[:, c:c + LANES]
        sq = vc * vc
        s_first = jnp.sum(jnp.where(first_half, sq, 0.0), axis=-1, keepdims=True)
        s_second = jnp.sum(jnp.where(first_half, 0.0, sq), axis=-1, keepdims=True)
        mean_sq = jnp.where(first_half, s_first, s_second) * (SCONV_GROUPS / D_SCONV)
        parts.append(vc * lax.rsqrt(mean_sq + EPS))
    return jnp.concatenate(parts, axis=1) * sconv_norm


def _ssd_prepare(dt_raw, dt_bias, a_log, tri, t):
    lane = lax.broadcasted_iota(jnp.int32, (t, LANES), 1)
    valid = lane < SSM_HEADS
    dt = jnp.where(valid, _softplus(dt_raw + dt_bias), 0.0)
    a = dt * (-jnp.exp(a_log))
    a_cum = _dot3_left(tri, a)
    return dt, a_cum


def _ssd_prepare_rows(dt_raw_t, dt_bias_t, a_log_t, tri, seq_ones=None):
    t = dt_raw_t.shape[1]
    dt_t = _softplus(dt_raw_t + dt_bias_t)
    a_t = dt_t * (-(jnp.exp(a_log_t) * LOG2_E))
    hi, mid, lo = _split3(a_t)
    a_cum_t = _dot_nt(hi, tri) + _dot_nt(mid, tri) + _dot_nt(lo, tri)
    if seq_ones is None:
        a_total_t = jnp.broadcast_to(a_cum_t[:, t - 1:t], (SSM_HEADS, t))
    else:
        a_total_t = _dot_nt(hi, seq_ones) + _dot_nt(mid, seq_ones) + _dot_nt(lo, seq_ones)
    w_t = jnp.exp2(a_total_t - a_cum_t) * dt_t
    pad = jnp.zeros((LANES - SSM_HEADS, t), F32)
    a_cum = jnp.concatenate([a_cum_t, pad], axis=0).T
    w = jnp.concatenate([w_t, pad], axis=0).T
    return a_cum, w, a_cum_t, dt_t, a_total_t


def _group_rms_ssm(y, w):
    parts = []
    for g in range(SSM_GROUPS):
        yg = y[:, g * D_GROUP:(g + 1) * D_GROUP]
        parts.append(yg * lax.rsqrt(jnp.mean(yg * yg, axis=-1, keepdims=True) + EPS))
    return jnp.concatenate(parts, axis=1) * w


def _ssd_heads(xs_of, bm_b, cm, a_cum, w, a_cum_t, dt_t, mask, state_t=None, gate=None, after_pair=None):
    t = a_cum.shape[0]
    assert t == LANES
    first_half = lax.broadcasted_iota(jnp.int32, (t, LANES), 1) < SSM_HEAD_DIM
    y_pairs, xw_pairs, ea_pairs, xs_pairs = [], [], [], []
    for g in range(SSM_GROUPS):
        cm_g = cm[:, g * SSM_STATE:(g + 1) * SSM_STATE]
        cb = _dot_nt(cm_g.astype(BF16), bm_b[:, g * SSM_STATE:(g + 1) * SSM_STATE])
        for q in range(HEADS_PER_GROUP // 2):
            pair = g * (HEADS_PER_GROUP // 2) + q
            cols = slice(pair * LANES, (pair + 1) * LANES)
            xs_p = xs_of(pair)
            rhs = xs_p.astype(BF16)
            if state_t is not None:
                rhs = jnp.concatenate([rhs, state_t[:, cols].astype(BF16)], axis=0)
            res, eas, ws = [], [], []
            for e in (2 * pair, 2 * pair + 1):
                a_col = jnp.broadcast_to(a_cum[:, e:e + 1], (t, LANES))
                decay = jnp.exp2(jnp.where(mask, a_col - a_cum_t[e:e + 1, :], -jnp.inf))
                lhs = (cb * decay * dt_t[e:e + 1, :]).astype(BF16)
                ea = jnp.exp2(a_col)
                if state_t is not None:
                    lhs = jnp.concatenate([lhs, (cm_g * ea).astype(BF16)], axis=1)
                res.append(_dot(lhs, rhs))
                eas.append(ea)
                ws.append(jnp.broadcast_to(w[:, e:e + 1], (t, LANES)))
            y_p = jnp.where(first_half, res[0], res[1])
            y_pairs.append(y_p if gate is None else gate(pair, y_p, xs_p))
            xw_pairs.append((xs_p * jnp.where(first_half, ws[0], ws[1])).astype(BF16))
            ea_pairs.append(jnp.where(first_half, eas[0], eas[1]))
            xs_pairs.append(xs_p)
            if after_pair is not None:
                after_pair(pair, y_pairs, xw_pairs, ea_pairs)
    return (jnp.concatenate(y_pairs, axis=1), jnp.concatenate(xw_pairs, axis=1),
            jnp.concatenate(ea_pairs, axis=1), jnp.concatenate(xs_pairs, axis=1))


def _ssm_tail(y, xs, z, d_full, ssm_norm):
    y = y + d_full * xs
    y = y * _silu(z)
    return _group_rms_ssm(y, ssm_norm)


def _w_main_spec(index_map, **kwargs):
    return pl.BlockSpec((1, D_MODEL, O_DT), index_map, **kwargs)


def _inproj_kernel(x_ref, npre_ref, w_main_ref, w_dt_ref, o_ref):
    hn = _rms(x_ref[...], npre_ref[...]).astype(BF16)
    o_ref[:, 0:O_DT] = _dot(hn, w_main_ref[0])
    o_ref[:, O_DT:D_IN_PAD] = _dot(hn, w_dt_ref[...])


def _inproj(x, npre, w_main_b, w_dt_b):
    m = x.shape[0]
    once = lambda shape: pl.BlockSpec(shape, lambda i: (0, 0), pipeline_mode=pl.Buffered(1))
    return pl.pallas_call(
        _inproj_kernel,
        grid=(1,),
        in_specs=[once((m, D_MODEL)), once((1, D_MODEL)),
                  _w_main_spec(lambda i: (0, 0, 0), pipeline_mode=pl.Buffered(1)), once((D_MODEL, LANES))],
        out_specs=once((m, D_IN_PAD)),
        out_shape=jax.ShapeDtypeStruct((m, D_IN_PAD), F32),
        compiler_params=pltpu.CompilerParams(dimension_semantics=("arbitrary",), vmem_limit_bytes=VMEM_LIMIT),
        name="inproj",
    )(x, npre, w_main_b, w_dt_b)


def _meta_kernel(proj_ref, xconv_w_ref, xconv_b_ref, dt_bias_ref, a_log_ref, tri_ref, expand_ref,
                 sc_tail_ref, xbc_tail_ref, state_t_ref):
    t = N_META
    proj = proj_ref[...]
    u = proj[:, O_GATE_C:O_GATE_C + D_SCONV] * proj[:, O_HV:O_HV + D_SCONV]
    sc_tail_ref[...] = u[t - SUBLANES:t]
    xbc = proj[:, O_XS:O_XS + D_XBC]
    xbc_tail_ref[...] = xbc[t - SUBLANES:t]
    zeros_prev = jnp.zeros((SUBLANES, D_XBC), F32)
    xbc_c = _silu(_causal_conv(xbc, zeros_prev, xconv_w_ref[...], None) + xconv_b_ref[...])
    xs = xbc_c[:, 0:D_SSM]
    dt, a_cum = _ssd_prepare(proj[:, O_DT:O_DT + LANES], dt_bias_ref[...], a_log_ref[...],
                             tri_ref[0:t, 0:t], t)
    w = jnp.exp(a_cum[t - 1:t, :] - a_cum) * dt
    xw_b = (xs * _dot3(w, expand_ref[...])).astype(BF16)
    for g in range(SSM_GROUPS):
        bm_b = xbc_c[:, D_SSM + g * SSM_STATE:D_SSM + (g + 1) * SSM_STATE].astype(BF16)
        state_t_ref[:, g * D_GROUP:(g + 1) * D_GROUP] = _dot_tn(bm_b, xw_b[:, g * D_GROUP:(g + 1) * D_GROUP])


def _meta_state(proj_meta, xconv_w, xconv_b, dt_bias, a_log, tri, expand):
    full = lambda shape: pl.BlockSpec(shape, lambda: tuple(0 for _ in shape))
    return pl.pallas_call(
        _meta_kernel,
        in_specs=[full(proj_meta.shape), full(xconv_w.shape), full(xconv_b.shape), full(dt_bias.shape),
                  full(a_log.shape), full(tri.shape), full(expand.shape)],
        out_specs=[full((SUBLANES, D_SCONV)), full((SUBLANES, D_XBC)), full((SSM_STATE, D_SSM))],
        out_shape=[jax.ShapeDtypeStruct((SUBLANES, D_SCONV), F32),
                   jax.ShapeDtypeStruct((SUBLANES, D_XBC), F32),
                   jax.ShapeDtypeStruct((SSM_STATE, D_SSM), F32)],
        name="meta_state",
    )(proj_meta, xconv_w, xconv_b, dt_bias, a_log, tri, expand)


def _pmix_kernel(x_ref, npre_ref, w_main_ref, w_dt_ref, sconv_w_ref, sconv_norm_ref,
                 xconv_w_ref, xconv_b_ref, dt_bias_ref, a_log_ref, d_full_ref, ssm_norm_ref,
                 tri_ref, w_out_ref, npost_ref, sc0_ref, xbc0_ref, state0_t_ref,
                 h1_ref, sc_out_ref, xbc_out_ref, state_ref,
                 proj_s0, proj_s1, dt_s0, dt_s1, h_s0, h_s1, hn_s, sc_tail, xbc_tail, state_t_ref, *,
                 steps_per_seq):
    rows = TILE_CHUNKS * CHUNK
    k = pl.program_id(0)
    step_in_seq = lax.rem(jnp.maximum(k - 1, 0), steps_per_seq)

    @pl.when(k == 0)
    def _():
        proj_s1[...] = jnp.zeros((rows, O_DT), F32)
        dt_s1[...] = jnp.zeros((SSM_HEADS, rows), F32)
        h_s1[...] = jnp.zeros((rows, D_MODEL), F32)

    @pl.when(step_in_seq == 0)
    def _():
        sc_tail[...] = sc0_ref[...]
        xbc_tail[...] = xbc0_ref[...]
        state_t_ref[...] = state0_t_ref[...]

    for parity in (0, 1):
        pl.when(lax.rem(k, 2) == parity)(functools.partial(
            _pmix_step, parity, x_ref, npre_ref, w_main_ref, w_dt_ref, sconv_w_ref, sconv_norm_ref,
            xconv_w_ref, xconv_b_ref, dt_bias_ref, a_log_ref, d_full_ref, ssm_norm_ref, tri_ref, w_out_ref,
            npost_ref, h1_ref, (proj_s0, proj_s1), (dt_s0, dt_s1), (h_s0, h_s1), hn_s, sc_tail, xbc_tail,
            state_t_ref))

    @pl.when(step_in_seq == steps_per_seq - 1)
    def _():
        sc_out_ref[0] = sc_tail[SUBLANES - (SCONV_WIDTH - 1):SUBLANES, :]
        xbc_out_ref[0] = xbc_tail[SUBLANES - (SSM_CONV_WIDTH - 1):SUBLANES, :]
        for p in range(HEAD_PAIRS):
            state_ref[0, p * LANES:(p + 1) * LANES, :] = state_t_ref[:, p * LANES:(p + 1) * LANES].T


def _pmix_step(slot_a, x_ref, npre_ref, w_main_ref, w_dt_ref, sconv_w_ref, sconv_norm_ref,
               xconv_w_ref, xconv_b_ref, dt_bias_ref, a_log_ref, d_full_ref, ssm_norm_ref, tri_ref, w_out_ref,
               npost_ref, h1_ref, proj_s, dt_s, h_s, hn_s, sc_tail, xbc_tail, state_t_ref):
    t = CHUNK
    slot_b = 1 - slot_a

    h_a = x_ref[0]
    h_s[slot_a][...] = h_a
    hn_s[...] = _rms(h_a, npre_ref[...]).astype(BF16)

    def stage_a(point):
        for piece, planned in enumerate(STAGE_A_PLAN):
            if planned == point:
                if piece + 1 < len(STAGE_A_COLS):
                    off, end = STAGE_A_COLS[piece], STAGE_A_COLS[piece + 1]
                    proj_s[slot_a][:, off:end] = _dot(hn_s[...], w_main_ref[0, :, off:end])
                else:
                    dt_s[slot_a][...] = _dot_nt(w_dt_ref[...], hn_s[...])

    for sub in range(TILE_CHUNKS):
        r = slice(sub * t, (sub + 1) * t)

        def proj(off, width):
            return proj_s[slot_b][r, off:off + width]

        ya, u = _sconv_mixer(proj(O_GATE_B, D_SCONV), proj(O_GATE_C, D_SCONV), proj(O_HV, D_SCONV),
                             sc_tail[...], sconv_w_ref[...], sconv_norm_ref[...], None)
        sc_tail[...] = u[t - SUBLANES:t]
        stage_a((sub, "sconv"))
        mix_ya = _dot(ya.astype(BF16), w_out_ref[0:D_SCONV, :])

        def xconv(col, width):
            cols = slice(col, col + width)
            raw = proj(O_XS + col, width)
            out = _silu(_causal_conv(raw, xbc_tail[:, cols], xconv_w_ref[:, cols], None) + xconv_b_ref[:, cols])
            xbc_tail[:, cols] = raw[t - SUBLANES:t]
            return out

        bc = xconv(D_SSM, D_BC)
        bm_b = bc[:, 0:SSM_GROUPS * SSM_STATE].astype(BF16)
        cm = bc[:, SSM_GROUPS * SSM_STATE:D_BC]
        a_cum, w, a_cum_t, dt_t, _ = _ssd_prepare_rows(dt_s[slot_b][:, r], dt_bias_ref[...], a_log_ref[...],
                                                       tri_ref[...])
        causal = lax.broadcasted_iota(jnp.int32, (t, t), 0) >= lax.broadcasted_iota(jnp.int32, (t, t), 1)
        stage_a((sub, "prepare"))

        def gate(p, y_p, xs_p):
            cols = slice(p * LANES, (p + 1) * LANES)
            return (y_p + d_full_ref[:, cols] * xs_p) * _silu(proj(O_Z + p * LANES, LANES))

        y, xw_b, ea_full, _ = _ssd_heads(lambda p: xconv(p * LANES, LANES), bm_b, cm, a_cum, w, a_cum_t, dt_t,
                                         causal, state_t_ref, gate, lambda p, *partial: stage_a((sub, p)))
        chunk_decay = ea_full[t - 1:t, :]
        for g in range(SSM_GROUPS):
            cols = slice(g * D_GROUP, (g + 1) * D_GROUP)
            state_t_ref[:, cols] = chunk_decay[:, cols] * state_t_ref[:, cols] + _dot_tn(
                bm_b[:, g * SSM_STATE:(g + 1) * SSM_STATE], xw_b[:, cols])
        yb = _group_rms_ssm(y, ssm_norm_ref[...])
        mix = mix_ya + _dot(yb.astype(BF16), w_out_ref[D_SCONV:D_MIX, :])
        stage_a((sub, "end"))
        h1_ref[0, r, :] = h_s[slot_b][r, :] + _rms(mix, npost_ref[...])


def _pmix(x_prompt, consts, sc0, xbc0, state0_t):
    batch, seq, _ = x_prompt.shape
    rows = TILE_CHUNKS * CHUNK
    sps = seq // rows
    n_tiles = batch * sps
    const_spec = lambda a: (pl.BlockSpec(a.shape, lambda k: (0, 0)) if a.ndim == 2
                            else _w_main_spec(lambda k: (0, 0, 0)))
    ins = list(consts) + [sc0, xbc0, state0_t]

    def tile_a(k):
        tile = jnp.minimum(k, n_tiles - 1)
        return (tile // sps, tile % sps, 0)

    def tile_b(k):
        tile = jnp.maximum(k - 1, 0)
        return (tile // sps, tile % sps, 0)

    def seq_b(k):
        return (jnp.maximum(k - 1, 0) // sps, 0, 0)

    return pl.pallas_call(
        functools.partial(_pmix_kernel, steps_per_seq=sps),
        grid=(n_tiles + 1,),
        in_specs=[pl.BlockSpec((1, rows, D_MODEL), tile_a)] + [const_spec(a) for a in ins],
        out_specs=[
            pl.BlockSpec((1, rows, D_MODEL), tile_b),
            pl.BlockSpec((1, SCONV_WIDTH - 1, D_SCONV), seq_b),
            pl.BlockSpec((1, SSM_CONV_WIDTH - 1, D_XBC), seq_b),
            pl.BlockSpec((1, D_SSM, SSM_STATE), seq_b),
        ],
        out_shape=[
            jax.ShapeDtypeStruct((batch, seq, D_MODEL), F32),
            jax.ShapeDtypeStruct((batch, SCONV_WIDTH - 1, D_SCONV), F32),
            jax.ShapeDtypeStruct((batch, SSM_CONV_WIDTH - 1, D_XBC), F32),
            jax.ShapeDtypeStruct((batch, D_SSM, SSM_STATE), F32),
        ],
        scratch_shapes=[
            pltpu.VMEM((rows, O_DT), F32),
            pltpu.VMEM((rows, O_DT), F32),
            pltpu.VMEM((SSM_HEADS, rows), F32),
            pltpu.VMEM((SSM_HEADS, rows), F32),
            pltpu.VMEM((rows, D_MODEL), F32),
            pltpu.VMEM((rows, D_MODEL), F32),
            pltpu.VMEM((rows, D_MODEL), BF16),
            pltpu.VMEM((SUBLANES, D_SCONV), F32),
            pltpu.VMEM((SUBLANES, D_XBC), F32),
            pltpu.VMEM((SSM_STATE, D_SSM), F32),
        ],
        compiler_params=pltpu.CompilerParams(dimension_semantics=("arbitrary",), vmem_limit_bytes=VMEM_LIMIT),
        name="prompt_mixer",
    )(x_prompt, *ins)


def _smix_kernel(x_ref, sc_prev_ref, xbc_prev_ref, state_in_ref, npre_ref, w_main_ref, w_dt_ref,
                 sconv_w_ref, sconv_norm_ref, xconv_w_ref, xconv_b_ref,
                 dt_bias_ref, a_log_ref, d_full_ref, ssm_norm_ref, w_out_ref, npost_ref,
                 h1_ref, u_ref, xbc_ref, state_ref, *, dec_seq):
    t = SEQ_TILE * dec_seq
    h = x_ref[...]
    hn = _rms(h, npre_ref[...]).astype(BF16)

    def proj(off, width):
        return _dot(hn, w_main_ref[0, :, off:off + width])

    ya, u = _sconv_mixer(proj(O_GATE_B, D_SCONV), proj(O_GATE_C, D_SCONV), proj(O_HV, D_SCONV),
                         sc_prev_ref[...], sconv_w_ref[...], sconv_norm_ref[...], dec_seq)
    for s in range(SEQ_TILE):
        u_ref[s] = u[(s + 1) * dec_seq - (SCONV_WIDTH - 1):(s + 1) * dec_seq, :]
    xbc = proj(O_XS, D_XBC)
    for s in range(SEQ_TILE):
        xbc_ref[s] = xbc[(s + 1) * dec_seq - (SSM_CONV_WIDTH - 1):(s + 1) * dec_seq, :]
    xbc_c = _silu(_causal_conv(xbc, xbc_prev_ref[...], xconv_w_ref[...], dec_seq) + xconv_b_ref[...])
    xs = xbc_c[:, 0:D_SSM]
    bm = xbc_c[:, D_SSM:D_SSM + SSM_GROUPS * SSM_STATE]
    bm_b = bm.astype(BF16)
    cm = xbc_c[:, D_SSM + SSM_GROUPS * SSM_STATE:D_XBC]

    row = lax.broadcasted_iota(jnp.int32, (t, t), 0)
    col = lax.broadcasted_iota(jnp.int32, (t, t), 1)
    same_seq = row // dec_seq == col // dec_seq
    mask = jnp.logical_and(row >= col, same_seq)
    tri = jnp.where(mask, 1.0, 0.0).astype(BF16)
    seq_ones = jnp.where(same_seq, 1.0, 0.0).astype(BF16)
    a_cum, w, a_cum_t, dt_t, a_total_t = _ssd_prepare_rows(_dot_nt(w_dt_ref[...], hn), dt_bias_ref[...],
                                                           a_log_ref[...], tri, seq_ones)
    y, xw_b, ea_full, _ = _ssd_heads(lambda p: xs[:, p * LANES:(p + 1) * LANES], bm_b, cm, a_cum, w, a_cum_t,
                                     dt_t, mask)

    row_id = lax.broadcasted_iota(jnp.int32, (t, SSM_STATE), 0) // dec_seq
    seq_decay_t = jnp.exp2(a_total_t)
    y_off_rows = []
    for s in range(SEQ_TILE):
        r0 = s * dec_seq
        y_off_g = []
        for g in range(SSM_GROUPS):
            s_prev = state_in_ref[s, g * D_GROUP:(g + 1) * D_GROUP, :]
            cm_s = cm[r0:r0 + dec_seq, g * SSM_STATE:(g + 1) * SSM_STATE].astype(BF16)
            y_off_g.append(_dot_nt(cm_s, s_prev.astype(BF16)))
        y_off_rows.append(jnp.concatenate(y_off_g, axis=1))
    y = y + jnp.concatenate(y_off_rows, axis=0) * ea_full
    yb = _ssm_tail(y, xs, proj(O_Z, D_SSM), d_full_ref[...], ssm_norm_ref[...])
    mix = _dot(ya.astype(BF16), w_out_ref[0:D_SCONV, :]) + _dot(yb.astype(BF16), w_out_ref[D_SCONV:D_MIX, :])
    h1_ref[...] = h + _rms(mix, npost_ref[...])

    for s in range(SEQ_TILE):
        r0 = s * dec_seq
        q_t = jnp.broadcast_to(seq_decay_t[:, r0:r0 + 1], (SSM_HEADS, SSM_STATE))
        for g in range(SSM_GROUPS):
            rows = slice(g * D_GROUP, (g + 1) * D_GROUP)
            bm_g = jnp.where(row_id == s, bm[:, g * SSM_STATE:(g + 1) * SSM_STATE], 0.0).astype(BF16)
            decay = jnp.concatenate(
                [jnp.broadcast_to(q_t[e:e + 1, :], (SSM_HEAD_DIM, SSM_STATE))
                 for e in range(g * HEADS_PER_GROUP, (g + 1) * HEADS_PER_GROUP)], axis=0)
            state_ref[s, rows, :] = decay * state_in_ref[s, rows, :] + _dot_tn(xw_b[:, rows], bm_g)


def _smix(x_rows, sc_prev, xbc_prev, state, consts, dec_seq):
    n_seq = state.shape[0]
    t = SEQ_TILE * dec_seq
    const_spec = lambda a: (pl.BlockSpec(a.shape, lambda i: (0, 0), pipeline_mode=pl.Buffered(1)) if a.ndim == 2
                            else _w_main_spec(lambda i: (0, 0, 0), pipeline_mode=pl.Buffered(1)))
    row_spec = lambda width: pl.BlockSpec((t, width), lambda i: (i, 0))
    state_spec = pl.BlockSpec((SEQ_TILE, D_SSM, SSM_STATE), lambda i: (i, 0, 0))
    return pl.pallas_call(
        functools.partial(_smix_kernel, dec_seq=dec_seq),
        grid=(n_seq // SEQ_TILE,),
        in_specs=[row_spec(D_MODEL),
                  pl.BlockSpec((SEQ_TILE, SCONV_WIDTH - 1, D_SCONV), lambda i: (i, 0, 0)),
                  pl.BlockSpec((SEQ_TILE, SSM_CONV_WIDTH - 1, D_XBC), lambda i: (i, 0, 0)), state_spec]
        + [const_spec(a) for a in consts],
        out_specs=[row_spec(D_MODEL),
                   pl.BlockSpec((SEQ_TILE, SCONV_WIDTH - 1, D_SCONV), lambda i: (i, 0, 0)),
                   pl.BlockSpec((SEQ_TILE, SSM_CONV_WIDTH - 1, D_XBC), lambda i: (i, 0, 0)), state_spec],
        out_shape=[
            jax.ShapeDtypeStruct((n_seq * dec_seq, D_MODEL), F32),
            jax.ShapeDtypeStruct((n_seq, SCONV_WIDTH - 1, D_SCONV), F32),
            jax.ShapeDtypeStruct((n_seq, SSM_CONV_WIDTH - 1, D_XBC), F32),
            jax.ShapeDtypeStruct(state.shape, F32),
        ],
        compiler_params=pltpu.CompilerParams(dimension_semantics=("arbitrary",), vmem_limit_bytes=VMEM_LIMIT),
        name="sample_mixer",
    )(x_rows, sc_prev, xbc_prev, state, *consts)


def _ffn_kernel(hp_ref, hs_ref, npre_ref, w_gate_ref, w_up_ref, w_down_ref, npost_ref, op_ref, os_ref, *,
                prompt_steps):
    def body(h_ref, o_ref):
        for r0 in range(0, FFN_ROWS, FFN_SUB_ROWS):
            r = slice(r0, r0 + FFN_SUB_ROWS)
            h1 = h_ref[r, :]
            fn = _rms(h1, npre_ref[...]).astype(BF16)
            act = (_silu(_dot(fn, w_gate_ref[...])) * _dot(fn, w_up_ref[...])).astype(BF16)
            o_ref[r, :] = h1 + _rms(_dot(act, w_down_ref[...]), npost_ref[...])

    step = pl.program_id(0)
    pl.when(step < prompt_steps)(functools.partial(body, hp_ref, op_ref))
    pl.when(step >= prompt_steps)(functools.partial(body, hs_ref, os_ref))


def _ffn(h_prompt, h_sample, ffn_consts):
    mp, ms = h_prompt.shape[0], h_sample.shape[0]
    rows = FFN_ROWS
    p_steps, s_steps = mp // rows, ms // rows
    p_spec = pl.BlockSpec((rows, D_MODEL), lambda i: (jnp.minimum(i, p_steps - 1), 0))
    s_spec = pl.BlockSpec((rows, D_MODEL), lambda i: (jnp.maximum(i - p_steps, 0), 0))
    const_spec = lambda a: pl.BlockSpec(a.shape, lambda i: (0, 0), pipeline_mode=pl.Buffered(1))
    return pl.pallas_call(
        functools.partial(_ffn_kernel, prompt_steps=p_steps),
        grid=(p_steps + s_steps,),
        in_specs=[p_spec, s_spec] + [const_spec(a) for a in ffn_consts],
        out_specs=[p_spec, s_spec],
        out_shape=[jax.ShapeDtypeStruct((mp, D_MODEL), F32), jax.ShapeDtypeStruct((ms, D_MODEL), F32)],
        compiler_params=pltpu.CompilerParams(dimension_semantics=("arbitrary",), vmem_limit_bytes=VMEM_LIMIT),
        name="ffn",
    )(h_prompt, h_sample, *ffn_consts)


def _row(v, width=None):
    v = v.reshape(1, -1).astype(F32)
    if width is not None and v.shape[1] < width:
        v = jnp.pad(v, ((0, 0), (0, width - v.shape[1])))
    return v


def _layer(hp, hs, sc_state, xbc_state, ssm_state, meta_tokens,
           norm_mix_pre, norm_mix_post, w_in_all, sconv_w, sconv_norm, ssm_conv_w, ssm_conv_b,
           dt_bias, a_log, d_skip, ssm_norm, w_out, norm_ffn_pre, norm_ffn_post, w_gate, w_up, w_down):
    batch, seq, _ = hp.shape
    w_in = w_in_all[0]
    n_seq, dec_seq, _ = hs.shape

    w_main_b = w_in_all.astype(BF16)
    w_dt_b = jnp.pad(w_in[:, O_DT:D_IN], ((0, 0), (0, D_IN_PAD - D_IN))).astype(BF16)
    w_dt_t = w_in[:, O_DT:D_IN].T.astype(BF16)
    w_out_b = w_out.astype(BF16)
    w_gate_b, w_up_b, w_down_b = w_gate.astype(BF16), w_up.astype(BF16), w_down.astype(BF16)
    npre, npost = _row(norm_mix_pre), _row(norm_mix_post)
    fpre, fpost = _row(norm_ffn_pre), _row(norm_ffn_post)
    sconv_w = sconv_w.astype(F32)
    xconv_w = ssm_conv_w.astype(F32)
    xconv_b = _row(ssm_conv_b)
    dt_bias_p, a_log_p = _row(dt_bias, LANES), _row(a_log, LANES)
    dt_bias_t = jnp.broadcast_to(dt_bias.astype(F32)[:, None], (SSM_HEADS, LANES))
    a_log_t = jnp.broadcast_to(a_log.astype(F32)[:, None], (SSM_HEADS, LANES))
    d_full = jnp.repeat(d_skip.astype(F32), SSM_HEAD_DIM).reshape(1, D_SSM)
    sconv_norm_r, ssm_norm_r = _row(sconv_norm), _row(ssm_norm)

    r = jnp.arange(CHUNK)
    tri = (r[:, None] >= r[None, :]).astype(BF16)
    expand = (jnp.arange(LANES)[:, None] == (jnp.arange(D_SSM)[None, :] // SSM_HEAD_DIM)).astype(BF16)

    mixer_consts = (sconv_w, sconv_norm_r, xconv_w, xconv_b, dt_bias_t, a_log_t, d_full, ssm_norm_r)
    ffn_consts = (fpre, w_gate_b, w_up_b, w_down_b, fpost)

    proj_meta = _inproj(meta_tokens.astype(F32), npre, w_main_b, w_dt_b)
    sc0, xbc0, state0_t = _meta_state(proj_meta, xconv_w, xconv_b, dt_bias_p, a_log_p, tri, expand)

    pm_consts = (npre, w_main_b, w_dt_t) + mixer_consts + (tri, w_out_b, npost)
    h1, new_sc_p, new_xbc_p, p_state = _pmix(hp, pm_consts, sc0, xbc0, state0_t)
    new_ssm_p = p_state.reshape(batch, SSM_GROUPS, HEADS_PER_GROUP, SSM_HEAD_DIM, SSM_STATE)

    hs_rows = hs.reshape(n_seq * dec_seq, D_MODEL)
    sm_consts = (npre, w_main_b, w_dt_t) + mixer_consts + (w_out_b, npost)
    h1_s, new_sc_s, new_xbc_s, s_state = _smix(hs_rows, sc_state, xbc_state, ssm_state.reshape(n_seq, D_SSM, SSM_STATE),
                                      sm_consts, dec_seq)

    y_prompt, y_sample = _ffn(h1.reshape(batch * seq, D_MODEL), h1_s, ffn_consts)
    y_prompt = y_prompt.reshape(batch, seq, D_MODEL)
    y_sample = y_sample.reshape(n_seq, dec_seq, D_MODEL)
    new_ssm_s = s_state.reshape(n_seq, SSM_GROUPS, HEADS_PER_GROUP, SSM_HEAD_DIM, SSM_STATE)
    return y_prompt, y_sample, new_sc_p, new_xbc_p, new_ssm_p, new_sc_s, new_xbc_s, new_ssm_s


def kernel(x_prompt, x_sample, state_sconv, state_ssm_conv, state_ssm, meta_tokens, norm_mix_pre, norm_mix_post, w_in, sconv_w, sconv_norm, ssm_conv_w, ssm_conv_b, dt_bias, A_log, D_skip, ssm_norm, w_out, norm_ffn_pre, norm_ffn_post, w_gate, w_up, w_down):
    depth = w_in.shape[0]
    assert depth == 1 and x_sample.shape[1] == SUBLANES
    outs = _layer(x_prompt, x_sample, state_sconv[0], state_ssm_conv[0], state_ssm[0], meta_tokens,
                  norm_mix_pre[0], norm_mix_post[0], w_in, sconv_w[0], sconv_norm[0], ssm_conv_w[0],
                  ssm_conv_b[0], dt_bias[0], A_log[0], D_skip[0], ssm_norm[0], w_out[0],
                  norm_ffn_pre[0], norm_ffn_post[0], w_gate[0], w_up[0], w_down[0])
    y_prompt, y_sample = outs[0], outs[1]
    return (y_prompt, y_sample) + tuple(o[None] for o in outs[2:])
```

```python
import functools

import jax
import jax.numpy as jnp
from jax import lax
from jax.experimental import pallas as pl
from jax.experimental.pallas import tpu as pltpu

D_MODEL = 1024
N_META = 16
D_SCONV = 512
SCONV_GROUPS = 8
SCONV_WIDTH = 3
SSM_HEAD_DIM = 64
SSM_HEADS = 16
D_SSM = 1024
SSM_GROUPS = 2
HEADS_PER_GROUP = 8
D_GROUP = HEADS_PER_GROUP * SSM_HEAD_DIM
SSM_STATE = 128
SSM_CONV_WIDTH = 4
D_BC = 2 * SSM_GROUPS * SSM_STATE
D_XBC = D_SSM + D_BC
D_MIX = D_SCONV + D_SSM
D_IN = 3 * D_SCONV + D_SSM + D_XBC + SSM_HEADS
EPS = 1e-6
LOG2_E = 1.4426950408889634

LANES = 128
SUBLANES = 8
D_IN_PAD = D_IN + (LANES - SSM_HEADS)
HEAD_PAIRS = D_SSM // LANES

O_GATE_B = 0
O_GATE_C = D_SCONV
O_HV = 2 * D_SCONV
O_Z = 3 * D_SCONV
O_XS = O_Z + D_SSM
O_BC = O_XS + D_SSM
O_DT = O_BC + D_BC

CHUNK = 128
TILE_CHUNKS = 2
STAGE_A_COLS = tuple(range(0, O_DT + 1, D_SCONV))
STAGE_A_PLAN = ((0, "sconv"), (0, "prepare"), (0, 1), (0, 3), (0, 5), (0, "end"), (1, "prepare"), (1, 1), (1, 3))
SEQ_TILE = 16
FFN_ROWS = 512
FFN_SUB_ROWS = 256
VMEM_LIMIT = 60000 * 1024

BF16 = jnp.bfloat16
F32 = jnp.float32


def _dot(a, b):
    return jnp.dot(a, b, preferred_element_type=F32)


def _dot_nt(a, b):
    return lax.dot_general(a, b, (((1,), (1,)), ((), ())), preferred_element_type=F32)


def _dot_tn(a, b):
    return lax.dot_general(a, b, (((0,), (0,)), ((), ())), preferred_element_type=F32)


def _split3(v):
    hi = v.astype(BF16)
    r1 = v - hi.astype(F32)
    mid = r1.astype(BF16)
    lo = (r1 - mid.astype(F32)).astype(BF16)
    return hi, mid, lo


def _dot3(v, m):
    hi, mid, lo = _split3(v)
    return _dot(hi, m) + _dot(mid, m) + _dot(lo, m)


def _dot3_left(m, v):
    hi, mid, lo = _split3(v)
    return _dot(m, hi) + _dot(m, mid) + _dot(m, lo)


def _rms(x, w):
    return x * lax.rsqrt(jnp.mean(x * x, axis=-1, keepdims=True) + EPS) * w


def _silu(x):
    hx = 0.5 * x
    return hx + hx * jnp.tanh(hx)


def _softplus(x):
    return jnp.maximum(x, 0.0) + jnp.log1p(jnp.exp(-jnp.abs(x)))


def _shifted(u, prev, j, rows_per_seq):
    t = u.shape[0]
    s = pltpu.roll(u, j, 0)
    if rows_per_seq is None:
        row = lax.broadcasted_iota(jnp.int32, (SUBLANES, u.shape[1]), 0)
        first = jnp.where(row < j, pltpu.roll(prev, j, 0), s[0:SUBLANES])
        if t == SUBLANES:
            return first
        return jnp.concatenate([first, s[SUBLANES:]], axis=0)
    row = lax.broadcasted_iota(jnp.int32, u.shape, 0) % rows_per_seq
    n_buf = prev.shape[1]
    for tt in range(j):
        buf_row = prev[:, n_buf - j + tt, :]
        spread = jnp.broadcast_to(buf_row[:, None, :], (buf_row.shape[0], rows_per_seq, u.shape[1])).reshape(u.shape)
        s = jnp.where(row == tt, spread, s)
    return s


def _causal_conv(u, prev, w, rows_per_seq):
    k = w.shape[0]
    out = _shifted(u, prev, k - 1, rows_per_seq) * w[0:1]
    for i in range(1, k - 1):
        out = out + _shifted(u, prev, k - 1 - i, rows_per_seq) * w[i:i + 1]
    return out + u * w[k - 1:k]


def _sconv_mixer(gate_b, gate_c, hv, prev, sconv_w, sconv_norm, rows_per_seq):
    u = gate_c * hv
    conv_u = _causal_conv(u, prev, sconv_w, rows_per_seq)
    return _sconv_norm(gate_b, conv_u, sconv_norm), u


def _sconv_norm(gate_b, conv_u, sconv_norm):
    v = gate_b * conv_u
    first_half = lax.broadcasted_iota(jnp.int32, (v.shape[0], LANES), 1) < D_SCONV // SCONV_GROUPS
    parts = []
    for c in range(0, D_SCONV, LANES):
        vc = v[:, c:c + LANES]
        sq = vc * vc
        s_first = jnp.sum(jnp.where(first_half, sq, 0.0), axis=-1, keepdims=True)
        s_second = jnp.sum(jnp.where(first_half, 0.0, sq), axis=-1, keepdims=True)
        mean_sq = jnp.where(first_half, s_first, s_second) * (SCONV_GROUPS / D_SCONV)
        parts.append(vc * lax.rsqrt(mean_sq + EPS))
    return jnp.concatenate(parts, axis=1) * sconv_norm


def _ssd_prepare(dt_raw, dt_bias, a_log, tri, t):
    lane = lax.broadcasted_iota(jnp.int32, (t, LANES), 1)
    valid = lane < SSM_HEADS
    dt = jnp.where(valid, _softplus(dt_raw + dt_bias), 0.0)
    a = dt * (-jnp.exp(a_log))
    a_cum = _dot3_left(tri, a)
    return dt, a_cum


def _ssd_prepare_rows(dt_raw_t, dt_bias_t, a_log_t, tri, seq_ones=None):
    t = dt_raw_t.shape[1]
    dt_t = _softplus(dt_raw_t + dt_bias_t)
    a_t = dt_t * (-(jnp.exp(a_log_t) * LOG2_E))
    hi, mid, lo = _split3(a_t)
    a_cum_t = _dot_nt(hi, tri) + _dot_nt(mid, tri) + _dot_nt(lo, tri)
    if seq_ones is None:
        a_total_t = jnp.broadcast_to(a_cum_t[:, t - 1:t], (SSM_HEADS, t))
    else:
        a_total_t = _dot_nt(hi, seq_ones) + _dot_nt(mid, seq_ones) + _dot_nt(lo, seq_ones)
    w_t = jnp.exp2(a_total_t - a_cum_t) * dt_t
    pad = jnp.zeros((LANES - SSM_HEADS, t), F32)
    a_cum = jnp.concatenate([a_cum_t, pad], axis=0).T
    w = jnp.concatenate([w_t, pad], axis=0).T
    return a_cum, w, a_cum_t - jnp.log2(dt_t), a_total_t


def _group_rms_ssm(y, w):
    parts = []
    for g in range(SSM_GROUPS):
        yg = y[:, g * D_GROUP:(g + 1) * D_GROUP]
        parts.append(yg * lax.rsqrt(jnp.mean(yg * yg, axis=-1, keepdims=True) + EPS))
    return jnp.concatenate(parts, axis=1) * w


def _ssd_heads(xs_of, bm_b, cm, a_cum, w, a_src_t, mask, state_t=None, gate=None, after_pair=None):
    t = a_cum.shape[0]
    assert t == LANES
    first_half = lax.broadcasted_iota(jnp.int32, (t, LANES), 1) < SSM_HEAD_DIM
    y_pairs, xw_pairs, ea_pairs, xs_pairs = [], [], [], []
    for g in range(SSM_GROUPS):
        cm_g = cm[:, g * SSM_STATE:(g + 1) * SSM_STATE]
        cb = _dot_nt(cm_g.astype(BF16), bm_b[:, g * SSM_STATE:(g + 1) * SSM_STATE])
        for q in range(HEADS_PER_GROUP // 2):
            pair = g * (HEADS_PER_GROUP // 2) + q
            cols = slice(pair * LANES, (pair + 1) * LANES)
            xs_p = xs_of(pair)
            rhs = xs_p.astype(BF16)
            if state_t is not None:
                rhs = jnp.concatenate([rhs, state_t[:, cols].astype(BF16)], axis=0)
            res, eas, ws = [], [], []
            for e in (2 * pair, 2 * pair + 1):
                a_col = jnp.broadcast_to(a_cum[:, e:e + 1], (t, LANES))
                decay_dt = jnp.exp2(jnp.where(mask, a_col - a_src_t[e:e + 1, :], -jnp.inf))
                lhs = (cb * decay_dt).astype(BF16)
                ea = jnp.exp2(a_col)
                if state_t is not None:
                    lhs = jnp.concatenate([lhs, (cm_g * ea).astype(BF16)], axis=1)
                res.append(_dot(lhs, rhs))
                eas.append(ea)
                ws.append(jnp.broadcast_to(w[:, e:e + 1], (t, LANES)))
            y_p = jnp.where(first_half, res[0], res[1])
            y_pairs.append(y_p if gate is None else gate(pair, y_p, xs_p))
            xw_pairs.append((xs_p * jnp.where(first_half, ws[0], ws[1])).astype(BF16))
            ea_pairs.append(jnp.where(first_half, eas[0], eas[1]))
            xs_pairs.append(xs_p)
            if after_pair is not None:
                after_pair(pair, y_pairs, xw_pairs, ea_pairs)
    return (jnp.concatenate(y_pairs, axis=1), jnp.concatenate(xw_pairs, axis=1),
            jnp.concatenate(ea_pairs, axis=1), jnp.concatenate(xs_pairs, axis=1))


def _ssm_tail(y, xs, z, d_full, ssm_norm):
    y = y + d_full * xs
    y = y * _silu(z)
    return _group_rms_ssm(y, ssm_norm)


def _w_main_spec(index_map, **kwargs):
    return pl.BlockSpec((1, D_MODEL, O_DT), index_map, **kwargs)


def _inproj_kernel(x_ref, npre_ref, w_main_ref, w_dt_ref, o_ref):
    hn = _rms(x_ref[...], npre_ref[...]).astype(BF16)
    o_ref[:, 0:O_DT] = _dot(hn, w_main_ref[0])
    o_ref[:, O_DT:D_IN_PAD] = _dot(hn, w_dt_ref[...])


def _inproj(x, npre, w_main_b, w_dt_b):
    m = x.shape[0]
    once = lambda shape: pl.BlockSpec(shape, lambda i: (0, 0), pipeline_mode=pl.Buffered(1))
    return pl.pallas_call(
        _inproj_kernel,
        grid=(1,),
        in_specs=[once((m, D_MODEL)), once((1, D_MODEL)),
                  _w_main_spec(lambda i: (0, 0, 0), pipeline_mode=pl.Buffered(1)), once((D_MODEL, LANES))],
        out_specs=once((m, D_IN_PAD)),
        out_shape=jax.ShapeDtypeStruct((m, D_IN_PAD), F32),
        compiler_params=pltpu.CompilerParams(dimension_semantics=("arbitrary",), vmem_limit_bytes=VMEM_LIMIT),
        name="inproj",
    )(x, npre, w_main_b, w_dt_b)


def _meta_kernel(proj_ref, xconv_w_ref, xconv_b_ref, dt_bias_ref, a_log_ref, tri_ref, expand_ref,
                 sc_tail_ref, xbc_tail_ref, state_t_ref):
    t = N_META
    proj = proj_ref[...]
    u = proj[:, O_GATE_C:O_GATE_C + D_SCONV] * proj[:, O_HV:O_HV + D_SCONV]
    sc_tail_ref[...] = u[t - SUBLANES:t]
    xbc = proj[:, O_XS:O_XS + D_XBC]
    xbc_tail_ref[...] = xbc[t - SUBLANES:t]
    zeros_prev = jnp.zeros((SUBLANES, D_XBC), F32)
    xbc_c = _silu(_causal_conv(xbc, zeros_prev, xconv_w_ref[...], None) + xconv_b_ref[...])
    xs = xbc_c[:, 0:D_SSM]
    dt, a_cum = _ssd_prepare(proj[:, O_DT:O_DT + LANES], dt_bias_ref[...], a_log_ref[...],
                             tri_ref[0:t, 0:t], t)
    w = jnp.exp(a_cum[t - 1:t, :] - a_cum) * dt
    xw_b = (xs * _dot3(w, expand_ref[...])).astype(BF16)
    for g in range(SSM_GROUPS):
        bm_b = xbc_c[:, D_SSM + g * SSM_STATE:D_SSM + (g + 1) * SSM_STATE].astype(BF16)
        state_t_ref[:, g * D_GROUP:(g + 1) * D_GROUP] = _dot_tn(bm_b, xw_b[:, g * D_GROUP:(g + 1) * D_GROUP])


def _meta_state(proj_meta, xconv_w, xconv_b, dt_bias, a_log, tri, expand):
    full = lambda shape: pl.BlockSpec(shape, lambda: tuple(0 for _ in shape))
    return pl.pallas_call(
        _meta_kernel,
        in_specs=[full(proj_meta.shape), full(xconv_w.shape), full(xconv_b.shape), full(dt_bias.shape),
                  full(a_log.shape), full(tri.shape), full(expand.shape)],
        out_specs=[full((SUBLANES, D_SCONV)), full((SUBLANES, D_XBC)), full((SSM_STATE, D_SSM))],
        out_shape=[jax.ShapeDtypeStruct((SUBLANES, D_SCONV), F32),
                   jax.ShapeDtypeStruct((SUBLANES, D_XBC), F32),
                   jax.ShapeDtypeStruct((SSM_STATE, D_SSM), F32)],
        name="meta_state",
    )(proj_meta, xconv_w, xconv_b, dt_bias, a_log, tri, expand)


def _pmix_kernel(x_ref, npre_ref, w_main_ref, w_dt_ref, sconv_w_ref, sconv_norm_ref,
                 xconv_w_ref, xconv_b_ref, dt_bias_ref, a_log_ref, d_full_ref, ssm_norm_ref,
                 tri_ref, w_out_ref, npost_ref, sc0_ref, xbc0_ref, state0_t_ref,
                 h1_ref, sc_out_ref, xbc_out_ref, state_ref,
                 proj_s0, proj_s1, dt_s0, dt_s1, h_s0, h_s1, hn_s, sc_tail, xbc_tail, state_t_ref, *,
                 steps_per_seq):
    rows = TILE_CHUNKS * CHUNK
    k = pl.program_id(0)
    step_in_seq = lax.rem(jnp.maximum(k - 1, 0), steps_per_seq)

    @pl.when(k == 0)
    def _():
        proj_s1[...] = jnp.zeros((rows, O_DT), F32)
        dt_s1[...] = jnp.zeros((SSM_HEADS, rows), F32)
        h_s1[...] = jnp.zeros((rows, D_MODEL), F32)

    @pl.when(step_in_seq == 0)
    def _():
        sc_tail[...] = sc0_ref[...]
        xbc_tail[...] = xbc0_ref[...]
        state_t_ref[...] = state0_t_ref[...]

    for parity in (0, 1):
        pl.when(lax.rem(k, 2) == parity)(functools.partial(
            _pmix_step, parity, x_ref, npre_ref, w_main_ref, w_dt_ref, sconv_w_ref, sconv_norm_ref,
            xconv_w_ref, xconv_b_ref, dt_bias_ref, a_log_ref, d_full_ref, ssm_norm_ref, tri_ref, w_out_ref,
            npost_ref, h1_ref, (proj_s0, proj_s1), (dt_s0, dt_s1), (h_s0, h_s1), hn_s, sc_tail, xbc_tail,
            state_t_ref))

    @pl.when(step_in_seq == steps_per_seq - 1)
    def _():
        sc_out_ref[0] = sc_tail[SUBLANES - (SCONV_WIDTH - 1):SUBLANES, :]
        xbc_out_ref[0] = xbc_tail[SUBLANES - (SSM_CONV_WIDTH - 1):SUBLANES, :]
        for p in range(HEAD_PAIRS):
            state_ref[0, p * LANES:(p + 1) * LANES, :] = state_t_ref[:, p * LANES:(p + 1) * LANES].T


def _pmix_step(slot_a, x_ref, npre_ref, w_main_ref, w_dt_ref, sconv_w_ref, sconv_norm_ref,
               xconv_w_ref, xconv_b_ref, dt_bias_ref, a_log_ref, d_full_ref, ssm_norm_ref, tri_ref, w_out_ref,
               npost_ref, h1_ref, proj_s, dt_s, h_s, hn_s, sc_tail, xbc_tail, state_t_ref):
    t = CHUNK
    slot_b = 1 - slot_a

    h_a = x_ref[0]
    h_s[slot_a][...] = h_a
    hn_s[...] = _rms(h_a, npre_ref[...]).astype(BF16)

    def stage_a(point):
        for piece, planned in enumerate(STAGE_A_PLAN):
            if planned == point:
                if piece + 1 < len(STAGE_A_COLS):
                    off, end = STAGE_A_COLS[piece], STAGE_A_COLS[piece + 1]
                    proj_s[slot_a][:, off:end] = _dot(hn_s[...], w_main_ref[0, :, off:end])
                else:
                    dt_s[slot_a][...] = _dot_nt(w_dt_ref[...], hn_s[...])

    for sub in range(TILE_CHUNKS):
        r = slice(sub * t, (sub + 1) * t)

        def proj(off, width):
            return proj_s[slot_b][r, off:off + width]

        ya, u = _sconv_mixer(proj(O_GATE_B, D_SCONV), proj(O_GATE_C, D_SCONV), proj(O_HV, D_SCONV),
                             sc_tail[...], sconv_w_ref[...], sconv_norm_ref[...], None)
        sc_tail[...] = u[t - SUBLANES:t]
        stage_a((sub, "sconv"))
        mix_ya = _dot(ya.astype(BF16), w_out_ref[0:D_SCONV, :])

        def xconv(col, width):
            cols = slice(col, col + width)
            raw = proj(O_XS + col, width)
            out = _silu(_causal_conv(raw, xbc_tail[:, cols], xconv_w_ref[:, cols], None) + xconv_b_ref[:, cols])
            xbc_tail[:, cols] = raw[t - SUBLANES:t]
            return out

        bc = xconv(D_SSM, D_BC)
        bm_b = bc[:, 0:SSM_GROUPS * SSM_STATE].astype(BF16)
        cm = bc[:, SSM_GROUPS * SSM_STATE:D_BC]
        a_cum, w, a_src_t, _ = _ssd_prepare_rows(dt_s[slot_b][:, r], dt_bias_ref[...], a_log_ref[...],
                                                 tri_ref[...])
        causal = lax.broadcasted_iota(jnp.int32, (t, t), 0) >= lax.broadcasted_iota(jnp.int32, (t, t), 1)
        stage_a((sub, "prepare"))

        def gate(p, y_p, xs_p):
            cols = slice(p * LANES, (p + 1) * LANES)
            return (y_p + d_full_ref[:, cols] * xs_p) * _silu(proj(O_Z + p * LANES, LANES))

        y, xw_b, ea_full, _ = _ssd_heads(lambda p: xconv(p * LANES, LANES), bm_b, cm, a_cum, w, a_src_t,
                                         causal, state_t_ref, gate, lambda p, *partial: stage_a((sub, p)))
        chunk_decay = ea_full[t - 1:t, :]
        for g in range(SSM_GROUPS):
            cols = slice(g * D_GROUP, (g + 1) * D_GROUP)
            state_t_ref[:, cols] = chunk_decay[:, cols] * state_t_ref[:, cols] + _dot_tn(
                bm_b[:, g * SSM_STATE:(g + 1) * SSM_STATE], xw_b[:, cols])
        yb = _group_rms_ssm(y, ssm_norm_ref[...])
        mix = mix_ya + _dot(yb.astype(BF16), w_out_ref[D_SCONV:D_MIX, :])
        stage_a((sub, "end"))
        h1_ref[0, r, :] = h_s[slot_b][r, :] + _rms(mix, npost_ref[...])


def _pmix(x_prompt, consts, sc0, xbc0, state0_t):
    batch, seq, _ = x_prompt.shape
    rows = TILE_CHUNKS * CHUNK
    sps = seq // rows
    n_tiles = batch * sps
    const_spec = lambda a: (pl.BlockSpec(a.shape, lambda k: (0, 0)) if a.ndim == 2
                            else _w_main_spec(lambda k: (0, 0, 0)))
    ins = list(consts) + [sc0, xbc0, state0_t]

    def tile_a(k):
        tile = jnp.minimum(k, n_tiles - 1)
        return (tile // sps, tile % sps, 0)

    def tile_b(k):
        tile = jnp.maximum(k - 1, 0)
        return (tile // sps, tile % sps, 0)

    def seq_b(k):
        return (jnp.maximum(k - 1, 0) // sps, 0, 0)

    return pl.pallas_call(
        functools.partial(_pmix_kernel, steps_per_seq=sps),
        grid=(n_tiles + 1,),
        in_specs=[pl.BlockSpec((1, rows, D_MODEL), tile_a)] + [const_spec(a) for a in ins],
        out_specs=[
            pl.BlockSpec((1, rows, D_MODEL), tile_b),
            pl.BlockSpec((1, SCONV_WIDTH - 1, D_SCONV), seq_b),
            pl.BlockSpec((1, SSM_CONV_WIDTH - 1, D_XBC), seq_b),
            pl.BlockSpec((1, D_SSM, SSM_STATE), seq_b),
        ],
        out_shape=[
            jax.ShapeDtypeStruct((batch, seq, D_MODEL), F32),
            jax.ShapeDtypeStruct((batch, SCONV_WIDTH - 1, D_SCONV), F32),
            jax.ShapeDtypeStruct((batch, SSM_CONV_WIDTH - 1, D_XBC), F32),
            jax.ShapeDtypeStruct((batch, D_SSM, SSM_STATE), F32),
        ],
        scratch_shapes=[
            pltpu.VMEM((rows, O_DT), F32),
            pltpu.VMEM((rows, O_DT), F32),
            pltpu.VMEM((SSM_HEADS, rows), F32),
            pltpu.VMEM((SSM_HEADS, rows), F32),
            pltpu.VMEM((rows, D_MODEL), F32),
            pltpu.VMEM((rows, D_MODEL), F32),
            pltpu.VMEM((rows, D_MODEL), BF16),
            pltpu.VMEM((SUBLANES, D_SCONV), F32),
            pltpu.VMEM((SUBLANES, D_XBC), F32),
            pltpu.VMEM((SSM_STATE, D_SSM), F32),
        ],
        compiler_params=pltpu.CompilerParams(dimension_semantics=("arbitrary",), vmem_limit_bytes=VMEM_LIMIT),
        name="prompt_mixer",
    )(x_prompt, *ins)


def _smix_kernel(x_ref, sc_prev_ref, xbc_prev_ref, state_in_ref, npre_ref, w_main_ref, w_dt_ref,
                 sconv_w_ref, sconv_norm_ref, xconv_w_ref, xconv_b_ref,
                 dt_bias_ref, a_log_ref, d_full_ref, ssm_norm_ref, w_out_ref, npost_ref,
                 h1_ref, u_ref, xbc_ref, state_ref, *, dec_seq):
    t = SEQ_TILE * dec_seq
    h = x_ref[...]
    hn = _rms(h, npre_ref[...]).astype(BF16)

    def proj(off, width):
        return _dot(hn, w_main_ref[0, :, off:off + width])

    ya, u = _sconv_mixer(proj(O_GATE_B, D_SCONV), proj(O_GATE_C, D_SCONV), proj(O_HV, D_SCONV),
                         sc_prev_ref[...], sconv_w_ref[...], sconv_norm_ref[...], dec_seq)
    for s in range(SEQ_TILE):
        u_ref[s] = u[(s + 1) * dec_seq - (SCONV_WIDTH - 1):(s + 1) * dec_seq, :]
    xbc = proj(O_XS, D_XBC)
    for s in range(SEQ_TILE):
        xbc_ref[s] = xbc[(s + 1) * dec_seq - (SSM_CONV_WIDTH - 1):(s + 1) * dec_seq, :]
    xbc_c = _silu(_causal_conv(xbc, xbc_prev_ref[...], xconv_w_ref[...], dec_seq) + xconv_b_ref[...])
    xs = xbc_c[:, 0:D_SSM]
    bm = xbc_c[:, D_SSM:D_SSM + SSM_GROUPS * SSM_STATE]
    bm_b = bm.astype(BF16)
    cm = xbc_c[:, D_SSM + SSM_GROUPS * SSM_STATE:D_XBC]

    row = lax.broadcasted_iota(jnp.int32, (t, t), 0)
    col = lax.broadcasted_iota(jnp.int32, (t, t), 1)
    same_seq = row // dec_seq == col // dec_seq
    mask = jnp.logical_and(row >= col, same_seq)
    tri = jnp.where(mask, 1.0, 0.0).astype(BF16)
    seq_ones = jnp.where(same_seq, 1.0, 0.0).astype(BF16)
    a_cum, w, a_src_t, a_total_t = _ssd_prepare_rows(_dot_nt(w_dt_ref[...], hn), dt_bias_ref[...],
                                                     a_log_ref[...], tri, seq_ones)
    y, xw_b, ea_full, _ = _ssd_heads(lambda p: xs[:, p * LANES:(p + 1) * LANES], bm_b, cm, a_cum, w, a_src_t,
                                     mask)

    row_id = lax.broadcasted_iota(jnp.int32, (t, SSM_STATE), 0) // dec_seq
    seq_decay_t = jnp.exp2(a_total_t)
    y_off_rows = []
    for s in range(SEQ_TILE):
        r0 = s * dec_seq
        y_off_g = []
        for g in range(SSM_GROUPS):
            s_prev = state_in_ref[s, g * D_GROUP:(g + 1) * D_GROUP, :]
            cm_s = cm[r0:r0 + dec_seq, g * SSM_STATE:(g + 1) * SSM_STATE].astype(BF16)
            y_off_g.append(_dot_nt(cm_s, s_prev.astype(BF16)))
        y_off_rows.append(jnp.concatenate(y_off_g, axis=1))
    y = y + jnp.concatenate(y_off_rows, axis=0) * ea_full
    yb = _ssm_tail(y, xs, proj(O_Z, D_SSM), d_full_ref[...], ssm_norm_ref[...])
    mix = _dot(ya.astype(BF16), w_out_ref[0:D_SCONV, :]) + _dot(yb.astype(BF16), w_out_ref[D_SCONV:D_MIX, :])
    h1_ref[...] = h + _rms(mix, npost_ref[...])

    for s in range(SEQ_TILE):
        r0 = s * dec_seq
        q_t = jnp.broadcast_to(seq_decay_t[:, r0:r0 + 1], (SSM_HEADS, SSM_STATE))
        for g in range(SSM_GROUPS):
            rows = slice(g * D_GROUP, (g + 1) * D_GROUP)
            bm_g = jnp.where(row_id == s, bm[:, g * SSM_STATE:(g + 1) * SSM_STATE], 0.0).astype(BF16)
            decay = jnp.concatenate(
                [jnp.broadcast_to(q_t[e:e + 1, :], (SSM_HEAD_DIM, SSM_STATE))
                 for e in range(g * HEADS_PER_GROUP, (g + 1) * HEADS_PER_GROUP)], axis=0)
            state_ref[s, rows, :] = decay * state_in_ref[s, rows, :] + _dot_tn(xw_b[:, rows], bm_g)


def _smix(x_rows, sc_prev, xbc_prev, state, consts, dec_seq):
    n_seq = state.shape[0]
    t = SEQ_TILE * dec_seq
    const_spec = lambda a: (pl.BlockSpec(a.shape, lambda i: (0, 0), pipeline_mode=pl.Buffered(1)) if a.ndim == 2
                            else _w_main_spec(lambda i: (0, 0, 0), pipeline_mode=pl.Buffered(1)))
    row_spec = lambda width: pl.BlockSpec((t, width), lambda i: (i, 0))
    state_spec = pl.BlockSpec((SEQ_TILE, D_SSM, SSM_STATE), lambda i: (i, 0, 0))
    return pl.pallas_call(
        functools.partial(_smix_kernel, dec_seq=dec_seq),
        grid=(n_seq // SEQ_TILE,),
        in_specs=[row_spec(D_MODEL),
                  pl.BlockSpec((SEQ_TILE, SCONV_WIDTH - 1, D_SCONV), lambda i: (i, 0, 0)),
                  pl.BlockSpec((SEQ_TILE, SSM_CONV_WIDTH - 1, D_XBC), lambda i: (i, 0, 0)), state_spec]
        + [const_spec(a) for a in consts],
        out_specs=[row_spec(D_MODEL),
                   pl.BlockSpec((SEQ_TILE, SCONV_WIDTH - 1, D_SCONV), lambda i: (i, 0, 0)),
                   pl.BlockSpec((SEQ_TILE, SSM_CONV_WIDTH - 1, D_XBC), lambda i: (i, 0, 0)), state_spec],
        out_shape=[
            jax.ShapeDtypeStruct((n_seq * dec_seq, D_MODEL), F32),
            jax.ShapeDtypeStruct((n_seq, SCONV_WIDTH - 1, D_SCONV), F32),
            jax.ShapeDtypeStruct((n_seq, SSM_CONV_WIDTH - 1, D_XBC), F32),
            jax.ShapeDtypeStruct(state.shape, F32),
        ],
        compiler_params=pltpu.CompilerParams(dimension_semantics=("arbitrary",), vmem_limit_bytes=VMEM_LIMIT),
        name="sample_mixer",
    )(x_rows, sc_prev, xbc_prev, state, *consts)


def _ffn_kernel(hp_ref, hs_ref, npre_ref, w_gate_ref, w_up_ref, w_down_ref, npost_ref, op_ref, os_ref, *,
                prompt_steps):
    def body(h_ref, o_ref):
        for r0 in range(0, FFN_ROWS, FFN_SUB_ROWS):
            r = slice(r0, r0 + FFN_SUB_ROWS)
            h1 = h_ref[r, :]
            fn = _rms(h1, npre_ref[...]).astype(BF16)
            act = (_silu(_dot(fn, w_gate_ref[...])) * _dot(fn, w_up_ref[...])).astype(BF16)
            o_ref[r, :] = h1 + _rms(_dot(act, w_down_ref[...]), npost_ref[...])

    step = pl.program_id(0)
    pl.when(step < prompt_steps)(functools.partial(body, hp_ref, op_ref))
    pl.when(step >= prompt_steps)(functools.partial(body, hs_ref, os_ref))


def _ffn(h_prompt, h_sample, ffn_consts):
    mp, ms = h_prompt.shape[0], h_sample.shape[0]
    rows = FFN_ROWS
    p_steps, s_steps = mp // rows, ms // rows
    p_spec = pl.BlockSpec((rows, D_MODEL), lambda i: (jnp.minimum(i, p_steps - 1), 0))
    s_spec = pl.BlockSpec((rows, D_MODEL), lambda i: (jnp.maximum(i - p_steps, 0), 0))
    const_spec = lambda a: pl.BlockSpec(a.shape, lambda i: (0, 0), pipeline_mode=pl.Buffered(1))
    return pl.pallas_call(
        functools.partial(_ffn_kernel, prompt_steps=p_steps),
        grid=(p_steps + s_steps,),
        in_specs=[p_spec, s_spec] + [const_spec(a) for a in ffn_consts],
        out_specs=[p_spec, s_spec],
        out_shape=[jax.ShapeDtypeStruct((mp, D_MODEL), F32), jax.ShapeDtypeStruct((ms, D_MODEL), F32)],
        compiler_params=pltpu.CompilerParams(dimension_semantics=("arbitrary",), vmem_limit_bytes=VMEM_LIMIT),
        name="ffn",
    )(h_prompt, h_sample, *ffn_consts)


def _row(v, width=None):
    v = v.reshape(1, -1).astype(F32)
    if width is not None and v.shape[1] < width:
        v = jnp.pad(v, ((0, 0), (0, width - v.shape[1])))
    return v


def _layer(hp, hs, sc_state, xbc_state, ssm_state, meta_tokens,
           norm_mix_pre, norm_mix_post, w_in_all, sconv_w, sconv_norm, ssm_conv_w, ssm_conv_b,
           dt_bias, a_log, d_skip, ssm_norm, w_out, norm_ffn_pre, norm_ffn_post, w_gate, w_up, w_down):
    batch, seq, _ = hp.shape
    w_in = w_in_all[0]
    n_seq, dec_seq, _ = hs.shape

    w_main_b = w_in_all.astype(BF16)
    w_dt_b = jnp.pad(w_in[:, O_DT:D_IN], ((0, 0), (0, D_IN_PAD - D_IN))).astype(BF16)
    w_dt_t = w_in[:, O_DT:D_IN].T.astype(BF16)
    w_out_b = w_out.astype(BF16)
    w_gate_b, w_up_b, w_down_b = w_gate.astype(BF16), w_up.astype(BF16), w_down.astype(BF16)
    npre, npost = _row(norm_mix_pre), _row(norm_mix_post)
    fpre, fpost = _row(norm_ffn_pre), _row(norm_ffn_post)
    sconv_w = sconv_w.astype(F32)
    xconv_w = ssm_conv_w.astype(F32)
    xconv_b = _row(ssm_conv_b)
    dt_bias_p, a_log_p = _row(dt_bias, LANES), _row(a_log, LANES)
    dt_bias_t = jnp.broadcast_to(dt_bias.astype(F32)[:, None], (SSM_HEADS, LANES))
    a_log_t = jnp.broadcast_to(a_log.astype(F32)[:, None], (SSM_HEADS, LANES))
    d_full = jnp.repeat(d_skip.astype(F32), SSM_HEAD_DIM).reshape(1, D_SSM)
    sconv_norm_r, ssm_norm_r = _row(sconv_norm), _row(ssm_norm)

    r = jnp.arange(CHUNK)
    tri = (r[:, None] >= r[None, :]).astype(BF16)
    expand = (jnp.arange(LANES)[:, None] == (jnp.arange(D_SSM)[None, :] // SSM_HEAD_DIM)).astype(BF16)

    mixer_consts = (sconv_w, sconv_norm_r, xconv_w, xconv_b, dt_bias_t, a_log_t, d_full, ssm_norm_r)
    ffn_consts = (fpre, w_gate_b, w_up_b, w_down_b, fpost)

    proj_meta = _inproj(meta_tokens.astype(F32), npre, w_main_b, w_dt_b)
    sc0, xbc0, state0_t = _meta_state(proj_meta, xconv_w, xconv_b, dt_bias_p, a_log_p, tri, expand)

    pm_consts = (npre, w_main_b, w_dt_t) + mixer_consts + (tri, w_out_b, npost)
    h1, new_sc_p, new_xbc_p, p_state = _pmix(hp, pm_consts, sc0, xbc0, state0_t)
    new_ssm_p = p_state.reshape(batch, SSM_GROUPS, HEADS_PER_GROUP, SSM_HEAD_DIM, SSM_STATE)

    hs_rows = hs.reshape(n_seq * dec_seq, D_MODEL)
    sm_consts = (npre, w_main_b, w_dt_t) + mixer_consts + (w_out_b, npost)
    h1_s, new_sc_s, new_xbc_s, s_state = _smix(hs_rows, sc_state, xbc_state, ssm_state.reshape(n_seq, D_SSM, SSM_STATE),
                                      sm_consts, dec_seq)

    y_prompt, y_sample = _ffn(h1.reshape(batch * seq, D_MODEL), h1_s, ffn_consts)
    y_prompt = y_prompt.reshape(batch, seq, D_MODEL)
    y_sample = y_sample.reshape(n_seq, dec_seq, D_MODEL)
    new_ssm_s = s_state.reshape(n_seq, SSM_GROUPS, HEADS_PER_GROUP, SSM_HEAD_DIM, SSM_STATE)
    return y_prompt, y_sample, new_sc_p, new_xbc_p, new_ssm_p, new_sc_s, new_xbc_s, new_ssm_s


def kernel(x_prompt, x_sample, state_sconv, state_ssm_conv, state_ssm, meta_tokens, norm_mix_pre, norm_mix_post, w_in, sconv_w, sconv_norm, ssm_conv_w, ssm_conv_b, dt_bias, A_log, D_skip, ssm_norm, w_out, norm_ffn_pre, norm_ffn_post, w_gate, w_up, w_down):
    depth = w_in.shape[0]
    assert depth == 1 and x_sample.shape[1] == SUBLANES
    outs = _layer(x_prompt, x_sample, state_sconv[0], state_ssm_conv[0], state_ssm[0], meta_tokens,
                  norm_mix_pre[0], norm_mix_post[0], w_in, sconv_w[0], sconv_norm[0], ssm_conv_w[0],
                  ssm_conv_b[0], dt_bias[0], A_log[0], D_skip[0], ssm_norm[0], w_out[0],
                  norm_ffn_pre[0], norm_ffn_post[0], w_gate[0], w_up[0], w_down[0])
    y_prompt, y_sample = outs[0], outs[1]
    return (y_prompt, y_sample) + tuple(o[None] for o in outs[2:])
```

```python
import functools

import jax
import jax.numpy as jnp
from jax import lax
from jax.experimental import pallas as pl
from jax.experimental.pallas import tpu as pltpu

D_MODEL = 1024
N_META = 16
D_SCONV = 512
SCONV_GROUPS = 8
SCONV_WIDTH = 3
SSM_HEAD_DIM = 64
SSM_HEADS = 16
D_SSM = 1024
SSM_GROUPS = 2
HEADS_PER_GROUP = 8
D_GROUP = HEADS_PER_GROUP * SSM_HEAD_DIM
SSM_STATE = 128
SSM_CONV_WIDTH = 4
D_BC = 2 * SSM_GROUPS * SSM_STATE
D_XBC = D_SSM + D_BC
D_MIX = D_SCONV + D_SSM
D_IN = 3 * D_SCONV + D_SSM + D_XBC + SSM_HEADS
EPS = 1e-6
LOG2_E = 1.4426950408889634

LANES = 128
SUBLANES = 8
D_IN_PAD = D_IN + (LANES - SSM_HEADS)
HEAD_PAIRS = D_SSM // LANES

O_GATE_B = 0
O_GATE_C = D_SCONV
O_HV = 2 * D_SCONV
O_Z = 3 * D_SCONV
O_XS = O_Z + D_SSM
O_BC = O_XS + D_SSM
O_DT = O_BC + D_BC

CHUNK = 128
TILE_CHUNKS = 2
STAGE_A_COLS = tuple(range(0, O_DT + 1, D_SCONV))
STAGE_A_PLAN = ((0, "sconv"), (0, "prepare"), (0, 1), (0, 3), (0, 5), (0, "end"), (1, "prepare"), (1, 1), (1, 3))
SEQ_TILE = 16
FFN_ROWS = 512
FFN_SUB_ROWS = 256
VMEM_LIMIT = 60000 * 1024

BF16 = jnp.bfloat16
F32 = jnp.float32


def _dot(a, b):
    return jnp.dot(a, b, preferred_element_type=F32)


def _dot_nt(a, b):
    return lax.dot_general(a, b, (((1,), (1,)), ((), ())), preferred_element_type=F32)


def _dot_tn(a, b):
    return lax.dot_general(a, b, (((0,), (0,)), ((), ())), preferred_element_type=F32)


def _split3(v):
    hi = v.astype(BF16)
    r1 = v - hi.astype(F32)
    mid = r1.astype(BF16)
    lo = (r1 - mid.astype(F32)).astype(BF16)
    return hi, mid, lo


def _dot3(v, m):
    hi, mid, lo = _split3(v)
    return _dot(hi, m) + _dot(mid, m) + _dot(lo, m)


def _dot3_left(m, v):
    hi, mid, lo = _split3(v)
    return _dot(m, hi) + _dot(m, mid) + _dot(m, lo)


def _rms(x, w):
    return x * lax.rsqrt(jnp.mean(x * x, axis=-1, keepdims=True) + EPS) * w


def _silu(x):
    hx = 0.5 * x
    return hx + hx * jnp.tanh(hx)


def _softplus(x):
    return jnp.maximum(x, 0.0) + jnp.log1p(jnp.exp(-jnp.abs(x)))


def _shifted(u, prev, j, rows_per_seq):
    t = u.shape[0]
    s = pltpu.roll(u, j, 0)
    if rows_per_seq is None:
        row = lax.broadcasted_iota(jnp.int32, (SUBLANES, u.shape[1]), 0)
        first = jnp.where(row < j, pltpu.roll(prev, j, 0), s[0:SUBLANES])
        if t == SUBLANES:
            return first
        return jnp.concatenate([first, s[SUBLANES:]], axis=0)
    row = lax.broadcasted_iota(jnp.int32, u.shape, 0) % rows_per_seq
    n_buf = prev.shape[1]
    for tt in range(j):
        buf_row = prev[:, n_buf - j + tt, :]
        spread = jnp.broadcast_to(buf_row[:, None, :], (buf_row.shape[0], rows_per_seq, u.shape[1])).reshape(u.shape)
        s = jnp.where(row == tt, spread, s)
    return s


def _causal_conv(u, prev, w, rows_per_seq):
    k = w.shape[0]
    out = _shifted(u, prev, k - 1, rows_per_seq) * w[0:1]
    for i in range(1, k - 1):
        out = out + _shifted(u, prev, k - 1 - i, rows_per_seq) * w[i:i + 1]
    return out + u * w[k - 1:k]


def _sconv_mixer(gate_b, gate_c, hv, prev, sconv_w, sconv_norm, rows_per_seq):
    u = gate_c * hv
    conv_u = _causal_conv(u, prev, sconv_w, rows_per_seq)
    return _sconv_norm(gate_b, conv_u, sconv_norm), u


def _sconv_norm(gate_b, conv_u, sconv_norm):
    v = gate_b * conv_u
    first_half = lax.broadcasted_iota(jnp.int32, (v.shape[0], LANES), 1) < D_SCONV // SCONV_GROUPS
    parts = []
    for c in range(0, D_SCONV, LANES):
        vc = v[:, c:c + LANES]
        sq = vc * vc
        s_first = jnp.sum(jnp.where(first_half, sq, 0.0), axis=-1, keepdims=True)
        s_second = jnp.sum(jnp.where(first_half, 0.0, sq), axis=-1, keepdims=True)
        mean_sq = jnp.where(first_half, s_first, s_second) * (SCONV_GROUPS / D_SCONV)
        parts.append(vc * lax.rsqrt(mean_sq + EPS))
    return jnp.concatenate(parts, axis=1) * sconv_norm


def _ssd_prepare(dt_raw, dt_bias, a_log, tri, t):
    lane = lax.broadcasted_iota(jnp.int32, (t, LANES), 1)
    valid = lane < SSM_HEADS
    dt = jnp.where(valid, _softplus(dt_raw + dt_bias), 0.0)
    a = dt * (-jnp.exp(a_log))
    a_cum = _dot3_left(tri, a)
    return dt, a_cum


def _ssd_prepare_rows(dt_raw_t, dt_bias_t, a_log_t, tri, seq_ones=None):
    t = dt_raw_t.shape[1]
    dt_t = _softplus(dt_raw_t + dt_bias_t)
    a_t = dt_t * (-(jnp.exp(a_log_t) * LOG2_E))
    hi, mid, lo = _split3(a_t)
    a_cum_t = _dot_nt(hi, tri) + _dot_nt(mid, tri) + _dot_nt(lo, tri)
    if seq_ones is None:
        a_total_t = jnp.broadcast_to(a_cum_t[:, t - 1:t], (SSM_HEADS, t))
    else:
        a_total_t = _dot_nt(hi, seq_ones) + _dot_nt(mid, seq_ones) + _dot_nt(lo, seq_ones)
    w_t = jnp.exp2(a_total_t - a_cum_t) * dt_t
    pad = jnp.zeros((LANES - SSM_HEADS, t), F32)
    a_cum = jnp.concatenate([a_cum_t, pad], axis=0).T
    w = jnp.concatenate([w_t, pad], axis=0).T
    return a_cum, w, a_cum_t, dt_t, a_total_t


def _group_rms_ssm(y, w):
    parts = []
    for g in range(SSM_GROUPS):
        yg = y[:, g * D_GROUP:(g + 1) * D_GROUP]
        parts.append(yg * lax.rsqrt(jnp.mean(yg * yg, axis=-1, keepdims=True) + EPS))
    return jnp.concatenate(parts, axis=1) * w


def _ssd_heads(xs_of, bm_b, cm, a_cum, w, a_cum_t, dt_t, mask, state_t=None, gate=None, after_pair=None):
    t = a_cum.shape[0]
    assert t == LANES
    first_half = lax.broadcasted_iota(jnp.int32, (t, LANES), 1) < SSM_HEAD_DIM
    y_pairs, xw_pairs, ea_pairs, xs_pairs = [], [], [], []
    for g in range(SSM_GROUPS):
        cm_g = cm[:, g * SSM_STATE:(g + 1) * SSM_STATE]
        cb = _dot_nt(cm_g.astype(BF16), bm_b[:, g * SSM_STATE:(g + 1) * SSM_STATE])
        for q in range(HEADS_PER_GROUP // 2):
            pair = g * (HEADS_PER_GROUP // 2) + q
            cols = slice(pair * LANES, (pair + 1) * LANES)
            xs_p = xs_of(pair)
            rhs = xs_p.astype(BF16)
            if state_t is not None:
                rhs = jnp.concatenate([rhs, state_t[:, cols].astype(BF16)], axis=0)
            res, eas, ws = [], [], []
            for e in (2 * pair, 2 * pair + 1):
                a_col = jnp.broadcast_to(a_cum[:, e:e + 1], (t, LANES))
                decay = jnp.exp2(jnp.where(mask, a_col - a_cum_t[e:e + 1, :], -jnp.inf))
                lhs = (cb * decay * dt_t[e:e + 1, :]).astype(BF16)
                ea = jnp.exp2(a_col)
                if state_t is not None:
                    lhs = jnp.concatenate([lhs, (cm_g * ea).astype(BF16)], axis=1)
                res.append(_dot(lhs, rhs))
                eas.append(ea)
                ws.append(jnp.broadcast_to(w[:, e:e + 1], (t, LANES)))
            y_p = jnp.where(first_half, res[0], res[1])
            y_pairs.append(y_p if gate is None else gate(pair, y_p, xs_p))
            xw_pairs.append((xs_p * jnp.where(first_half, ws[0], ws[1])).astype(BF16))
            ea_pairs.append(jnp.where(first_half, eas[0], eas[1]))
            xs_pairs.append(xs_p)
            if after_pair is not None:
                after_pair(pair, y_pairs, xw_pairs, ea_pairs)
    return (jnp.concatenate(y_pairs, axis=1), jnp.concatenate(xw_pairs, axis=1),
            jnp.concatenate(ea_pairs, axis=1), jnp.concatenate(xs_pairs, axis=1))


def _ssm_tail(y, xs, z, d_full, ssm_norm):
    y = y + d_full * xs
    y = y * _silu(z)
    return _group_rms_ssm(y, ssm_norm)


def _w_main_spec(index_map, **kwargs):
    return pl.BlockSpec((1, D_MODEL, O_DT), index_map, **kwargs)


def _inproj_kernel(x_ref, npre_ref, w_main_ref, w_dt_ref, o_ref):
    hn = _rms(x_ref[...], npre_ref[...]).astype(BF16)
    o_ref[:, 0:O_DT] = _dot(hn, w_main_ref[0])
    o_ref[:, O_DT:D_IN_PAD] = _dot(hn, w_dt_ref[...])


def _inproj(x, npre, w_main_b, w_dt_b):
    m = x.shape[0]
    once = lambda shape: pl.BlockSpec(shape, lambda i: (0, 0), pipeline_mode=pl.Buffered(1))
    return pl.pallas_call(
        _inproj_kernel,
        grid=(1,),
        in_specs=[once((m, D_MODEL)), once((1, D_MODEL)),
                  _w_main_spec(lambda i: (0, 0, 0), pipeline_mode=pl.Buffered(1)), once((D_MODEL, LANES))],
        out_specs=once((m, D_IN_PAD)),
        out_shape=jax.ShapeDtypeStruct((m, D_IN_PAD), F32),
        compiler_params=pltpu.CompilerParams(dimension_semantics=("arbitrary",), vmem_limit_bytes=VMEM_LIMIT),
        name="inproj",
    )(x, npre, w_main_b, w_dt_b)


def _meta_kernel(proj_ref, xconv_w_ref, xconv_b_ref, dt_bias_ref, a_log_ref, tri_ref, expand_ref,
                 sc_tail_ref, xbc_tail_ref, state_t_ref):
    t = N_META
    proj = proj_ref[...]
    u = proj[:, O_GATE_C:O_GATE_C + D_SCONV] * proj[:, O_HV:O_HV + D_SCONV]
    sc_tail_ref[...] = u[t - SUBLANES:t]
    xbc = proj[:, O_XS:O_XS + D_XBC]
    xbc_tail_ref[...] = xbc[t - SUBLANES:t]
    zeros_prev = jnp.zeros((SUBLANES, D_XBC), F32)
    xbc_c = _silu(_causal_conv(xbc, zeros_prev, xconv_w_ref[...], None) + xconv_b_ref[...])
    xs = xbc_c[:, 0:D_SSM]
    dt, a_cum = _ssd_prepare(proj[:, O_DT:O_DT + LANES], dt_bias_ref[...], a_log_ref[...],
                             tri_ref[0:t, 0:t], t)
    w = jnp.exp(a_cum[t - 1:t, :] - a_cum) * dt
    xw_b = (xs * _dot3(w, expand_ref[...])).astype(BF16)
    for g in range(SSM_GROUPS):
        bm_b = xbc_c[:, D_SSM + g * SSM_STATE:D_SSM + (g + 1) * SSM_STATE].astype(BF16)
        state_t_ref[:, g * D_GROUP:(g + 1) * D_GROUP] = _dot_tn(bm_b, xw_b[:, g * D_GROUP:(g + 1) * D_GROUP])


def _meta_state(proj_meta, xconv_w, xconv_b, dt_bias, a_log, tri, expand):
    full = lambda shape: pl.BlockSpec(shape, lambda: tuple(0 for _ in shape))
    return pl.pallas_call(
        _meta_kernel,
        in_specs=[full(proj_meta.shape), full(xconv_w.shape), full(xconv_b.shape), full(dt_bias.shape),
                  full(a_log.shape), full(tri.shape), full(expand.shape)],
        out_specs=[full((SUBLANES, D_SCONV)), full((SUBLANES, D_XBC)), full((SSM_STATE, D_SSM))],
        out_shape=[jax.ShapeDtypeStruct((SUBLANES, D_SCONV), F32),
                   jax.ShapeDtypeStruct((SUBLANES, D_XBC), F32),
                   jax.ShapeDtypeStruct((SSM_STATE, D_SSM), F32)],
        name="meta_state",
    )(proj_meta, xconv_w, xconv_b, dt_bias, a_log, tri, expand)


def _pmix_kernel(x_ref, npre_ref, w_main_ref, w_dt_ref, sconv_w_ref, sconv_norm_ref,
                 xconv_w_ref, xconv_b_ref, dt_bias_ref, a_log_ref, d_full_ref, ssm_norm_ref,
                 tri_ref, w_out_ref, npost_ref, sc0_ref, xbc0_ref, state0_t_ref,
                 h1_ref, sc_out_ref, xbc_out_ref, state_ref,
                 proj_s0, proj_s1, dt_s0, dt_s1, h_s0, h_s1, hn_s, sc_tail, xbc_tail, state_t_ref, *,
                 steps_per_seq):
    rows = TILE_CHUNKS * CHUNK
    k = pl.program_id(0)
    step_in_seq = lax.rem(jnp.maximum(k - 1, 0), steps_per_seq)

    @pl.when(k == 0)
    def _():
        proj_s1[...] = jnp.zeros((rows, O_DT), F32)
        dt_s1[...] = jnp.zeros((SSM_HEADS, rows), F32)
        h_s1[...] = jnp.zeros((rows, D_MODEL), F32)

    @pl.when(step_in_seq == 0)
    def _():
        sc_tail[...] = sc0_ref[...]
        xbc_tail[...] = xbc0_ref[...]
        state_t_ref[...] = state0_t_ref[...]

    for parity in (0, 1):
        pl.when(lax.rem(k, 2) == parity)(functools.partial(
            _pmix_step, parity, x_ref, npre_ref, w_main_ref, w_dt_ref, sconv_w_ref, sconv_norm_ref,
            xconv_w_ref, xconv_b_ref, dt_bias_ref, a_log_ref, d_full_ref, ssm_norm_ref, tri_ref, w_out_ref,
            npost_ref, h1_ref, (proj_s0, proj_s1), (dt_s0, dt_s1), (h_s0, h_s1), hn_s, sc_tail, xbc_tail,
            state_t_ref))

    @pl.when(step_in_seq == steps_per_seq - 1)
    def _():
        sc_out_ref[0] = sc_tail[SUBLANES - (SCONV_WIDTH - 1):SUBLANES, :]
        xbc_out_ref[0] = xbc_tail[SUBLANES - (SSM_CONV_WIDTH - 1):SUBLANES, :]
        for p in range(HEAD_PAIRS):
            state_ref[0, p * LANES:(p + 1) * LANES, :] = state_t_ref[:, p * LANES:(p + 1) * LANES].T


def _pmix_step(slot_a, x_ref, npre_ref, w_main_ref, w_dt_ref, sconv_w_ref, sconv_norm_ref,
               xconv_w_ref, xconv_b_ref, dt_bias_ref, a_log_ref, d_full_ref, ssm_norm_ref, tri_ref, w_out_ref,
               npost_ref, h1_ref, proj_s, dt_s, h_s, hn_s, sc_tail, xbc_tail, state_t_ref):
    t = CHUNK
    slot_b = 1 - slot_a

    h_a = x_ref[0]
    h_s[slot_a][...] = h_a
    hn_s[...] = _rms(h_a, npre_ref[...]).astype(BF16)

    def stage_a(point):
        for piece, planned in enumerate(STAGE_A_PLAN):
            if planned == point:
                if piece + 1 < len(STAGE_A_COLS):
                    off, end = STAGE_A_COLS[piece], STAGE_A_COLS[piece + 1]
                    proj_s[slot_a][:, off:end] = _dot(hn_s[...], w_main_ref[0, :, off:end])
                else:
                    dt_s[slot_a][...] = _dot_nt(w_dt_ref[...], hn_s[...])

    for sub in range(TILE_CHUNKS):
        r = slice(sub * t, (sub + 1) * t)

        def proj(off, width):
            return proj_s[slot_b][r, off:off + width]

        ya, u = _sconv_mixer(proj(O_GATE_B, D_SCONV), proj(O_GATE_C, D_SCONV), proj(O_HV, D_SCONV),
                             sc_tail[...], sconv_w_ref[...], sconv_norm_ref[...], None)
        sc_tail[...] = u[t - SUBLANES:t]
        stage_a((sub, "sconv"))
        mix_ya = _dot(ya.astype(BF16), w_out_ref[0:D_SCONV, :])

        def xconv(col, width):
            cols = slice(col, col + width)
            raw = proj(O_XS + col, width)
            out = _silu(_causal_conv(raw, xbc_tail[:, cols], xconv_w_ref[:, cols], None) + xconv_b_ref[:, cols])
            xbc_tail[:, cols] = raw[t - SUBLANES:t]
            return out

        bc = xconv(D_SSM, D_BC)
        bm_b = bc[:, 0:SSM_GROUPS * SSM_STATE].astype(BF16)
        cm = bc[:, SSM_GROUPS * SSM_STATE:D_BC]
        a_cum, w, a_cum_t, dt_t, _ = _ssd_prepare_rows(dt_s[slot_b][:, r], dt_bias_ref[...], a_log_ref[...],
                                                       tri_ref[...])
        causal = lax.broadcasted_iota(jnp.int32, (t, t), 0) >= lax.broadcasted_iota(jnp.int32, (t, t), 1)
        stage_a((sub, "prepare"))

        def gate(p, y_p, xs_p):
            cols = slice(p * LANES, (p + 1) * LANES)
            return (y_p + d_full_ref[:, cols] * xs_p) * _silu(proj(O_Z + p * LANES, LANES))

        y, xw_b, ea_full, _ = _ssd_heads(lambda p: xconv(p * LANES, LANES), bm_b, cm, a_cum, w, a_cum_t, dt_t,
                                         causal, state_t_ref, gate, lambda p, *partial: stage_a((sub, p)))
        chunk_decay = ea_full[t - 1:t, :]
        for g in range(SSM_GROUPS):
            cols = slice(g * D_GROUP, (g + 1) * D_GROUP)
            state_t_ref[:, cols] = chunk_decay[:, cols] * state_t_ref[:, cols] + _dot_tn(
                bm_b[:, g * SSM_STATE:(g + 1) * SSM_STATE], xw_b[:, cols])
        yb = _group_rms_ssm(y, ssm_norm_ref[...])
        mix = mix_ya + _dot(yb.astype(BF16), w_out_ref[D_SCONV:D_MIX, :])
        stage_a((sub, "end"))
        h1_ref[0, r, :] = h_s[slot_b][r, :] + _rms(mix, npost_ref[...])


def _pmix(x_prompt, consts, sc0, xbc0, state0_t):
    batch, seq, _ = x_prompt.shape
    rows = TILE_CHUNKS * CHUNK
    sps = seq // rows
    n_tiles = batch * sps
    const_spec = lambda a: (pl.BlockSpec(a.shape, lambda k: (0, 0), pipeline_mode=pl.Buffered(1)) if a.ndim == 2
                            else _w_main_spec(lambda k: (0, 0, 0), pipeline_mode=pl.Buffered(1)))
    ins = list(consts) + [sc0, xbc0, state0_t]

    def tile_a(k):
        tile = jnp.minimum(k, n_tiles - 1)
        return (tile // sps, tile % sps, 0)

    def tile_b(k):
        tile = jnp.maximum(k - 1, 0)
        return (tile // sps, tile % sps, 0)

    def seq_b(k):
        return (jnp.maximum(k - 1, 0) // sps, 0, 0)

    return pl.pallas_call(
        functools.partial(_pmix_kernel, steps_per_seq=sps),
        grid=(n_tiles + 1,),
        in_specs=[pl.BlockSpec((1, rows, D_MODEL), tile_a)] + [const_spec(a) for a in ins],
        out_specs=[
            pl.BlockSpec((1, rows, D_MODEL), tile_b),
            pl.BlockSpec((1, SCONV_WIDTH - 1, D_SCONV), seq_b),
            pl.BlockSpec((1, SSM_CONV_WIDTH - 1, D_XBC), seq_b),
            pl.BlockSpec((1, D_SSM, SSM_STATE), seq_b),
        ],
        out_shape=[
            jax.ShapeDtypeStruct((batch, seq, D_MODEL), F32),
            jax.ShapeDtypeStruct((batch, SCONV_WIDTH - 1, D_SCONV), F32),
            jax.ShapeDtypeStruct((batch, SSM_CONV_WIDTH - 1, D_XBC), F32),
            jax.ShapeDtypeStruct((batch, D_SSM, SSM_STATE), F32),
        ],
        scratch_shapes=[
            pltpu.VMEM((rows, O_DT), F32),
            pltpu.VMEM((rows, O_DT), F32),
            pltpu.VMEM((SSM_HEADS, rows), F32),
            pltpu.VMEM((SSM_HEADS, rows), F32),
            pltpu.VMEM((rows, D_MODEL), F32),
            pltpu.VMEM((rows, D_MODEL), F32),
            pltpu.VMEM((rows, D_MODEL), BF16),
            pltpu.VMEM((SUBLANES, D_SCONV), F32),
            pltpu.VMEM((SUBLANES, D_XBC), F32),
            pltpu.VMEM((SSM_STATE, D_SSM), F32),
        ],
        compiler_params=pltpu.CompilerParams(dimension_semantics=("arbitrary",), vmem_limit_bytes=VMEM_LIMIT),
        name="prompt_mixer",
    )(x_prompt, *ins)


def _smix_kernel(x_ref, sc_prev_ref, xbc_prev_ref, state_in_ref, npre_ref, w_main_ref, w_dt_ref,
                 sconv_w_ref, sconv_norm_ref, xconv_w_ref, xconv_b_ref,
                 dt_bias_ref, a_log_ref, d_full_ref, ssm_norm_ref, w_out_ref, npost_ref,
                 h1_ref, u_ref, xbc_ref, state_ref, *, dec_seq):
    t = SEQ_TILE * dec_seq
    h = x_ref[...]
    hn = _rms(h, npre_ref[...]).astype(BF16)

    def proj(off, width):
        return _dot(hn, w_main_ref[0, :, off:off + width])

    ya, u = _sconv_mixer(proj(O_GATE_B, D_SCONV), proj(O_GATE_C, D_SCONV), proj(O_HV, D_SCONV),
                         sc_prev_ref[...], sconv_w_ref[...], sconv_norm_ref[...], dec_seq)
    for s in range(SEQ_TILE):
        u_ref[s] = u[(s + 1) * dec_seq - (SCONV_WIDTH - 1):(s + 1) * dec_seq, :]
    xbc = proj(O_XS, D_XBC)
    for s in range(SEQ_TILE):
        xbc_ref[s] = xbc[(s + 1) * dec_seq - (SSM_CONV_WIDTH - 1):(s + 1) * dec_seq, :]
    xbc_c = _silu(_causal_conv(xbc, xbc_prev_ref[...], xconv_w_ref[...], dec_seq) + xconv_b_ref[...])
    xs = xbc_c[:, 0:D_SSM]
    bm = xbc_c[:, D_SSM:D_SSM + SSM_GROUPS * SSM_STATE]
    bm_b = bm.astype(BF16)
    cm = xbc_c[:, D_SSM + SSM_GROUPS * SSM_STATE:D_XBC]

    row = lax.broadcasted_iota(jnp.int32, (t, t), 0)
    col = lax.broadcasted_iota(jnp.int32, (t, t), 1)
    same_seq = row // dec_seq == col // dec_seq
    mask = jnp.logical_and(row >= col, same_seq)
    tri = jnp.where(mask, 1.0, 0.0).astype(BF16)
    seq_ones = jnp.where(same_seq, 1.0, 0.0).astype(BF16)
    a_cum, w, a_cum_t, dt_t, a_total_t = _ssd_prepare_rows(_dot_nt(w_dt_ref[...], hn), dt_bias_ref[...],
                                                           a_log_ref[...], tri, seq_ones)
    y, xw_b, ea_full, _ = _ssd_heads(lambda p: xs[:, p * LANES:(p + 1) * LANES], bm_b, cm, a_cum, w, a_cum_t,
                                     dt_t, mask)

    row_id = lax.broadcasted_iota(jnp.int32, (t, SSM_STATE), 0) // dec_seq
    seq_decay_t = jnp.exp2(a_total_t)
    y_off_rows = []
    for s in range(SEQ_TILE):
        r0 = s * dec_seq
        y_off_g = []
        for g in range(SSM_GROUPS):
            s_prev = state_in_ref[s, g * D_GROUP:(g + 1) * D_GROUP, :]
            cm_s = cm[r0:r0 + dec_seq, g * SSM_STATE:(g + 1) * SSM_STATE].astype(BF16)
            y_off_g.append(_dot_nt(cm_s, s_prev.astype(BF16)))
        y_off_rows.append(jnp.concatenate(y_off_g, axis=1))
    y = y + jnp.concatenate(y_off_rows, axis=0) * ea_full
    yb = _ssm_tail(y, xs, proj(O_Z, D_SSM), d_full_ref[...], ssm_norm_ref[...])
    mix = _dot(ya.astype(BF16), w_out_ref[0:D_SCONV, :]) + _dot(yb.astype(BF16), w_out_ref[D_SCONV:D_MIX, :])
    h1_ref[...] = h + _rms(mix, npost_ref[...])

    for s in range(SEQ_TILE):
        r0 = s * dec_seq
        q_t = jnp.broadcast_to(seq_decay_t[:, r0:r0 + 1], (SSM_HEADS, SSM_STATE))
        for g in range(SSM_GROUPS):
            rows = slice(g * D_GROUP, (g + 1) * D_GROUP)
            bm_g = jnp.where(row_id == s, bm[:, g * SSM_STATE:(g + 1) * SSM_STATE], 0.0).astype(BF16)
            decay = jnp.concatenate(
                [jnp.broadcast_to(q_t[e:e + 1, :], (SSM_HEAD_DIM, SSM_STATE))
                 for e in range(g * HEADS_PER_GROUP, (g + 1) * HEADS_PER_GROUP)], axis=0)
            state_ref[s, rows, :] = decay * state_in_ref[s, rows, :] + _dot_tn(xw_b[:, rows], bm_g)


def _smix(x_rows, sc_prev, xbc_prev, state, consts, dec_seq):
    n_seq = state.shape[0]
    t = SEQ_TILE * dec_seq
    const_spec = lambda a: (pl.BlockSpec(a.shape, lambda i: (0, 0), pipeline_mode=pl.Buffered(1)) if a.ndim == 2
                            else _w_main_spec(lambda i: (0, 0, 0), pipeline_mode=pl.Buffered(1)))
    row_spec = lambda width: pl.BlockSpec((t, width), lambda i: (i, 0))
    state_spec = pl.BlockSpec((SEQ_TILE, D_SSM, SSM_STATE), lambda i: (i, 0, 0))
    return pl.pallas_call(
        functools.partial(_smix_kernel, dec_seq=dec_seq),
        grid=(n_seq // SEQ_TILE,),
        in_specs=[row_spec(D_MODEL),
                  pl.BlockSpec((SEQ_TILE, SCONV_WIDTH - 1, D_SCONV), lambda i: (i, 0, 0)),
                  pl.BlockSpec((SEQ_TILE, SSM_CONV_WIDTH - 1, D_XBC), lambda i: (i, 0, 0)), state_spec]
        + [const_spec(a) for a in consts],
        out_specs=[row_spec(D_MODEL),
                   pl.BlockSpec((SEQ_TILE, SCONV_WIDTH - 1, D_SCONV), lambda i: (i, 0, 0)),
                   pl.BlockSpec((SEQ_TILE, SSM_CONV_WIDTH - 1, D_XBC), lambda i: (i, 0, 0)), state_spec],
        out_shape=[
            jax.ShapeDtypeStruct((n_seq * dec_seq, D_MODEL), F32),
            jax.ShapeDtypeStruct((n_seq, SCONV_WIDTH - 1, D_SCONV), F32),
            jax.ShapeDtypeStruct((n_seq, SSM_CONV_WIDTH - 1, D_XBC), F32),
            jax.ShapeDtypeStruct(state.shape, F32),
        ],
        compiler_params=pltpu.CompilerParams(dimension_semantics=("arbitrary",), vmem_limit_bytes=VMEM_LIMIT),
        name="sample_mixer",
    )(x_rows, sc_prev, xbc_prev, state, *consts)


def _ffn_kernel(hp_ref, hs_ref, npre_ref, w_gate_ref, w_up_ref, w_down_ref, npost_ref, op_ref, os_ref, *,
                prompt_steps):
    def body(h_ref, o_ref):
        for r0 in range(0, FFN_ROWS, FFN_SUB_ROWS):
            r = slice(r0, r0 + FFN_SUB_ROWS)
            h1 = h_ref[r, :]
            fn = _rms(h1, npre_ref[...]).astype(BF16)
            act = (_silu(_dot(fn, w_gate_ref[...])) * _dot(fn, w_up_ref[...])).astype(BF16)
            o_ref[r, :] = h1 + _rms(_dot(act, w_down_ref[...]), npost_ref[...])

    step = pl.program_id(0)
    pl.when(step < prompt_steps)(functools.partial(body, hp_ref, op_ref))
    pl.when(step >= prompt_steps)(functools.partial(body, hs_ref, os_ref))


def _ffn(h_prompt, h_sample, ffn_consts):
    mp, ms = h_prompt.shape[0], h_sample.shape[0]
    rows = FFN_ROWS
    p_steps, s_steps = mp // rows, ms // rows
    p_spec = pl.BlockSpec((rows, D_MODEL), lambda i: (jnp.minimum(i, p_steps - 1), 0))
    s_spec = pl.BlockSpec((rows, D_MODEL), lambda i: (jnp.maximum(i - p_steps, 0), 0))
    const_spec = lambda a: pl.BlockSpec(a.shape, lambda i: (0, 0), pipeline_mode=pl.Buffered(1))
    return pl.pallas_call(
        functools.partial(_ffn_kernel, prompt_steps=p_steps),
        grid=(p_steps + s_steps,),
        in_specs=[p_spec, s_spec] + [const_spec(a) for a in ffn_consts],
        out_specs=[p_spec, s_spec],
        out_shape=[jax.ShapeDtypeStruct((mp, D_MODEL), F32), jax.ShapeDtypeStruct((ms, D_MODEL), F32)],
        compiler_params=pltpu.CompilerParams(dimension_semantics=("arbitrary",), vmem_limit_bytes=VMEM_LIMIT),
        name="ffn",
    )(h_prompt, h_sample, *ffn_consts)


def _row(v, width=None):
    v = v.reshape(1, -1).astype(F32)
    if width is not None and v.shape[1] < width:
        v = jnp.pad(v, ((0, 0), (0, width - v.shape[1])))
    return v


def _layer(hp, hs, sc_state, xbc_state, ssm_state, meta_tokens,
           norm_mix_pre, norm_mix_post, w_in_all, sconv_w, sconv_norm, ssm_conv_w, ssm_conv_b,
           dt_bias, a_log, d_skip, ssm_norm, w_out, norm_ffn_pre, norm_ffn_post, w_gate, w_up, w_down):
    batch, seq, _ = hp.shape
    w_in = w_in_all[0]
    n_seq, dec_seq, _ = hs.shape

    w_main_b = w_in_all.astype(BF16)
    w_dt_b = jnp.pad(w_in[:, O_DT:D_IN], ((0, 0), (0, D_IN_PAD - D_IN))).astype(BF16)
    w_dt_t = w_in[:, O_DT:D_IN].T.astype(BF16)
    w_out_b = w_out.astype(BF16)
    w_gate_b, w_up_b, w_down_b = w_gate.astype(BF16), w_up.astype(BF16), w_down.astype(BF16)
    npre, npost = _row(norm_mix_pre), _row(norm_mix_post)
    fpre, fpost = _row(norm_ffn_pre), _row(norm_ffn_post)
    sconv_w = sconv_w.astype(F32)
    xconv_w = ssm_conv_w.astype(F32)
    xconv_b = _row(ssm_conv_b)
    dt_bias_p, a_log_p = _row(dt_bias, LANES), _row(a_log, LANES)
    dt_bias_t = jnp.broadcast_to(dt_bias.astype(F32)[:, None], (SSM_HEADS, LANES))
    a_log_t = jnp.broadcast_to(a_log.astype(F32)[:, None], (SSM_HEADS, LANES))
    d_full = jnp.repeat(d_skip.astype(F32), SSM_HEAD_DIM).reshape(1, D_SSM)
    sconv_norm_r, ssm_norm_r = _row(sconv_norm), _row(ssm_norm)

    r = jnp.arange(CHUNK)
    tri = (r[:, None] >= r[None, :]).astype(BF16)
    expand = (jnp.arange(LANES)[:, None] == (jnp.arange(D_SSM)[None, :] // SSM_HEAD_DIM)).astype(BF16)

    mixer_consts = (sconv_w, sconv_norm_r, xconv_w, xconv_b, dt_bias_t, a_log_t, d_full, ssm_norm_r)
    ffn_consts = (fpre, w_gate_b, w_up_b, w_down_b, fpost)

    proj_meta = _inproj(meta_tokens.astype(F32), npre, w_main_b, w_dt_b)
    sc0, xbc0, state0_t = _meta_state(proj_meta, xconv_w, xconv_b, dt_bias_p, a_log_p, tri, expand)

    pm_consts = (npre, w_main_b, w_dt_t) + mixer_consts + (tri, w_out_b, npost)
    h1, new_sc_p, new_xbc_p, p_state = _pmix(hp, pm_consts, sc0, xbc0, state0_t)
    new_ssm_p = p_state.reshape(batch, SSM_GROUPS, HEADS_PER_GROUP, SSM_HEAD_DIM, SSM_STATE)

    hs_rows = hs.reshape(n_seq * dec_seq, D_MODEL)
    sm_consts = (npre, w_main_b, w_dt_t) + mixer_consts + (w_out_b, npost)
    h1_s, new_sc_s, new_xbc_s, s_state = _smix(hs_rows, sc_state, xbc_state, ssm_state.reshape(n_seq, D_SSM, SSM_STATE),
                                      sm_consts, dec_seq)

    y_prompt, y_sample = _ffn(h1.reshape(batch * seq, D_MODEL), h1_s, ffn_consts)
    y_prompt = y_prompt.reshape(batch, seq, D_MODEL)
    y_sample = y_sample.reshape(n_seq, dec_seq, D_MODEL)
    new_ssm_s = s_state.reshape(n_seq, SSM_GROUPS, HEADS_PER_GROUP, SSM_HEAD_DIM, SSM_STATE)
    return y_prompt, y_sample, new_sc_p, new_xbc_p, new_ssm_p, new_sc_s, new_xbc_s, new_ssm_s


def kernel(x_prompt, x_sample, state_sconv, state_ssm_conv, state_ssm, meta_tokens, norm_mix_pre, norm_mix_post, w_in, sconv_w, sconv_norm, ssm_conv_w, ssm_conv_b, dt_bias, A_log, D_skip, ssm_norm, w_out, norm_ffn_pre, norm_ffn_post, w_gate, w_up, w_down):
    depth = w_in.shape[0]
    assert depth == 1 and x_sample.shape[1] == SUBLANES
    outs = _layer(x_prompt, x_sample, state_sconv[0], state_ssm_conv[0], state_ssm[0], meta_tokens,
                  norm_mix_pre[0], norm_mix_post[0], w_in, sconv_w[0], sconv_norm[0], ssm_conv_w[0],
                  ssm_conv_b[0], dt_bias[0], A_log[0], D_skip[0], ssm_norm[0], w_out[0],
                  norm_ffn_pre[0], norm_ffn_post[0], w_gate[0], w_up[0], w_down[0])
    y_prompt, y_sample = outs[0], outs[1]
    return (y_prompt, y_sample) + tuple(o[None] for o in outs[2:])
```

```python
import functools

import jax
import jax.numpy as jnp
from jax import lax
from jax.experimental import pallas as pl
from jax.experimental.pallas import tpu as pltpu

D_MODEL = 1024
N_META = 16
D_SCONV = 512
SCONV_GROUPS = 8
SCONV_WIDTH = 3
SSM_HEAD_DIM = 64
SSM_HEADS = 16
D_SSM = 1024
SSM_GROUPS = 2
HEADS_PER_GROUP = 8
D_GROUP = HEADS_PER_GROUP * SSM_HEAD_DIM
SSM_STATE = 128
SSM_CONV_WIDTH = 4
D_BC = 2 * SSM_GROUPS * SSM_STATE
D_XBC = D_SSM + D_BC
D_MIX = D_SCONV + D_SSM
D_IN = 3 * D_SCONV + D_SSM + D_XBC + SSM_HEADS
EPS = 1e-6
LOG2_E = 1.4426950408889634

LANES = 128
SUBLANES = 8
D_IN_PAD = D_IN + (LANES - SSM_HEADS)
HEAD_PAIRS = D_SSM // LANES

O_GATE_B = 0
O_GATE_C = D_SCONV
O_HV = 2 * D_SCONV
O_Z = 3 * D_SCONV
O_XS = O_Z + D_SSM
O_BC = O_XS + D_SSM
O_DT = O_BC + D_BC

CHUNK = 128
TILE_CHUNKS = 2
STAGE_A_COLS = tuple(range(0, O_DT + 1, D_SCONV))
STAGE_A_PLAN = ((0, "sconv"), (0, "prepare"), (0, 1), (0, 3), (0, 5), (0, "end"), (1, "prepare"), (1, 1), (1, 3))
SEQ_TILE = 16
FFN_ROWS = 512
FFN_SUB_ROWS = 256
VMEM_LIMIT = 60000 * 1024

BF16 = jnp.bfloat16
F32 = jnp.float32


def _dot(a, b):
    return jnp.dot(a, b, preferred_element_type=F32)


def _dot_nt(a, b):
    return lax.dot_general(a, b, (((1,), (1,)), ((), ())), preferred_element_type=F32)


def _dot_tn(a, b):
    return lax.dot_general(a, b, (((0,), (0,)), ((), ())), preferred_element_type=F32)


def _split3(v):
    hi = v.astype(BF16)
    r1 = v - hi.astype(F32)
    mid = r1.astype(BF16)
    lo = (r1 - mid.astype(F32)).astype(BF16)
    return hi, mid, lo


def _dot3(v, m):
    hi, mid, lo = _split3(v)
    return _dot(hi, m) + _dot(mid, m) + _dot(lo, m)


def _dot3_left(m, v):
    hi, mid, lo = _split3(v)
    return _dot(m, hi) + _dot(m, mid) + _dot(m, lo)


def _rms(x, w):
    return x * lax.rsqrt(jnp.mean(x * x, axis=-1, keepdims=True) + EPS) * w


def _silu(x):
    hx = 0.5 * x
    return hx + hx * jnp.tanh(hx)


def _softplus(x):
    return jnp.maximum(x, 0.0) + jnp.log1p(jnp.exp(-jnp.abs(x)))


def _shifted(u, prev, j, rows_per_seq):
    t = u.shape[0]
    s = pltpu.roll(u, j, 0)
    if rows_per_seq is None:
        row = lax.broadcasted_iota(jnp.int32, (SUBLANES, u.shape[1]), 0)
        first = jnp.where(row < j, pltpu.roll(prev, j, 0), s[0:SUBLANES])
        if t == SUBLANES:
            return first
        return jnp.concatenate([first, s[SUBLANES:]], axis=0)
    row = lax.broadcasted_iota(jnp.int32, u.shape, 0) % rows_per_seq
    n_buf = prev.shape[1]
    for tt in range(j):
        buf_row = prev[:, n_buf - j + tt, :]
        spread = jnp.broadcast_to(buf_row[:, None, :], (buf_row.shape[0], rows_per_seq, u.shape[1])).reshape(u.shape)
        s = jnp.where(row == tt, spread, s)
    return s


def _causal_conv(u, prev, w, rows_per_seq):
    k = w.shape[0]
    out = _shifted(u, prev, k - 1, rows_per_seq) * w[0:1]
    for i in range(1, k - 1):
        out = out + _shifted(u, prev, k - 1 - i, rows_per_seq) * w[i:i + 1]
    return out + u * w[k - 1:k]


def _sconv_mixer(gate_b, gate_c, hv, prev, sconv_w, sconv_norm, rows_per_seq):
    u = gate_c * hv
    conv_u = _causal_conv(u, prev, sconv_w, rows_per_seq)
    return _sconv_norm(gate_b, conv_u, sconv_norm), u


def _sconv_norm(gate_b, conv_u, sconv_norm):
    v = gate_b * conv_u
    first_half = lax.broadcasted_iota(jnp.int32, (v.shape[0], LANES), 1) < D_SCONV // SCONV_GROUPS
    parts = []
    for c in range(0, D_SCONV, LANES):
        vc = v[:, c:c + LANES]
        sq = vc * vc
        s_first = jnp.sum(jnp.where(first_half, sq, 0.0), axis=-1, keepdims=True)
        s_second = jnp.sum(jnp.where(first_half, 0.0, sq), axis=-1, keepdims=True)
        mean_sq = jnp.where(first_half, s_first, s_second) * (SCONV_GROUPS / D_SCONV)
        parts.append(vc * lax.rsqrt(mean_sq + EPS))
    return jnp.concatenate(parts, axis=1) * sconv_norm


def _ssd_prepare(dt_raw, dt_bias, a_log, tri, t):
    lane = lax.broadcasted_iota(jnp.int32, (t, LANES), 1)
    valid = lane < SSM_HEADS
    dt = jnp.where(valid, _softplus(dt_raw + dt_bias), 0.0)
    a = dt * (-jnp.exp(a_log))
    a_cum = _dot3_left(tri, a)
    return dt, a_cum


def _ssd_prepare_rows(dt_raw_t, dt_bias_t, a_log_t, tri, seq_ones=None):
    t = dt_raw_t.shape[1]
    dt_t = _softplus(dt_raw_t + dt_bias_t)
    a_t = dt_t * (-(jnp.exp(a_log_t) * LOG2_E))
    hi, mid, lo = _split3(a_t)
    a_cum_t = _dot_nt(hi, tri) + _dot_nt(mid, tri) + _dot_nt(lo, tri)
    if seq_ones is None:
        a_total_t = jnp.broadcast_to(a_cum_t[:, t - 1:t], (SSM_HEADS, t))
    else:
        a_total_t = _dot_nt(hi, seq_ones) + _dot_nt(mid, seq_ones) + _dot_nt(lo, seq_ones)
    w_t = jnp.exp2(a_total_t - a_cum_t) * dt_t
    pad = jnp.zeros((LANES - SSM_HEADS, t), F32)
    a_cum = jnp.concatenate([a_cum_t, pad], axis=0).T
    w = jnp.concatenate([w_t, pad], axis=0).T
    return a_cum, w, a_cum_t, dt_t, a_total_t


def _group_rms_ssm(y, w):
    parts = []
    for g in range(SSM_GROUPS):
        yg = y[:, g * D_GROUP:(g + 1) * D_GROUP]
        parts.append(yg * lax.rsqrt(jnp.mean(yg * yg, axis=-1, keepdims=True) + EPS))
    return jnp.concatenate(parts, axis=1) * w


def _ssd_heads(xs_of, bm_b, cm, a_cum, w, a_cum_t, dt_t, mask, state_t=None, gate=None, after_pair=None):
    t = a_cum.shape[0]
    assert t == LANES
    first_half = lax.broadcasted_iota(jnp.int32, (t, LANES), 1) < SSM_HEAD_DIM
    y_pairs, xw_pairs, ea_pairs, xs_pairs = [], [], [], []
    for g in range(SSM_GROUPS):
        cm_g = cm[:, g * SSM_STATE:(g + 1) * SSM_STATE]
        cb = _dot_nt(cm_g.astype(BF16), bm_b[:, g * SSM_STATE:(g + 1) * SSM_STATE])
        for q in range(HEADS_PER_GROUP // 2):
            pair = g * (HEADS_PER_GROUP // 2) + q
            cols = slice(pair * LANES, (pair + 1) * LANES)
            xs_p = xs_of(pair)
            rhs = xs_p.astype(BF16)
            if state_t is not None:
                rhs = jnp.concatenate([rhs, state_t[:, cols].astype(BF16)], axis=0)
            res, eas, ws = [], [], []
            for e in (2 * pair, 2 * pair + 1):
                a_col = jnp.broadcast_to(a_cum[:, e:e + 1], (t, LANES))
                decay = jnp.exp2(jnp.where(mask, a_col - a_cum_t[e:e + 1, :], -jnp.inf))
                lhs = (cb * decay * dt_t[e:e + 1, :]).astype(BF16)
                ea = jnp.exp2(a_col)
                if state_t is not None:
                    lhs = jnp.concatenate([lhs, (cm_g * ea).astype(BF16)], axis=1)
                res.append(_dot(lhs, rhs))
                eas.append(ea)
                ws.append(jnp.broadcast_to(w[:, e:e + 1], (t, LANES)))
            y_p = jnp.where(first_half, res[0], res[1])
            y_pairs.append(y_p if gate is None else gate(pair, y_p, xs_p))
            xw_pairs.append((xs_p * jnp.where(first_half, ws[0], ws[1])).astype(BF16))
            ea_pairs.append(jnp.where(first_half, eas[0], eas[1]))
            xs_pairs.append(xs_p)
            if after_pair is not None:
                after_pair(pair, y_pairs, xw_pairs, ea_pairs)
    return (jnp.concatenate(y_pairs, axis=1), jnp.concatenate(xw_pairs, axis=1),
            jnp.concatenate(ea_pairs, axis=1), jnp.concatenate(xs_pairs, axis=1))


def _ssm_tail(y, xs, z, d_full, ssm_norm):
    y = y + d_full * xs
    y = y * _silu(z)
    return _group_rms_ssm(y, ssm_norm)


def _w_main_spec(index_map, **kwargs):
    return pl.BlockSpec((1, D_MODEL, O_DT), index_map, **kwargs)


def _inproj_kernel(x_ref, npre_ref, w_main_ref, w_dt_ref, o_ref):
    hn = _rms(x_ref[...], npre_ref[...]).astype(BF16)
    o_ref[:, 0:O_DT] = _dot(hn, w_main_ref[0])
    o_ref[:, O_DT:D_IN_PAD] = _dot(hn, w_dt_ref[...])


def _inproj(x, npre, w_main_b, w_dt_b):
    m = x.shape[0]
    once = lambda shape: pl.BlockSpec(shape, lambda i: (0, 0), pipeline_mode=pl.Buffered(1))
    return pl.pallas_call(
        _inproj_kernel,
        grid=(1,),
        in_specs=[once((m, D_MODEL)), once((1, D_MODEL)),
                  _w_main_spec(lambda i: (0, 0, 0), pipeline_mode=pl.Buffered(1)), once((D_MODEL, LANES))],
        out_specs=once((m, D_IN_PAD)),
        out_shape=jax.ShapeDtypeStruct((m, D_IN_PAD), F32),
        compiler_params=pltpu.CompilerParams(dimension_semantics=("arbitrary",), vmem_limit_bytes=VMEM_LIMIT),
        name="inproj",
    )(x, npre, w_main_b, w_dt_b)


def _meta_kernel(proj_ref, xconv_w_ref, xconv_b_ref, dt_bias_ref, a_log_ref, tri_ref, expand_ref,
                 sc_tail_ref, xbc_tail_ref, state_t_ref):
    t = N_META
    proj = proj_ref[...]
    u = proj[:, O_GATE_C:O_GATE_C + D_SCONV] * proj[:, O_HV:O_HV + D_SCONV]
    sc_tail_ref[...] = u[t - SUBLANES:t]
    xbc = proj[:, O_XS:O_XS + D_XBC]
    xbc_tail_ref[...] = xbc[t - SUBLANES:t]
    zeros_prev = jnp.zeros((SUBLANES, D_XBC), F32)
    xbc_c = _silu(_causal_conv(xbc, zeros_prev, xconv_w_ref[...], None) + xconv_b_ref[...])
    xs = xbc_c[:, 0:D_SSM]
    dt, a_cum = _ssd_prepare(proj[:, O_DT:O_DT + LANES], dt_bias_ref[...], a_log_ref[...],
                             tri_ref[0:t, 0:t], t)
    w = jnp.exp(a_cum[t - 1:t, :] - a_cum) * dt
    xw_b = (xs * _dot3(w, expand_ref[...])).astype(BF16)
    for g in range(SSM_GROUPS):
        bm_b = xbc_c[:, D_SSM + g * SSM_STATE:D_SSM + (g + 1) * SSM_STATE].astype(BF16)
        state_t_ref[:, g * D_GROUP:(g + 1) * D_GROUP] = _dot_tn(bm_b, xw_b[:, g * D_GROUP:(g + 1) * D_GROUP])


def _meta_state(proj_meta, xconv_w, xconv_b, dt_bias, a_log, tri, expand):
    full = lambda shape: pl.BlockSpec(shape, lambda: tuple(0 for _ in shape))
    return pl.pallas_call(
        _meta_kernel,
        in_specs=[full(proj_meta.shape), full(xconv_w.shape), full(xconv_b.shape), full(dt_bias.shape),
                  full(a_log.shape), full(tri.shape), full(expand.shape)],
        out_specs=[full((SUBLANES, D_SCONV)), full((SUBLANES, D_XBC)), full((SSM_STATE, D_SSM))],
        out_shape=[jax.ShapeDtypeStruct((SUBLANES, D_SCONV), F32),
                   jax.ShapeDtypeStruct((SUBLANES, D_XBC), F32),
                   jax.ShapeDtypeStruct((SSM_STATE, D_SSM), F32)],
        name="meta_state",
    )(proj_meta, xconv_w, xconv_b, dt_bias, a_log, tri, expand)


def _pmix_kernel(x_ref, npre_ref, w_main_ref, w_dt_ref, sconv_w_ref, sconv_norm_ref,
                 xconv_w_ref, xconv_b_ref, dt_bias_ref, a_log_ref, d_full_ref, ssm_norm_ref,
                 tri_ref, w_out_ref, npost_ref, sc0_ref, xbc0_ref, state0_t_ref,
                 h1_ref, sc_out_ref, xbc_out_ref, state_ref,
                 proj_s0, proj_s1, dt_s0, dt_s1, h_s0, h_s1, sc_tail, xbc_tail, state_t_ref, *,
                 steps_per_seq):
    rows = TILE_CHUNKS * CHUNK
    k = pl.program_id(0)
    step_in_seq = lax.rem(jnp.maximum(k - 1, 0), steps_per_seq)

    @pl.when(k == 0)
    def _():
        proj_s1[...] = jnp.zeros((rows, O_DT), F32)
        dt_s1[...] = jnp.zeros((SSM_HEADS, rows), F32)
        h_s1[...] = jnp.zeros((rows, D_MODEL), F32)

    @pl.when(step_in_seq == 0)
    def _():
        sc_tail[...] = sc0_ref[...]
        xbc_tail[...] = xbc0_ref[...]
        state_t_ref[...] = state0_t_ref[...]

    for parity in (0, 1):
        pl.when(lax.rem(k, 2) == parity)(functools.partial(
            _pmix_step, parity, x_ref, npre_ref, w_main_ref, w_dt_ref, sconv_w_ref, sconv_norm_ref,
            xconv_w_ref, xconv_b_ref, dt_bias_ref, a_log_ref, d_full_ref, ssm_norm_ref, tri_ref, w_out_ref,
            npost_ref, h1_ref, (proj_s0, proj_s1), (dt_s0, dt_s1), (h_s0, h_s1), sc_tail, xbc_tail,
            state_t_ref))

    @pl.when(step_in_seq == steps_per_seq - 1)
    def _():
        sc_out_ref[0] = sc_tail[SUBLANES - (SCONV_WIDTH - 1):SUBLANES, :]
        xbc_out_ref[0] = xbc_tail[SUBLANES - (SSM_CONV_WIDTH - 1):SUBLANES, :]
        for p in range(HEAD_PAIRS):
            state_ref[0, p * LANES:(p + 1) * LANES, :] = state_t_ref[:, p * LANES:(p + 1) * LANES].T


def _pmix_step(slot_a, x_ref, npre_ref, w_main_ref, w_dt_ref, sconv_w_ref, sconv_norm_ref,
               xconv_w_ref, xconv_b_ref, dt_bias_ref, a_log_ref, d_full_ref, ssm_norm_ref, tri_ref, w_out_ref,
               npost_ref, h1_ref, proj_s, dt_s, h_s, sc_tail, xbc_tail, state_t_ref):
    t = CHUNK
    slot_b = 1 - slot_a

    h_a = x_ref[0]
    h_s[slot_a][...] = h_a
    hn = _rms(h_a, npre_ref[...]).astype(BF16)

    def stage_a(point):
        for piece, planned in enumerate(STAGE_A_PLAN):
            if planned == point:
                if piece + 1 < len(STAGE_A_COLS):
                    off, end = STAGE_A_COLS[piece], STAGE_A_COLS[piece + 1]
                    proj_s[slot_a][:, off:end] = _dot(hn, w_main_ref[0, :, off:end])
                else:
                    dt_s[slot_a][...] = _dot_nt(w_dt_ref[...], hn)

    for sub in range(TILE_CHUNKS):
        r = slice(sub * t, (sub + 1) * t)

        def proj(off, width):
            return proj_s[slot_b][r, off:off + width]

        ya, u = _sconv_mixer(proj(O_GATE_B, D_SCONV), proj(O_GATE_C, D_SCONV), proj(O_HV, D_SCONV),
                             sc_tail[...], sconv_w_ref[...], sconv_norm_ref[...], None)
        sc_tail[...] = u[t - SUBLANES:t]
        stage_a((sub, "sconv"))
        mix_ya = _dot(ya.astype(BF16), w_out_ref[0:D_SCONV, :])

        def xconv(col, width):
            cols = slice(col, col + width)
            raw = proj(O_XS + col, width)
            out = _silu(_causal_conv(raw, xbc_tail[:, cols], xconv_w_ref[:, cols], None) + xconv_b_ref[:, cols])
            xbc_tail[:, cols] = raw[t - SUBLANES:t]
            return out

        bc = xconv(D_SSM, D_BC)
        bm_b = bc[:, 0:SSM_GROUPS * SSM_STATE].astype(BF16)
        cm = bc[:, SSM_GROUPS * SSM_STATE:D_BC]
        a_cum, w, a_cum_t, dt_t, _ = _ssd_prepare_rows(dt_s[slot_b][:, r], dt_bias_ref[...], a_log_ref[...],
                                                       tri_ref[...])
        causal = lax.broadcasted_iota(jnp.int32, (t, t), 0) >= lax.broadcasted_iota(jnp.int32, (t, t), 1)
        stage_a((sub, "prepare"))

        def gate(p, y_p, xs_p):
            cols = slice(p * LANES, (p + 1) * LANES)
            return (y_p + d_full_ref[:, cols] * xs_p) * _silu(proj(O_Z + p * LANES, LANES))

        y, xw_b, ea_full, _ = _ssd_heads(lambda p: xconv(p * LANES, LANES), bm_b, cm, a_cum, w, a_cum_t, dt_t,
                                         causal, state_t_ref, gate, lambda p, *partial: stage_a((sub, p)))
        chunk_decay = ea_full[t - 1:t, :]
        for g in range(SSM_GROUPS):
            cols = slice(g * D_GROUP, (g + 1) * D_GROUP)
            state_t_ref[:, cols] = chunk_decay[:, cols] * state_t_ref[:, cols] + _dot_tn(
                bm_b[:, g * SSM_STATE:(g + 1) * SSM_STATE], xw_b[:, cols])
        yb = _group_rms_ssm(y, ssm_norm_ref[...])
        mix = mix_ya + _dot(yb.astype(BF16), w_out_ref[D_SCONV:D_MIX, :])
        stage_a((sub, "end"))
        h1_ref[0, r, :] = h_s[slot_b][r, :] + _rms(mix, npost_ref[...])


def _pmix(x_prompt, consts, sc0, xbc0, state0_t):
    batch, seq, _ = x_prompt.shape
    rows = TILE_CHUNKS * CHUNK
    sps = seq // rows
    n_tiles = batch * sps
    const_spec = lambda a: (pl.BlockSpec(a.shape, lambda k: (0, 0), pipeline_mode=pl.Buffered(1)) if a.ndim == 2
                            else _w_main_spec(lambda k: (0, 0, 0), pipeline_mode=pl.Buffered(1)))
    ins = list(consts) + [sc0, xbc0, state0_t]

    def tile_a(k):
        tile = jnp.minimum(k, n_tiles - 1)
        return (tile // sps, tile % sps, 0)

    def tile_b(k):
        tile = jnp.maximum(k - 1, 0)
        return (tile // sps, tile % sps, 0)

    def seq_b(k):
        return (jnp.maximum(k - 1, 0) // sps, 0, 0)

    return pl.pallas_call(
        functools.partial(_pmix_kernel, steps_per_seq=sps),
        grid=(n_tiles + 1,),
        in_specs=[pl.BlockSpec((1, rows, D_MODEL), tile_a)] + [const_spec(a) for a in ins],
        out_specs=[
            pl.BlockSpec((1, rows, D_MODEL), tile_b),
            pl.BlockSpec((1, SCONV_WIDTH - 1, D_SCONV), seq_b),
            pl.BlockSpec((1, SSM_CONV_WIDTH - 1, D_XBC), seq_b),
            pl.BlockSpec((1, D_SSM, SSM_STATE), seq_b),
        ],
        out_shape=[
            jax.ShapeDtypeStruct((batch, seq, D_MODEL), F32),
            jax.ShapeDtypeStruct((batch, SCONV_WIDTH - 1, D_SCONV), F32),
            jax.ShapeDtypeStruct((batch, SSM_CONV_WIDTH - 1, D_XBC), F32),
            jax.ShapeDtypeStruct((batch, D_SSM, SSM_STATE), F32),
        ],
        scratch_shapes=[
            pltpu.VMEM((rows, O_DT), F32),
            pltpu.VMEM((rows, O_DT), F32),
            pltpu.VMEM((SSM_HEADS, rows), F32),
            pltpu.VMEM((SSM_HEADS, rows), F32),
            pltpu.VMEM((rows, D_MODEL), F32),
            pltpu.VMEM((rows, D_MODEL), F32),
            pltpu.VMEM((SUBLANES, D_SCONV), F32),
            pltpu.VMEM((SUBLANES, D_XBC), F32),
            pltpu.VMEM((SSM_STATE, D_SSM), F32),
        ],
        compiler_params=pltpu.CompilerParams(dimension_semantics=("arbitrary",), vmem_limit_bytes=VMEM_LIMIT),
        name="prompt_mixer",
    )(x_prompt, *ins)


def _smix_kernel(x_ref, sc_prev_ref, xbc_prev_ref, state_in_ref, npre_ref, w_main_ref, w_dt_ref,
                 sconv_w_ref, sconv_norm_ref, xconv_w_ref, xconv_b_ref,
                 dt_bias_ref, a_log_ref, d_full_ref, ssm_norm_ref, w_out_ref, npost_ref,
                 h1_ref, u_ref, xbc_ref, state_ref, *, dec_seq):
    t = SEQ_TILE * dec_seq
    h = x_ref[...]
    hn = _rms(h, npre_ref[...]).astype(BF16)

    def proj(off, width):
        return _dot(hn, w_main_ref[0, :, off:off + width])

    ya, u = _sconv_mixer(proj(O_GATE_B, D_SCONV), proj(O_GATE_C, D_SCONV), proj(O_HV, D_SCONV),
                         sc_prev_ref[...], sconv_w_ref[...], sconv_norm_ref[...], dec_seq)
    for s in range(SEQ_TILE):
        u_ref[s] = u[(s + 1) * dec_seq - (SCONV_WIDTH - 1):(s + 1) * dec_seq, :]
    xbc = proj(O_XS, D_XBC)
    for s in range(SEQ_TILE):
        xbc_ref[s] = xbc[(s + 1) * dec_seq - (SSM_CONV_WIDTH - 1):(s + 1) * dec_seq, :]
    xbc_c = _silu(_causal_conv(xbc, xbc_prev_ref[...], xconv_w_ref[...], dec_seq) + xconv_b_ref[...])
    xs = xbc_c[:, 0:D_SSM]
    bm = xbc_c[:, D_SSM:D_SSM + SSM_GROUPS * SSM_STATE]
    bm_b = bm.astype(BF16)
    cm = xbc_c[:, D_SSM + SSM_GROUPS * SSM_STATE:D_XBC]

    row = lax.broadcasted_iota(jnp.int32, (t, t), 0)
    col = lax.broadcasted_iota(jnp.int32, (t, t), 1)
    same_seq = row // dec_seq == col // dec_seq
    mask = jnp.logical_and(row >= col, same_seq)
    tri = jnp.where(mask, 1.0, 0.0).astype(BF16)
    seq_ones = jnp.where(same_seq, 1.0, 0.0).astype(BF16)
    a_cum, w, a_cum_t, dt_t, a_total_t = _ssd_prepare_rows(_dot_nt(w_dt_ref[...], hn), dt_bias_ref[...],
                                                           a_log_ref[...], tri, seq_ones)
    y, xw_b, ea_full, _ = _ssd_heads(lambda p: xs[:, p * LANES:(p + 1) * LANES], bm_b, cm, a_cum, w, a_cum_t,
                                     dt_t, mask)

    row_id = lax.broadcasted_iota(jnp.int32, (t, SSM_STATE), 0) // dec_seq
    seq_decay_t = jnp.exp2(a_total_t)
    y_off_rows = []
    for s in range(SEQ_TILE):
        r0 = s * dec_seq
        y_off_g = []
        for g in range(SSM_GROUPS):
            s_prev = state_in_ref[s, g * D_GROUP:(g + 1) * D_GROUP, :]
            cm_s = cm[r0:r0 + dec_seq, g * SSM_STATE:(g + 1) * SSM_STATE].astype(BF16)
            y_off_g.append(_dot_nt(cm_s, s_prev.astype(BF16)))
        y_off_rows.append(jnp.concatenate(y_off_g, axis=1))
    y = y + jnp.concatenate(y_off_rows, axis=0) * ea_full
    yb = _ssm_tail(y, xs, proj(O_Z, D_SSM), d_full_ref[...], ssm_norm_ref[...])
    mix = _dot(ya.astype(BF16), w_out_ref[0:D_SCONV, :]) + _dot(yb.astype(BF16), w_out_ref[D_SCONV:D_MIX, :])
    h1_ref[...] = h + _rms(mix, npost_ref[...])

    for s in range(SEQ_TILE):
        r0 = s * dec_seq
        q_t = jnp.broadcast_to(seq_decay_t[:, r0:r0 + 1], (SSM_HEADS, SSM_STATE))
        for g in range(SSM_GROUPS):
            rows = slice(g * D_GROUP, (g + 1) * D_GROUP)
            bm_g = jnp.where(row_id == s, bm[:, g * SSM_STATE:(g + 1) * SSM_STATE], 0.0).astype(BF16)
            decay = jnp.concatenate(
                [jnp.broadcast_to(q_t[e:e + 1, :], (SSM_HEAD_DIM, SSM_STATE))
                 for e in range(g * HEADS_PER_GROUP, (g + 1) * HEADS_PER_GROUP)], axis=0)
            state_ref[s, rows, :] = decay * state_in_ref[s, rows, :] + _dot_tn(xw_b[:, rows], bm_g)


def _smix(x_rows, sc_prev, xbc_prev, state, consts, dec_seq):
    n_seq = state.shape[0]
    t = SEQ_TILE * dec_seq
    const_spec = lambda a: (pl.BlockSpec(a.shape, lambda i: (0, 0), pipeline_mode=pl.Buffered(1)) if a.ndim == 2
                            else _w_main_spec(lambda i: (0, 0, 0), pipeline_mode=pl.Buffered(1)))
    row_spec = lambda width: pl.BlockSpec((t, width), lambda i: (i, 0))
    state_spec = pl.BlockSpec((SEQ_TILE, D_SSM, SSM_STATE), lambda i: (i, 0, 0))
    return pl.pallas_call(
        functools.partial(_smix_kernel, dec_seq=dec_seq),
        grid=(n_seq // SEQ_TILE,),
        in_specs=[row_spec(D_MODEL),
                  pl.BlockSpec((SEQ_TILE, SCONV_WIDTH - 1, D_SCONV), lambda i: (i, 0, 0)),
                  pl.BlockSpec((SEQ_TILE, SSM_CONV_WIDTH - 1, D_XBC), lambda i: (i, 0, 0)), state_spec]
        + [const_spec(a) for a in consts],
        out_specs=[row_spec(D_MODEL),
                   pl.BlockSpec((SEQ_TILE, SCONV_WIDTH - 1, D_SCONV), lambda i: (i, 0, 0)),
                   pl.BlockSpec((SEQ_TILE, SSM_CONV_WIDTH - 1, D_XBC), lambda i: (i, 0, 0)), state_spec],
        out_shape=[
            jax.ShapeDtypeStruct((n_seq * dec_seq, D_MODEL), F32),
            jax.ShapeDtypeStruct((n_seq, SCONV_WIDTH - 1, D_SCONV), F32),
            jax.ShapeDtypeStruct((n_seq, SSM_CONV_WIDTH - 1, D_XBC), F32),
            jax.ShapeDtypeStruct(state.shape, F32),
        ],
        compiler_params=pltpu.CompilerParams(dimension_semantics=("arbitrary",), vmem_limit_bytes=VMEM_LIMIT),
        name="sample_mixer",
    )(x_rows, sc_prev, xbc_prev, state, *consts)


def _ffn_kernel(hp_ref, hs_ref, npre_ref, w_gate_ref, w_up_ref, w_down_ref, npost_ref, op_ref, os_ref, *,
                prompt_steps):
    def body(h_ref, o_ref):
        for r0 in range(0, FFN_ROWS, FFN_SUB_ROWS):
            r = slice(r0, r0 + FFN_SUB_ROWS)
            h1 = h_ref[r, :]
            fn = _rms(h1, npre_ref[...]).astype(BF16)
            act = (_silu(_dot(fn, w_gate_ref[...])) * _dot(fn, w_up_ref[...])).astype(BF16)
            o_ref[r, :] = h1 + _rms(_dot(act, w_down_ref[...]), npost_ref[...])

    step = pl.program_id(0)
    pl.when(step < prompt_steps)(functools.partial(body, hp_ref, op_ref))
    pl.when(step >= prompt_steps)(functools.partial(body, hs_ref, os_ref))


def _ffn(h_prompt, h_sample, ffn_consts):
    mp, ms = h_prompt.shape[0], h_sample.shape[0]
    rows = FFN_ROWS
    p_steps, s_steps = mp // rows, ms // rows
    p_spec = pl.BlockSpec((rows, D_MODEL), lambda i: (jnp.minimum(i, p_steps - 1), 0))
    s_spec = pl.BlockSpec((rows, D_MODEL), lambda i: (jnp.maximum(i - p_steps, 0), 0))
    const_spec = lambda a: pl.BlockSpec(a.shape, lambda i: (0, 0), pipeline_mode=pl.Buffered(1))
    return pl.pallas_call(
        functools.partial(_ffn_kernel, prompt_steps=p_steps),
        grid=(p_steps + s_steps,),
        in_specs=[p_spec, s_spec] + [const_spec(a) for a in ffn_consts],
        out_specs=[p_spec, s_spec],
        out_shape=[jax.ShapeDtypeStruct((mp, D_MODEL), F32), jax.ShapeDtypeStruct((ms, D_MODEL), F32)],
        compiler_params=pltpu.CompilerParams(dimension_semantics=("arbitrary",), vmem_limit_bytes=VMEM_LIMIT),
        name="ffn",
    )(h_prompt, h_sample, *ffn_consts)


def _row(v, width=None):
    v = v.reshape(1, -1).astype(F32)
    if width is not None and v.shape[1] < width:
        v = jnp.pad(v, ((0, 0), (0, width - v.shape[1])))
    return v


def _layer(hp, hs, sc_state, xbc_state, ssm_state, meta_tokens,
           norm_mix_pre, norm_mix_post, w_in_all, sconv_w, sconv_norm, ssm_conv_w, ssm_conv_b,
           dt_bias, a_log, d_skip, ssm_norm, w_out, norm_ffn_pre, norm_ffn_post, w_gate, w_up, w_down):
    batch, seq, _ = hp.shape
    w_in = w_in_all[0]
    n_seq, dec_seq, _ = hs.shape

    w_main_b = w_in_all.astype(BF16)
    w_dt_b = jnp.pad(w_in[:, O_DT:D_IN], ((0, 0), (0, D_IN_PAD - D_IN))).astype(BF16)
    w_dt_t = w_in[:, O_DT:D_IN].T.astype(BF16)
    w_out_b = w_out.astype(BF16)
    w_gate_b, w_up_b, w_down_b = w_gate.astype(BF16), w_up.astype(BF16), w_down.astype(BF16)
    npre, npost = _row(norm_mix_pre), _row(norm_mix_post)
    fpre, fpost = _row(norm_ffn_pre), _row(norm_ffn_post)
    sconv_w = sconv_w.astype(F32)
    xconv_w = ssm_conv_w.astype(F32)
    xconv_b = _row(ssm_conv_b)
    dt_bias_p, a_log_p = _row(dt_bias, LANES), _row(a_log, LANES)
    dt_bias_t = jnp.broadcast_to(dt_bias.astype(F32)[:, None], (SSM_HEADS, LANES))
    a_log_t = jnp.broadcast_to(a_log.astype(F32)[:, None], (SSM_HEADS, LANES))
    d_full = jnp.repeat(d_skip.astype(F32), SSM_HEAD_DIM).reshape(1, D_SSM)
    sconv_norm_r, ssm_norm_r = _row(sconv_norm), _row(ssm_norm)

    r = jnp.arange(CHUNK)
    tri = (r[:, None] >= r[None, :]).astype(BF16)
    expand = (jnp.arange(LANES)[:, None] == (jnp.arange(D_SSM)[None, :] // SSM_HEAD_DIM)).astype(BF16)

    mixer_consts = (sconv_w, sconv_norm_r, xconv_w, xconv_b, dt_bias_t, a_log_t, d_full, ssm_norm_r)
    ffn_consts = (fpre, w_gate_b, w_up_b, w_down_b, fpost)

    proj_meta = _inproj(meta_tokens.astype(F32), npre, w_main_b, w_dt_b)
    sc0, xbc0, state0_t = _meta_state(proj_meta, xconv_w, xconv_b, dt_bias_p, a_log_p, tri, expand)

    pm_consts = (npre, w_main_b, w_dt_t) + mixer_consts + (tri, w_out_b, npost)
    h1, new_sc_p, new_xbc_p, p_state = _pmix(hp, pm_consts, sc0, xbc0, state0_t)
    new_ssm_p = p_state.reshape(batch, SSM_GROUPS, HEADS_PER_GROUP, SSM_HEAD_DIM, SSM_STATE)

    hs_rows = hs.reshape(n_seq * dec_seq, D_MODEL)
    sm_consts = (npre, w_main_b, w_dt_t) + mixer_consts + (w_out_b, npost)
    h1_s, new_sc_s, new_xbc_s, s_state = _smix(hs_rows, sc_state, xbc_state, ssm_state.reshape(n_seq, D_SSM, SSM_STATE),
                                      sm_consts, dec_seq)

    y_prompt, y_sample = _ffn(h1.reshape(batch * seq, D_MODEL), h1_s, ffn_consts)
    y_prompt = y_prompt.reshape(batch, seq, D_MODEL)
    y_sample = y_sample.reshape(n_seq, dec_seq, D_MODEL)
    new_ssm_s = s_state.reshape(n_seq, SSM_GROUPS, HEADS_PER_GROUP, SSM_HEAD_DIM, SSM_STATE)
    return y_prompt, y_sample, new_sc_p, new_xbc_p, new_ssm_p, new_sc_s, new_xbc_s, new_ssm_s


def kernel(x_prompt, x_sample, state_sconv, state_ssm_conv, state_ssm, meta_tokens, norm_mix_pre, norm_mix_post, w_in, sconv_w, sconv_norm, ssm_conv_w, ssm_conv_b, dt_bias, A_log, D_skip, ssm_norm, w_out, norm_ffn_pre, norm_ffn_post, w_gate, w_up, w_down):
    depth = w_in.shape[0]
    assert depth == 1 and x_sample.shape[1] == SUBLANES
    outs = _layer(x_prompt, x_sample, state_sconv[0], state_ssm_conv[0], state_ssm[0], meta_tokens,
                  norm_mix_pre[0], norm_mix_post[0], w_in, sconv_w[0], sconv_norm[0], ssm_conv_w[0],
                  ssm_conv_b[0], dt_bias[0], A_log[0], D_skip[0], ssm_norm[0], w_out[0],
                  norm_ffn_pre[0], norm_ffn_post[0], w_gate[0], w_up[0], w_down[0])
    y_prompt, y_sample = outs[0], outs[1]
    return (y_prompt, y_sample) + tuple(o[None] for o in outs[2:])
```
